```python
import functools
import jax, jax.numpy as jnp
from jax import lax
import numpy as np

D_MODEL = 2048
BATCH = 2
SEQ = 4096
DEPTH = 1
DEC_BATCH = 8
DEC_SEQ = 16
PAST_LEN = 1024

CHUNK = 64
HEAD_DIM = 64
RWKV_WIDTH = D_MODEL // 2
FOX_WIDTH = D_MODEL - RWKV_WIDTH
RWKV_HEADS = RWKV_WIDTH // HEAD_DIM
FOX_HEADS = FOX_WIDTH // HEAD_DIM
DECAY_LORA = 64
AAA_LORA = 64
GATE_LORA = 160
RWKV_PROJ = 3 * RWKV_WIDTH + DECAY_LORA + AAA_LORA + GATE_LORA
FOX_PROJ = 3 * FOX_WIDTH + FOX_HEADS + FOX_WIDTH
P_TOTAL = RWKV_PROJ + FOX_PROJ
D_FF = 4 * D_MODEL
Q_BLOCK = 128
ALPHA = (2 * DEPTH) ** 0.25
BETA = (8 * DEPTH) ** -0.25
LN_EPS = 1e-5
GN_EPS = 64e-5
RMS_EPS = 1e-6
ATTN_SCALE = HEAD_DIM ** -0.5

kernel_name = 'hybrid_rwkv7_fox_stream_encoder'


def layer_norm(x, g, b):
    xf = x.astype(jnp.float32)
    mu = jnp.mean(xf, -1, keepdims=True)
    var = jnp.mean(jnp.square(xf - mu), -1, keepdims=True)
    return ((xf - mu) * lax.rsqrt(var + LN_EPS) * g + b).astype(x.dtype)


def token_shift(p, prev, mu):
    p_prev = jnp.concatenate([prev.astype(p.dtype), p[:, :-1]], axis=1)
    return p + (p_prev - p) * mu


def rwkv7_mixer(ps, s0, w0, w2, a0, a2, g2, k_k, k_a, r_k, lnx_g, lnx_b):
    B, T, _ = ps.shape
    f32 = jnp.float32
    o1, o2, o3 = RWKV_WIDTH, 2 * RWKV_WIDTH, 3 * RWKV_WIDTH
    o4 = o3 + DECAY_LORA
    o5 = o4 + AAA_LORA
    r, k, v = ps[..., :o1], ps[..., o1:o2], ps[..., o2:o3]
    xw, xa, xg = ps[..., o3:o4], ps[..., o4:o5], ps[..., o5:]
    hs = (B, T, RWKV_HEADS, HEAD_DIM)
    wlog = -jax.nn.softplus(-(w0 + jnp.tanh(xw) @ w2).astype(f32)) - 0.5
    decay = jnp.exp(-jnp.exp(wlog)).reshape(hs)
    a_lr = jax.nn.sigmoid((a0 + xa @ a2).astype(f32))
    g = jax.nn.sigmoid(xg) @ g2
    kk = (k * k_k).astype(f32).reshape(hs)
    kk = kk / jnp.maximum(jnp.sqrt(jnp.sum(kk * kk, -1, keepdims=True)), 1e-12)
    k_h = (k.astype(f32) * (1.0 + (a_lr - 1.0) * k_a)).reshape(hs)
    a_h = a_lr.reshape(hs)
    r_h = r.astype(f32).reshape(hs)
    v_h = v.astype(f32).reshape(hs)

    def step(S, inp):
        r_t, w_t, k_t, v_t, a_t, b_t = inp
        sa = jnp.einsum('bhvk,bhk->bhv', S, a_t)
        S = S * w_t[:, :, None, :] + sa[..., None] * b_t[:, :, None, :] + v_t[..., None] * k_t[:, :, None, :]
        return S, jnp.einsum('bhvk,bhk->bhv', S, r_t)

    seqs = tuple(jnp.moveaxis(z, 1, 0) for z in (r_h, decay, k_h, v_h, -kk, kk * a_h))
    s_final, y = lax.scan(step, s0.astype(f32), seqs)
    y = jnp.moveaxis(y, 0, 1)
    mu = jnp.mean(y, -1, keepdims=True)
    var = jnp.mean(jnp.square(y - mu), -1, keepdims=True)
    yn = ((y - mu) * lax.rsqrt(var + GN_EPS)).reshape(B, T, RWKV_WIDTH) * lnx_g + lnx_b
    bonus = jnp.sum(r_h * k_h * r_k, -1, keepdims=True) * v_h
    out = (yn + bonus.reshape(B, T, RWKV_WIDTH)) * g
    return out.astype(ps.dtype), s_final


def fox_prompt_attention(q, k, v, logf):
    B, T, H, Dh = q.shape
    nb = T // Q_BLOCK
    f32 = jnp.float32
    c = jnp.cumsum(logf, axis=1)
    cT = jnp.moveaxis(c, -1, 1)
    kf = k.astype(f32)
    vf = v.astype(f32)
    qb = (q.astype(f32) * ATTN_SCALE).reshape(B, nb, Q_BLOCK, H, Dh).transpose(1, 0, 2, 3, 4)
    cb = c.reshape(B, nb, Q_BLOCK, H).transpose(1, 0, 3, 2)
    kpos = jnp.arange(T)

    def block(args):
        i, q_i, c_i = args
        qpos = i * Q_BLOCK + jnp.arange(Q_BLOCK)
        s = jnp.einsum('bqhd,bkhd->bhqk', q_i, kf) + (c_i[..., :, None] - cT[..., None, :])
        s = jnp.where(kpos[None, :] <= qpos[:, None], s, -jnp.inf)
        p = jax.nn.softmax(s, axis=-1)
        return jnp.einsum('bhqk,bkhd->bqhd', p, vf)

    o = lax.map(block, (jnp.arange(nb), qb, cb))
    return o.transpose(1, 0, 2, 3, 4).reshape(B, T, H, Dh)


def fox_sample_attention(q, k, v, logf, cache_k, cache_v, cache_logf):
    B, T, H, Dh = q.shape
    P = cache_k.shape[1]
    f32 = jnp.float32
    k_all = jnp.concatenate([cache_k.astype(f32), k.astype(f32)], axis=1)
    v_all = jnp.concatenate([cache_v.astype(f32), v.astype(f32)], axis=1)
    c = jnp.cumsum(jnp.concatenate([cache_logf.astype(f32), logf], axis=1), axis=1)
    cT = jnp.moveaxis(c, -1, 1)
    s = jnp.einsum('bqhd,bkhd->bhqk', q.astype(f32) * ATTN_SCALE, k_all) + (cT[..., P:, None] - cT[..., None, :])
    qpos = P + jnp.arange(T)
    kpos = jnp.arange(P + T)
    s = jnp.where(kpos[None, :] <= qpos[:, None], s, -jnp.inf)
    p = jax.nn.softmax(s, axis=-1)
    return jnp.einsum('bhqk,bkhd->bqhd', p, v_all)


def hybrid_layer(x, shift_prev, s0, fox_attn, w_in, rwkv_mu, rwkv_w0, rwkv_w2, rwkv_a0, rwkv_a2, rwkv_g2,
                 rwkv_k_k, rwkv_k_a, rwkv_r_k, rwkv_lnx_g, rwkv_lnx_b, fox_b_f, fox_out_g, w_o,
                 ln1_g, ln1_b, w_up, w_down, ln2_g, ln2_b):
    B, T, _ = x.shape
    f32 = jnp.float32
    proj = x @ w_in
    p_rwkv = proj[..., :RWKV_PROJ]
    p_fox = proj[..., RWKV_PROJ:]
    ps = token_shift(p_rwkv, shift_prev, rwkv_mu)
    y_r, s_new = rwkv7_mixer(ps, s0, rwkv_w0, rwkv_w2, rwkv_a0, rwkv_a2, rwkv_g2,
                             rwkv_k_k, rwkv_k_a, rwkv_r_k, rwkv_lnx_g, rwkv_lnx_b)
    hs = (B, T, FOX_HEADS, HEAD_DIM)
    q = p_fox[..., :FOX_WIDTH].reshape(hs)
    k = p_fox[..., FOX_WIDTH:2 * FOX_WIDTH].reshape(hs)
    v = p_fox[..., 2 * FOX_WIDTH:3 * FOX_WIDTH].reshape(hs)
    f_logit = p_fox[..., 3 * FOX_WIDTH:3 * FOX_WIDTH + FOX_HEADS]
    og = p_fox[..., 3 * FOX_WIDTH + FOX_HEADS:]
    logf = jax.nn.log_sigmoid((f_logit + fox_b_f).astype(f32))
    o = fox_attn(q, k, v, logf)
    o = o * lax.rsqrt(jnp.mean(jnp.square(o), -1, keepdims=True) + RMS_EPS)
    y_f = (o.reshape(B, T, FOX_WIDTH) * fox_out_g * jax.nn.sigmoid(og.astype(f32))).astype(x.dtype)
    mix = jnp.concatenate([y_r, y_f], axis=-1) @ w_o
    h = layer_norm(ALPHA * x + mix, ln1_g, ln1_b)
    ffn = jnp.square(jax.nn.relu(h @ w_up)) @ w_down
    out = layer_norm(ALPHA * h + ffn, ln2_g, ln2_b)
    return out, k, v, logf, s_new, p_rwkv[:, -1:]


def setup_inputs(seed: int = 0) -> dict:
    key = jax.random.key(seed)
    ks = iter(jax.random.split(key, 40))
    f32 = jnp.float32

    def nrm(shape, scale):
        return scale * jax.random.normal(next(ks), shape, f32)

    w0_base = jnp.tile(jnp.linspace(-6.0, -1.0, HEAD_DIM, dtype=f32), RWKV_HEADS)
    return {
        'x_prompt': nrm((BATCH, SEQ, D_MODEL), 1.0),
        'x_sample': nrm((DEC_BATCH, DEC_SEQ, D_MODEL), 1.0),
        'cache_fox_k': nrm((DEPTH, DEC_BATCH, PAST_LEN, FOX_HEADS, HEAD_DIM), 1.0),
        'cache_fox_v': nrm((DEPTH, DEC_BATCH, PAST_LEN, FOX_HEADS, HEAD_DIM), 1.0),
        'cache_fox_logf': jax.nn.log_sigmoid(3.0 + nrm((DEPTH, DEC_BATCH, PAST_LEN, FOX_HEADS), 0.5)),
        'state_rwkv_wkv': nrm((DEPTH, DEC_BATCH, RWKV_HEADS, HEAD_DIM, HEAD_DIM), 0.3),
        'state_rwkv_shift': nrm((DEPTH, DEC_BATCH, 1, RWKV_PROJ), 1.0),
        'w_in': nrm((DEPTH, D_MODEL, P_TOTAL), D_MODEL ** -0.5),
        'rwkv_mu': jax.random.uniform(next(ks), (DEPTH, RWKV_PROJ), f32),
        'rwkv_w0': w0_base + nrm((DEPTH, RWKV_WIDTH), 0.1),
        'rwkv_w2': nrm((DEPTH, DECAY_LORA, RWKV_WIDTH), 0.1 * DECAY_LORA ** -0.5),
        'rwkv_a0': nrm((DEPTH, RWKV_WIDTH), 0.1),
        'rwkv_a2': nrm((DEPTH, AAA_LORA, RWKV_WIDTH), 0.5 * AAA_LORA ** -0.5),
        'rwkv_g2': nrm((DEPTH, GATE_LORA, RWKV_WIDTH), GATE_LORA ** -0.5),
        'rwkv_k_k': 0.85 + nrm((DEPTH, RWKV_WIDTH), 0.02),
        'rwkv_k_a': 1.0 + nrm((DEPTH, RWKV_WIDTH), 0.02),
        'rwkv_r_k': -0.04 + nrm((DEPTH, RWKV_HEADS, HEAD_DIM), 0.01),
        'rwkv_lnx_g': 1.0 + nrm((DEPTH, RWKV_WIDTH), 0.02),
        'rwkv_lnx_b': nrm((DEPTH, RWKV_WIDTH), 0.02),
        'fox_b_f': 3.0 + nrm((DEPTH, FOX_HEADS), 0.5),
        'fox_out_g': 1.0 + nrm((DEPTH, FOX_WIDTH), 0.02),
        'w_o': nrm((DEPTH, D_MODEL, D_MODEL), BETA * D_MODEL ** -0.5),
        'ln1_g': 1.0 + nrm((DEPTH, D_MODEL), 0.02),
        'ln1_b': nrm((DEPTH, D_MODEL), 0.02),
        'w_up': nrm((DEPTH, D_MODEL, D_FF), D_MODEL ** -0.5),
        'w_down': nrm((DEPTH, D_FF, D_MODEL), BETA * D_FF ** -0.5),
        'ln2_g': 1.0 + nrm((DEPTH, D_MODEL), 0.02),
        'ln2_b': nrm((DEPTH, D_MODEL), 0.02),
    }


def reference(x_prompt, x_sample, cache_fox_k, cache_fox_v, cache_fox_logf, state_rwkv_wkv, state_rwkv_shift,
              w_in, rwkv_mu, rwkv_w0, rwkv_w2, rwkv_a0, rwkv_a2, rwkv_g2, rwkv_k_k, rwkv_k_a, rwkv_r_k,
              rwkv_lnx_g, rwkv_lnx_b, fox_b_f, fox_out_g, w_o, ln1_g, ln1_b, w_up, w_down, ln2_g, ln2_b):
    assert x_sample.shape[1] <= CHUNK
    xp, xs = x_prompt, x_sample
    bp = xp.shape[0]
    pk, pv, pf, pS, psh = [], [], [], [], []
    sk, sv, sf, sS, ssh = [], [], [], [], []
    for l in range(DEPTH):
        lw = (w_in[l], rwkv_mu[l], rwkv_w0[l], rwkv_w2[l], rwkv_a0[l], rwkv_a2[l], rwkv_g2[l],
              rwkv_k_k[l], rwkv_k_a[l], rwkv_r_k[l], rwkv_lnx_g[l], rwkv_lnx_b[l], fox_b_f[l], fox_out_g[l],
              w_o[l], ln1_g[l], ln1_b[l], w_up[l], w_down[l], ln2_g[l], ln2_b[l])
        shift0 = jnp.zeros((bp, 1, RWKV_PROJ), xp.dtype)
        s_zero = jnp.zeros((bp, RWKV_HEADS, HEAD_DIM, HEAD_DIM), jnp.float32)
        xp, k_p, v_p, f_p, S_p, sh_p = hybrid_layer(xp, shift0, s_zero, fox_prompt_attention, *lw)
        fox_s = functools.partial(fox_sample_attention, cache_k=cache_fox_k[l], cache_v=cache_fox_v[l],
                                  cache_logf=cache_fox_logf[l])
        xs, k_s, v_s, f_s, S_s, sh_s = hybrid_layer(xs, state_rwkv_shift[l], state_rwkv_wkv[l], fox_s, *lw)
        pk.append(k_p); pv.append(v_p); pf.append(f_p); pS.append(S_p); psh.append(sh_p)
        sk.append(k_s); sv.append(v_s); sf.append(f_s); sS.append(S_s); ssh.append(sh_s)
    return (xp, xs,
            jnp.stack(pk), jnp.stack(pv), jnp.stack(pf), jnp.stack(pS), jnp.stack(psh),
            jnp.stack(sk), jnp.stack(sv), jnp.stack(sf), jnp.stack(sS), jnp.stack(ssh))
```

```python
import functools
import math

import jax
import jax.numpy as jnp
from jax import lax
from jax.experimental import pallas as pl
from jax.experimental.pallas import tpu as pltpu

F32 = jnp.float32
BF16 = jnp.bfloat16
HI = lax.Precision.HIGHEST

HEAD_DIM = 64
LANES = 128
LN_EPS = 1e-5
GN_EPS = 64e-5
RMS_EPS = 1e-6
ATTN_SCALE = HEAD_DIM ** -0.5
EXP_NEG_HALF = math.exp(-0.5)
NEG_BIG = -1e30
VMEM_LIMIT = 56 * 1024 * 1024

NT = (((1,), (1,)), ((), ()))
TN = (((0,), (0,)), ((), ()))


def _sigmoid(x):
    return 1.0 / (1.0 + jnp.exp(-x))


def _log_sigmoid(x):
    return jnp.minimum(x, 0.0) - jnp.log1p(jnp.exp(-jnp.abs(x)))


def _dot(a, b, precision=None):
    return jnp.dot(a, b, preferred_element_type=F32, precision=precision)


def _dg(a, b, dims, precision=None):
    return lax.dot_general(a, b, dims, preferred_element_type=F32, precision=precision)


def _iota(shape, axis):
    return lax.broadcasted_iota(jnp.int32, shape, axis)


def _head_sum(x):
    r = _iota((LANES, LANES), 0) // HEAD_DIM
    c = _iota((LANES, LANES), 1) // HEAD_DIM
    ones_blk = jnp.where(r == c, 1.0, 0.0).astype(BF16)
    outs = []
    for g in range(x.shape[1] // LANES):
        xs = x[:, g * LANES:(g + 1) * LANES]
        hi = xs.astype(BF16)
        lo = (xs - hi.astype(F32)).astype(BF16)
        outs.append(_dot(hi, ones_blk) + _dot(lo, ones_blk))
    return outs[0] if len(outs) == 1 else jnp.concatenate(outs, axis=1)


def _layer_norm(z, g, b):
    mu = jnp.mean(z, axis=-1, keepdims=True)
    zc = z - mu
    var = jnp.mean(zc * zc, axis=-1, keepdims=True)
    return zc * lax.rsqrt(var + LN_EPS) * g + b


def _const_spec(shape):
    nd = len(shape)
    return pl.BlockSpec(shape, lambda *_: (0,) * nd, pipeline_mode=pl.Buffered(1))


def _params(sem):
    return pltpu.CompilerParams(dimension_semantics=sem, vmem_limit_bytes=VMEM_LIMIT)


RW = 1024
RP_PAD = 3 * RW + 128 + 128 + 256


def _rwkv_proj_kernel(x_ref, w_ref, mu_ref, fp_ref, w0_ref, w2_ref, a0_ref, a2_ref, g2_ref, kk_ref, ka_ref,
                      r_ref, lw_ref, kh_ref, v_ref, a_ref, b_ref, g_ref, sh_ref, carry_ref, *, tm, t_seq):
    i = pl.program_id(0)
    xb = x_ref[...].astype(BF16)
    rows = _iota((tm, 1), 0)

    def seg(c0, n):
        p = _dot(xb, w_ref[:, c0:c0 + n])
        prev = pltpu.roll(p, 1, 0)
        if t_seq >= tm:
            tiles = t_seq // tm
            pos = lax.rem(i, tiles)
            sidx = lax.div(i, tiles)
            row0 = jnp.where(pos == 0, fp_ref[sidx, :, c0:c0 + n], carry_ref[:, c0:c0 + n])
            prev = jnp.where(rows == 0, row0, prev)
            carry_ref[:, c0:c0 + n] = p[tm - 1:tm, :]

            @pl.when(pos == tiles - 1)
            def _():
                sh_ref[sidx, :, c0:c0 + n] = p[tm - 1:tm, :]
        else:
            per_tile = tm // t_seq
            for j in range(per_tile):
                prev = jnp.where(rows == j * t_seq, fp_ref[i * per_tile + j, :, c0:c0 + n], prev)
                sh_ref[i * per_tile + j, :, c0:c0 + n] = p[(j + 1) * t_seq - 1:(j + 1) * t_seq, :]
        return p + (prev - p) * mu_ref[:, c0:c0 + n]

    r_ref[...] = seg(0, RW)
    v_ref[...] = seg(2 * RW, RW)
    xw = seg(3 * RW, 128)
    xa = seg(3 * RW + 128, 128)
    xg = seg(3 * RW + 256, 256)
    wl = w0_ref[...] + _dot(jnp.tanh(xw), w2_ref[...], HI)
    lw_ref[...] = -EXP_NEG_HALF * _sigmoid(wl)
    alr = _sigmoid(a0_ref[...] + _dot(xa, a2_ref[...], HI))
    g_ref[...] = _dot(_sigmoid(xg), g2_ref[...], HI)
    k = seg(RW, RW)
    kk = k * kk_ref[...]
    nrm = jnp.maximum(jnp.sqrt(_head_sum(kk * kk)), 1e-12)
    kkn = kk / nrm
    kh_ref[...] = k * (1.0 + (alr - 1.0) * ka_ref[...])
    a_ref[...] = -kkn
    b_ref[...] = kkn * alr


def _rwkv_proj(x, w, mu, fprev, w0, w2p, a0, a2p, g2p, k_k, k_a, *, tm, t_seq):
    m, d = x.shape
    n_seq = m // t_seq
    row = lambda i: (i, 0)
    out_big = jax.ShapeDtypeStruct((m, RW), F32)
    return pl.pallas_call(
        functools.partial(_rwkv_proj_kernel, tm=tm, t_seq=t_seq),
        grid=(m // tm,),
        in_specs=[pl.BlockSpec((tm, d), row),
                  _const_spec(w.shape), _const_spec(mu.shape), _const_spec(fprev.shape),
                  _const_spec(w0.shape), _const_spec(w2p.shape), _const_spec(a0.shape), _const_spec(a2p.shape),
                  _const_spec(g2p.shape), _const_spec(k_k.shape), _const_spec(k_a.shape)],
        out_specs=[pl.BlockSpec((tm, RW), row)] * 7 + [pl.BlockSpec((n_seq, 1, RP_PAD), lambda i: (0, 0, 0))],
        out_shape=[out_big] * 7 + [jax.ShapeDtypeStruct((n_seq, 1, RP_PAD), F32)],
        scratch_shapes=[pltpu.VMEM((1, RP_PAD), F32)],
        compiler_params=_params(("arbitrary",)),
        name="rwkv_proj",
    )(x, w, mu, fprev, w0, w2p, a0, a2p, g2p, k_k, k_a)


FW = 1024
FOX_HEADS = FW // HEAD_DIM


def _fox_proj_kernel(x_ref, w_ref, wft_ref, bf_ref, bfc_ref, og_ref,
                     qb_ref, k_ref, v_ref, kb_ref, vb_ref, gate_ref, lf_ref, lfr_ref):
    xb = x_ref[...].astype(BF16)
    qb_ref[...] = (_dot(xb, w_ref[:, 0:FW]) * ATTN_SCALE).astype(BF16)
    k = _dot(xb, w_ref[:, FW:2 * FW])
    k_ref[...] = k
    kb_ref[...] = k.astype(BF16)
    v = _dot(xb, w_ref[:, 2 * FW:3 * FW])
    v_ref[...] = v
    vb_ref[...] = v.astype(BF16)
    og = _dot(xb, w_ref[:, 3 * FW:4 * FW])
    gate_ref[...] = _sigmoid(og) * og_ref[...]
    fl = _dot(xb, w_ref[:, 4 * FW:4 * FW + LANES]) + bf_ref[...]
    lf_ref[...] = _log_sigmoid(fl)[:, :FOX_HEADS]
    flt = _dg(wft_ref[...], xb, NT) + bfc_ref[...]
    lfr_ref[...] = _log_sigmoid(flt)


def _fox_proj(x, w, wft, bf_row, bf_col, og_g, *, tm):
    m, d = x.shape
    row = lambda i: (i, 0)
    big = lambda dt: jax.ShapeDtypeStruct((m, FW), dt)
    return pl.pallas_call(
        _fox_proj_kernel,
        grid=(m // tm,),
        in_specs=[pl.BlockSpec((tm, d), row), _const_spec(w.shape), _const_spec(wft.shape),
                  _const_spec(bf_row.shape), _const_spec(bf_col.shape), _const_spec(og_g.shape)],
        out_specs=[pl.BlockSpec((tm, FW), row)] * 6
                  + [pl.BlockSpec((tm, FOX_HEADS), row), pl.BlockSpec((FOX_HEADS, tm), lambda i: (0, i))],
        out_shape=[big(BF16), big(F32), big(F32), big(BF16), big(BF16), big(F32),
                   jax.ShapeDtypeStruct((m, FOX_HEADS), F32), jax.ShapeDtypeStruct((FOX_HEADS, m), F32)],
        compiler_params=_params(("arbitrary",)),
        name="fox_proj",
    )(x, w, wft, bf_row, bf_col, og_g)


def _stack_heads(x, mask0):
    return jnp.concatenate([jnp.where(mask0, x, 0.0), jnp.where(mask0, 0.0, x)], axis=0)


def _rwkv_scan_kernel(r_ref, lw_ref, k_ref, v_ref, a_ref, b_ref, g_ref, s0_ref, lng_ref, lnb_ref, rk_ref,
                      y_ref, sout_ref, state_ref, *, chunk, n_chunks):
    c_idx = pl.program_id(1)
    n_pairs = r_ref.shape[1] // LANES
    C = chunk

    @pl.when(c_idx == 0)
    def _():
        zero = jnp.zeros((HEAD_DIM, HEAD_DIM), F32)
        for p in range(n_pairs):
            top = jnp.concatenate([s0_ref[0, 2 * p], zero], axis=1)
            bot = jnp.concatenate([zero, s0_ref[0, 2 * p + 1]], axis=1)
            state_ref[p] = jnp.concatenate([top, bot], axis=0)

    r = r_ref[...]
    lw = lw_ref[...]
    k = k_ref[...]
    v = v_ref[...]
    tri_incl = jnp.where(_iota((C, C), 0) >= _iota((C, C), 1), 1.0, 0.0)
    lwc = _dot(tri_incl, lw, HI)
    e_in = jnp.exp(lwc)
    e_out = jnp.exp(-lwc)
    at_all = a_ref[...] * jnp.exp(lwc - lw)
    rt_all = r * e_in
    bt_all = b_ref[...] * e_out
    kt_all = k * e_out
    w_tot = e_in[C - 1:C, :]

    fmask0 = _iota((1, LANES), 1) < HEAD_DIM
    tcol = _iota((C, 2 * C), 1)
    trow = _iota((C, 2 * C), 0)
    tmask0 = tcol < C
    tj = jnp.where(tmask0, tcol, tcol - C)
    strict = tj < trow
    incl = tj <= trow
    eye_pair = jnp.where(tj == trow, 1.0, 0.0)
    blk = (_iota((LANES, LANES), 0) // HEAD_DIM) == (_iota((LANES, LANES), 1) // HEAD_DIM)

    def stack_t(x):
        return jnp.concatenate([jnp.where(tmask0, x, 0.0), jnp.where(tmask0, 0.0, x)], axis=0)

    ys = []
    for p in range(n_pairs):
        sl = slice(p * LANES, (p + 1) * LANES)
        at, rt, bt, kt, vp = at_all[:, sl], rt_all[:, sl], bt_all[:, sl], kt_all[:, sl], v[:, sl]
        sd = state_ref[p]
        lhs = jnp.concatenate([at, rt], axis=0)
        rhs = jnp.concatenate([_stack_heads(bt, fmask0), _stack_heads(kt, fmask0)], axis=0)
        gm = _dg(lhs, rhs, NT, HI)
        lab = jnp.where(strict, gm[:C, :2 * C], 0.0)
        lak = jnp.where(strict, gm[:C, 2 * C:], 0.0)
        mrb = jnp.where(incl, gm[C:, :2 * C], 0.0)
        mrk = jnp.where(incl, gm[C:, 2 * C:], 0.0)
        pw = lab
        tinv = eye_pair + lab
        for _ in range(int(math.log2(C)) - 1):
            pw = _dot(pw, stack_t(pw))
            tinv = tinv + _dot(tinv, stack_t(pw))
        vd = _stack_heads(vp, fmask0)
        x = _dg(at, sd, NT, HI) + _dot(lak, vd, HI)
        u = _dot(tinv, _stack_heads(x, fmask0), HI)
        y = _dg(rt, sd, NT, HI) + _dot(mrb, _stack_heads(u, fmask0), HI) + _dot(mrk, vd, HI)
        ds = _dg(jnp.concatenate([u, vp], axis=0), jnp.concatenate([bt, kt], axis=0), TN, HI)
        state_ref[p] = (sd + jnp.where(blk, ds, 0.0)) * w_tot[:, sl]
        ys.append(y)

    y = jnp.concatenate(ys, axis=1)
    inv_n = 1.0 / HEAD_DIM
    mu = _head_sum(y) * inv_n
    yc = y - mu
    var = _head_sum(yc * yc) * inv_n
    yn = yc * lax.rsqrt(var + GN_EPS) * lng_ref[...] + lnb_ref[...]
    bonus = _head_sum(r * k * rk_ref[...]) * v
    y_ref[...] = ((yn + bonus) * g_ref[...]).astype(y_ref.dtype)

    @pl.when(c_idx == n_chunks - 1)
    def _():
        for p in range(n_pairs):
            sd = state_ref[p]
            sout_ref[0, 2 * p] = sd[:HEAD_DIM, :HEAD_DIM]
            sout_ref[0, 2 * p + 1] = sd[HEAD_DIM:, HEAD_DIM:]


def _rwkv_scan(r, lw, kh, v, a, b, g, s0, lng, lnb, rk, *, chunk, t_seq):
    m, w = r.shape
    n_b = m // t_seq
    n_chunks = t_seq // chunk
    n_heads = w // HEAD_DIM
    blk = pl.BlockSpec((chunk, w), lambda bi, ci: (bi * n_chunks + ci, 0))
    st = pl.BlockSpec((1, n_heads, HEAD_DIM, HEAD_DIM), lambda bi, ci: (bi, 0, 0, 0))
    return pl.pallas_call(
        functools.partial(_rwkv_scan_kernel, chunk=chunk, n_chunks=n_chunks),
        grid=(n_b, n_chunks),
        in_specs=[blk] * 7 + [st, _const_spec(lng.shape), _const_spec(lnb.shape), _const_spec(rk.shape)],
        out_specs=[blk, st],
        out_shape=[jax.ShapeDtypeStruct((m, w), BF16),
                   jax.ShapeDtypeStruct((n_b, n_heads, HEAD_DIM, HEAD_DIM), F32)],
        scratch_shapes=[pltpu.VMEM((w // LANES, LANES, LANES), F32)],
        compiler_params=_params(("arbitrary", "arbitrary")),
        name="rwkv_scan",
    )(r, lw, kh, v, a, b, g, s0, lng, lnb, rk)


def _rms_gate(o0, o1, gate, lane):
    first = lane < HEAD_DIM
    o = jnp.where(first, o0, o1)
    sq = o * o
    ms0 = jnp.sum(jnp.where(first, sq, 0.0), axis=-1, keepdims=True)
    ms1 = jnp.sum(jnp.where(first, 0.0, sq), axis=-1, keepdims=True)
    ms = jnp.where(first, ms0, ms1) * (1.0 / HEAD_DIM)
    return o * lax.rsqrt(ms + RMS_EPS) * gate


def _lane_cumsum(x, block):
    tri = jnp.where(_iota((block, block), 0) <= _iota((block, block), 1), 1.0, 0.0)
    carry = jnp.zeros((x.shape[0], 1), F32)
    outs = []
    for j in range(x.shape[1] // block):
        c = _dot(x[:, j * block:(j + 1) * block], tri, HI) + carry
        outs.append(c)
        carry = c[:, block - 1:block]
    return outs[0] if len(outs) == 1 else jnp.concatenate(outs, axis=1)


def _fox_prompt_kernel(q_ref, k_ref, v_ref, lfr_ref, gate_ref, y_ref, crow_ref, *, tq):
    qi = pl.program_id(2)
    t = k_ref.shape[0]

    @pl.when(qi == 0)
    def _():
        c = _lane_cumsum(lfr_ref[0], tq)
        for j in range(t // tq):
            crow_ref[j] = c[:, j * tq:(j + 1) * tq]

    lane = _iota((tq, LANES), 1)
    first = lane < HEAD_DIM
    q = q_ref[...]
    zero = jnp.zeros_like(q)
    qs = jnp.concatenate([jnp.where(first, q, zero), jnp.where(first, zero, q)], axis=0)
    cq = crow_ref[qi]
    eye = jnp.where(_iota((tq, tq), 0) == _iota((tq, tq), 1), 1.0, 0.0)
    ccol = _dg(eye, cq, NT, HI)
    ccols = (ccol[:, 0:1], ccol[:, 1:2])

    def block(j, carry, masked):
        k0 = pl.multiple_of(j * tq, tq)
        kb = k_ref[pl.ds(k0, tq), :]
        vb = v_ref[pl.ds(k0, tq), :]
        cr = crow_ref[j]
        s = _dg(qs, kb, NT)
        out = []
        for h in range(2):
            m_old, l_old, acc = carry[h]
            sh = s[h * tq:(h + 1) * tq] + ccols[h] - cr[h:h + 1]
            if masked:
                sh = jnp.where(_iota((tq, tq), 1) <= _iota((tq, tq), 0), sh, NEG_BIG)
            m_new = jnp.maximum(m_old, jnp.max(sh, axis=-1, keepdims=True))
            alpha = jnp.exp(m_old - m_new)
            p = jnp.exp(sh - m_new)
            l_new = alpha * l_old + jnp.sum(p, axis=-1, keepdims=True)
            acc = alpha * acc + _dot(p.astype(BF16), vb)
            out.append((m_new, l_new, acc))
        return tuple(out)

    init = tuple((jnp.full((tq, 1), NEG_BIG, F32), jnp.zeros((tq, 1), F32), jnp.zeros((tq, LANES), F32))
                 for _ in range(2))
    carry = lax.fori_loop(0, qi, lambda j, c: block(j, c, False), init)
    (_, l0, a0), (_, l1, a1) = block(qi, carry, True)
    y_ref[...] = _rms_gate(a0 / l0, a1 / l1, gate_ref[...], lane).astype(y_ref.dtype)


def _fox_prompt_attn(qb, kb, vb, lfr, gate, *, n_b, t_seq, tq):
    m, w = qb.shape
    n_pairs = w // LANES
    nq = t_seq // tq
    qspec = pl.BlockSpec((tq, LANES), lambda b, hp, qi: (b * nq + qi, hp))
    kspec = pl.BlockSpec((t_seq, LANES), lambda b, hp, qi: (b, hp))
    lfr3 = lfr.reshape(n_pairs, 2, m)
    return pl.pallas_call(
        functools.partial(_fox_prompt_kernel, tq=tq),
        grid=(n_b, n_pairs, nq),
        in_specs=[qspec, kspec, kspec, pl.BlockSpec((1, 2, t_seq), lambda b, hp, qi: (hp, 0, b)), qspec],
        out_specs=qspec,
        out_shape=jax.ShapeDtypeStruct((m, w), BF16),
        scratch_shapes=[pltpu.VMEM((nq, 2, tq), F32)],
        compiler_params=_params(("arbitrary", "arbitrary", "arbitrary")),
        name="fox_prompt_attn",
    )(qb, kb, vb, lfr3, gate)


def _fox_sample_kernel(q_ref, kn_ref, vn_ref, ck_ref, cv_ref, clf_ref, lf_ref, gate_ref, y_ref):
    t = q_ref.shape[0]
    n_heads = lf_ref.shape[1]
    eye_h = jnp.where(_iota((n_heads, n_heads), 0) == _iota((n_heads, n_heads), 1), 1.0, 0.0)
    tril = jnp.where(_iota((t, t), 0) >= _iota((t, t), 1), 1.0, 0.0)
    clf = clf_ref[0]
    c_tot = jnp.sum(clf, axis=0, keepdims=True)
    cn_col = _dot(tril, lf_ref[...], HI)
    cn_row = _dg(eye_h, cn_col, NT, HI)
    cq_col = cn_col + c_tot
    cc_row = _lane_cumsum(_dg(eye_h, clf, NT, HI), 256)
    lane = _iota((t, LANES), 1)
    first = lane < HEAD_DIM
    causal = _iota((t, t), 1) <= _iota((t, t), 0)
    ys = []
    for p in range(q_ref.shape[1] // LANES):
        sl = slice(p * LANES, (p + 1) * LANES)
        q = q_ref[:, sl]
        zero = jnp.zeros_like(q)
        qs = jnp.concatenate([jnp.where(first, q, zero), jnp.where(first, zero, q)], axis=0)
        kc = ck_ref[0, :, sl].astype(BF16)
        vc = cv_ref[0, :, sl].astype(BF16)
        kn = kn_ref[:, sl]
        vn = vn_ref[:, sl]
        s_c = _dg(qs, kc, NT)
        s_n = _dg(qs, kn, NT)
        o = []
        for h in range(2):
            hd = 2 * p + h
            sc = s_c[h * t:(h + 1) * t] + cq_col[:, hd:hd + 1] - cc_row[hd:hd + 1]
            sn = s_n[h * t:(h + 1) * t] + cn_col[:, hd:hd + 1] - cn_row[hd:hd + 1]
            sn = jnp.where(causal, sn, NEG_BIG)
            mx = jnp.maximum(jnp.max(sc, axis=-1, keepdims=True), jnp.max(sn, axis=-1, keepdims=True))
            pc = jnp.exp(sc - mx)
            pn = jnp.exp(sn - mx)
            den = jnp.sum(pc, axis=-1, keepdims=True) + jnp.sum(pn, axis=-1, keepdims=True)
            o.append((_dot(pc.astype(BF16), vc) + _dot(pn.astype(BF16), vn)) / den)
        ys.append(_rms_gate(o[0], o[1], gate_ref[:, sl], lane))
    y_ref[...] = jnp.concatenate(ys, axis=1).astype(y_ref.dtype)


def _fox_sample_attn(qb, kb, vb, cache_k, cache_v, cache_lf, lf, gate, *, t_seq):
    m, w = qb.shape
    n_b = m // t_seq
    past = cache_k.shape[1]
    row = pl.BlockSpec((t_seq, w), lambda b: (b, 0))
    cache = pl.BlockSpec((1, past, w), lambda b: (b, 0, 0))
    return pl.pallas_call(
        _fox_sample_kernel,
        grid=(n_b,),
        in_specs=[row, row, row, cache, cache,
                  pl.BlockSpec((1, past, lf.shape[1]), lambda b: (b, 0, 0)),
                  pl.BlockSpec((t_seq, lf.shape[1]), lambda b: (b, 0)), row],
        out_specs=row,
        out_shape=jax.ShapeDtypeStruct((m, w), BF16),
        compiler_params=_params(("arbitrary",)),
        name="fox_sample_attn",
    )(qb, kb, vb, cache_k, cache_v, cache_lf, lf, gate)


def _out_ln_kernel(yr_ref, yf_ref, x_ref, wo_ref, g_ref, b_ref, h_ref, *, alpha):
    half = yr_ref.shape[1]
    mix = _dot(yr_ref[...], wo_ref[0:half, :]) + _dot(yf_ref[...], wo_ref[half:, :])
    h_ref[...] = _layer_norm(alpha * x_ref[...] + mix, g_ref[...], b_ref[...])


def _out_ln(yr, yf, x, wo, g, b, *, tm, alpha):
    m, d = x.shape
    row = lambda i: (i, 0)
    return pl.pallas_call(
        functools.partial(_out_ln_kernel, alpha=alpha),
        grid=(m // tm,),
        in_specs=[pl.BlockSpec((tm, yr.shape[1]), row), pl.BlockSpec((tm, yf.shape[1]), row),
                  pl.BlockSpec((tm, d), row), _const_spec(wo.shape), _const_spec(g.shape), _const_spec(b.shape)],
        out_specs=pl.BlockSpec((tm, d), row),
        out_shape=jax.ShapeDtypeStruct((m, d), F32),
        compiler_params=_params(("arbitrary",)),
        name="out_ln",
    )(yr, yf, x, wo, g, b)


def _ffn_ln_kernel(h_ref, wu_ref, wd_ref, g_ref, b_ref, o_ref, hb_ref, acc_ref, *, alpha, n_f):
    j = pl.program_id(1)

    @pl.when(j == 0)
    def _():
        hb_ref[...] = h_ref[...].astype(BF16)
        acc_ref[...] = jnp.zeros_like(acc_ref)

    u = jnp.maximum(_dot(hb_ref[...], wu_ref[...]), 0.0)
    acc_ref[...] += _dot((u * u).astype(BF16), wd_ref[...])

    @pl.when(j == n_f - 1)
    def _():
        o_ref[...] = _layer_norm(alpha * h_ref[...] + acc_ref[...], g_ref[...], b_ref[...])


def _ffn_ln(h, wu, wd, g, b, *, tm, tf, alpha):
    m, d = h.shape
    n_f = wu.shape[1] // tf
    return pl.pallas_call(
        functools.partial(_ffn_ln_kernel, alpha=alpha, n_f=n_f),
        grid=(m // tm, n_f),
        in_specs=[pl.BlockSpec((tm, d), lambda i, j: (i, 0)),
                  pl.BlockSpec((d, tf), lambda i, j: (0, j)),
                  pl.BlockSpec((tf, d), lambda i, j: (j, 0)),
                  _const_spec(g.shape), _const_spec(b.shape)],
        out_specs=pl.BlockSpec((tm, d), lambda i, j: (i, 0)),
        out_shape=jax.ShapeDtypeStruct((m, d), F32),
        scratch_shapes=[pltpu.VMEM((tm, d), BF16), pltpu.VMEM((tm, d), F32)],
        compiler_params=_params(("arbitrary", "arbitrary")),
        name="ffn_ln",
    )(h, wu, wd, g, b)


def _pad_cols(x, n):
    return jnp.pad(x, [(0, 0)] * (x.ndim - 1) + [(0, n - x.shape[-1])])


def _rwkv_cols(x, lora):
    o = 3 * RW
    dl, al = lora
    return jnp.concatenate([x[..., :o], _pad_cols(x[..., o:o + dl], 128), _pad_cols(x[..., o + dl:o + dl + al], 128),
                            _pad_cols(x[..., o + dl + al:], 256)], axis=-1)


def _rwkv_cols_inv(x, lora):
    o = 3 * RW
    dl, al, gl = lora
    return jnp.concatenate([x[..., :o], x[..., o:o + dl], x[..., o + 128:o + 128 + al],
                            x[..., o + 256:o + 256 + gl]], axis=-1)


def _stream(x, shift_prev, s0, wts, *, t_seq, tm, tm_ffn, chunk, cache=None):
    n_b, _, d = x.shape
    m = n_b * t_seq
    x2 = x.reshape(m, d)
    lora = wts["lora"]
    fprev = _rwkv_cols(shift_prev, lora[:2])
    r, lw, kh, v, a, b, g, sh = _rwkv_proj(x2, wts["w_r"], wts["mu"], fprev, wts["w0"], wts["w2p"], wts["a0"],
                                           wts["a2p"], wts["g2p"], wts["k_k"], wts["k_a"], tm=tm, t_seq=t_seq)
    y_r, s_new = _rwkv_scan(r, lw, kh, v, a, b, g, s0, wts["lnx_g"], wts["lnx_b"], wts["r_k"],
                            chunk=chunk, t_seq=t_seq)
    qb, k32, v32, kb, vb, gate, lf, lfr = _fox_proj(x2, wts["w_f"], wts["wft"], wts["bf_row"], wts["bf_col"],
                                                    wts["og_g"], tm=tm)
    if cache is None:
        y_f = _fox_prompt_attn(qb, kb, vb, lfr, gate, n_b=n_b, t_seq=t_seq, tq=256)
    else:
        ck, cv, clf = cache
        y_f = _fox_sample_attn(qb, kb, vb, ck.reshape(n_b, ck.shape[1], FW), cv.reshape(n_b, cv.shape[1], FW),
                               clf, lf, gate, t_seq=t_seq)
    h = _out_ln(y_r, y_f, x2, wts["w_o"], wts["ln1_g"], wts["ln1_b"], tm=tm, alpha=wts["alpha"])
    y = _ffn_ln(h, wts["w_up"], wts["w_down"], wts["ln2_g"], wts["ln2_b"], tm=tm_ffn, tf=512, alpha=wts["alpha"])
    heads = FW // HEAD_DIM
    return (y.reshape(n_b, t_seq, d), k32.reshape(n_b, t_seq, heads, HEAD_DIM),
            v32.reshape(n_b, t_seq, heads, HEAD_DIM), lf.reshape(n_b, t_seq, heads), s_new,
            _rwkv_cols_inv(sh, lora))


def kernel(x_prompt, x_sample, cache_fox_k, cache_fox_v, cache_fox_logf, state_rwkv_wkv, state_rwkv_shift,
           w_in, rwkv_mu, rwkv_w0, rwkv_w2, rwkv_a0, rwkv_a2, rwkv_g2, rwkv_k_k, rwkv_k_a, rwkv_r_k,
           rwkv_lnx_g, rwkv_lnx_b, fox_b_f, fox_out_g, w_o, ln1_g, ln1_b, w_up, w_down, ln2_g, ln2_b):
    depth = w_in.shape[0]
    assert depth == 1, "single-layer problem"
    d_model = x_prompt.shape[-1]
    rwkv_proj = rwkv_mu.shape[-1]
    lora = (rwkv_w2.shape[1], rwkv_a2.shape[1], rwkv_g2.shape[1])
    assert rwkv_w0.shape[-1] == RW and fox_out_g.shape[-1] == FW and rwkv_proj == 3 * RW + sum(lora)
    alpha = (2 * depth) ** 0.25
    l = 0
    w = w_in[l]
    fo = rwkv_proj
    row = lambda z: z.reshape(1, -1)
    pad_rows = lambda z, n: jnp.pad(z, ((0, n - z.shape[0]), (0, 0)))
    wts = dict(
        lora=lora, alpha=alpha,
        w_r=_rwkv_cols(w[:, :rwkv_proj], lora[:2]).astype(BF16),
        w_f=jnp.concatenate([w[:, fo:fo + 3 * FW], w[:, fo + 3 * FW + FOX_HEADS:],
                             _pad_cols(w[:, fo + 3 * FW:fo + 3 * FW + FOX_HEADS], LANES)], axis=-1).astype(BF16),
        wft=w[:, fo + 3 * FW:fo + 3 * FW + FOX_HEADS].T.astype(BF16),
        mu=_rwkv_cols(row(rwkv_mu[l]), lora[:2]),
        w0=row(rwkv_w0[l]), w2p=pad_rows(rwkv_w2[l], 128),
        a0=row(rwkv_a0[l]), a2p=pad_rows(rwkv_a2[l], 128), g2p=pad_rows(rwkv_g2[l], 256),
        k_k=row(rwkv_k_k[l]), k_a=row(rwkv_k_a[l]), r_k=row(rwkv_r_k[l]),
        lnx_g=row(rwkv_lnx_g[l]), lnx_b=row(rwkv_lnx_b[l]),
        bf_row=_pad_cols(row(fox_b_f[l]), LANES), bf_col=fox_b_f[l].reshape(-1, 1), og_g=row(fox_out_g[l]),
        w_o=w_o[l].astype(BF16), ln1_g=row(ln1_g[l]), ln1_b=row(ln1_b[l]),
        w_up=w_up[l].astype(BF16), w_down=w_down[l].astype(BF16), ln2_g=row(ln2_g[l]), ln2_b=row(ln2_b[l]),
    )
    n_p, t_p, _ = x_prompt.shape
    n_s, t_s, _ = x_sample.shape
    heads = RW // HEAD_DIM
    shift0 = jnp.zeros((n_p, 1, rwkv_proj), F32)
    s_zero = jnp.zeros((n_p, heads, HEAD_DIM, HEAD_DIM), F32)
    yp, kp, vp, fp, sp, shp = _stream(x_prompt, shift0, s_zero, wts, t_seq=t_p, tm=256, tm_ffn=512, chunk=64)
    ys, ks, vs, fs, ss, shs = _stream(x_sample, state_rwkv_shift[l], state_rwkv_wkv[l], wts, t_seq=t_s,
                                      tm=n_s * t_s, tm_ffn=n_s * t_s, chunk=t_s,
                                      cache=(cache_fox_k[l], cache_fox_v[l], cache_fox_logf[l]))
    return (yp, ys, kp[None], vp[None], fp[None], sp[None], shp[None],
            ks[None], vs[None], fs[None], ss[None], shs[None])
```

```python
import functools
import math

import jax
import jax.numpy as jnp
from jax import lax
from jax.experimental import pallas as pl
from jax.experimental.pallas import tpu as pltpu

F32 = jnp.float32
BF16 = jnp.bfloat16
HI = lax.Precision.HIGHEST

HEAD_DIM = 64
LANES = 128
LN_EPS = 1e-5
GN_EPS = 64e-5
RMS_EPS = 1e-6
ATTN_SCALE = HEAD_DIM ** -0.5
EXP_NEG_HALF = math.exp(-0.5)
LOG2E = math.log2(math.e)
NEG_BIG = -1e30
VMEM_LIMIT = 56 * 1024 * 1024

NT = (((1,), (1,)), ((), ()))
TN = (((0,), (0,)), ((), ()))


def _sigmoid(x):
    return 1.0 / (1.0 + jnp.exp(-x))


def _log_sigmoid(x):
    return jnp.minimum(x, 0.0) - jnp.log1p(jnp.exp(-jnp.abs(x)))


def _dot(a, b, precision=None):
    return jnp.dot(a, b, preferred_element_type=F32, precision=precision)


def _dg(a, b, dims, precision=None):
    return lax.dot_general(a, b, dims, preferred_element_type=F32, precision=precision)


def _iota(shape, axis):
    return lax.broadcasted_iota(jnp.int32, shape, axis)


def _head_sum(x):
    r = _iota((LANES, LANES), 0) // HEAD_DIM
    c = _iota((LANES, LANES), 1) // HEAD_DIM
    ones_blk = jnp.where(r == c, 1.0, 0.0).astype(BF16)
    outs = []
    for g in range(x.shape[1] // LANES):
        xs = x[:, g * LANES:(g + 1) * LANES]
        hi = xs.astype(BF16)
        lo = (xs - hi.astype(F32)).astype(BF16)
        outs.append(_dot(hi, ones_blk) + _dot(lo, ones_blk))
    return outs[0] if len(outs) == 1 else jnp.concatenate(outs, axis=1)


def _layer_norm(z, g, b):
    mu = jnp.mean(z, axis=-1, keepdims=True)
    zc = z - mu
    var = jnp.mean(zc * zc, axis=-1, keepdims=True)
    return zc * lax.rsqrt(var + LN_EPS) * g + b


def _const_spec(shape):
    nd = len(shape)
    return pl.BlockSpec(shape, lambda *_: (0,) * nd, pipeline_mode=pl.Buffered(1))


def _params(sem):
    return pltpu.CompilerParams(dimension_semantics=sem, vmem_limit_bytes=VMEM_LIMIT)


RW = 1024
RP_PAD = 3 * RW + 128 + 128 + 256


def _rwkv_proj_kernel(x_ref, w_ref, mu_ref, fp_ref, w0_ref, w2_ref, a0_ref, a2_ref, g2_ref, kk_ref, ka_ref,
                      r_ref, lw_ref, kh_ref, v_ref, a_ref, b_ref, g_ref, sh_ref, carry_ref, *, tm, t_seq):
    i = pl.program_id(0)
    xb = x_ref[...].astype(BF16)
    rows = _iota((tm, 1), 0)

    def seg(c0, n):
        p = _dot(xb, w_ref[:, c0:c0 + n])
        prev = pltpu.roll(p, 1, 0)
        if t_seq >= tm:
            tiles = t_seq // tm
            pos = lax.rem(i, tiles)
            sidx = lax.div(i, tiles)
            row0 = jnp.where(pos == 0, fp_ref[sidx, :, c0:c0 + n], carry_ref[:, c0:c0 + n])
            prev = jnp.where(rows == 0, row0, prev)
            carry_ref[:, c0:c0 + n] = p[tm - 1:tm, :]

            @pl.when(pos == tiles - 1)
            def _():
                sh_ref[sidx, :, c0:c0 + n] = p[tm - 1:tm, :]
        else:
            per_tile = tm // t_seq
            for j in range(per_tile):
                prev = jnp.where(rows == j * t_seq, fp_ref[i * per_tile + j, :, c0:c0 + n], prev)
                sh_ref[i * per_tile + j, :, c0:c0 + n] = p[(j + 1) * t_seq - 1:(j + 1) * t_seq, :]
        return p + (prev - p) * mu_ref[:, c0:c0 + n]

    r_ref[...] = seg(0, RW)
    v_ref[...] = seg(2 * RW, RW)
    xw = seg(3 * RW, 128)
    xa = seg(3 * RW + 128, 128)
    xg = seg(3 * RW + 256, 256)
    wl = w0_ref[...] + _dot(jnp.tanh(xw), w2_ref[...], HI)
    lw_ref[...] = -EXP_NEG_HALF * _sigmoid(wl)
    alr = _sigmoid(a0_ref[...] + _dot(xa, a2_ref[...], HI))
    g_ref[...] = _dot(_sigmoid(xg), g2_ref[...], HI)
    k = seg(RW, RW)
    kk = k * kk_ref[...]
    nrm = jnp.maximum(jnp.sqrt(_head_sum(kk * kk)), 1e-12)
    kkn = kk / nrm
    kh_ref[...] = k * (1.0 + (alr - 1.0) * ka_ref[...])
    a_ref[...] = -kkn
    b_ref[...] = kkn * alr


def _rwkv_proj(x, w, mu, fprev, w0, w2p, a0, a2p, g2p, k_k, k_a, *, tm, t_seq):
    m, d = x.shape
    n_seq = m // t_seq
    row = lambda i: (i, 0)
    out_big = jax.ShapeDtypeStruct((m, RW), F32)
    return pl.pallas_call(
        functools.partial(_rwkv_proj_kernel, tm=tm, t_seq=t_seq),
        grid=(m // tm,),
        in_specs=[pl.BlockSpec((tm, d), row),
                  _const_spec(w.shape), _const_spec(mu.shape), _const_spec(fprev.shape),
                  _const_spec(w0.shape), _const_spec(w2p.shape), _const_spec(a0.shape), _const_spec(a2p.shape),
                  _const_spec(g2p.shape), _const_spec(k_k.shape), _const_spec(k_a.shape)],
        out_specs=[pl.BlockSpec((tm, RW), row)] * 7 + [pl.BlockSpec((n_seq, 1, RP_PAD), lambda i: (0, 0, 0))],
        out_shape=[out_big] * 7 + [jax.ShapeDtypeStruct((n_seq, 1, RP_PAD), F32)],
        scratch_shapes=[pltpu.VMEM((1, RP_PAD), F32)],
        compiler_params=_params(("arbitrary",)),
        name="rwkv_proj",
    )(x, w, mu, fprev, w0, w2p, a0, a2p, g2p, k_k, k_a)


FW = 1024
FOX_HEADS = FW // HEAD_DIM


def _fox_proj_kernel(x_ref, w_ref, bf_ref, og_ref,
                     qb_ref, k_ref, v_ref, kb_ref, vb_ref, vbt_ref, gate_ref, lf_ref, caug_ref, carry_ref,
                     *, tm, t_seq):
    i = pl.program_id(0)
    xb = x_ref[...].astype(BF16)
    qb_ref[...] = (_dot(xb, w_ref[:, 0:FW]) * (ATTN_SCALE * LOG2E)).astype(BF16)
    k = _dot(xb, w_ref[:, FW:2 * FW])
    k_ref[...] = k
    kb_ref[...] = k.astype(BF16)
    v = _dot(xb, w_ref[:, 2 * FW:3 * FW])
    v_ref[...] = v
    vb_ref[...] = v.astype(BF16)
    vbt_ref[0] = v.T.astype(BF16)
    og = _dot(xb, w_ref[:, 3 * FW:4 * FW])
    gate_ref[...] = _sigmoid(og) * og_ref[...]
    logf = _log_sigmoid(_dot(xb, w_ref[:, 4 * FW:4 * FW + LANES]) + bf_ref[...])
    lf_ref[...] = logf[:, :FOX_HEADS]
    r = _iota((tm, tm), 0)
    c = _iota((tm, tm), 1)
    if t_seq >= tm:
        cs = _dot(jnp.where(r >= c, 1.0, 0.0), logf, HI)
        cs = cs + jnp.where(lax.rem(i, t_seq // tm) == 0, 0.0, carry_ref[...])
        carry_ref[...] = cs[tm - 1:tm, :]
    else:
        same_seq = (r // t_seq) == (c // t_seq)
        cs = _dot(jnp.where(same_seq, jnp.where(r >= c, 1.0, 0.0), 0.0), logf, HI)
    xs = -LOG2E * cs
    hi = xs.astype(BF16)
    rem1 = xs - hi.astype(F32)
    mid = rem1.astype(BF16)
    lo = (rem1 - mid.astype(F32)).astype(BF16)
    lane = _iota((tm, LANES), 1)
    zero = jnp.zeros_like(hi)
    caug_ref[...] = jnp.where(lane < FOX_HEADS, hi,
                              jnp.where(lane < 2 * FOX_HEADS, mid, jnp.where(lane < 3 * FOX_HEADS, lo, zero)))


def _fox_proj(x, w, bf_row, og_g, *, tm, t_seq):
    m, d = x.shape
    row = lambda i: (i, 0)
    big = lambda dt: jax.ShapeDtypeStruct((m, FW), dt)
    return pl.pallas_call(
        functools.partial(_fox_proj_kernel, tm=tm, t_seq=t_seq),
        grid=(m // tm,),
        in_specs=[pl.BlockSpec((tm, d), row), _const_spec(w.shape), _const_spec(bf_row.shape),
                  _const_spec(og_g.shape)],
        out_specs=[pl.BlockSpec((tm, FW), row)] * 5
                  + [pl.BlockSpec((1, FW, tm), lambda i: (i, 0, 0)), pl.BlockSpec((tm, FW), row),
                     pl.BlockSpec((tm, FOX_HEADS), row), pl.BlockSpec((tm, LANES), row)],
        out_shape=[big(BF16), big(F32), big(F32), big(BF16), big(BF16),
                   jax.ShapeDtypeStruct((m // tm, FW, tm), BF16), big(F32),
                   jax.ShapeDtypeStruct((m, FOX_HEADS), F32), jax.ShapeDtypeStruct((m, LANES), BF16)],
        scratch_shapes=[pltpu.VMEM((1, LANES), F32)],
        compiler_params=_params(("arbitrary",)),
        name="fox_proj",
    )(x, w, bf_row, og_g)


def _stack_heads(x, mask0):
    return jnp.concatenate([jnp.where(mask0, x, 0.0), jnp.where(mask0, 0.0, x)], axis=0)


def _rwkv_scan_kernel(r_ref, lw_ref, k_ref, v_ref, a_ref, b_ref, g_ref, s0_ref, lng_ref, lnb_ref, rk_ref,
                      y_ref, sout_ref, state_ref, *, chunk, n_chunks):
    c_idx = pl.program_id(1)
    n_pairs = r_ref.shape[1] // LANES
    C = chunk

    @pl.when(c_idx == 0)
    def _():
        zero = jnp.zeros((HEAD_DIM, HEAD_DIM), F32)
        for p in range(n_pairs):
            top = jnp.concatenate([s0_ref[0, 2 * p], zero], axis=1)
            bot = jnp.concatenate([zero, s0_ref[0, 2 * p + 1]], axis=1)
            state_ref[p] = jnp.concatenate([top, bot], axis=0)

    r = r_ref[...]
    lw = lw_ref[...]
    k = k_ref[...]
    v = v_ref[...]
    tri_incl = jnp.where(_iota((C, C), 0) >= _iota((C, C), 1), 1.0, 0.0)
    lwc = _dot(tri_incl, lw, HI)
    e_in = jnp.exp(lwc)
    e_out = jnp.exp(-lwc)
    at_all = a_ref[...] * jnp.exp(lwc - lw)
    rt_all = r * e_in
    bt_all = b_ref[...] * e_out
    kt_all = k * e_out
    w_tot = e_in[C - 1:C, :]

    fmask0 = _iota((1, LANES), 1) < HEAD_DIM
    tcol = _iota((C, 2 * C), 1)
    trow = _iota((C, 2 * C), 0)
    tmask0 = tcol < C
    tj = jnp.where(tmask0, tcol, tcol - C)
    strict = tj < trow
    incl = tj <= trow
    eye_pair = jnp.where(tj == trow, 1.0, 0.0)
    blk = (_iota((LANES, LANES), 0) // HEAD_DIM) == (_iota((LANES, LANES), 1) // HEAD_DIM)

    def stack_t(x):
        return _stack_heads(x, tmask0)

    pairs = range(n_pairs)
    sls = [slice(p * LANES, (p + 1) * LANES) for p in pairs]
    at_b, rt_b, bt_b, kt_b, v_b = (z.astype(BF16) for z in (at_all, rt_all, bt_all, kt_all, v))
    sd = [state_ref[p] for p in pairs]
    sd_b = [z.astype(BF16) for z in sd]
    gm = [_dg(jnp.concatenate([at_b[:, s], rt_b[:, s]], axis=0),
              jnp.concatenate([_stack_heads(bt_b[:, s], fmask0), _stack_heads(kt_b[:, s], fmask0)], axis=0), NT)
          for s in sls]
    lab = [jnp.where(strict, z[:C, :2 * C], 0.0) for z in gm]
    lak_b = [jnp.where(strict, z[:C, 2 * C:], 0.0).astype(BF16) for z in gm]
    mr_b = [jnp.concatenate([jnp.where(incl, z[C:, :2 * C], 0.0), jnp.where(incl, z[C:, 2 * C:], 0.0)],
                            axis=1).astype(BF16) for z in gm]
    pw_b = [z.astype(BF16) for z in lab]
    tinv = [eye_pair + z for z in lab]
    for _ in range(int(math.log2(C)) - 1):
        pw_b = [_dot(z, stack_t(z)).astype(BF16) for z in pw_b]
        tinv = [t + _dot(t.astype(BF16), stack_t(z)) for t, z in zip(tinv, pw_b)]
    vd_b = [_stack_heads(v_b[:, s], fmask0) for s in sls]
    x = [_dg(at_b[:, s], sd_b[p], NT) + _dot(lak_b[p], vd_b[p]) for p, s in zip(pairs, sls)]
    u_b = [_dot(tinv[p].astype(BF16), _stack_heads(x[p].astype(BF16), fmask0)).astype(BF16) for p in pairs]
    ys = [_dg(rt_b[:, s], sd_b[p], NT)
          + _dot(mr_b[p], jnp.concatenate([_stack_heads(u_b[p], fmask0), vd_b[p]], axis=0))
          for p, s in zip(pairs, sls)]
    for p, s in zip(pairs, sls):
        ds = _dg(jnp.concatenate([u_b[p], v_b[:, s]], axis=0), jnp.concatenate([bt_b[:, s], kt_b[:, s]], axis=0), TN)
        state_ref[p] = (sd[p] + jnp.where(blk, ds, 0.0)) * w_tot[:, s]

    y = jnp.concatenate(ys, axis=1)
    inv_n = 1.0 / HEAD_DIM
    mu = _head_sum(y) * inv_n
    yc = y - mu
    var = _head_sum(yc * yc) * inv_n
    yn = yc * lax.rsqrt(var + GN_EPS) * lng_ref[...] + lnb_ref[...]
    bonus = _head_sum(r * k * rk_ref[...]) * v
    y_ref[...] = ((yn + bonus) * g_ref[...]).astype(y_ref.dtype)

    @pl.when(c_idx == n_chunks - 1)
    def _():
        for p in range(n_pairs):
            sd = state_ref[p]
            sout_ref[0, 2 * p] = sd[:HEAD_DIM, :HEAD_DIM]
            sout_ref[0, 2 * p + 1] = sd[HEAD_DIM:, HEAD_DIM:]


def _rwkv_scan(r, lw, kh, v, a, b, g, s0, lng, lnb, rk, *, chunk, t_seq):
    m, w = r.shape
    n_b = m // t_seq
    n_chunks = t_seq // chunk
    n_heads = w // HEAD_DIM
    blk = pl.BlockSpec((chunk, w), lambda bi, ci: (bi * n_chunks + ci, 0))
    st = pl.BlockSpec((1, n_heads, HEAD_DIM, HEAD_DIM), lambda bi, ci: (bi, 0, 0, 0))
    return pl.pallas_call(
        functools.partial(_rwkv_scan_kernel, chunk=chunk, n_chunks=n_chunks),
        grid=(n_b, n_chunks),
        in_specs=[blk] * 7 + [st, _const_spec(lng.shape), _const_spec(lnb.shape), _const_spec(rk.shape)],
        out_specs=[blk, st],
        out_shape=[jax.ShapeDtypeStruct((m, w), BF16),
                   jax.ShapeDtypeStruct((n_b, n_heads, HEAD_DIM, HEAD_DIM), F32)],
        scratch_shapes=[pltpu.VMEM((w // LANES, LANES, LANES), F32)],
        compiler_params=_params(("arbitrary", "arbitrary")),
        name="rwkv_scan",
    )(r, lw, kh, v, a, b, g, s0, lng, lnb, rk)


def _rms_gate(o0, o1, gate, lane):
    first = lane < HEAD_DIM
    o = jnp.where(first, o0, o1)
    sq = o * o
    ms0 = jnp.sum(jnp.where(first, sq, 0.0), axis=-1, keepdims=True)
    ms1 = jnp.sum(jnp.where(first, 0.0, sq), axis=-1, keepdims=True)
    ms = jnp.where(first, ms0, ms1) * (1.0 / HEAD_DIM)
    return o * lax.rsqrt(ms + RMS_EPS) * gate


def _lane_cumsum(x, block):
    tri = jnp.where(_iota((block, block), 0) <= _iota((block, block), 1), 1.0, 0.0)
    carry = jnp.zeros((x.shape[0], 1), F32)
    outs = []
    for j in range(x.shape[1] // block):
        c = _dot(x[:, j * block:(j + 1) * block], tri, HI) + carry
        outs.append(c)
        carry = c[:, block - 1:block]
    return outs[0] if len(outs) == 1 else jnp.concatenate(outs, axis=1)


def _fox_prompt_kernel(q_ref, k_ref, caug_ref, vt_ref, gate_ref, y_ref, *, tq):
    hp = pl.program_id(1)
    qi = pl.program_id(2)
    lane = _iota((tq, LANES), 1)
    first = lane < HEAD_DIM
    q = q_ref[...]
    zero = jnp.zeros_like(q)
    hslot = jnp.where(lane < 3 * FOX_HEADS, lax.rem(lane, FOX_HEADS), -1)
    qsa = []
    for h in range(2):
        ones = jnp.where(hslot == 2 * hp + h, 1.0, 0.0).astype(BF16)
        qh = jnp.where(first, q, zero) if h == 0 else jnp.where(first, zero, q)
        qsa.append(jnp.concatenate([qh, ones], axis=1))
    qsa = jnp.concatenate(qsa, axis=0)

    def block(j, carry, masked):
        k0 = pl.multiple_of(j * tq, tq)
        kk = jnp.concatenate([k_ref[pl.ds(k0, tq), :], caug_ref[pl.ds(k0, tq), :]], axis=1)
        nv = tq // vt_ref.shape[2]
        vt = jnp.concatenate([vt_ref[nv * j + i] for i in range(nv)], axis=1)
        st = _dg(kk, qsa, NT)
        out = []
        for h in range(2):
            m_old, l_old, acc = carry[h]
            sh = st[:, h * tq:(h + 1) * tq]
            if masked:
                sh = jnp.where(_iota((tq, tq), 0) <= _iota((tq, tq), 1), sh, NEG_BIG)
            m_new = jnp.maximum(m_old, jnp.max(sh, axis=0, keepdims=True))
            alpha = jnp.exp2(m_old - m_new)
            p = jnp.exp2(sh - m_new)
            l_new = alpha * l_old + jnp.sum(p, axis=0, keepdims=True)
            acc = alpha * acc + _dot(vt, p.astype(BF16))
            out.append((m_new, l_new, acc))
        return tuple(out)

    init = tuple((jnp.full((1, tq), NEG_BIG, F32), jnp.zeros((1, tq), F32), jnp.zeros((LANES, tq), F32))
                 for _ in range(2))
    carry = lax.fori_loop(0, qi, lambda j, c: block(j, c, False), init)
    (_, l0, a0), (_, l1, a1) = block(qi, carry, True)
    top = _iota((LANES, tq), 0) < HEAD_DIM
    o = jnp.where(top, a0 * (1.0 / l0), a1 * (1.0 / l1))
    sq = o * o
    ms0 = jnp.sum(jnp.where(top, sq, 0.0), axis=0, keepdims=True)
    ms1 = jnp.sum(jnp.where(top, 0.0, sq), axis=0, keepdims=True)
    yt = o * lax.rsqrt(jnp.where(top, ms0, ms1) * (1.0 / HEAD_DIM) + RMS_EPS)
    y_ref[...] = (yt.T * gate_ref[...]).astype(y_ref.dtype)


def _fox_prompt_attn(qb, kb, caug, vbt, gate, *, n_b, t_seq, tq):
    m, w = qb.shape
    n_pairs = w // LANES
    nq = t_seq // tq
    vblk = vbt.shape[2]
    assert vbt.shape == (m // vblk, w, vblk) and tq % vblk == 0
    qspec = pl.BlockSpec((tq, LANES), lambda b, hp, qi: (b * nq + qi, hp))
    return pl.pallas_call(
        functools.partial(_fox_prompt_kernel, tq=tq),
        grid=(n_b, n_pairs, nq),
        in_specs=[qspec,
                  pl.BlockSpec((t_seq, LANES), lambda b, hp, qi: (b, hp)),
                  pl.BlockSpec((t_seq, LANES), lambda b, hp, qi: (b, 0)),
                  pl.BlockSpec((t_seq // vblk, LANES, vblk), lambda b, hp, qi: (b, hp, 0)),
                  qspec],
        out_specs=qspec,
        out_shape=jax.ShapeDtypeStruct((m, w), BF16),
        compiler_params=_params(("arbitrary", "arbitrary", "arbitrary")),
        name="fox_prompt_attn",
    )(qb, kb, caug, vbt, gate)


def _fox_sample_kernel(q_ref, kn_ref, vn_ref, ck_ref, cv_ref, clf_ref, lf_ref, gate_ref, y_ref):
    t = q_ref.shape[0]
    n_heads = lf_ref.shape[1]
    eye_h = jnp.where(_iota((n_heads, n_heads), 0) == _iota((n_heads, n_heads), 1), 1.0, 0.0)
    tril = jnp.where(_iota((t, t), 0) >= _iota((t, t), 1), 1.0, 0.0)
    clf = clf_ref[0]
    c_tot = jnp.sum(clf, axis=0, keepdims=True)
    cn_col = _dot(tril, lf_ref[...], HI)
    cn_row = _dg(eye_h, cn_col, NT, HI)
    cq_col = cn_col + c_tot
    cc_row = _lane_cumsum(_dg(eye_h, clf, NT, HI), 256)
    lane = _iota((t, LANES), 1)
    first = lane < HEAD_DIM
    causal = _iota((t, t), 1) <= _iota((t, t), 0)
    ys = []
    for p in range(q_ref.shape[1] // LANES):
        sl = slice(p * LANES, (p + 1) * LANES)
        q = q_ref[:, sl]
        zero = jnp.zeros_like(q)
        qs = jnp.concatenate([jnp.where(first, q, zero), jnp.where(first, zero, q)], axis=0)
        kc = ck_ref[0, :, sl].astype(BF16)
        vc = cv_ref[0, :, sl].astype(BF16)
        kn = kn_ref[:, sl]
        vn = vn_ref[:, sl]
        s_c = _dg(qs, kc, NT)
        s_n = _dg(qs, kn, NT)
        o = []
        for h in range(2):
            hd = 2 * p + h
            sc = s_c[h * t:(h + 1) * t] + (cq_col[:, hd:hd + 1] - cc_row[hd:hd + 1]) * LOG2E
            sn = s_n[h * t:(h + 1) * t] + (cn_col[:, hd:hd + 1] - cn_row[hd:hd + 1]) * LOG2E
            sn = jnp.where(causal, sn, NEG_BIG)
            mx = jnp.maximum(jnp.max(sc, axis=-1, keepdims=True), jnp.max(sn, axis=-1, keepdims=True))
            pc = jnp.exp2(sc - mx)
            pn = jnp.exp2(sn - mx)
            den = jnp.sum(pc, axis=-1, keepdims=True) + jnp.sum(pn, axis=-1, keepdims=True)
            o.append((_dot(pc.astype(BF16), vc) + _dot(pn.astype(BF16), vn)) / den)
        ys.append(_rms_gate(o[0], o[1], gate_ref[:, sl], lane))
    y_ref[...] = jnp.concatenate(ys, axis=1).astype(y_ref.dtype)


def _fox_sample_attn(qb, kb, vb, cache_k, cache_v, cache_lf, lf, gate, *, t_seq):
    m, w = qb.shape
    n_b = m // t_seq
    past = cache_k.shape[1]
    row = pl.BlockSpec((t_seq, w), lambda b: (b, 0))
    cache = pl.BlockSpec((1, past, w), lambda b: (b, 0, 0))
    return pl.pallas_call(
        _fox_sample_kernel,
        grid=(n_b,),
        in_specs=[row, row, row, cache, cache,
                  pl.BlockSpec((1, past, lf.shape[1]), lambda b: (b, 0, 0)),
                  pl.BlockSpec((t_seq, lf.shape[1]), lambda b: (b, 0)), row],
        out_specs=row,
        out_shape=jax.ShapeDtypeStruct((m, w), BF16),
        compiler_params=_params(("arbitrary",)),
        name="fox_sample_attn",
    )(qb, kb, vb, cache_k, cache_v, cache_lf, lf, gate)


def _out_ln_kernel(yr_ref, yf_ref, x_ref, wo_ref, g_ref, b_ref, h_ref, *, alpha):
    half = yr_ref.shape[1]
    mix = _dot(yr_ref[...], wo_ref[0:half, :]) + _dot(yf_ref[...], wo_ref[half:, :])
    h_ref[...] = _layer_norm(alpha * x_ref[...] + mix, g_ref[...], b_ref[...])


def _out_ln(yr, yf, x, wo, g, b, *, tm, alpha):
    m, d = x.shape
    row = lambda i: (i, 0)
    return pl.pallas_call(
        functools.partial(_out_ln_kernel, alpha=alpha),
        grid=(m // tm,),
        in_specs=[pl.BlockSpec((tm, yr.shape[1]), row), pl.BlockSpec((tm, yf.shape[1]), row),
                  pl.BlockSpec((tm, d), row), _const_spec(wo.shape), _const_spec(g.shape), _const_spec(b.shape)],
        out_specs=pl.BlockSpec((tm, d), row),
        out_shape=jax.ShapeDtypeStruct((m, d), F32),
        compiler_params=_params(("arbitrary",)),
        name="out_ln",
    )(yr, yf, x, wo, g, b)


def _ffn_ln_kernel(h_ref, wu_ref, wd_ref, g_ref, b_ref, o_ref, hb_ref, acc_ref, *, alpha, n_f):
    j = pl.program_id(1)

    @pl.when(j == 0)
    def _():
        hb_ref[...] = h_ref[...].astype(BF16)
        acc_ref[...] = jnp.zeros_like(acc_ref)

    u = jnp.maximum(_dot(hb_ref[...], wu_ref[...]), 0.0)
    acc_ref[...] += _dot((u * u).astype(BF16), wd_ref[...])

    @pl.when(j == n_f - 1)
    def _():
        o_ref[...] = _layer_norm(alpha * h_ref[...] + acc_ref[...], g_ref[...], b_ref[...])


def _ffn_ln(h, wu, wd, g, b, *, tm, tf, alpha):
    m, d = h.shape
    n_f = wu.shape[1] // tf
    return pl.pallas_call(
        functools.partial(_ffn_ln_kernel, alpha=alpha, n_f=n_f),
        grid=(m // tm, n_f),
        in_specs=[pl.BlockSpec((tm, d), lambda i, j: (i, 0)),
                  pl.BlockSpec((d, tf), lambda i, j: (0, j)),
                  pl.BlockSpec((tf, d), lambda i, j: (j, 0)),
                  _const_spec(g.shape), _const_spec(b.shape)],
        out_specs=pl.BlockSpec((tm, d), lambda i, j: (i, 0)),
        out_shape=jax.ShapeDtypeStruct((m, d), F32),
        scratch_shapes=[pltpu.VMEM((tm, d), BF16), pltpu.VMEM((tm, d), F32)],
        compiler_params=_params(("arbitrary", "arbitrary")),
        name="ffn_ln",
    )(h, wu, wd, g, b)


def _pad_cols(x, n):
    return jnp.pad(x, [(0, 0)] * (x.ndim - 1) + [(0, n - x.shape[-1])])


def _rwkv_cols(x, lora):
    o = 3 * RW
    dl, al = lora
    return jnp.concatenate([x[..., :o], _pad_cols(x[..., o:o + dl], 128), _pad_cols(x[..., o + dl:o + dl + al], 128),
                            _pad_cols(x[..., o + dl + al:], 256)], axis=-1)


def _rwkv_cols_inv(x, lora):
    o = 3 * RW
    dl, al, gl = lora
    return jnp.concatenate([x[..., :o], x[..., o:o + dl], x[..., o + 128:o + 128 + al],
                            x[..., o + 256:o + 256 + gl]], axis=-1)


def _stream(x, shift_prev, s0, wts, *, t_seq, tm, tm_ffn, chunk, cache=None):
    n_b, _, d = x.shape
    m = n_b * t_seq
    x2 = x.reshape(m, d)
    lora = wts["lora"]
    fprev = _rwkv_cols(shift_prev, lora[:2])
    r, lw, kh, v, a, b, g, sh = _rwkv_proj(x2, wts["w_r"], wts["mu"], fprev, wts["w0"], wts["w2p"], wts["a0"],
                                           wts["a2p"], wts["g2p"], wts["k_k"], wts["k_a"], tm=tm, t_seq=t_seq)
    y_r, s_new = _rwkv_scan(r, lw, kh, v, a, b, g, s0, wts["lnx_g"], wts["lnx_b"], wts["r_k"],
                            chunk=chunk, t_seq=t_seq)
    qb, k32, v32, kb, vb, vbt, gate, lf, caug = _fox_proj(x2, wts["w_f"], wts["bf_row"], wts["og_g"],
                                                          tm=tm, t_seq=t_seq)
    if cache is None:
        y_f = _fox_prompt_attn(qb, kb, caug, vbt, gate, n_b=n_b, t_seq=t_seq, tq=512)
    else:
        ck, cv, clf = cache
        y_f = _fox_sample_attn(qb, kb, vb, ck.reshape(n_b, ck.shape[1], FW), cv.reshape(n_b, cv.shape[1], FW),
                               clf, lf, gate, t_seq=t_seq)
    h = _out_ln(y_r, y_f, x2, wts["w_o"], wts["ln1_g"], wts["ln1_b"], tm=tm, alpha=wts["alpha"])
    y = _ffn_ln(h, wts["w_up"], wts["w_down"], wts["ln2_g"], wts["ln2_b"], tm=tm_ffn, tf=512, alpha=wts["alpha"])
    heads = FW // HEAD_DIM
    return (y.reshape(n_b, t_seq, d), k32.reshape(n_b, t_seq, heads, HEAD_DIM),
            v32.reshape(n_b, t_seq, heads, HEAD_DIM), lf.reshape(n_b, t_seq, heads), s_new,
            _rwkv_cols_inv(sh, lora))


def kernel(x_prompt, x_sample, cache_fox_k, cache_fox_v, cache_fox_logf, state_rwkv_wkv, state_rwkv_shift,
           w_in, rwkv_mu, rwkv_w0, rwkv_w2, rwkv_a0, rwkv_a2, rwkv_g2, rwkv_k_k, rwkv_k_a, rwkv_r_k,
           rwkv_lnx_g, rwkv_lnx_b, fox_b_f, fox_out_g, w_o, ln1_g, ln1_b, w_up, w_down, ln2_g, ln2_b):
    depth = w_in.shape[0]
    assert depth == 1, "single-layer problem"
    d_model = x_prompt.shape[-1]
    rwkv_proj = rwkv_mu.shape[-1]
    lora = (rwkv_w2.shape[1], rwkv_a2.shape[1], rwkv_g2.shape[1])
    assert rwkv_w0.shape[-1] == RW and fox_out_g.shape[-1] == FW and rwkv_proj == 3 * RW + sum(lora)
    alpha = (2 * depth) ** 0.25
    l = 0
    w = w_in[l]
    fo = rwkv_proj
    row = lambda z: z.reshape(1, -1)
    pad_rows = lambda z, n: jnp.pad(z, ((0, n - z.shape[0]), (0, 0)))
    wts = dict(
        lora=lora, alpha=alpha,
        w_r=_rwkv_cols(w[:, :rwkv_proj], lora[:2]).astype(BF16),
        w_f=jnp.concatenate([w[:, fo:fo + 3 * FW], w[:, fo + 3 * FW + FOX_HEADS:],
                             _pad_cols(jnp.tile(w[:, fo + 3 * FW:fo + 3 * FW + FOX_HEADS], (1, 3)), LANES)],
                            axis=-1).astype(BF16),
        mu=_rwkv_cols(row(rwkv_mu[l]), lora[:2]),
        w0=row(rwkv_w0[l]), w2p=pad_rows(rwkv_w2[l], 128),
        a0=row(rwkv_a0[l]), a2p=pad_rows(rwkv_a2[l], 128), g2p=pad_rows(rwkv_g2[l], 256),
        k_k=row(rwkv_k_k[l]), k_a=row(rwkv_k_a[l]), r_k=row(rwkv_r_k[l]),
        lnx_g=row(rwkv_lnx_g[l]), lnx_b=row(rwkv_lnx_b[l]),
        bf_row=_pad_cols(jnp.tile(row(fox_b_f[l]), (1, 3)), LANES), og_g=row(fox_out_g[l]),
        w_o=w_o[l].astype(BF16), ln1_g=row(ln1_g[l]), ln1_b=row(ln1_b[l]),
        w_up=w_up[l].astype(BF16), w_down=w_down[l].astype(BF16), ln2_g=row(ln2_g[l]), ln2_b=row(ln2_b[l]),
    )
    n_p, t_p, _ = x_prompt.shape
    n_s, t_s, _ = x_sample.shape
    heads = RW // HEAD_DIM
    shift0 = jnp.zeros((n_p, 1, rwkv_proj), F32)
    s_zero = jnp.zeros((n_p, heads, HEAD_DIM, HEAD_DIM), F32)
    yp, kp, vp, fp, sp, shp = _stream(x_prompt, shift0, s_zero, wts, t_seq=t_p, tm=256, tm_ffn=512, chunk=64)
    ys, ks, vs, fs, ss, shs = _stream(x_sample, state_rwkv_shift[l], state_rwkv_wkv[l], wts, t_seq=t_s,
                                      tm=n_s * t_s, tm_ffn=n_s * t_s, chunk=t_s,
                                      cache=(cache_fox_k[l], cache_fox_v[l], cache_fox_logf[l]))
    return (yp, ys, kp[None], vp[None], fp[None], sp[None], shp[None],
            ks[None], vs[None], fs[None], ss[None], shs[None])
```

```python
import functools
import math

import jax
import jax.numpy as jnp
from jax import lax
from jax.experimental import pallas as pl
from jax.experimental.pallas import tpu as pltpu

F32 = jnp.float32
BF16 = jnp.bfloat16
HI = lax.Precision.HIGHEST

HEAD_DIM = 64
LANES = 128
LN_EPS = 1e-5
GN_EPS = 64e-5
RMS_EPS = 1e-6
ATTN_SCALE = HEAD_DIM ** -0.5
EXP_NEG_HALF = math.exp(-0.5)
LOG2E = math.log2(math.e)
NEG_BIG = -1e30
VMEM_LIMIT = 56 * 1024 * 1024

NT = (((1,), (1,)), ((), ()))
TN = (((0,), (0,)), ((), ()))


def _sigmoid(x):
    return 1.0 / (1.0 + jnp.exp(-x))


def _log_sigmoid(x):
    return jnp.minimum(x, 0.0) - jnp.log1p(jnp.exp(-jnp.abs(x)))


def _dot(a, b, precision=None):
    return jnp.dot(a, b, preferred_element_type=F32, precision=precision)


def _dg(a, b, dims, precision=None):
    return lax.dot_general(a, b, dims, preferred_element_type=F32, precision=precision)


def _split3(x):
    hi = x.astype(BF16)
    rem = x - hi.astype(F32)
    mid = rem.astype(BF16)
    return hi, mid, (rem - mid.astype(F32)).astype(BF16)


def _dot01(m01, x):
    hi, mid, lo = _split3(x)
    return _dot(m01, hi) + _dot(m01, mid) + _dot(m01, lo)


def _iota(shape, axis):
    return lax.broadcasted_iota(jnp.int32, shape, axis)


def _head_sum(x):
    r = _iota((LANES, LANES), 0) // HEAD_DIM
    c = _iota((LANES, LANES), 1) // HEAD_DIM
    ones_blk = jnp.where(r == c, 1.0, 0.0).astype(BF16)
    outs = []
    for g in range(x.shape[1] // LANES):
        xs = x[:, g * LANES:(g + 1) * LANES]
        hi = xs.astype(BF16)
        lo = (xs - hi.astype(F32)).astype(BF16)
        outs.append(_dot(hi, ones_blk) + _dot(lo, ones_blk))
    return outs[0] if len(outs) == 1 else jnp.concatenate(outs, axis=1)


def _layer_norm(z, g, b):
    mu = jnp.mean(z, axis=-1, keepdims=True)
    zc = z - mu
    var = jnp.mean(zc * zc, axis=-1, keepdims=True)
    return zc * lax.rsqrt(var + LN_EPS) * g + b


def _const_spec(shape):
    nd = len(shape)
    return pl.BlockSpec(shape, lambda *_: (0,) * nd, pipeline_mode=pl.Buffered(1))


def _params(sem):
    return pltpu.CompilerParams(dimension_semantics=sem, vmem_limit_bytes=VMEM_LIMIT)


RW = 1024
RP_PAD = 3 * RW + 128 + 128 + 256


def _rwkv_proj_kernel(x_ref, w_ref, mu_ref, fp_ref, w0_ref, w2_ref, a0_ref, a2_ref, g2_ref, kk_ref, ka_ref,
                      r_ref, lw_ref, kh_ref, v_ref, a_ref, b_ref, g_ref, sh_ref, carry_ref, *, tm, t_seq):
    i = pl.program_id(0)
    xb = x_ref[...].astype(BF16)
    rows = _iota((tm, 1), 0)

    def seg(c0, n):
        p = _dot(xb, w_ref[:, c0:c0 + n])
        prev = pltpu.roll(p, 1, 0)
        if t_seq >= tm:
            tiles = t_seq // tm
            pos = lax.rem(i, tiles)
            sidx = lax.div(i, tiles)
            row0 = jnp.where(pos == 0, fp_ref[sidx, :, c0:c0 + n], carry_ref[:, c0:c0 + n])
            prev = jnp.where(rows == 0, row0, prev)
            carry_ref[:, c0:c0 + n] = p[tm - 1:tm, :]

            @pl.when(pos == tiles - 1)
            def _():
                sh_ref[sidx, :, c0:c0 + n] = p[tm - 1:tm, :]
        else:
            per_tile = tm // t_seq
            for j in range(per_tile):
                prev = jnp.where(rows == j * t_seq, fp_ref[i * per_tile + j, :, c0:c0 + n], prev)
                sh_ref[i * per_tile + j, :, c0:c0 + n] = p[(j + 1) * t_seq - 1:(j + 1) * t_seq, :]
        return p + (prev - p) * mu_ref[:, c0:c0 + n]

    r_ref[...] = seg(0, RW)
    v_ref[...] = seg(2 * RW, RW)
    xw = seg(3 * RW, 128)
    xa = seg(3 * RW + 128, 128)
    xg = seg(3 * RW + 256, 256)
    wl = w0_ref[...] + _dot(jnp.tanh(xw).astype(BF16), w2_ref[...])
    lw_ref[...] = -EXP_NEG_HALF * _sigmoid(wl)
    alr = _sigmoid(a0_ref[...] + _dot(xa.astype(BF16), a2_ref[...]))
    g_ref[...] = _dot(_sigmoid(xg).astype(BF16), g2_ref[...])
    k = seg(RW, RW)
    kk = k * kk_ref[...]
    nrm = jnp.maximum(jnp.sqrt(_head_sum(kk * kk)), 1e-12)
    kkn = kk / nrm
    kh_ref[...] = k * (1.0 + (alr - 1.0) * ka_ref[...])
    a_ref[...] = -kkn
    b_ref[...] = kkn * alr


def _rwkv_proj(x, w, mu, fprev, w0, w2p, a0, a2p, g2p, k_k, k_a, *, tm, t_seq):
    m, d = x.shape
    n_seq = m // t_seq
    row = lambda i: (i, 0)
    out_big = jax.ShapeDtypeStruct((m, RW), F32)
    return pl.pallas_call(
        functools.partial(_rwkv_proj_kernel, tm=tm, t_seq=t_seq),
        grid=(m // tm,),
        in_specs=[pl.BlockSpec((tm, d), row),
                  _const_spec(w.shape), _const_spec(mu.shape), _const_spec(fprev.shape),
                  _const_spec(w0.shape), _const_spec(w2p.shape), _const_spec(a0.shape), _const_spec(a2p.shape),
                  _const_spec(g2p.shape), _const_spec(k_k.shape), _const_spec(k_a.shape)],
        out_specs=[pl.BlockSpec((tm, RW), row)] * 7 + [pl.BlockSpec((n_seq, 1, RP_PAD), lambda i: (0, 0, 0))],
        out_shape=[out_big] * 7 + [jax.ShapeDtypeStruct((n_seq, 1, RP_PAD), F32)],
        scratch_shapes=[pltpu.VMEM((1, RP_PAD), F32)],
        compiler_params=_params(("arbitrary",)),
        name="rwkv_proj",
    )(x, w, mu, fprev, w0, w2p, a0, a2p, g2p, k_k, k_a)


FW = 1024
FOX_HEADS = FW // HEAD_DIM


def _fox_proj_kernel(x_ref, w_ref, bf_ref, og_ref,
                     qb_ref, k_ref, v_ref, kb_ref, vb_ref, vbt_ref, gate_ref, lf_ref, caug_ref, carry_ref,
                     *, tm, t_seq):
    i = pl.program_id(0)
    xb = x_ref[...].astype(BF16)
    qb_ref[...] = (_dot(xb, w_ref[:, 0:FW]) * (ATTN_SCALE * LOG2E)).astype(BF16)
    k = _dot(xb, w_ref[:, FW:2 * FW])
    for h in range(FOX_HEADS):
        k_ref[:, h, :] = k[:, h * HEAD_DIM:(h + 1) * HEAD_DIM]
    kb_ref[...] = k.astype(BF16)
    v = _dot(xb, w_ref[:, 2 * FW:3 * FW])
    for h in range(FOX_HEADS):
        v_ref[:, h, :] = v[:, h * HEAD_DIM:(h + 1) * HEAD_DIM]
    vb_ref[...] = v.astype(BF16)
    vbt_ref[0] = v.T.astype(BF16)
    og = _dot(xb, w_ref[:, 3 * FW:4 * FW])
    gate_ref[...] = _sigmoid(og) * og_ref[...]
    logf = _log_sigmoid(_dot(xb, w_ref[:, 4 * FW:4 * FW + LANES]) + bf_ref[...])
    lf_ref[...] = logf[:, :FOX_HEADS]
    r = _iota((tm, tm), 0)
    c = _iota((tm, tm), 1)
    if t_seq >= tm:
        cs = _dot01(jnp.where(r >= c, 1.0, 0.0).astype(BF16), logf)
        cs = cs + jnp.where(lax.rem(i, t_seq // tm) == 0, 0.0, carry_ref[...])
        carry_ref[...] = cs[tm - 1:tm, :]
    else:
        same_seq = (r // t_seq) == (c // t_seq)
        cs = _dot01(jnp.where(same_seq, jnp.where(r >= c, 1.0, 0.0), 0.0).astype(BF16), logf)
    xs = -LOG2E * cs
    hi, mid, lo = _split3(xs)
    lane = _iota((tm, LANES), 1)
    zero = jnp.zeros_like(hi)
    caug_ref[...] = jnp.where(lane < FOX_HEADS, hi,
                              jnp.where(lane < 2 * FOX_HEADS, mid, jnp.where(lane < 3 * FOX_HEADS, lo, zero)))


def _fox_proj(x, w, bf_row, og_g, *, tm, t_seq):
    m, d = x.shape
    row = lambda i: (i, 0)
    big = lambda dt: jax.ShapeDtypeStruct((m, FW), dt)
    heads_spec = pl.BlockSpec((tm, FOX_HEADS, HEAD_DIM), lambda i: (i, 0, 0))
    heads_out = jax.ShapeDtypeStruct((m, FOX_HEADS, HEAD_DIM), F32)
    return pl.pallas_call(
        functools.partial(_fox_proj_kernel, tm=tm, t_seq=t_seq),
        grid=(m // tm,),
        in_specs=[pl.BlockSpec((tm, d), row), _const_spec(w.shape), _const_spec(bf_row.shape),
                  _const_spec(og_g.shape)],
        out_specs=[pl.BlockSpec((tm, FW), row), heads_spec, heads_spec, pl.BlockSpec((tm, FW), row),
                   pl.BlockSpec((tm, FW), row),
                   pl.BlockSpec((1, FW, tm), lambda i: (i, 0, 0)), pl.BlockSpec((tm, FW), row),
                   pl.BlockSpec((tm, FOX_HEADS), row), pl.BlockSpec((tm, LANES), row)],
        out_shape=[big(BF16), heads_out, heads_out, big(BF16), big(BF16),
                   jax.ShapeDtypeStruct((m // tm, FW, tm), BF16), big(F32),
                   jax.ShapeDtypeStruct((m, FOX_HEADS), F32), jax.ShapeDtypeStruct((m, LANES), BF16)],
        scratch_shapes=[pltpu.VMEM((1, LANES), F32)],
        compiler_params=_params(("arbitrary",)),
        name="fox_proj",
    )(x, w, bf_row, og_g)


def _stack_heads(x, mask0):
    return jnp.concatenate([jnp.where(mask0, x, 0.0), jnp.where(mask0, 0.0, x)], axis=0)


def _rwkv_scan_kernel(r_ref, lw_ref, k_ref, v_ref, a_ref, b_ref, g_ref, s0_ref, lng_ref, lnb_ref, rk_ref,
                      y_ref, sout_ref, state_ref, *, chunk, n_chunks):
    c_idx = pl.program_id(1)
    n_pairs = r_ref.shape[1] // LANES
    C = chunk

    @pl.when(c_idx == 0)
    def _():
        zero = jnp.zeros((HEAD_DIM, HEAD_DIM), F32)
        for p in range(n_pairs):
            top = jnp.concatenate([s0_ref[0, 2 * p], zero], axis=1)
            bot = jnp.concatenate([zero, s0_ref[0, 2 * p + 1]], axis=1)
            state_ref[p] = jnp.concatenate([top, bot], axis=0)

    r = r_ref[...]
    lw = lw_ref[...]
    k = k_ref[...]
    v = v_ref[...]
    tri_incl = jnp.where(_iota((C, C), 0) >= _iota((C, C), 1), 1.0, 0.0)
    lwc = _dot01(tri_incl.astype(BF16), lw)
    e_in = jnp.exp(lwc)
    e_out = jnp.exp(-lwc)
    at_all = a_ref[...] * jnp.exp(lwc - lw)
    rt_all = r * e_in
    bt_all = b_ref[...] * e_out
    kt_all = k * e_out
    w_tot = e_in[C - 1:C, :]

    fmask0 = _iota((1, LANES), 1) < HEAD_DIM
    tcol = _iota((C, 2 * C), 1)
    trow = _iota((C, 2 * C), 0)
    tmask0 = tcol < C
    tj = jnp.where(tmask0, tcol, tcol - C)
    strict = tj < trow
    incl = tj <= trow
    eye_pair = jnp.where(tj == trow, 1.0, 0.0)
    blk = (_iota((LANES, LANES), 0) // HEAD_DIM) == (_iota((LANES, LANES), 1) // HEAD_DIM)

    def stack_t(x):
        return _stack_heads(x, tmask0)

    pairs = range(n_pairs)
    sls = [slice(p * LANES, (p + 1) * LANES) for p in pairs]
    at_b, rt_b, bt_b, kt_b, v_b = (z.astype(BF16) for z in (at_all, rt_all, bt_all, kt_all, v))
    sd = [state_ref[p] for p in pairs]
    sd_b = [z.astype(BF16) for z in sd]
    gm = [_dg(jnp.concatenate([at_b[:, s], rt_b[:, s]], axis=0),
              jnp.concatenate([_stack_heads(bt_b[:, s], fmask0), _stack_heads(kt_b[:, s], fmask0)], axis=0), NT)
          for s in sls]
    lab = [jnp.where(strict, z[:C, :2 * C], 0.0) for z in gm]
    lak_b = [jnp.where(strict, z[:C, 2 * C:], 0.0).astype(BF16) for z in gm]
    mr_b = [jnp.concatenate([jnp.where(incl, z[C:, :2 * C], 0.0), jnp.where(incl, z[C:, 2 * C:], 0.0)],
                            axis=1).astype(BF16) for z in gm]
    pw_b = [z.astype(BF16) for z in lab]
    tinv = [eye_pair + z for z in lab]
    for _ in range(int(math.log2(C)) - 1):
        pw_b = [_dot(z, stack_t(z)).astype(BF16) for z in pw_b]
        tinv = [t + _dot(t.astype(BF16), stack_t(z)) for t, z in zip(tinv, pw_b)]
    vd_b = [_stack_heads(v_b[:, s], fmask0) for s in sls]
    x = [_dg(at_b[:, s], sd_b[p], NT) + _dot(lak_b[p], vd_b[p]) for p, s in zip(pairs, sls)]
    u_b = [_dot(tinv[p].astype(BF16), _stack_heads(x[p].astype(BF16), fmask0)).astype(BF16) for p in pairs]
    ys = [_dg(rt_b[:, s], sd_b[p], NT)
          + _dot(mr_b[p], jnp.concatenate([_stack_heads(u_b[p], fmask0), vd_b[p]], axis=0))
          for p, s in zip(pairs, sls)]
    for p, s in zip(pairs, sls):
        ds = _dg(jnp.concatenate([u_b[p], v_b[:, s]], axis=0), jnp.concatenate([bt_b[:, s], kt_b[:, s]], axis=0), TN)
        state_ref[p] = (sd[p] + jnp.where(blk, ds, 0.0)) * w_tot[:, s]

    y = jnp.concatenate(ys, axis=1)
    inv_n = 1.0 / HEAD_DIM
    mu = _head_sum(y) * inv_n
    yc = y - mu
    var = _head_sum(yc * yc) * inv_n
    yn = yc * lax.rsqrt(var + GN_EPS) * lng_ref[...] + lnb_ref[...]
    bonus = _head_sum(r * k * rk_ref[...]) * v
    y_ref[...] = ((yn + bonus) * g_ref[...]).astype(y_ref.dtype)

    @pl.when(c_idx == n_chunks - 1)
    def _():
        for p in range(n_pairs):
            sd = state_ref[p]
            sout_ref[0, 2 * p] = sd[:HEAD_DIM, :HEAD_DIM]
            sout_ref[0, 2 * p + 1] = sd[HEAD_DIM:, HEAD_DIM:]


def _rwkv_scan(r, lw, kh, v, a, b, g, s0, lng, lnb, rk, *, chunk, t_seq):
    m, w = r.shape
    n_b = m // t_seq
    n_chunks = t_seq // chunk
    n_heads = w // HEAD_DIM
    blk = pl.BlockSpec((chunk, w), lambda bi, ci: (bi * n_chunks + ci, 0))
    st = pl.BlockSpec((1, n_heads, HEAD_DIM, HEAD_DIM), lambda bi, ci: (bi, 0, 0, 0))
    return pl.pallas_call(
        functools.partial(_rwkv_scan_kernel, chunk=chunk, n_chunks=n_chunks),
        grid=(n_b, n_chunks),
        in_specs=[blk] * 7 + [st, _const_spec(lng.shape), _const_spec(lnb.shape), _const_spec(rk.shape)],
        out_specs=[blk, st],
        out_shape=[jax.ShapeDtypeStruct((m, w), BF16),
                   jax.ShapeDtypeStruct((n_b, n_heads, HEAD_DIM, HEAD_DIM), F32)],
        scratch_shapes=[pltpu.VMEM((w // LANES, LANES, LANES), F32)],
        compiler_params=_params(("arbitrary", "arbitrary")),
        name="rwkv_scan",
    )(r, lw, kh, v, a, b, g, s0, lng, lnb, rk)


def _rms_gate(o0, o1, gate, lane):
    first = lane < HEAD_DIM
    o = jnp.where(first, o0, o1)
    sq = o * o
    ms0 = jnp.sum(jnp.where(first, sq, 0.0), axis=-1, keepdims=True)
    ms1 = jnp.sum(jnp.where(first, 0.0, sq), axis=-1, keepdims=True)
    ms = jnp.where(first, ms0, ms1) * (1.0 / HEAD_DIM)
    return o * lax.rsqrt(ms + RMS_EPS) * gate


def _lane_cumsum(x, block):
    tri = jnp.where(_iota((block, block), 0) <= _iota((block, block), 1), 1.0, 0.0)
    carry = jnp.zeros((x.shape[0], 1), F32)
    outs = []
    for j in range(x.shape[1] // block):
        c = _dot(x[:, j * block:(j + 1) * block], tri, HI) + carry
        outs.append(c)
        carry = c[:, block - 1:block]
    return outs[0] if len(outs) == 1 else jnp.concatenate(outs, axis=1)


def _fox_prompt_kernel(q_ref, k_ref, caug_ref, vt_ref, gate_ref, y_ref, *, tq):
    hp = pl.program_id(1)
    qi = pl.program_id(2)
    lane = _iota((tq, LANES), 1)
    first = lane < HEAD_DIM
    q = q_ref[...]
    zero = jnp.zeros_like(q)
    hslot = jnp.where(lane < 3 * FOX_HEADS, lax.rem(lane, FOX_HEADS), -1)
    qsa = []
    for h in range(2):
        ones = jnp.where(hslot == 2 * hp + h, 1.0, 0.0).astype(BF16)
        qh = jnp.where(first, q, zero) if h == 0 else jnp.where(first, zero, q)
        qsa.append(jnp.concatenate([qh, ones], axis=1))
    qsa = jnp.concatenate(qsa, axis=0)

    def block(j, carry, masked):
        k0 = pl.multiple_of(j * tq, tq)
        kk = jnp.concatenate([k_ref[pl.ds(k0, tq), :], caug_ref[pl.ds(k0, tq), :]], axis=1)
        nv = tq // vt_ref.shape[2]
        vt = jnp.concatenate([vt_ref[nv * j + i] for i in range(nv)], axis=1)
        st = _dg(kk, qsa, NT)
        out = []
        for h in range(2):
            m_old, l_old, acc = carry[h]
            sh = st[:, h * tq:(h + 1) * tq]
            if masked:
                sh = jnp.where(_iota((tq, tq), 0) <= _iota((tq, tq), 1), sh, NEG_BIG)
            m_new = jnp.maximum(m_old, jnp.max(sh, axis=0, keepdims=True))
            alpha = jnp.exp2(m_old - m_new)
            p = jnp.exp2(sh - m_new)
            l_new = alpha * l_old + jnp.sum(p, axis=0, keepdims=True)
            acc = alpha * acc + _dot(vt, p.astype(BF16))
            out.append((m_new, l_new, acc))
        return tuple(out)

    init = tuple((jnp.full((1, tq), NEG_BIG, F32), jnp.zeros((1, tq), F32), jnp.zeros((LANES, tq), F32))
                 for _ in range(2))
    carry = lax.fori_loop(0, qi, lambda j, c: block(j, c, False), init)
    (_, l0, a0), (_, l1, a1) = block(qi, carry, True)
    top = _iota((LANES, tq), 0) < HEAD_DIM
    o = jnp.where(top, a0 * (1.0 / l0), a1 * (1.0 / l1))
    sq = o * o
    ms0 = jnp.sum(jnp.where(top, sq, 0.0), axis=0, keepdims=True)
    ms1 = jnp.sum(jnp.where(top, 0.0, sq), axis=0, keepdims=True)
    yt = o * lax.rsqrt(jnp.where(top, ms0, ms1) * (1.0 / HEAD_DIM) + RMS_EPS)
    y_ref[...] = (yt.T * gate_ref[...]).astype(y_ref.dtype)


def _fox_prompt_attn(qb, kb, caug, vbt, gate, *, n_b, t_seq, tq):
    m, w = qb.shape
    n_pairs = w // LANES
    nq = t_seq // tq
    vblk = vbt.shape[2]
    assert vbt.shape == (m // vblk, w, vblk) and tq % vblk == 0
    qspec = pl.BlockSpec((tq, LANES), lambda b, hp, qi: (b * nq + qi, hp))
    return pl.pallas_call(
        functools.partial(_fox_prompt_kernel, tq=tq),
        grid=(n_b, n_pairs, nq),
        in_specs=[qspec,
                  pl.BlockSpec((t_seq, LANES), lambda b, hp, qi: (b, hp)),
                  pl.BlockSpec((t_seq, LANES), lambda b, hp, qi: (b, 0)),
                  pl.BlockSpec((t_seq // vblk, LANES, vblk), lambda b, hp, qi: (b, hp, 0)),
                  qspec],
        out_specs=qspec,
        out_shape=jax.ShapeDtypeStruct((m, w), BF16),
        compiler_params=_params(("arbitrary", "arbitrary", "arbitrary")),
        name="fox_prompt_attn",
    )(qb, kb, caug, vbt, gate)


def _fox_sample_kernel(q_ref, kn_ref, vn_ref, ck_ref, cv_ref, clf_ref, lf_ref, gate_ref, y_ref):
    t = q_ref.shape[0]
    n_heads = lf_ref.shape[1]
    eye_h = jnp.where(_iota((n_heads, n_heads), 0) == _iota((n_heads, n_heads), 1), 1.0, 0.0)
    tril = jnp.where(_iota((t, t), 0) >= _iota((t, t), 1), 1.0, 0.0)
    clf = clf_ref[0]
    c_tot = jnp.sum(clf, axis=0, keepdims=True)
    cn_col = _dot(tril, lf_ref[...], HI)
    cn_row = _dg(eye_h, cn_col, NT, HI)
    cq_col = cn_col + c_tot
    cc_row = _lane_cumsum(_dg(eye_h, clf, NT, HI), 256)
    lane = _iota((t, LANES), 1)
    first = lane < HEAD_DIM
    causal = _iota((t, t), 1) <= _iota((t, t), 0)
    ys = []
    for p in range(q_ref.shape[1] // LANES):
        sl = slice(p * LANES, (p + 1) * LANES)
        q = q_ref[:, sl]
        zero = jnp.zeros_like(q)
        qs = jnp.concatenate([jnp.where(first, q, zero), jnp.where(first, zero, q)], axis=0)
        kc = ck_ref[0, :, sl].astype(BF16)
        vc = cv_ref[0, :, sl].astype(BF16)
        kn = kn_ref[:, sl]
        vn = vn_ref[:, sl]
        s_c = _dg(qs, kc, NT)
        s_n = _dg(qs, kn, NT)
        o = []
        for h in range(2):
            hd = 2 * p + h
            sc = s_c[h * t:(h + 1) * t] + (cq_col[:, hd:hd + 1] - cc_row[hd:hd + 1]) * LOG2E
            sn = s_n[h * t:(h + 1) * t] + (cn_col[:, hd:hd + 1] - cn_row[hd:hd + 1]) * LOG2E
            sn = jnp.where(causal, sn, NEG_BIG)
            mx = jnp.maximum(jnp.max(sc, axis=-1, keepdims=True), jnp.max(sn, axis=-1, keepdims=True))
            pc = jnp.exp2(sc - mx)
            pn = jnp.exp2(sn - mx)
            den = jnp.sum(pc, axis=-1, keepdims=True) + jnp.sum(pn, axis=-1, keepdims=True)
            o.append((_dot(pc.astype(BF16), vc) + _dot(pn.astype(BF16), vn)) / den)
        ys.append(_rms_gate(o[0], o[1], gate_ref[:, sl], lane))
    y_ref[...] = jnp.concatenate(ys, axis=1).astype(y_ref.dtype)


def _fox_sample_attn(qb, kb, vb, cache_k, cache_v, cache_lf, lf, gate, *, t_seq):
    m, w = qb.shape
    n_b = m // t_seq
    past = cache_k.shape[1]
    row = pl.BlockSpec((t_seq, w), lambda b: (b, 0))
    cache = pl.BlockSpec((1, past, w), lambda b: (b, 0, 0))
    return pl.pallas_call(
        _fox_sample_kernel,
        grid=(n_b,),
        in_specs=[row, row, row, cache, cache,
                  pl.BlockSpec((1, past, lf.shape[1]), lambda b: (b, 0, 0)),
                  pl.BlockSpec((t_seq, lf.shape[1]), lambda b: (b, 0)), row],
        out_specs=row,
        out_shape=jax.ShapeDtypeStruct((m, w), BF16),
        compiler_params=_params(("arbitrary",)),
        name="fox_sample_attn",
    )(qb, kb, vb, cache_k, cache_v, cache_lf, lf, gate)


def _out_ln_kernel(yr_ref, yf_ref, x_ref, wo_ref, g_ref, b_ref, h_ref, wob_ref, *, alpha):
    @pl.when(pl.program_id(0) == 0)
    def _():
        wob_ref[...] = wo_ref[...].astype(BF16)

    half = yr_ref.shape[1]
    mix = _dot(yr_ref[...], wob_ref[0:half, :]) + _dot(yf_ref[...], wob_ref[half:, :])
    h_ref[...] = _layer_norm(alpha * x_ref[...] + mix, g_ref[...], b_ref[...])


def _out_ln(yr, yf, x, wo, g, b, *, tm, alpha):
    m, d = x.shape
    row = lambda i: (i, 0)
    return pl.pallas_call(
        functools.partial(_out_ln_kernel, alpha=alpha),
        grid=(m // tm,),
        in_specs=[pl.BlockSpec((tm, yr.shape[1]), row), pl.BlockSpec((tm, yf.shape[1]), row),
                  pl.BlockSpec((tm, d), row), _const_spec(wo.shape), _const_spec(g.shape), _const_spec(b.shape)],
        out_specs=pl.BlockSpec((tm, d), row),
        out_shape=jax.ShapeDtypeStruct((m, d), F32),
        scratch_shapes=[pltpu.VMEM(wo.shape, BF16)],
        compiler_params=_params(("arbitrary",)),
        name="out_ln",
    )(yr, yf, x, wo, g, b)


def _ffn_ln_kernel(h_ref, wu_ref, wd_ref, g_ref, b_ref, o_ref, hb_ref, *, alpha, n_f):
    j = pl.program_id(1)

    @pl.when(j == 0)
    def _():
        h = h_ref[...]
        hb_ref[...] = h.astype(BF16)
        o_ref[...] = alpha * h

    u = jnp.maximum(_dot(hb_ref[...], wu_ref[...]), 0.0)
    o_ref[...] += _dot((u * u).astype(BF16), wd_ref[...])

    @pl.when(j == n_f - 1)
    def _():
        o_ref[...] = _layer_norm(o_ref[...], g_ref[...], b_ref[...])


def _ffn_ln(h, wu, wd, g, b, *, tm, tf, alpha):
    m, d = h.shape
    n_f = wu.shape[1] // tf
    return pl.pallas_call(
        functools.partial(_ffn_ln_kernel, alpha=alpha, n_f=n_f),
        grid=(m // tm, n_f),
        in_specs=[pl.BlockSpec((tm, d), lambda i, j: (i, 0), pipeline_mode=pl.Buffered(1)),
                  pl.BlockSpec((d, tf), lambda i, j: (0, j)),
                  pl.BlockSpec((tf, d), lambda i, j: (j, 0)),
                  _const_spec(g.shape), _const_spec(b.shape)],
        out_specs=pl.BlockSpec((tm, d), lambda i, j: (i, 0)),
        out_shape=jax.ShapeDtypeStruct((m, d), F32),
        scratch_shapes=[pltpu.VMEM((tm, d), BF16)],
        compiler_params=_params(("arbitrary", "arbitrary")),
        name="ffn_ln",
    )(h, wu, wd, g, b)


def _pad_cols(x, n):
    return jnp.pad(x, [(0, 0)] * (x.ndim - 1) + [(0, n - x.shape[-1])])


def _rwkv_cols(x, lora):
    o = 3 * RW
    dl, al = lora
    return jnp.concatenate([x[..., :o], _pad_cols(x[..., o:o + dl], 128), _pad_cols(x[..., o + dl:o + dl + al], 128),
                            _pad_cols(x[..., o + dl + al:], 256)], axis=-1)


def _rwkv_cols_inv(x, lora):
    o = 3 * RW
    dl, al, gl = lora
    return jnp.concatenate([x[..., :o], x[..., o:o + dl], x[..., o + 128:o + 128 + al],
                            x[..., o + 256:o + 256 + gl]], axis=-1)


def _stream(x, shift_prev, s0, wts, *, t_seq, tm, tm_ffn, chunk, cache=None):
    n_b, _, d = x.shape
    m = n_b * t_seq
    x2 = x.reshape(m, d)
    lora = wts["lora"]
    fprev = _rwkv_cols(shift_prev, lora[:2])
    r, lw, kh, v, a, b, g, sh = _rwkv_proj(x2, wts["w_r"], wts["mu"], fprev, wts["w0"], wts["w2p"], wts["a0"],
                                           wts["a2p"], wts["g2p"], wts["k_k"], wts["k_a"], tm=tm, t_seq=t_seq)
    y_r, s_new = _rwkv_scan(r, lw, kh, v, a, b, g, s0, wts["lnx_g"], wts["lnx_b"], wts["r_k"],
                            chunk=chunk, t_seq=t_seq)
    qb, k32, v32, kb, vb, vbt, gate, lf, caug = _fox_proj(x2, wts["w_f"], wts["bf_row"], wts["og_g"],
                                                          tm=tm, t_seq=t_seq)
    if cache is None:
        y_f = _fox_prompt_attn(qb, kb, caug, vbt, gate, n_b=n_b, t_seq=t_seq, tq=512)
    else:
        ck, cv, clf = cache
        y_f = _fox_sample_attn(qb, kb, vb, ck.reshape(n_b, ck.shape[1], FW), cv.reshape(n_b, cv.shape[1], FW),
                               clf, lf, gate, t_seq=t_seq)
    h = _out_ln(y_r, y_f, x2, wts["w_o"], wts["ln1_g"], wts["ln1_b"], tm=tm, alpha=wts["alpha"])
    y = _ffn_ln(h, wts["w_up"], wts["w_down"], wts["ln2_g"], wts["ln2_b"], tm=tm_ffn, tf=1024, alpha=wts["alpha"])
    heads = FW // HEAD_DIM
    return (y.reshape(n_b, t_seq, d), k32.reshape(n_b, t_seq, heads, HEAD_DIM),
            v32.reshape(n_b, t_seq, heads, HEAD_DIM), lf.reshape(n_b, t_seq, heads), s_new,
            _rwkv_cols_inv(sh, lora))


def kernel(x_prompt, x_sample, cache_fox_k, cache_fox_v, cache_fox_logf, state_rwkv_wkv, state_rwkv_shift,
           w_in, rwkv_mu, rwkv_w0, rwkv_w2, rwkv_a0, rwkv_a2, rwkv_g2, rwkv_k_k, rwkv_k_a, rwkv_r_k,
           rwkv_lnx_g, rwkv_lnx_b, fox_b_f, fox_out_g, w_o, ln1_g, ln1_b, w_up, w_down, ln2_g, ln2_b):
    depth = w_in.shape[0]
    assert depth == 1, "single-layer problem"
    d_model = x_prompt.shape[-1]
    rwkv_proj = rwkv_mu.shape[-1]
    lora = (rwkv_w2.shape[1], rwkv_a2.shape[1], rwkv_g2.shape[1])
    assert rwkv_w0.shape[-1] == RW and fox_out_g.shape[-1] == FW and rwkv_proj == 3 * RW + sum(lora)
    alpha = (2 * depth) ** 0.25
    l = 0
    w = w_in[l]
    fo = rwkv_proj
    row = lambda z: z.reshape(1, -1)
    pad_rows = lambda z, n: jnp.pad(z, ((0, n - z.shape[0]), (0, 0)))
    wts = dict(
        lora=lora, alpha=alpha,
        w_r=_rwkv_cols(w[:, :rwkv_proj], lora[:2]).astype(BF16),
        w_f=jnp.concatenate([w[:, fo:fo + 3 * FW], w[:, fo + 3 * FW + FOX_HEADS:],
                             _pad_cols(jnp.tile(w[:, fo + 3 * FW:fo + 3 * FW + FOX_HEADS], (1, 3)), LANES)],
                            axis=-1).astype(BF16),
        mu=_rwkv_cols(row(rwkv_mu[l]), lora[:2]),
        w0=row(rwkv_w0[l]), w2p=pad_rows(rwkv_w2[l], 128).astype(BF16),
        a0=row(rwkv_a0[l]), a2p=pad_rows(rwkv_a2[l], 128).astype(BF16),
        g2p=pad_rows(rwkv_g2[l], 256).astype(BF16),
        k_k=row(rwkv_k_k[l]), k_a=row(rwkv_k_a[l]), r_k=row(rwkv_r_k[l]),
        lnx_g=row(rwkv_lnx_g[l]), lnx_b=row(rwkv_lnx_b[l]),
        bf_row=_pad_cols(jnp.tile(row(fox_b_f[l]), (1, 3)), LANES), og_g=row(fox_out_g[l]),
        w_o=w_o[l], ln1_g=row(ln1_g[l]), ln1_b=row(ln1_b[l]),
        w_up=w_up[l].astype(BF16), w_down=w_down[l].astype(BF16), ln2_g=row(ln2_g[l]), ln2_b=row(ln2_b[l]),
    )
    n_p, t_p, _ = x_prompt.shape
    n_s, t_s, _ = x_sample.shape
    heads = RW // HEAD_DIM
    shift0 = jnp.zeros((n_p, 1, rwkv_proj), F32)
    s_zero = jnp.zeros((n_p, heads, HEAD_DIM, HEAD_DIM), F32)
    yp, kp, vp, fp, sp, shp = _stream(x_prompt, shift0, s_zero, wts, t_seq=t_p, tm=256, tm_ffn=1024, chunk=64)
    ys, ks, vs, fs, ss, shs = _stream(x_sample, state_rwkv_shift[l], state_rwkv_wkv[l], wts, t_seq=t_s,
                                      tm=n_s * t_s, tm_ffn=n_s * t_s, chunk=t_s,
                                      cache=(cache_fox_k[l], cache_fox_v[l], cache_fox_logf[l]))
    return (yp, ys, kp[None], vp[None], fp[None], sp[None], shp[None],
            ks[None], vs[None], fs[None], ss[None], shs[None])
```

```python
import functools
import math

import jax
import jax.numpy as jnp
from jax import lax
from jax.experimental import pallas as pl
from jax.experimental.pallas import tpu as pltpu

F32 = jnp.float32
BF16 = jnp.bfloat16
HI = lax.Precision.HIGHEST

HEAD_DIM = 64
LANES = 128
LN_EPS = 1e-5
GN_EPS = 64e-5
RMS_EPS = 1e-6
ATTN_SCALE = HEAD_DIM ** -0.5
EXP_NEG_HALF = math.exp(-0.5)
LOG2E = math.log2(math.e)
NEG_BIG = -1e30
VMEM_LIMIT = 56 * 1024 * 1024

NT = (((1,), (1,)), ((), ()))
TN = (((0,), (0,)), ((), ()))


def _sigmoid(x):
    return 1.0 / (1.0 + jnp.exp(-x))


def _log_sigmoid(x):
    return jnp.minimum(x, 0.0) - jnp.log1p(jnp.exp(-jnp.abs(x)))


def _dot(a, b, precision=None):
    return jnp.dot(a, b, preferred_element_type=F32, precision=precision)


def _dg(a, b, dims, precision=None):
    return lax.dot_general(a, b, dims, preferred_element_type=F32, precision=precision)


def _split3(x):
    hi = x.astype(BF16)
    rem = x - hi.astype(F32)
    mid = rem.astype(BF16)
    return hi, mid, (rem - mid.astype(F32)).astype(BF16)


def _dot01(m01, x):
    hi, mid, lo = _split3(x)
    return _dot(m01, hi) + _dot(m01, mid) + _dot(m01, lo)


def _iota(shape, axis):
    return lax.broadcasted_iota(jnp.int32, shape, axis)


def _head_sum(x):
    r = _iota((LANES, LANES), 0) // HEAD_DIM
    c = _iota((LANES, LANES), 1) // HEAD_DIM
    ones_blk = jnp.where(r == c, 1.0, 0.0).astype(BF16)
    outs = []
    for g in range(x.shape[1] // LANES):
        xs = x[:, g * LANES:(g + 1) * LANES]
        hi = xs.astype(BF16)
        lo = (xs - hi.astype(F32)).astype(BF16)
        outs.append(_dot(hi, ones_blk) + _dot(lo, ones_blk))
    return outs[0] if len(outs) == 1 else jnp.concatenate(outs, axis=1)


def _layer_norm(z, g, b):
    mu = jnp.mean(z, axis=-1, keepdims=True)
    zc = z - mu
    var = jnp.mean(zc * zc, axis=-1, keepdims=True)
    return zc * lax.rsqrt(var + LN_EPS) * g + b


def _const_spec(shape):
    nd = len(shape)
    return pl.BlockSpec(shape, lambda *_: (0,) * nd, pipeline_mode=pl.Buffered(1))


def _params(sem):
    return pltpu.CompilerParams(dimension_semantics=sem, vmem_limit_bytes=VMEM_LIMIT)


RW = 1024
RP_PAD = 3 * RW + 128 + 128 + 256


def _rwkv_proj_kernel(x_ref, w_ref, mu_ref, fp_ref, w0_ref, w2_ref, a0_ref, a2_ref, g2_ref, kk_ref, ka_ref,
                      r_ref, lw_ref, kh_ref, v_ref, a_ref, b_ref, g_ref, sh_ref, carry_ref, *, tm, t_seq):
    i = pl.program_id(0)
    xb = x_ref[...].astype(BF16)
    rows = _iota((tm, 1), 0)

    def seg(c0, n):
        p = _dot(xb, w_ref[:, c0:c0 + n])
        prev = pltpu.roll(p, 1, 0)
        if t_seq >= tm:
            tiles = t_seq // tm
            pos = lax.rem(i, tiles)
            sidx = lax.div(i, tiles)
            row0 = jnp.where(pos == 0, fp_ref[sidx, :, c0:c0 + n], carry_ref[:, c0:c0 + n])
            prev = jnp.where(rows == 0, row0, prev)
            carry_ref[:, c0:c0 + n] = p[tm - 1:tm, :]

            @pl.when(pos == tiles - 1)
            def _():
                sh_ref[sidx, :, c0:c0 + n] = p[tm - 1:tm, :]
        else:
            per_tile = tm // t_seq
            for j in range(per_tile):
                prev = jnp.where(rows == j * t_seq, fp_ref[i * per_tile + j, :, c0:c0 + n], prev)
                sh_ref[i * per_tile + j, :, c0:c0 + n] = p[(j + 1) * t_seq - 1:(j + 1) * t_seq, :]
        return p + (prev - p) * mu_ref[:, c0:c0 + n]

    xw = seg(3 * RW, 128)
    xa = seg(3 * RW + 128, 128)
    xg = seg(3 * RW + 256, 256)
    k = seg(RW, RW)
    wl = w0_ref[...] + _dot(jnp.tanh(xw).astype(BF16), w2_ref[...])
    lw_ref[...] = -EXP_NEG_HALF * _sigmoid(wl)
    alr = _sigmoid(a0_ref[...] + _dot(xa.astype(BF16), a2_ref[...]))
    g_ref[...] = _dot(_sigmoid(xg).astype(BF16), g2_ref[...])
    kk = k * kk_ref[...]
    nrm = jnp.maximum(jnp.sqrt(_head_sum(kk * kk)), 1e-12)
    kkn = kk / nrm
    kh_ref[...] = k * (1.0 + (alr - 1.0) * ka_ref[...])
    a_ref[...] = -kkn
    b_ref[...] = kkn * alr
    r_ref[...] = seg(0, RW)
    v_ref[...] = seg(2 * RW, RW)


def _rwkv_proj(x, w, mu, fprev, w0, w2p, a0, a2p, g2p, k_k, k_a, *, tm, t_seq):
    m, d = x.shape
    n_seq = m // t_seq
    row = lambda i: (i, 0)
    out_big = jax.ShapeDtypeStruct((m, RW), F32)
    return pl.pallas_call(
        functools.partial(_rwkv_proj_kernel, tm=tm, t_seq=t_seq),
        grid=(m // tm,),
        in_specs=[pl.BlockSpec((tm, d), row),
                  _const_spec(w.shape), _const_spec(mu.shape), _const_spec(fprev.shape),
                  _const_spec(w0.shape), _const_spec(w2p.shape), _const_spec(a0.shape), _const_spec(a2p.shape),
                  _const_spec(g2p.shape), _const_spec(k_k.shape), _const_spec(k_a.shape)],
        out_specs=[pl.BlockSpec((tm, RW), row)] * 7 + [pl.BlockSpec((n_seq, 1, RP_PAD), lambda i: (0, 0, 0))],
        out_shape=[out_big] * 7 + [jax.ShapeDtypeStruct((n_seq, 1, RP_PAD), F32)],
        scratch_shapes=[pltpu.VMEM((1, RP_PAD), F32)],
        compiler_params=_params(("arbitrary",)),
        name="rwkv_proj",
    )(x, w, mu, fprev, w0, w2p, a0, a2p, g2p, k_k, k_a)


FW = 1024
FOX_HEADS = FW // HEAD_DIM


def _fox_proj_kernel(x_ref, w_ref, bf_ref, og_ref,
                     qb_ref, k_ref, v_ref, kb_ref, vb_ref, vbt_ref, gate_ref, lf_ref, caug_ref, carry_ref,
                     *, tm, t_seq):
    i = pl.program_id(0)
    xb = x_ref[...].astype(BF16)
    qb_ref[...] = (_dot(xb, w_ref[:, 0:FW]) * (ATTN_SCALE * LOG2E)).astype(BF16)
    k = _dot(xb, w_ref[:, FW:2 * FW])
    for h in range(FOX_HEADS):
        k_ref[:, h, :] = k[:, h * HEAD_DIM:(h + 1) * HEAD_DIM]
    kb_ref[...] = k.astype(BF16)
    v = _dot(xb, w_ref[:, 2 * FW:3 * FW])
    for h in range(FOX_HEADS):
        v_ref[:, h, :] = v[:, h * HEAD_DIM:(h + 1) * HEAD_DIM]
    vb_ref[...] = v.astype(BF16)
    vbt_ref[0] = v.T.astype(BF16)
    og = _dot(xb, w_ref[:, 3 * FW:4 * FW])
    gate_ref[...] = _sigmoid(og) * og_ref[...]
    logf = _log_sigmoid(_dot(xb, w_ref[:, 4 * FW:4 * FW + LANES]) + bf_ref[...])
    lf_ref[...] = logf[:, :FOX_HEADS]
    r = _iota((tm, tm), 0)
    c = _iota((tm, tm), 1)
    if t_seq >= tm:
        cs = _dot01(jnp.where(r >= c, 1.0, 0.0).astype(BF16), logf)
        cs = cs + jnp.where(lax.rem(i, t_seq // tm) == 0, 0.0, carry_ref[...])
        carry_ref[...] = cs[tm - 1:tm, :]
    else:
        same_seq = (r // t_seq) == (c // t_seq)
        cs = _dot01(jnp.where(same_seq, jnp.where(r >= c, 1.0, 0.0), 0.0).astype(BF16), logf)
    xs = -LOG2E * cs
    hi, mid, lo = _split3(xs)
    lane = _iota((tm, LANES), 1)
    zero = jnp.zeros_like(hi)
    caug_ref[...] = jnp.where(lane < FOX_HEADS, hi,
                              jnp.where(lane < 2 * FOX_HEADS, mid, jnp.where(lane < 3 * FOX_HEADS, lo, zero)))


def _fox_proj(x, w, bf_row, og_g, *, tm, t_seq):
    m, d = x.shape
    row = lambda i: (i, 0)
    big = lambda dt: jax.ShapeDtypeStruct((m, FW), dt)
    heads_spec = pl.BlockSpec((tm, FOX_HEADS, HEAD_DIM), lambda i: (i, 0, 0))
    heads_out = jax.ShapeDtypeStruct((m, FOX_HEADS, HEAD_DIM), F32)
    return pl.pallas_call(
        functools.partial(_fox_proj_kernel, tm=tm, t_seq=t_seq),
        grid=(m // tm,),
        in_specs=[pl.BlockSpec((tm, d), row), _const_spec(w.shape), _const_spec(bf_row.shape),
                  _const_spec(og_g.shape)],
        out_specs=[pl.BlockSpec((tm, FW), row), heads_spec, heads_spec, pl.BlockSpec((tm, FW), row),
                   pl.BlockSpec((tm, FW), row),
                   pl.BlockSpec((1, FW, tm), lambda i: (i, 0, 0)), pl.BlockSpec((tm, FW), row),
                   pl.BlockSpec((tm, FOX_HEADS), row), pl.BlockSpec((tm, LANES), row)],
        out_shape=[big(BF16), heads_out, heads_out, big(BF16), big(BF16),
                   jax.ShapeDtypeStruct((m // tm, FW, tm), BF16), big(F32),
                   jax.ShapeDtypeStruct((m, FOX_HEADS), F32), jax.ShapeDtypeStruct((m, LANES), BF16)],
        scratch_shapes=[pltpu.VMEM((1, LANES), F32)],
        compiler_params=_params(("arbitrary",)),
        name="fox_proj",
    )(x, w, bf_row, og_g)


def _stack_heads(x, mask0):
    return jnp.concatenate([jnp.where(mask0, x, 0.0), jnp.where(mask0, 0.0, x)], axis=0)


def _rwkv_scan_kernel(r_ref, lw_ref, k_ref, v_ref, a_ref, b_ref, g_ref, s0_ref, lng_ref, lnb_ref, rk_ref,
                      y_ref, sout_ref, state_ref, *, chunk, n_chunks):
    c_idx = pl.program_id(1)
    n_pairs = r_ref.shape[1] // LANES
    C = chunk

    @pl.when(c_idx == 0)
    def _():
        zero = jnp.zeros((HEAD_DIM, HEAD_DIM), F32)
        for p in range(n_pairs):
            top = jnp.concatenate([s0_ref[0, 2 * p], zero], axis=1)
            bot = jnp.concatenate([zero, s0_ref[0, 2 * p + 1]], axis=1)
            state_ref[p] = jnp.concatenate([top, bot], axis=0)

    r = r_ref[...]
    lw = lw_ref[...]
    k = k_ref[...]
    v = v_ref[...]
    tri_incl = jnp.where(_iota((C, C), 0) >= _iota((C, C), 1), 1.0, 0.0)
    lwc = _dot01(tri_incl.astype(BF16), lw)
    e_in = jnp.exp(lwc)
    e_out = jnp.exp(-lwc)
    at_all = a_ref[...] * jnp.exp(lwc - lw)
    rt_all = r * e_in
    bt_all = b_ref[...] * e_out
    kt_all = k * e_out
    w_tot = e_in[C - 1:C, :]

    fmask0 = _iota((1, LANES), 1) < HEAD_DIM
    tcol = _iota((C, 2 * C), 1)
    trow = _iota((C, 2 * C), 0)
    tmask0 = tcol < C
    tj = jnp.where(tmask0, tcol, tcol - C)
    strict = tj < trow
    incl = tj <= trow
    eye_pair = jnp.where(tj == trow, 1.0, 0.0)
    blk = (_iota((LANES, LANES), 0) // HEAD_DIM) == (_iota((LANES, LANES), 1) // HEAD_DIM)

    def stack_t(x):
        return _stack_heads(x, tmask0)

    pairs = range(n_pairs)
    sls = [slice(p * LANES, (p + 1) * LANES) for p in pairs]
    at_b, rt_b, bt_b, kt_b, v_b = (z.astype(BF16) for z in (at_all, rt_all, bt_all, kt_all, v))
    sd = [state_ref[p] for p in pairs]
    sd_b = [z.astype(BF16) for z in sd]
    gm = [_dg(jnp.concatenate([at_b[:, s], rt_b[:, s]], axis=0),
              jnp.concatenate([_stack_heads(bt_b[:, s], fmask0), _stack_heads(kt_b[:, s], fmask0)], axis=0), NT)
          for s in sls]
    lab = [jnp.where(strict, z[:C, :2 * C], 0.0) for z in gm]
    lak_b = [jnp.where(strict, z[:C, 2 * C:], 0.0).astype(BF16) for z in gm]
    mr_b = [jnp.concatenate([jnp.where(incl, z[C:, :2 * C], 0.0), jnp.where(incl, z[C:, 2 * C:], 0.0)],
                            axis=1).astype(BF16) for z in gm]
    pw_b = [z.astype(BF16) for z in lab]
    tinv = [eye_pair + z for z in lab]
    for _ in range(int(math.log2(C)) - 1):
        pw_b = [_dot(z, stack_t(z)).astype(BF16) for z in pw_b]
        tinv = [t + _dot(t.astype(BF16), stack_t(z)) for t, z in zip(tinv, pw_b)]
    vd_b = [_stack_heads(v_b[:, s], fmask0) for s in sls]
    x = [_dg(at_b[:, s], sd_b[p], NT) + _dot(lak_b[p], vd_b[p]) for p, s in zip(pairs, sls)]
    u_b = [_dot(tinv[p].astype(BF16), _stack_heads(x[p].astype(BF16), fmask0)).astype(BF16) for p in pairs]
    ys = [_dg(rt_b[:, s], sd_b[p], NT)
          + _dot(mr_b[p], jnp.concatenate([_stack_heads(u_b[p], fmask0), vd_b[p]], axis=0))
          for p, s in zip(pairs, sls)]
    for p, s in zip(pairs, sls):
        ds = _dg(jnp.concatenate([u_b[p], v_b[:, s]], axis=0), jnp.concatenate([bt_b[:, s], kt_b[:, s]], axis=0), TN)
        state_ref[p] = (sd[p] + jnp.where(blk, ds, 0.0)) * w_tot[:, s]

    y = jnp.concatenate(ys, axis=1)
    inv_n = 1.0 / HEAD_DIM
    mu = _head_sum(y) * inv_n
    yc = y - mu
    var = _head_sum(yc * yc) * inv_n
    yn = yc * lax.rsqrt(var + GN_EPS) * lng_ref[...] + lnb_ref[...]
    bonus = _head_sum(r * k * rk_ref[...]) * v
    y_ref[...] = ((yn + bonus) * g_ref[...]).astype(y_ref.dtype)

    @pl.when(c_idx == n_chunks - 1)
    def _():
        for p in range(n_pairs):
            sd = state_ref[p]
            sout_ref[0, 2 * p] = sd[:HEAD_DIM, :HEAD_DIM]
            sout_ref[0, 2 * p + 1] = sd[HEAD_DIM:, HEAD_DIM:]


def _rwkv_scan(r, lw, kh, v, a, b, g, s0, lng, lnb, rk, *, chunk, t_seq):
    m, w = r.shape
    n_b = m // t_seq
    n_chunks = t_seq // chunk
    n_heads = w // HEAD_DIM
    blk = pl.BlockSpec((chunk, w), lambda bi, ci: (bi * n_chunks + ci, 0))
    st = pl.BlockSpec((1, n_heads, HEAD_DIM, HEAD_DIM), lambda bi, ci: (bi, 0, 0, 0))
    return pl.pallas_call(
        functools.partial(_rwkv_scan_kernel, chunk=chunk, n_chunks=n_chunks),
        grid=(n_b, n_chunks),
        in_specs=[blk] * 7 + [st, _const_spec(lng.shape), _const_spec(lnb.shape), _const_spec(rk.shape)],
        out_specs=[blk, st],
        out_shape=[jax.ShapeDtypeStruct((m, w), BF16),
                   jax.ShapeDtypeStruct((n_b, n_heads, HEAD_DIM, HEAD_DIM), F32)],
        scratch_shapes=[pltpu.VMEM((w // LANES, LANES, LANES), F32)],
        compiler_params=_params(("arbitrary", "arbitrary")),
        name="rwkv_scan",
    )(r, lw, kh, v, a, b, g, s0, lng, lnb, rk)


def _rms_gate(o0, o1, gate, lane):
    first = lane < HEAD_DIM
    o = jnp.where(first, o0, o1)
    sq = o * o
    ms0 = jnp.sum(jnp.where(first, sq, 0.0), axis=-1, keepdims=True)
    ms1 = jnp.sum(jnp.where(first, 0.0, sq), axis=-1, keepdims=True)
    ms = jnp.where(first, ms0, ms1) * (1.0 / HEAD_DIM)
    return o * lax.rsqrt(ms + RMS_EPS) * gate


def _lane_cumsum(x, block):
    tri = jnp.where(_iota((block, block), 0) <= _iota((block, block), 1), 1.0, 0.0)
    carry = jnp.zeros((x.shape[0], 1), F32)
    outs = []
    for j in range(x.shape[1] // block):
        c = _dot(x[:, j * block:(j + 1) * block], tri, HI) + carry
        outs.append(c)
        carry = c[:, block - 1:block]
    return outs[0] if len(outs) == 1 else jnp.concatenate(outs, axis=1)


def _fox_prompt_kernel(q_ref, k_ref, caug_ref, vt_ref, gate_ref, y_ref,
                       s0_ref, s1_ref, m_ref, l_ref, acc_ref, *, tq):
    hp = pl.program_id(1)
    qi = pl.program_id(2)
    lane = _iota((tq, LANES), 1)
    first = lane < HEAD_DIM
    q = q_ref[...]
    zero = jnp.zeros_like(q)
    hslot = jnp.where(lane < 3 * FOX_HEADS, lax.rem(lane, FOX_HEADS), -1)
    qsa = []
    for h in range(2):
        ones = jnp.where(hslot == 2 * hp + h, 1.0, 0.0).astype(BF16)
        qh = jnp.where(first, q, zero) if h == 0 else jnp.where(first, zero, q)
        qsa.append(jnp.concatenate([qh, ones], axis=1))
    qsa = jnp.concatenate(qsa, axis=0)

    def scores(j):
        k0 = pl.multiple_of(j * tq, tq)
        kk = jnp.concatenate([k_ref[pl.ds(k0, tq), :], caug_ref[pl.ds(k0, tq), :]], axis=1)
        return _dg(kk, qsa, NT)

    def block(j, st_ref, masked):
        nv = tq // vt_ref.shape[2]
        vt = jnp.concatenate([vt_ref[nv * j + i] for i in range(nv)], axis=1)
        for h in range(2):
            m_old = m_ref[h]
            sh = st_ref[:, h * tq:(h + 1) * tq]
            if masked:
                sh = jnp.where(_iota((tq, tq), 0) <= _iota((tq, tq), 1), sh, NEG_BIG)
            m_new = jnp.maximum(m_old, jnp.max(sh, axis=0, keepdims=True))
            alpha = jnp.exp2(m_old - m_new)
            p = jnp.exp2(sh - m_new)
            m_ref[h] = m_new
            l_ref[h] = alpha * l_ref[h] + jnp.sum(p, axis=0, keepdims=True)
            acc_ref[h] = alpha * acc_ref[h] + _dot(vt[h * HEAD_DIM:(h + 1) * HEAD_DIM], p.astype(BF16))

    m_ref[...] = jnp.full(m_ref.shape, NEG_BIG, F32)
    l_ref[...] = jnp.zeros(l_ref.shape, F32)
    acc_ref[...] = jnp.zeros(acc_ref.shape, F32)
    s0_ref[...] = scores(0)

    def two_blocks(jj, _):
        j = 2 * jj
        s1_ref[...] = scores(j + 1)
        block(j, s0_ref, False)
        s0_ref[...] = scores(j + 2)
        block(j + 1, s1_ref, False)
        return _

    lax.fori_loop(0, lax.div(qi, 2), two_blocks, 0)

    @pl.when(lax.rem(qi, 2) == 0)
    def _():
        block(qi, s0_ref, True)

    @pl.when(lax.rem(qi, 2) == 1)
    def _():
        s1_ref[...] = scores(qi)
        block(qi - 1, s0_ref, False)
        block(qi, s1_ref, True)

    yts = []
    for h in range(2):
        o = acc_ref[h] * (1.0 / l_ref[h])
        ms = jnp.mean(o * o, axis=0, keepdims=True)
        yts.append(o * lax.rsqrt(ms + RMS_EPS))
    yt = jnp.concatenate(yts, axis=0)
    y_ref[...] = (yt.T * gate_ref[...]).astype(y_ref.dtype)


def _fox_prompt_attn(qb, kb, caug, vbt, gate, *, n_b, t_seq, tq):
    m, w = qb.shape
    n_pairs = w // LANES
    nq = t_seq // tq
    vblk = vbt.shape[2]
    assert vbt.shape == (m // vblk, w, vblk) and tq % vblk == 0
    qspec = pl.BlockSpec((tq, LANES), lambda b, hp, qi: (b * nq + qi, hp))
    return pl.pallas_call(
        functools.partial(_fox_prompt_kernel, tq=tq),
        grid=(n_b, n_pairs, nq),
        in_specs=[qspec,
                  pl.BlockSpec((t_seq, LANES), lambda b, hp, qi: (b, hp)),
                  pl.BlockSpec((t_seq, LANES), lambda b, hp, qi: (b, 0)),
                  pl.BlockSpec((t_seq // vblk, LANES, vblk), lambda b, hp, qi: (b, hp, 0)),
                  qspec],
        out_specs=qspec,
        out_shape=jax.ShapeDtypeStruct((m, w), BF16),
        scratch_shapes=[pltpu.VMEM((tq, 2 * tq), F32), pltpu.VMEM((tq, 2 * tq), F32),
                        pltpu.VMEM((2, 1, tq), F32), pltpu.VMEM((2, 1, tq), F32),
                        pltpu.VMEM((2, HEAD_DIM, tq), F32)],
        compiler_params=_params(("arbitrary", "arbitrary", "arbitrary")),
        name="fox_prompt_attn",
    )(qb, kb, caug, vbt, gate)


def _fox_sample_kernel(q_ref, kn_ref, vn_ref, ck_ref, cv_ref, clf_ref, lf_ref, gate_ref, y_ref):
    t = q_ref.shape[0]
    n_heads = lf_ref.shape[1]
    eye_h = jnp.where(_iota((n_heads, n_heads), 0) == _iota((n_heads, n_heads), 1), 1.0, 0.0)
    tril = jnp.where(_iota((t, t), 0) >= _iota((t, t), 1), 1.0, 0.0)
    clf = clf_ref[0]
    c_tot = jnp.sum(clf, axis=0, keepdims=True)
    cn_col = _dot(tril, lf_ref[...], HI)
    cn_row = _dg(eye_h, cn_col, NT, HI)
    cq_col = cn_col + c_tot
    cc_row = _lane_cumsum(_dg(eye_h, clf, NT, HI), 256)
    lane = _iota((t, LANES), 1)
    first = lane < HEAD_DIM
    causal = _iota((t, t), 1) <= _iota((t, t), 0)
    ys = []
    for p in range(q_ref.shape[1] // LANES):
        sl = slice(p * LANES, (p + 1) * LANES)
        q = q_ref[:, sl]
        zero = jnp.zeros_like(q)
        qs = jnp.concatenate([jnp.where(first, q, zero), jnp.where(first, zero, q)], axis=0)
        kc = ck_ref[0, :, sl].astype(BF16)
        vc = cv_ref[0, :, sl].astype(BF16)
        kn = kn_ref[:, sl]
        vn = vn_ref[:, sl]
        s_c = _dg(qs, kc, NT)
        s_n = _dg(qs, kn, NT)
        o = []
        for h in range(2):
            hd = 2 * p + h
            sc = s_c[h * t:(h + 1) * t] + (cq_col[:, hd:hd + 1] - cc_row[hd:hd + 1]) * LOG2E
            sn = s_n[h * t:(h + 1) * t] + (cn_col[:, hd:hd + 1] - cn_row[hd:hd + 1]) * LOG2E
            sn = jnp.where(causal, sn, NEG_BIG)
            mx = jnp.maximum(jnp.max(sc, axis=-1, keepdims=True), jnp.max(sn, axis=-1, keepdims=True))
            pc = jnp.exp2(sc - mx)
            pn = jnp.exp2(sn - mx)
            den = jnp.sum(pc, axis=-1, keepdims=True) + jnp.sum(pn, axis=-1, keepdims=True)
            o.append((_dot(pc.astype(BF16), vc) + _dot(pn.astype(BF16), vn)) / den)
        ys.append(_rms_gate(o[0], o[1], gate_ref[:, sl], lane))
    y_ref[...] = jnp.concatenate(ys, axis=1).astype(y_ref.dtype)


def _fox_sample_attn(qb, kb, vb, cache_k, cache_v, cache_lf, lf, gate, *, t_seq):
    m, w = qb.shape
    n_b = m // t_seq
    past = cache_k.shape[1]
    row = pl.BlockSpec((t_seq, w), lambda b: (b, 0))
    cache = pl.BlockSpec((1, past, w), lambda b: (b, 0, 0))
    return pl.pallas_call(
        _fox_sample_kernel,
        grid=(n_b,),
        in_specs=[row, row, row, cache, cache,
                  pl.BlockSpec((1, past, lf.shape[1]), lambda b: (b, 0, 0)),
                  pl.BlockSpec((t_seq, lf.shape[1]), lambda b: (b, 0)), row],
        out_specs=row,
        out_shape=jax.ShapeDtypeStruct((m, w), BF16),
        compiler_params=_params(("arbitrary",)),
        name="fox_sample_attn",
    )(qb, kb, vb, cache_k, cache_v, cache_lf, lf, gate)


def _out_ln_kernel(yr_ref, yf_ref, x_ref, wo_ref, g_ref, b_ref, h_ref, wob_ref, *, alpha):
    @pl.when(pl.program_id(0) == 0)
    def _():
        wob_ref[...] = wo_ref[...].astype(BF16)

    half = yr_ref.shape[1]
    mix = _dot(yr_ref[...], wob_ref[0:half, :]) + _dot(yf_ref[...], wob_ref[half:, :])
    h_ref[...] = _layer_norm(alpha * x_ref[...] + mix, g_ref[...], b_ref[...])


def _out_ln(yr, yf, x, wo, g, b, *, tm, alpha):
    m, d = x.shape
    row = lambda i: (i, 0)
    return pl.pallas_call(
        functools.partial(_out_ln_kernel, alpha=alpha),
        grid=(m // tm,),
        in_specs=[pl.BlockSpec((tm, yr.shape[1]), row), pl.BlockSpec((tm, yf.shape[1]), row),
                  pl.BlockSpec((tm, d), row), _const_spec(wo.shape), _const_spec(g.shape), _const_spec(b.shape)],
        out_specs=pl.BlockSpec((tm, d), row),
        out_shape=jax.ShapeDtypeStruct((m, d), F32),
        scratch_shapes=[pltpu.VMEM(wo.shape, BF16)],
        compiler_params=_params(("arbitrary",)),
        name="out_ln",
    )(yr, yf, x, wo, g, b)


def _ffn_ln_kernel(h_ref, wu_ref, wd_ref, g_ref, b_ref, o_ref, hb_ref, *, alpha, n_f):
    j = pl.program_id(1)

    @pl.when(j == 0)
    def _():
        h = h_ref[...]
        hb_ref[...] = h.astype(BF16)
        o_ref[...] = alpha * h

    u = jnp.maximum(_dot(hb_ref[...], wu_ref[...]), 0.0)
    o_ref[...] += _dot((u * u).astype(BF16), wd_ref[...])

    @pl.when(j == n_f - 1)
    def _():
        o_ref[...] = _layer_norm(o_ref[...], g_ref[...], b_ref[...])


def _ffn_ln(h, wu, wd, g, b, *, tm, tf, alpha):
    m, d = h.shape
    n_f = wu.shape[1] // tf
    return pl.pallas_call(
        functools.partial(_ffn_ln_kernel, alpha=alpha, n_f=n_f),
        grid=(m // tm, n_f),
        in_specs=[pl.BlockSpec((tm, d), lambda i, j: (i, 0), pipeline_mode=pl.Buffered(1)),
                  pl.BlockSpec((d, tf), lambda i, j: (0, j)),
                  pl.BlockSpec((tf, d), lambda i, j: (j, 0)),
                  _const_spec(g.shape), _const_spec(b.shape)],
        out_specs=pl.BlockSpec((tm, d), lambda i, j: (i, 0)),
        out_shape=jax.ShapeDtypeStruct((m, d), F32),
        scratch_shapes=[pltpu.VMEM((tm, d), BF16)],
        compiler_params=_params(("arbitrary", "arbitrary")),
        name="ffn_ln",
    )(h, wu, wd, g, b)


def _pad_cols(x, n):
    return jnp.pad(x, [(0, 0)] * (x.ndim - 1) + [(0, n - x.shape[-1])])


def _rwkv_cols(x, lora):
    o = 3 * RW
    dl, al = lora
    return jnp.concatenate([x[..., :o], _pad_cols(x[..., o:o + dl], 128), _pad_cols(x[..., o + dl:o + dl + al], 128),
                            _pad_cols(x[..., o + dl + al:], 256)], axis=-1)


def _rwkv_cols_inv(x, lora):
    o = 3 * RW
    dl, al, gl = lora
    return jnp.concatenate([x[..., :o], x[..., o:o + dl], x[..., o + 128:o + 128 + al],
                            x[..., o + 256:o + 256 + gl]], axis=-1)


def _stream(x, shift_prev, s0, wts, *, t_seq, tm, tm_ffn, chunk, cache=None):
    n_b, _, d = x.shape
    m = n_b * t_seq
    x2 = x.reshape(m, d)
    lora = wts["lora"]
    fprev = _rwkv_cols(shift_prev, lora[:2])
    r, lw, kh, v, a, b, g, sh = _rwkv_proj(x2, wts["w_r"], wts["mu"], fprev, wts["w0"], wts["w2p"], wts["a0"],
                                           wts["a2p"], wts["g2p"], wts["k_k"], wts["k_a"], tm=tm, t_seq=t_seq)
    y_r, s_new = _rwkv_scan(r, lw, kh, v, a, b, g, s0, wts["lnx_g"], wts["lnx_b"], wts["r_k"],
                            chunk=chunk, t_seq=t_seq)
    qb, k32, v32, kb, vb, vbt, gate, lf, caug = _fox_proj(x2, wts["w_f"], wts["bf_row"], wts["og_g"],
                                                          tm=tm, t_seq=t_seq)
    if cache is None:
        y_f = _fox_prompt_attn(qb, kb, caug, vbt, gate, n_b=n_b, t_seq=t_seq, tq=512)
    else:
        ck, cv, clf = cache
        y_f = _fox_sample_attn(qb, kb, vb, ck.reshape(n_b, ck.shape[1], FW), cv.reshape(n_b, cv.shape[1], FW),
                               clf, lf, gate, t_seq=t_seq)
    h = _out_ln(y_r, y_f, x2, wts["w_o"], wts["ln1_g"], wts["ln1_b"], tm=tm, alpha=wts["alpha"])
    y = _ffn_ln(h, wts["w_up"], wts["w_down"], wts["ln2_g"], wts["ln2_b"], tm=tm_ffn, tf=1024, alpha=wts["alpha"])
    heads = FW // HEAD_DIM
    return (y.reshape(n_b, t_seq, d), k32.reshape(n_b, t_seq, heads, HEAD_DIM),
            v32.reshape(n_b, t_seq, heads, HEAD_DIM), lf.reshape(n_b, t_seq, heads), s_new,
            _rwkv_cols_inv(sh, lora))


def kernel(x_prompt, x_sample, cache_fox_k, cache_fox_v, cache_fox_logf, state_rwkv_wkv, state_rwkv_shift,
           w_in, rwkv_mu, rwkv_w0, rwkv_w2, rwkv_a0, rwkv_a2, rwkv_g2, rwkv_k_k, rwkv_k_a, rwkv_r_k,
           rwkv_lnx_g, rwkv_lnx_b, fox_b_f, fox_out_g, w_o, ln1_g, ln1_b, w_up, w_down, ln2_g, ln2_b):
    depth = w_in.shape[0]
    assert depth == 1, "single-layer problem"
    d_model = x_prompt.shape[-1]
    rwkv_proj = rwkv_mu.shape[-1]
    lora = (rwkv_w2.shape[1], rwkv_a2.shape[1], rwkv_g2.shape[1])
    assert rwkv_w0.shape[-1] == RW and fox_out_g.shape[-1] == FW and rwkv_proj == 3 * RW + sum(lora)
    alpha = (2 * depth) ** 0.25
    l = 0
    w = w_in[l]
    fo = rwkv_proj
    row = lambda z: z.reshape(1, -1)
    pad_rows = lambda z, n: jnp.pad(z, ((0, n - z.shape[0]), (0, 0)))
    wts = dict(
        lora=lora, alpha=alpha,
        w_r=_rwkv_cols(w[:, :rwkv_proj], lora[:2]).astype(BF16),
        w_f=jnp.concatenate([w[:, fo:fo + 3 * FW], w[:, fo + 3 * FW + FOX_HEADS:],
                             _pad_cols(jnp.tile(w[:, fo + 3 * FW:fo + 3 * FW + FOX_HEADS], (1, 3)), LANES)],
                            axis=-1).astype(BF16),
        mu=_rwkv_cols(row(rwkv_mu[l]), lora[:2]),
        w0=row(rwkv_w0[l]), w2p=pad_rows(rwkv_w2[l], 128).astype(BF16),
        a0=row(rwkv_a0[l]), a2p=pad_rows(rwkv_a2[l], 128).astype(BF16),
        g2p=pad_rows(rwkv_g2[l], 256).astype(BF16),
        k_k=row(rwkv_k_k[l]), k_a=row(rwkv_k_a[l]), r_k=row(rwkv_r_k[l]),
        lnx_g=row(rwkv_lnx_g[l]), lnx_b=row(rwkv_lnx_b[l]),
        bf_row=_pad_cols(jnp.tile(row(fox_b_f[l]), (1, 3)), LANES), og_g=row(fox_out_g[l]),
        w_o=w_o[l], ln1_g=row(ln1_g[l]), ln1_b=row(ln1_b[l]),
        w_up=w_up[l].astype(BF16), w_down=w_down[l].astype(BF16), ln2_g=row(ln2_g[l]), ln2_b=row(ln2_b[l]),
    )
    n_p, t_p, _ = x_prompt.shape
    n_s, t_s, _ = x_sample.shape
    heads = RW // HEAD_DIM
    shift0 = jnp.zeros((n_p, 1, rwkv_proj), F32)
    s_zero = jnp.zeros((n_p, heads, HEAD_DIM, HEAD_DIM), F32)
    yp, kp, vp, fp, sp, shp = _stream(x_prompt, shift0, s_zero, wts, t_seq=t_p, tm=256, tm_ffn=1024, chunk=64)
    ys, ks, vs, fs, ss, shs = _stream(x_sample, state_rwkv_shift[l], state_rwkv_wkv[l], wts, t_seq=t_s,
                                      tm=n_s * t_s, tm_ffn=n_s * t_s, chunk=t_s,
                                      cache=(cache_fox_k[l], cache_fox_v[l], cache_fox_logf[l]))
    return (yp, ys, kp[None], vp[None], fp[None], sp[None], shp[None],
            ks[None], vs[None], fs[None], ss[None], shs[None])
```

```python
import functools
import math

import jax
import jax.numpy as jnp
from jax import lax
from jax.experimental import pallas as pl
from jax.experimental.pallas import tpu as pltpu

F32 = jnp.float32
BF16 = jnp.bfloat16
HI = lax.Precision.HIGHEST

HEAD_DIM = 64
LANES = 128
LN_EPS = 1e-5
GN_EPS = 64e-5
RMS_EPS = 1e-6
ATTN_SCALE = HEAD_DIM ** -0.5
EXP_NEG_HALF = math.exp(-0.5)
LOG2E = math.log2(math.e)
NEG_BIG = -1e30
VMEM_LIMIT = 56 * 1024 * 1024

NT = (((1,), (1,)), ((), ()))
TN = (((0,), (0,)), ((), ()))


def _sigmoid(x):
    return 1.0 / (1.0 + jnp.exp(-x))


def _log_sigmoid(x):
    return jnp.minimum(x, 0.0) - jnp.log1p(jnp.exp(-jnp.abs(x)))


def _dot(a, b, precision=None):
    return jnp.dot(a, b, preferred_element_type=F32, precision=precision)


def _dg(a, b, dims, precision=None):
    return lax.dot_general(a, b, dims, preferred_element_type=F32, precision=precision)


def _split3(x):
    hi = x.astype(BF16)
    rem = x - hi.astype(F32)
    mid = rem.astype(BF16)
    return hi, mid, (rem - mid.astype(F32)).astype(BF16)


def _dot01(m01, x):
    hi, mid, lo = _split3(x)
    return _dot(m01, hi) + _dot(m01, mid) + _dot(m01, lo)


def _iota(shape, axis):
    return lax.broadcasted_iota(jnp.int32, shape, axis)


def _head_sum(x):
    r = _iota((LANES, LANES), 0) // HEAD_DIM
    c = _iota((LANES, LANES), 1) // HEAD_DIM
    ones_blk = jnp.where(r == c, 1.0, 0.0).astype(BF16)
    outs = []
    for g in range(x.shape[1] // LANES):
        xs = x[:, g * LANES:(g + 1) * LANES]
        hi = xs.astype(BF16)
        lo = (xs - hi.astype(F32)).astype(BF16)
        outs.append(_dot(hi, ones_blk) + _dot(lo, ones_blk))
    return outs[0] if len(outs) == 1 else jnp.concatenate(outs, axis=1)


def _layer_norm(z, g, b):
    mu = jnp.mean(z, axis=-1, keepdims=True)
    zc = z - mu
    var = jnp.mean(zc * zc, axis=-1, keepdims=True)
    return zc * lax.rsqrt(var + LN_EPS) * g + b


def _const_spec(shape):
    nd = len(shape)
    return pl.BlockSpec(shape, lambda *_: (0,) * nd, pipeline_mode=pl.Buffered(1))


def _params(sem):
    return pltpu.CompilerParams(dimension_semantics=sem, vmem_limit_bytes=VMEM_LIMIT)


RW = 1024
RP_PAD = 3 * RW + 128 + 128 + 256


def _rwkv_proj_kernel(x_ref, w_ref, mu_ref, fp_ref, w0_ref, w2_ref, a0_ref, a2_ref, g2_ref, kk_ref, ka_ref, rk_ref,
                      at_ref, rt_ref, bt_ref, kt_ref, vb_ref, bonus_ref, g_ref, wtot_ref, sh_ref, carry_ref,
                      *, tm, t_seq, chunk):
    i = pl.program_id(0)
    xb = x_ref[...].astype(BF16)
    rows = _iota((tm, 1), 0)

    def proj(c0, n):
        return _dot(xb, w_ref[:, c0:c0 + n])

    def shift(p, c0):
        n = p.shape[1]
        prev = pltpu.roll(p, 1, 0)
        if t_seq >= tm:
            tiles = t_seq // tm
            pos = lax.rem(i, tiles)
            sidx = lax.div(i, tiles)
            row0 = jnp.where(pos == 0, fp_ref[sidx, :, c0:c0 + n], carry_ref[:, c0:c0 + n])
            prev = jnp.where(rows == 0, row0, prev)
            carry_ref[:, c0:c0 + n] = p[tm - 1:tm, :]

            @pl.when(pos == tiles - 1)
            def _():
                sh_ref[sidx, :, c0:c0 + n] = p[tm - 1:tm, :]
        else:
            per_tile = tm // t_seq
            for j in range(per_tile):
                prev = jnp.where(rows == j * t_seq, fp_ref[i * per_tile + j, :, c0:c0 + n], prev)
                sh_ref[i * per_tile + j, :, c0:c0 + n] = p[(j + 1) * t_seq - 1:(j + 1) * t_seq, :]
        return p + (prev - p) * mu_ref[:, c0:c0 + n]

    p_w = proj(3 * RW, 128)
    p_a = proj(3 * RW + 128, 128)
    p_g = proj(3 * RW + 256, 256)
    p_k = proj(RW, RW)
    xw = shift(p_w, 3 * RW)
    xa = shift(p_a, 3 * RW + 128)
    xg = shift(p_g, 3 * RW + 256)
    wl = w0_ref[...] + _dot(jnp.tanh(xw).astype(BF16), w2_ref[...])
    alr = _sigmoid(a0_ref[...] + _dot(xa.astype(BF16), a2_ref[...]))
    g_ref[...] = _dot(_sigmoid(xg).astype(BF16), g2_ref[...])
    k = shift(p_k, RW)
    kk = k * kk_ref[...]
    kk_ss = _head_sum(kk * kk)
    p_r = proj(0, RW)
    lw = -EXP_NEG_HALF * _sigmoid(wl)
    rr = _iota((tm, tm), 0)
    cc = _iota((tm, tm), 1)
    same_chunk_tri = jnp.where(rr // chunk == cc // chunk, jnp.where(rr >= cc, 1.0, 0.0), 0.0).astype(BF16)
    lwc = _dot01(same_chunk_tri, lw)
    p_v = proj(2 * RW, RW)
    kkn = kk / jnp.maximum(jnp.sqrt(kk_ss), 1e-12)
    kh = k * (1.0 + (alr - 1.0) * ka_ref[...])
    e_in = jnp.exp(lwc)
    e_out = jnp.exp(-lwc)
    at_ref[...] = (-kkn * jnp.exp(lwc - lw)).astype(BF16)
    bt_ref[...] = (kkn * alr * e_out).astype(BF16)
    kt_ref[...] = (kh * e_out).astype(BF16)
    for c in range(tm // chunk):
        wtot_ref[c] = e_in[(c + 1) * chunk - 1:(c + 1) * chunk, :]
    r = shift(p_r, 0)
    rt_ref[...] = (r * e_in).astype(BF16)
    rkk = _head_sum(r * kh * rk_ref[...])
    v = shift(p_v, 2 * RW)
    vb_ref[...] = v.astype(BF16)
    bonus_ref[...] = (rkk * v).astype(BF16)


def _rwkv_proj(x, w, mu, fprev, w0, w2p, a0, a2p, g2p, k_k, k_a, r_k, *, tm, t_seq, chunk):
    m, d = x.shape
    n_seq = m // t_seq
    row = lambda i: (i, 0)
    big = lambda dt: jax.ShapeDtypeStruct((m, RW), dt)
    return pl.pallas_call(
        functools.partial(_rwkv_proj_kernel, tm=tm, t_seq=t_seq, chunk=chunk),
        grid=(m // tm,),
        in_specs=[pl.BlockSpec((tm, d), row),
                  _const_spec(w.shape), _const_spec(mu.shape), _const_spec(fprev.shape),
                  _const_spec(w0.shape), _const_spec(w2p.shape), _const_spec(a0.shape), _const_spec(a2p.shape),
                  _const_spec(g2p.shape), _const_spec(k_k.shape), _const_spec(k_a.shape), _const_spec(r_k.shape)],
        out_specs=[pl.BlockSpec((tm, RW), row)] * 7
                  + [pl.BlockSpec((tm // chunk, 1, RW), lambda i: (i, 0, 0)),
                     pl.BlockSpec((n_seq, 1, RP_PAD), lambda i: (0, 0, 0))],
        out_shape=[big(BF16)] * 6 + [big(F32), jax.ShapeDtypeStruct((m // chunk, 1, RW), F32),
                                     jax.ShapeDtypeStruct((n_seq, 1, RP_PAD), F32)],
        scratch_shapes=[pltpu.VMEM((1, RP_PAD), F32)],
        compiler_params=_params(("arbitrary",)),
        name="rwkv_proj",
    )(x, w, mu, fprev, w0, w2p, a0, a2p, g2p, k_k, k_a, r_k)


FW = 1024
FOX_HEADS = FW // HEAD_DIM


def _fox_proj_kernel(x_ref, w_ref, bf_ref, og_ref,
                     qb_ref, k_ref, v_ref, kb_ref, vb_ref, vbt_ref, gate_ref, lf_ref, caug_ref, carry_ref,
                     *, tm, t_seq):
    i = pl.program_id(0)
    xb = x_ref[...].astype(BF16)
    qb_ref[...] = (_dot(xb, w_ref[:, 0:FW]) * (ATTN_SCALE * LOG2E)).astype(BF16)
    k = _dot(xb, w_ref[:, FW:2 * FW])
    for h in range(FOX_HEADS):
        k_ref[:, h, :] = k[:, h * HEAD_DIM:(h + 1) * HEAD_DIM]
    kb_ref[...] = k.astype(BF16)
    v = _dot(xb, w_ref[:, 2 * FW:3 * FW])
    for h in range(FOX_HEADS):
        v_ref[:, h, :] = v[:, h * HEAD_DIM:(h + 1) * HEAD_DIM]
    vb_ref[...] = v.astype(BF16)
    vbt_ref[0] = v.T.astype(BF16)
    og = _dot(xb, w_ref[:, 3 * FW:4 * FW])
    gate_ref[...] = _sigmoid(og) * og_ref[...]
    logf = _log_sigmoid(_dot(xb, w_ref[:, 4 * FW:4 * FW + LANES]) + bf_ref[...])
    lf_ref[...] = logf[:, :FOX_HEADS]
    r = _iota((tm, tm), 0)
    c = _iota((tm, tm), 1)
    if t_seq >= tm:
        cs = _dot01(jnp.where(r >= c, 1.0, 0.0).astype(BF16), logf)
        cs = cs + jnp.where(lax.rem(i, t_seq // tm) == 0, 0.0, carry_ref[...])
        carry_ref[...] = cs[tm - 1:tm, :]
    else:
        same_seq = (r // t_seq) == (c // t_seq)
        cs = _dot01(jnp.where(same_seq, jnp.where(r >= c, 1.0, 0.0), 0.0).astype(BF16), logf)
    xs = -LOG2E * cs
    hi, mid, lo = _split3(xs)
    lane = _iota((tm, LANES), 1)
    zero = jnp.zeros_like(hi)
    caug_ref[...] = jnp.where(lane < FOX_HEADS, hi,
                              jnp.where(lane < 2 * FOX_HEADS, mid, jnp.where(lane < 3 * FOX_HEADS, lo, zero)))


def _fox_proj(x, w, bf_row, og_g, *, tm, t_seq):
    m, d = x.shape
    row = lambda i: (i, 0)
    big = lambda dt: jax.ShapeDtypeStruct((m, FW), dt)
    heads_spec = pl.BlockSpec((tm, FOX_HEADS, HEAD_DIM), lambda i: (i, 0, 0))
    heads_out = jax.ShapeDtypeStruct((m, FOX_HEADS, HEAD_DIM), F32)
    return pl.pallas_call(
        functools.partial(_fox_proj_kernel, tm=tm, t_seq=t_seq),
        grid=(m // tm,),
        in_specs=[pl.BlockSpec((tm, d), row), _const_spec(w.shape), _const_spec(bf_row.shape),
                  _const_spec(og_g.shape)],
        out_specs=[pl.BlockSpec((tm, FW), row), heads_spec, heads_spec, pl.BlockSpec((tm, FW), row),
                   pl.BlockSpec((tm, FW), row),
                   pl.BlockSpec((1, FW, tm), lambda i: (i, 0, 0)), pl.BlockSpec((tm, FW), row),
                   pl.BlockSpec((tm, FOX_HEADS), row), pl.BlockSpec((tm, LANES), row)],
        out_shape=[big(BF16), heads_out, heads_out, big(BF16), big(BF16),
                   jax.ShapeDtypeStruct((m // tm, FW, tm), BF16), big(F32),
                   jax.ShapeDtypeStruct((m, FOX_HEADS), F32), jax.ShapeDtypeStruct((m, LANES), BF16)],
        scratch_shapes=[pltpu.VMEM((1, LANES), F32)],
        compiler_params=_params(("arbitrary",)),
        name="fox_proj",
    )(x, w, bf_row, og_g)


def _stack_heads(x, mask0):
    return jnp.concatenate([jnp.where(mask0, x, 0.0), jnp.where(mask0, 0.0, x)], axis=0)


def _rwkv_scan_kernel(at_ref, rt_ref, bt_ref, kt_ref, vb_ref, wtot_ref, bonus_ref, g_ref, s0_ref, lng_ref, lnb_ref,
                      y_ref, sout_ref, state_ref, *, chunk, n_steps):
    step = pl.program_id(1)
    n_pairs = at_ref.shape[1] // LANES
    C = chunk
    n_ch = at_ref.shape[0] // C
    pairs = range(n_pairs)

    @pl.when(step == 0)
    def _():
        zero = jnp.zeros((HEAD_DIM, HEAD_DIM), F32)
        for p in pairs:
            top = jnp.concatenate([s0_ref[0, 2 * p], zero], axis=1)
            bot = jnp.concatenate([zero, s0_ref[0, 2 * p + 1]], axis=1)
            state_ref[p] = jnp.concatenate([top, bot], axis=0)

    fmask0 = _iota((1, LANES), 1) < HEAD_DIM
    tcol = _iota((C, 2 * C), 1)
    trow = _iota((C, 2 * C), 0)
    tmask0 = tcol < C
    tj = jnp.where(tmask0, tcol, tcol - C)
    strict = tj < trow
    incl = tj <= trow
    eye_pair = jnp.where(tj == trow, 1.0, 0.0)
    blk = (_iota((LANES, LANES), 0) // HEAD_DIM) == (_iota((LANES, LANES), 1) // HEAD_DIM)

    def stack_t(x):
        return _stack_heads(x, tmask0)

    def tile(ref, c, p):
        return ref[c * C:(c + 1) * C, p * LANES:(p + 1) * LANES]

    probs = [(c, p) for c in range(n_ch) for p in pairs]
    at = {cp: tile(at_ref, *cp) for cp in probs}
    rt = {cp: tile(rt_ref, *cp) for cp in probs}
    bt = {cp: tile(bt_ref, *cp) for cp in probs}
    kt = {cp: tile(kt_ref, *cp) for cp in probs}
    vb = {cp: tile(vb_ref, *cp) for cp in probs}
    gm = {cp: _dg(jnp.concatenate([at[cp], rt[cp]], axis=0),
                  jnp.concatenate([_stack_heads(bt[cp], fmask0), _stack_heads(kt[cp], fmask0)], axis=0), NT)
          for cp in probs}
    lab = {cp: jnp.where(strict, gm[cp][:C, :2 * C], 0.0) for cp in probs}
    lak_b = {cp: jnp.where(strict, gm[cp][:C, 2 * C:], 0.0).astype(BF16) for cp in probs}
    mr_b = {cp: jnp.concatenate([jnp.where(incl, gm[cp][C:, :2 * C], 0.0),
                                 jnp.where(incl, gm[cp][C:, 2 * C:], 0.0)], axis=1).astype(BF16) for cp in probs}
    kmax = int(math.log2(C)) - 1
    tinv = {cp: eye_pair + lab[cp] for cp in probs}
    pw = {cp: lab[cp].astype(BF16) for cp in probs}
    pw = {cp: _dot(pw[cp], stack_t(pw[cp])).astype(BF16) for cp in probs}
    for _ in range(1, kmax):
        res = {cp: _dot(jnp.concatenate([pw[cp], tinv[cp].astype(BF16)], axis=0), stack_t(pw[cp])) for cp in probs}
        pw = {cp: res[cp][:C].astype(BF16) for cp in probs}
        tinv = {cp: tinv[cp] + res[cp][C:] for cp in probs}
    tinv_b = {cp: (tinv[cp] + _dot(tinv[cp].astype(BF16), stack_t(pw[cp]))).astype(BF16) for cp in probs}

    sd = [state_ref[p] for p in pairs]
    ys = []
    for c in range(n_ch):
        sd_b = [z.astype(BF16) for z in sd]
        vd_b = [_stack_heads(vb[c, p], fmask0) for p in pairs]
        x = [_dg(at[c, p], sd_b[p], NT) + _dot(lak_b[c, p], vd_b[p]) for p in pairs]
        u_b = [_dot(tinv_b[c, p], _stack_heads(x[p].astype(BF16), fmask0)).astype(BF16) for p in pairs]
        ys.append(jnp.concatenate(
            [_dg(rt[c, p], sd_b[p], NT)
             + _dot(mr_b[c, p], jnp.concatenate([_stack_heads(u_b[p], fmask0), vd_b[p]], axis=0)) for p in pairs],
            axis=1))
        w_tot = wtot_ref[c]
        for p in pairs:
            ds = _dg(jnp.concatenate([u_b[p], vb[c, p]], axis=0), jnp.concatenate([bt[c, p], kt[c, p]], axis=0), TN)
            sd[p] = (sd[p] + jnp.where(blk, ds, 0.0)) * w_tot[:, p * LANES:(p + 1) * LANES]
    for p in pairs:
        state_ref[p] = sd[p]

    y = ys[0] if n_ch == 1 else jnp.concatenate(ys, axis=0)
    inv_n = 1.0 / HEAD_DIM
    mu = _head_sum(y) * inv_n
    yc = y - mu
    var = _head_sum(yc * yc) * inv_n
    yn = yc * lax.rsqrt(var + GN_EPS) * lng_ref[...] + lnb_ref[...]
    y_ref[...] = ((yn + bonus_ref[...].astype(F32)) * g_ref[...]).astype(y_ref.dtype)

    @pl.when(step == n_steps - 1)
    def _():
        for p in pairs:
            sout_ref[0, 2 * p] = sd[p][:HEAD_DIM, :HEAD_DIM]
            sout_ref[0, 2 * p + 1] = sd[p][HEAD_DIM:, HEAD_DIM:]


def _rwkv_scan(at, rt, bt, kt, vb, wtot, bonus, g, s0, lng, lnb, *, chunk, rows, t_seq):
    m, w = at.shape
    n_b = m // t_seq
    n_steps = t_seq // rows
    n_heads = w // HEAD_DIM
    blk = pl.BlockSpec((rows, w), lambda bi, si: (bi * n_steps + si, 0))
    wt = pl.BlockSpec((rows // chunk, 1, w), lambda bi, si: (bi * n_steps + si, 0, 0))
    st = pl.BlockSpec((1, n_heads, HEAD_DIM, HEAD_DIM), lambda bi, si: (bi, 0, 0, 0))
    return pl.pallas_call(
        functools.partial(_rwkv_scan_kernel, chunk=chunk, n_steps=n_steps),
        grid=(n_b, n_steps),
        in_specs=[blk] * 5 + [wt, blk, blk, st, _const_spec(lng.shape), _const_spec(lnb.shape)],
        out_specs=[blk, st],
        out_shape=[jax.ShapeDtypeStruct((m, w), BF16),
                   jax.ShapeDtypeStruct((n_b, n_heads, HEAD_DIM, HEAD_DIM), F32)],
        scratch_shapes=[pltpu.VMEM((w // LANES, LANES, LANES), F32)],
        compiler_params=_params(("arbitrary", "arbitrary")),
        name="rwkv_scan",
    )(at, rt, bt, kt, vb, wtot, bonus, g, s0, lng, lnb)


def _rms_gate(o0, o1, gate, lane):
    first = lane < HEAD_DIM
    o = jnp.where(first, o0, o1)
    sq = o * o
    ms0 = jnp.sum(jnp.where(first, sq, 0.0), axis=-1, keepdims=True)
    ms1 = jnp.sum(jnp.where(first, 0.0, sq), axis=-1, keepdims=True)
    ms = jnp.where(first, ms0, ms1) * (1.0 / HEAD_DIM)
    return o * lax.rsqrt(ms + RMS_EPS) * gate


def _lane_cumsum(x, block):
    tri = jnp.where(_iota((block, block), 0) <= _iota((block, block), 1), 1.0, 0.0)
    carry = jnp.zeros((x.shape[0], 1), F32)
    outs = []
    for j in range(x.shape[1] // block):
        c = _dot(x[:, j * block:(j + 1) * block], tri, HI) + carry
        outs.append(c)
        carry = c[:, block - 1:block]
    return outs[0] if len(outs) == 1 else jnp.concatenate(outs, axis=1)


def _fox_prompt_kernel(q_ref, k_ref, caug_ref, vt_ref, gate_ref, y_ref,
                       s0_ref, s1_ref, m_ref, l_ref, acc_ref, *, tq):
    hp = pl.program_id(1)
    qi = pl.program_id(2)
    lane = _iota((tq, LANES), 1)
    first = lane < HEAD_DIM
    q = q_ref[...]
    zero = jnp.zeros_like(q)
    hslot = jnp.where(lane < 3 * FOX_HEADS, lax.rem(lane, FOX_HEADS), -1)
    qsa = []
    for h in range(2):
        ones = jnp.where(hslot == 2 * hp + h, 1.0, 0.0).astype(BF16)
        qh = jnp.where(first, q, zero) if h == 0 else jnp.where(first, zero, q)
        qsa.append(jnp.concatenate([qh, ones], axis=1))
    qsa = jnp.concatenate(qsa, axis=0)

    def scores(j):
        k0 = pl.multiple_of(j * tq, tq)
        kk = jnp.concatenate([k_ref[pl.ds(k0, tq), :], caug_ref[pl.ds(k0, tq), :]], axis=1)
        return _dg(kk, qsa, NT)

    def block(j, st_ref, masked):
        nv = tq // vt_ref.shape[2]
        vt = jnp.concatenate([vt_ref[nv * j + i] for i in range(nv)], axis=1)
        for h in range(2):
            m_old = m_ref[h]
            sh = st_ref[:, h * tq:(h + 1) * tq]
            if masked:
                sh = jnp.where(_iota((tq, tq), 0) <= _iota((tq, tq), 1), sh, NEG_BIG)
            m_new = jnp.maximum(m_old, jnp.max(sh, axis=0, keepdims=True))
            alpha = jnp.exp2(m_old - m_new)
            p = jnp.exp2(sh - m_new)
            m_ref[h] = m_new
            l_ref[h] = alpha * l_ref[h] + jnp.sum(p, axis=0, keepdims=True)
            acc_ref[h] = alpha * acc_ref[h] + _dot(vt[h * HEAD_DIM:(h + 1) * HEAD_DIM], p.astype(BF16))

    m_ref[...] = jnp.full(m_ref.shape, NEG_BIG, F32)
    l_ref[...] = jnp.zeros(l_ref.shape, F32)
    acc_ref[...] = jnp.zeros(acc_ref.shape, F32)
    s0_ref[...] = scores(0)

    def two_blocks(jj, _):
        j = 2 * jj
        s1_ref[...] = scores(j + 1)
        block(j, s0_ref, False)
        s0_ref[...] = scores(j + 2)
        block(j + 1, s1_ref, False)
        return _

    lax.fori_loop(0, lax.div(qi, 2), two_blocks, 0)

    @pl.when(lax.rem(qi, 2) == 0)
    def _():
        block(qi, s0_ref, True)

    @pl.when(lax.rem(qi, 2) == 1)
    def _():
        s1_ref[...] = scores(qi)
        block(qi - 1, s0_ref, False)
        block(qi, s1_ref, True)

    yts = []
    for h in range(2):
        o = acc_ref[h] * (1.0 / l_ref[h])
        ms = jnp.mean(o * o, axis=0, keepdims=True)
        yts.append(o * lax.rsqrt(ms + RMS_EPS))
    yt = jnp.concatenate(yts, axis=0)
    y_ref[...] = (yt.T * gate_ref[...]).astype(y_ref.dtype)


def _fox_prompt_attn(qb, kb, caug, vbt, gate, *, n_b, t_seq, tq):
    m, w = qb.shape
    n_pairs = w // LANES
    nq = t_seq // tq
    vblk = vbt.shape[2]
    assert vbt.shape == (m // vblk, w, vblk) and tq % vblk == 0
    qspec = pl.BlockSpec((tq, LANES), lambda b, hp, qi: (b * nq + qi, hp))
    return pl.pallas_call(
        functools.partial(_fox_prompt_kernel, tq=tq),
        grid=(n_b, n_pairs, nq),
        in_specs=[qspec,
                  pl.BlockSpec((t_seq, LANES), lambda b, hp, qi: (b, hp)),
                  pl.BlockSpec((t_seq, LANES), lambda b, hp, qi: (b, 0)),
                  pl.BlockSpec((t_seq // vblk, LANES, vblk), lambda b, hp, qi: (b, hp, 0)),
                  qspec],
        out_specs=qspec,
        out_shape=jax.ShapeDtypeStruct((m, w), BF16),
        scratch_shapes=[pltpu.VMEM((tq, 2 * tq), F32), pltpu.VMEM((tq, 2 * tq), F32),
                        pltpu.VMEM((2, 1, tq), F32), pltpu.VMEM((2, 1, tq), F32),
                        pltpu.VMEM((2, HEAD_DIM, tq), F32)],
        compiler_params=_params(("arbitrary", "arbitrary", "arbitrary")),
        name="fox_prompt_attn",
    )(qb, kb, caug, vbt, gate)


def _fox_sample_kernel(q_ref, kn_ref, vn_ref, ck_ref, cv_ref, clf_ref, lf_ref, gate_ref, y_ref):
    t = q_ref.shape[0]
    n_heads = lf_ref.shape[1]
    eye_h = jnp.where(_iota((n_heads, n_heads), 0) == _iota((n_heads, n_heads), 1), 1.0, 0.0)
    tril = jnp.where(_iota((t, t), 0) >= _iota((t, t), 1), 1.0, 0.0)
    clf = clf_ref[0]
    c_tot = jnp.sum(clf, axis=0, keepdims=True)
    cn_col = _dot(tril, lf_ref[...], HI)
    cn_row = _dg(eye_h, cn_col, NT, HI)
    cq_col = cn_col + c_tot
    cc_row = _lane_cumsum(_dg(eye_h, clf, NT, HI), 256)
    lane = _iota((t, LANES), 1)
    first = lane < HEAD_DIM
    causal = _iota((t, t), 1) <= _iota((t, t), 0)
    ys = []
    for p in range(q_ref.shape[1] // LANES):
        sl = slice(p * LANES, (p + 1) * LANES)
        q = q_ref[:, sl]
        zero = jnp.zeros_like(q)
        qs = jnp.concatenate([jnp.where(first, q, zero), jnp.where(first, zero, q)], axis=0)
        kc = ck_ref[0, :, sl].astype(BF16)
        vc = cv_ref[0, :, sl].astype(BF16)
        kn = kn_ref[:, sl]
        vn = vn_ref[:, sl]
        s_c = _dg(qs, kc, NT)
        s_n = _dg(qs, kn, NT)
        o = []
        for h in range(2):
            hd = 2 * p + h
            sc = s_c[h * t:(h + 1) * t] + (cq_col[:, hd:hd + 1] - cc_row[hd:hd + 1]) * LOG2E
            sn = s_n[h * t:(h + 1) * t] + (cn_col[:, hd:hd + 1] - cn_row[hd:hd + 1]) * LOG2E
            sn = jnp.where(causal, sn, NEG_BIG)
            mx = jnp.maximum(jnp.max(sc, axis=-1, keepdims=True), jnp.max(sn, axis=-1, keepdims=True))
            pc = jnp.exp2(sc - mx)
            pn = jnp.exp2(sn - mx)
            den = jnp.sum(pc, axis=-1, keepdims=True) + jnp.sum(pn, axis=-1, keepdims=True)
            o.append((_dot(pc.astype(BF16), vc) + _dot(pn.astype(BF16), vn)) / den)
        ys.append(_rms_gate(o[0], o[1], gate_ref[:, sl], lane))
    y_ref[...] = jnp.concatenate(ys, axis=1).astype(y_ref.dtype)


def _fox_sample_attn(qb, kb, vb, cache_k, cache_v, cache_lf, lf, gate, *, t_seq):
    m, w = qb.shape
    n_b = m // t_seq
    past = cache_k.shape[1]
    row = pl.BlockSpec((t_seq, w), lambda b: (b, 0))
    cache = pl.BlockSpec((1, past, w), lambda b: (b, 0, 0))
    return pl.pallas_call(
        _fox_sample_kernel,
        grid=(n_b,),
        in_specs=[row, row, row, cache, cache,
                  pl.BlockSpec((1, past, lf.shape[1]), lambda b: (b, 0, 0)),
                  pl.BlockSpec((t_seq, lf.shape[1]), lambda b: (b, 0)), row],
        out_specs=row,
        out_shape=jax.ShapeDtypeStruct((m, w), BF16),
        compiler_params=_params(("arbitrary",)),
        name="fox_sample_attn",
    )(qb, kb, vb, cache_k, cache_v, cache_lf, lf, gate)


def _out_ln_kernel(yr_ref, yf_ref, x_ref, wo_ref, g_ref, b_ref, h_ref, wob_ref, *, alpha):
    @pl.when(pl.program_id(0) == 0)
    def _():
        wob_ref[...] = wo_ref[...].astype(BF16)

    half = yr_ref.shape[1]
    mix = _dot(yr_ref[...], wob_ref[0:half, :]) + _dot(yf_ref[...], wob_ref[half:, :])
    h_ref[...] = _layer_norm(alpha * x_ref[...] + mix, g_ref[...], b_ref[...])


def _out_ln(yr, yf, x, wo, g, b, *, tm, alpha):
    m, d = x.shape
    row = lambda i: (i, 0)
    return pl.pallas_call(
        functools.partial(_out_ln_kernel, alpha=alpha),
        grid=(m // tm,),
        in_specs=[pl.BlockSpec((tm, yr.shape[1]), row), pl.BlockSpec((tm, yf.shape[1]), row),
                  pl.BlockSpec((tm, d), row), _const_spec(wo.shape), _const_spec(g.shape), _const_spec(b.shape)],
        out_specs=pl.BlockSpec((tm, d), row),
        out_shape=jax.ShapeDtypeStruct((m, d), F32),
        scratch_shapes=[pltpu.VMEM(wo.shape, BF16)],
        compiler_params=_params(("arbitrary",)),
        name="out_ln",
    )(yr, yf, x, wo, g, b)


def _ffn_ln_kernel(h_ref, wu_ref, wd_ref, g_ref, b_ref, o_ref, hb_ref, *, alpha, n_f):
    j = pl.program_id(1)

    @pl.when(j == 0)
    def _():
        h = h_ref[...]
        hb_ref[...] = h.astype(BF16)
        o_ref[...] = alpha * h

    u = jnp.maximum(_dot(hb_ref[...], wu_ref[...]), 0.0)
    o_ref[...] += _dot((u * u).astype(BF16), wd_ref[...])

    @pl.when(j == n_f - 1)
    def _():
        o_ref[...] = _layer_norm(o_ref[...], g_ref[...], b_ref[...])


def _ffn_ln(h, wu, wd, g, b, *, tm, tf, alpha):
    m, d = h.shape
    n_f = wu.shape[1] // tf
    return pl.pallas_call(
        functools.partial(_ffn_ln_kernel, alpha=alpha, n_f=n_f),
        grid=(m // tm, n_f),
        in_specs=[pl.BlockSpec((tm, d), lambda i, j: (i, 0), pipeline_mode=pl.Buffered(1)),
                  pl.BlockSpec((d, tf), lambda i, j: (0, j)),
                  pl.BlockSpec((tf, d), lambda i, j: (j, 0)),
                  _const_spec(g.shape), _const_spec(b.shape)],
        out_specs=pl.BlockSpec((tm, d), lambda i, j: (i, 0)),
        out_shape=jax.ShapeDtypeStruct((m, d), F32),
        scratch_shapes=[pltpu.VMEM((tm, d), BF16)],
        compiler_params=_params(("arbitrary", "arbitrary")),
        name="ffn_ln",
    )(h, wu, wd, g, b)


def _pad_cols(x, n):
    return jnp.pad(x, [(0, 0)] * (x.ndim - 1) + [(0, n - x.shape[-1])])


def _rwkv_cols(x, lora):
    o = 3 * RW
    dl, al = lora
    return jnp.concatenate([x[..., :o], _pad_cols(x[..., o:o + dl], 128), _pad_cols(x[..., o + dl:o + dl + al], 128),
                            _pad_cols(x[..., o + dl + al:], 256)], axis=-1)


def _rwkv_cols_inv(x, lora):
    o = 3 * RW
    dl, al, gl = lora
    return jnp.concatenate([x[..., :o], x[..., o:o + dl], x[..., o + 128:o + 128 + al],
                            x[..., o + 256:o + 256 + gl]], axis=-1)


def _stream(x, shift_prev, s0, wts, *, t_seq, tm, tm_ffn, chunk, cache=None):
    n_b, _, d = x.shape
    m = n_b * t_seq
    x2 = x.reshape(m, d)
    lora = wts["lora"]
    fprev = _rwkv_cols(shift_prev, lora[:2])
    at, rt, bt, kt, vr, bonus, g, wtot, sh = _rwkv_proj(
        x2, wts["w_r"], wts["mu"], fprev, wts["w0"], wts["w2p"], wts["a0"], wts["a2p"], wts["g2p"], wts["k_k"],
        wts["k_a"], wts["r_k"], tm=tm, t_seq=t_seq, chunk=chunk)
    y_r, s_new = _rwkv_scan(at, rt, bt, kt, vr, wtot, bonus, g, s0, wts["lnx_g"], wts["lnx_b"],
                            chunk=chunk, rows=min(tm, t_seq), t_seq=t_seq)
    qb, k32, v32, kb, vb, vbt, gate, lf, caug = _fox_proj(x2, wts["w_f"], wts["bf_row"], wts["og_g"],
                                                          tm=tm, t_seq=t_seq)
    if cache is None:
        y_f = _fox_prompt_attn(qb, kb, caug, vbt, gate, n_b=n_b, t_seq=t_seq, tq=512)
    else:
        ck, cv, clf = cache
        y_f = _fox_sample_attn(qb, kb, vb, ck.reshape(n_b, ck.shape[1], FW), cv.reshape(n_b, cv.shape[1], FW),
                               clf, lf, gate, t_seq=t_seq)
    h = _out_ln(y_r, y_f, x2, wts["w_o"], wts["ln1_g"], wts["ln1_b"], tm=tm, alpha=wts["alpha"])
    y = _ffn_ln(h, wts["w_up"], wts["w_down"], wts["ln2_g"], wts["ln2_b"], tm=tm_ffn, tf=1024, alpha=wts["alpha"])
    heads = FW // HEAD_DIM
    return (y.reshape(n_b, t_seq, d), k32.reshape(n_b, t_seq, heads, HEAD_DIM),
            v32.reshape(n_b, t_seq, heads, HEAD_DIM), lf.reshape(n_b, t_seq, heads), s_new,
            _rwkv_cols_inv(sh, lora))


def kernel(x_prompt, x_sample, cache_fox_k, cache_fox_v, cache_fox_logf, state_rwkv_wkv, state_rwkv_shift,
           w_in, rwkv_mu, rwkv_w0, rwkv_w2, rwkv_a0, rwkv_a2, rwkv_g2, rwkv_k_k, rwkv_k_a, rwkv_r_k,
           rwkv_lnx_g, rwkv_lnx_b, fox_b_f, fox_out_g, w_o, ln1_g, ln1_b, w_up, w_down, ln2_g, ln2_b):
    depth = w_in.shape[0]
    assert depth == 1, "single-layer problem"
    d_model = x_prompt.shape[-1]
    rwkv_proj = rwkv_mu.shape[-1]
    lora = (rwkv_w2.shape[1], rwkv_a2.shape[1], rwkv_g2.shape[1])
    assert rwkv_w0.shape[-1] == RW and fox_out_g.shape[-1] == FW and rwkv_proj == 3 * RW + sum(lora)
    alpha = (2 * depth) ** 0.25
    l = 0
    w = w_in[l]
    fo = rwkv_proj
    row = lambda z: z.reshape(1, -1)
    pad_rows = lambda z, n: jnp.pad(z, ((0, n - z.shape[0]), (0, 0)))
    wts = dict(
        lora=lora, alpha=alpha,
        w_r=_rwkv_cols(w[:, :rwkv_proj], lora[:2]).astype(BF16),
        w_f=jnp.concatenate([w[:, fo:fo + 3 * FW], w[:, fo + 3 * FW + FOX_HEADS:],
                             _pad_cols(jnp.tile(w[:, fo + 3 * FW:fo + 3 * FW + FOX_HEADS], (1, 3)), LANES)],
                            axis=-1).astype(BF16),
        mu=_rwkv_cols(row(rwkv_mu[l]), lora[:2]),
        w0=row(rwkv_w0[l]), w2p=pad_rows(rwkv_w2[l], 128).astype(BF16),
        a0=row(rwkv_a0[l]), a2p=pad_rows(rwkv_a2[l], 128).astype(BF16),
        g2p=pad_rows(rwkv_g2[l], 256).astype(BF16),
        k_k=row(rwkv_k_k[l]), k_a=row(rwkv_k_a[l]), r_k=row(rwkv_r_k[l]),
        lnx_g=row(rwkv_lnx_g[l]), lnx_b=row(rwkv_lnx_b[l]),
        bf_row=_pad_cols(jnp.tile(row(fox_b_f[l]), (1, 3)), LANES), og_g=row(fox_out_g[l]),
        w_o=w_o[l], ln1_g=row(ln1_g[l]), ln1_b=row(ln1_b[l]),
        w_up=w_up[l].astype(BF16), w_down=w_down[l].astype(BF16), ln2_g=row(ln2_g[l]), ln2_b=row(ln2_b[l]),
    )
    n_p, t_p, _ = x_prompt.shape
    n_s, t_s, _ = x_sample.shape
    heads = RW // HEAD_DIM
    shift0 = jnp.zeros((n_p, 1, rwkv_proj), F32)
    s_zero = jnp.zeros((n_p, heads, HEAD_DIM, HEAD_DIM), F32)
    yp, kp, vp, fp, sp, shp = _stream(x_prompt, shift0, s_zero, wts, t_seq=t_p, tm=256, tm_ffn=1024, chunk=64)
    ys, ks, vs, fs, ss, shs = _stream(x_sample, state_rwkv_shift[l], state_rwkv_wkv[l], wts, t_seq=t_s,
                                      tm=n_s * t_s, tm_ffn=n_s * t_s, chunk=t_s,
                                      cache=(cache_fox_k[l], cache_fox_v[l], cache_fox_logf[l]))
    return (yp, ys, kp[None], vp[None], fp[None], sp[None], shp[None],
            ks[None], vs[None], fs[None], ss[None], shs[None])
```

```python
import functools
import math

import jax
import jax.numpy as jnp
from jax import lax
from jax.experimental import pallas as pl
from jax.experimental.pallas import tpu as pltpu

F32 = jnp.float32
BF16 = jnp.bfloat16
HI = lax.Precision.HIGHEST

HEAD_DIM = 64
LANES = 128
LN_EPS = 1e-5
GN_EPS = 64e-5
RMS_EPS = 1e-6
ATTN_SCALE = HEAD_DIM ** -0.5
EXP_NEG_HALF = math.exp(-0.5)
LOG2E = math.log2(math.e)
NEG_BIG = -1e30
VMEM_LIMIT = 60 * 1024 * 1024

NT = (((1,), (1,)), ((), ()))
TN = (((0,), (0,)), ((), ()))


def _sigmoid(x):
    return 1.0 / (1.0 + jnp.exp(-x))


def _log_sigmoid(x):
    return jnp.minimum(x, 0.0) - jnp.log1p(jnp.exp(-jnp.abs(x)))


def _dot(a, b, precision=None):
    return jnp.dot(a, b, preferred_element_type=F32, precision=precision)


def _dg(a, b, dims, precision=None):
    return lax.dot_general(a, b, dims, preferred_element_type=F32, precision=precision)


def _split3(x):
    hi = x.astype(BF16)
    rem = x - hi.astype(F32)
    mid = rem.astype(BF16)
    return hi, mid, (rem - mid.astype(F32)).astype(BF16)


def _dot01(m01, x):
    hi, mid, lo = _split3(x)
    return _dot(m01, hi) + _dot(m01, mid) + _dot(m01, lo)


def _iota(shape, axis):
    return lax.broadcasted_iota(jnp.int32, shape, axis)


def _head_sum(x):
    r = _iota((LANES, LANES), 0) // HEAD_DIM
    c = _iota((LANES, LANES), 1) // HEAD_DIM
    ones_blk = jnp.where(r == c, 1.0, 0.0).astype(BF16)
    outs = []
    for g in range(x.shape[1] // LANES):
        xs = x[:, g * LANES:(g + 1) * LANES]
        hi = xs.astype(BF16)
        lo = (xs - hi.astype(F32)).astype(BF16)
        outs.append(_dot(hi, ones_blk) + _dot(lo, ones_blk))
    return outs[0] if len(outs) == 1 else jnp.concatenate(outs, axis=1)


def _layer_norm(z, g, b):
    mu = jnp.mean(z, axis=-1, keepdims=True)
    zc = z - mu
    var = jnp.mean(zc * zc, axis=-1, keepdims=True)
    return zc * lax.rsqrt(var + LN_EPS) * g + b


def _const_spec(shape):
    nd = len(shape)
    return pl.BlockSpec(shape, lambda *_: (0,) * nd, pipeline_mode=pl.Buffered(1))


def _params(sem):
    return pltpu.CompilerParams(dimension_semantics=sem, vmem_limit_bytes=VMEM_LIMIT)


RW = 1024
RP_PAD = 3 * RW + 128 + 128 + 256


def _rwkv_proj_kernel(x_ref, w_ref, mu_ref, fp_ref, w0_ref, w2_ref, a0_ref, a2_ref, g2_ref, kk_ref, ka_ref, rk_ref,
                      at_ref, rt_ref, bt_ref, kt_ref, vb_ref, bonus_ref, g_ref, wtot_ref, sh_ref, carry_ref,
                      *, tm, t_seq, chunk):
    i = pl.program_id(0)
    xb = x_ref[...].astype(BF16)
    rows = _iota((tm, 1), 0)

    def proj(c0, n):
        return _dot(xb, w_ref[:, c0:c0 + n])

    def shift(p, c0):
        n = p.shape[1]
        prev = pltpu.roll(p, 1, 0)
        if t_seq >= tm:
            tiles = t_seq // tm
            pos = lax.rem(i, tiles)
            sidx = lax.div(i, tiles)
            row0 = jnp.where(pos == 0, fp_ref[sidx, :, c0:c0 + n], carry_ref[:, c0:c0 + n])
            prev = jnp.where(rows == 0, row0, prev)
            carry_ref[:, c0:c0 + n] = p[tm - 1:tm, :]

            @pl.when(pos == tiles - 1)
            def _():
                sh_ref[sidx, :, c0:c0 + n] = p[tm - 1:tm, :]
        else:
            per_tile = tm // t_seq
            for j in range(per_tile):
                prev = jnp.where(rows == j * t_seq, fp_ref[i * per_tile + j, :, c0:c0 + n], prev)
                sh_ref[i * per_tile + j, :, c0:c0 + n] = p[(j + 1) * t_seq - 1:(j + 1) * t_seq, :]
        return p + (prev - p) * mu_ref[:, c0:c0 + n]

    p_w = proj(3 * RW, 128)
    p_a = proj(3 * RW + 128, 128)
    p_g = proj(3 * RW + 256, 256)
    p_k = proj(RW, RW)
    xw = shift(p_w, 3 * RW)
    xa = shift(p_a, 3 * RW + 128)
    xg = shift(p_g, 3 * RW + 256)
    wl = w0_ref[...] + _dot(jnp.tanh(xw).astype(BF16), w2_ref[...])
    alr = _sigmoid(a0_ref[...] + _dot(xa.astype(BF16), a2_ref[...]))
    g_ref[...] = _dot(_sigmoid(xg).astype(BF16), g2_ref[...])
    k = shift(p_k, RW)
    kk = k * kk_ref[...]
    kk_ss = _head_sum(kk * kk)
    p_r = proj(0, RW)
    lw = -EXP_NEG_HALF * _sigmoid(wl)
    rr = _iota((tm, tm), 0)
    cc = _iota((tm, tm), 1)
    same_chunk_tri = jnp.where(rr // chunk == cc // chunk, jnp.where(rr >= cc, 1.0, 0.0), 0.0).astype(BF16)
    lwc = _dot01(same_chunk_tri, lw)
    p_v = proj(2 * RW, RW)
    kkn = kk / jnp.maximum(jnp.sqrt(kk_ss), 1e-12)
    kh = k * (1.0 + (alr - 1.0) * ka_ref[...])
    e_in = jnp.exp(lwc)
    e_out = jnp.exp(-lwc)
    at_ref[...] = (-kkn * jnp.exp(lwc - lw)).astype(BF16)
    bt_ref[...] = (kkn * alr * e_out).astype(BF16)
    kt_ref[...] = (kh * e_out).astype(BF16)
    for c in range(tm // chunk):
        wtot_ref[c] = e_in[(c + 1) * chunk - 1:(c + 1) * chunk, :]
    r = shift(p_r, 0)
    rt_ref[...] = (r * e_in).astype(BF16)
    rkk = _head_sum(r * kh * rk_ref[...])
    v = shift(p_v, 2 * RW)
    vb_ref[...] = v.astype(BF16)
    bonus_ref[...] = (rkk * v).astype(BF16)


def _rwkv_proj(x, w, mu, fprev, w0, w2p, a0, a2p, g2p, k_k, k_a, r_k, *, tm, t_seq, chunk):
    m, d = x.shape
    n_seq = m // t_seq
    row = lambda i: (i, 0)
    big = lambda dt: jax.ShapeDtypeStruct((m, RW), dt)
    return pl.pallas_call(
        functools.partial(_rwkv_proj_kernel, tm=tm, t_seq=t_seq, chunk=chunk),
        grid=(m // tm,),
        in_specs=[pl.BlockSpec((tm, d), row),
                  _const_spec(w.shape), _const_spec(mu.shape), _const_spec(fprev.shape),
                  _const_spec(w0.shape), _const_spec(w2p.shape), _const_spec(a0.shape), _const_spec(a2p.shape),
                  _const_spec(g2p.shape), _const_spec(k_k.shape), _const_spec(k_a.shape), _const_spec(r_k.shape)],
        out_specs=[pl.BlockSpec((tm, RW), row)] * 7
                  + [pl.BlockSpec((tm // chunk, 1, RW), lambda i: (i, 0, 0)),
                     pl.BlockSpec((n_seq, 1, RP_PAD), lambda i: (0, 0, 0))],
        out_shape=[big(BF16)] * 6 + [big(F32), jax.ShapeDtypeStruct((m // chunk, 1, RW), F32),
                                     jax.ShapeDtypeStruct((n_seq, 1, RP_PAD), F32)],
        scratch_shapes=[pltpu.VMEM((1, RP_PAD), F32)],
        compiler_params=_params(("arbitrary",)),
        name="rwkv_proj",
    )(x, w, mu, fprev, w0, w2p, a0, a2p, g2p, k_k, k_a, r_k)


FW = 1024
FOX_HEADS = FW // HEAD_DIM


def _fox_proj_kernel(x_ref, w_ref, bf_ref, og_ref,
                     qb_ref, k_ref, v_ref, kb_ref, vb_ref, vbt_ref, gate_ref, lf_ref, caug_ref, carry_ref,
                     *, tm, t_seq):
    i = pl.program_id(0)
    xb = x_ref[...].astype(BF16)
    qb_ref[...] = (_dot(xb, w_ref[:, 0:FW]) * (ATTN_SCALE * LOG2E)).astype(BF16)
    k = _dot(xb, w_ref[:, FW:2 * FW])
    k_ref[...] = k
    kb_ref[...] = k.astype(BF16)
    v = _dot(xb, w_ref[:, 2 * FW:3 * FW])
    v_ref[...] = v
    vb_ref[...] = v.astype(BF16)
    vbt_ref[0] = v.T.astype(BF16)
    og = _dot(xb, w_ref[:, 3 * FW:4 * FW])
    gate_ref[...] = _sigmoid(og) * og_ref[...]
    logf = _log_sigmoid(_dot(xb, w_ref[:, 4 * FW:4 * FW + LANES]) + bf_ref[...])
    lf_ref[...] = logf[:, :FOX_HEADS]
    r = _iota((tm, tm), 0)
    c = _iota((tm, tm), 1)
    if t_seq >= tm:
        cs = _dot01(jnp.where(r >= c, 1.0, 0.0).astype(BF16), logf)
        cs = cs + jnp.where(lax.rem(i, t_seq // tm) == 0, 0.0, carry_ref[...])
        carry_ref[...] = cs[tm - 1:tm, :]
    else:
        same_seq = (r // t_seq) == (c // t_seq)
        cs = _dot01(jnp.where(same_seq, jnp.where(r >= c, 1.0, 0.0), 0.0).astype(BF16), logf)
    xs = -LOG2E * cs
    hi, mid, lo = _split3(xs)
    lane = _iota((tm, LANES), 1)
    zero = jnp.zeros_like(hi)
    caug_ref[...] = jnp.where(lane < FOX_HEADS, hi,
                              jnp.where(lane < 2 * FOX_HEADS, mid, jnp.where(lane < 3 * FOX_HEADS, lo, zero)))


def _fox_proj(x, w, bf_row, og_g, *, tm, t_seq):
    m, d = x.shape
    row = lambda i: (i, 0)
    big = lambda dt: jax.ShapeDtypeStruct((m, FW), dt)
    return pl.pallas_call(
        functools.partial(_fox_proj_kernel, tm=tm, t_seq=t_seq),
        grid=(m // tm,),
        in_specs=[pl.BlockSpec((tm, d), row), _const_spec(w.shape), _const_spec(bf_row.shape),
                  _const_spec(og_g.shape)],
        out_specs=[pl.BlockSpec((tm, FW), row)] * 5
                  + [pl.BlockSpec((1, FW, tm), lambda i: (i, 0, 0)), pl.BlockSpec((tm, FW), row),
                     pl.BlockSpec((tm, FOX_HEADS), row), pl.BlockSpec((tm, LANES), row)],
        out_shape=[big(BF16), big(F32), big(F32), big(BF16), big(BF16),
                   jax.ShapeDtypeStruct((m // tm, FW, tm), BF16), big(F32),
                   jax.ShapeDtypeStruct((m, FOX_HEADS), F32), jax.ShapeDtypeStruct((m, LANES), BF16)],
        scratch_shapes=[pltpu.VMEM((1, LANES), F32)],
        compiler_params=_params(("arbitrary",)),
        name="fox_proj",
    )(x, w, bf_row, og_g)


def _stack_heads(x, mask0):
    return jnp.concatenate([jnp.where(mask0, x, 0.0), jnp.where(mask0, 0.0, x)], axis=0)


def _rwkv_scan_kernel(at_ref, rt_ref, bt_ref, kt_ref, vb_ref, wtot_ref, bonus_ref, g_ref, s0_ref, lng_ref, lnb_ref,
                      y_ref, sout_ref, state_ref, *, chunk, n_steps):
    step = pl.program_id(1)
    n_pairs = at_ref.shape[1] // LANES
    C = chunk
    n_ch = at_ref.shape[0] // C
    pairs = range(n_pairs)

    @pl.when(step == 0)
    def _():
        zero = jnp.zeros((HEAD_DIM, HEAD_DIM), F32)
        for p in pairs:
            top = jnp.concatenate([s0_ref[0, 2 * p], zero], axis=1)
            bot = jnp.concatenate([zero, s0_ref[0, 2 * p + 1]], axis=1)
            state_ref[p] = jnp.concatenate([top, bot], axis=0)

    fmask0 = _iota((1, LANES), 1) < HEAD_DIM
    tcol = _iota((C, 2 * C), 1)
    trow = _iota((C, 2 * C), 0)
    tmask0 = tcol < C
    tj = jnp.where(tmask0, tcol, tcol - C)
    strict = tj < trow
    incl = tj <= trow
    eye_pair = jnp.where(tj == trow, 1.0, 0.0)
    blk = (_iota((LANES, LANES), 0) // HEAD_DIM) == (_iota((LANES, LANES), 1) // HEAD_DIM)

    def stack_t(x):
        return _stack_heads(x, tmask0)

    def tile(ref, c, p):
        return ref[c * C:(c + 1) * C, p * LANES:(p + 1) * LANES]

    probs = [(c, p) for c in range(n_ch) for p in pairs]
    at = {cp: tile(at_ref, *cp) for cp in probs}
    rt = {cp: tile(rt_ref, *cp) for cp in probs}
    bt = {cp: tile(bt_ref, *cp) for cp in probs}
    kt = {cp: tile(kt_ref, *cp) for cp in probs}
    vb = {cp: tile(vb_ref, *cp) for cp in probs}
    gm = {cp: _dg(jnp.concatenate([at[cp], rt[cp]], axis=0),
                  jnp.concatenate([_stack_heads(bt[cp], fmask0), _stack_heads(kt[cp], fmask0)], axis=0), NT)
          for cp in probs}
    lab = {cp: jnp.where(strict, gm[cp][:C, :2 * C], 0.0) for cp in probs}
    lak_b = {cp: jnp.where(strict, gm[cp][:C, 2 * C:], 0.0).astype(BF16) for cp in probs}
    mr_b = {cp: jnp.concatenate([jnp.where(incl, gm[cp][C:, :2 * C], 0.0),
                                 jnp.where(incl, gm[cp][C:, 2 * C:], 0.0)], axis=1).astype(BF16) for cp in probs}
    kmax = int(math.log2(C)) - 1
    tinv = {cp: eye_pair + lab[cp] for cp in probs}
    pw = {cp: lab[cp].astype(BF16) for cp in probs}
    pw = {cp: _dot(pw[cp], stack_t(pw[cp])).astype(BF16) for cp in probs}
    for _ in range(1, kmax):
        res = {cp: _dot(jnp.concatenate([pw[cp], tinv[cp].astype(BF16)], axis=0), stack_t(pw[cp])) for cp in probs}
        pw = {cp: res[cp][:C].astype(BF16) for cp in probs}
        tinv = {cp: tinv[cp] + res[cp][C:] for cp in probs}
    tinv_b = {cp: (tinv[cp] + _dot(tinv[cp].astype(BF16), stack_t(pw[cp]))).astype(BF16) for cp in probs}

    sd = [state_ref[p] for p in pairs]
    ys = []
    for c in range(n_ch):
        sd_b = [z.astype(BF16) for z in sd]
        vd_b = [_stack_heads(vb[c, p], fmask0) for p in pairs]
        x = [_dg(at[c, p], sd_b[p], NT) + _dot(lak_b[c, p], vd_b[p]) for p in pairs]
        u_b = [_dot(tinv_b[c, p], _stack_heads(x[p].astype(BF16), fmask0)).astype(BF16) for p in pairs]
        ys.append(jnp.concatenate(
            [_dg(rt[c, p], sd_b[p], NT)
             + _dot(mr_b[c, p], jnp.concatenate([_stack_heads(u_b[p], fmask0), vd_b[p]], axis=0)) for p in pairs],
            axis=1))
        w_tot = wtot_ref[c]
        for p in pairs:
            ds = _dg(jnp.concatenate([u_b[p], vb[c, p]], axis=0), jnp.concatenate([bt[c, p], kt[c, p]], axis=0), TN)
            sd[p] = (sd[p] + jnp.where(blk, ds, 0.0)) * w_tot[:, p * LANES:(p + 1) * LANES]
    for p in pairs:
        state_ref[p] = sd[p]

    y = ys[0] if n_ch == 1 else jnp.concatenate(ys, axis=0)
    inv_n = 1.0 / HEAD_DIM
    mu = _head_sum(y) * inv_n
    yc = y - mu
    var = _head_sum(yc * yc) * inv_n
    yn = yc * lax.rsqrt(var + GN_EPS) * lng_ref[...] + lnb_ref[...]
    y_ref[...] = ((yn + bonus_ref[...].astype(F32)) * g_ref[...]).astype(y_ref.dtype)

    @pl.when(step == n_steps - 1)
    def _():
        for p in pairs:
            sout_ref[0, 2 * p] = sd[p][:HEAD_DIM, :HEAD_DIM]
            sout_ref[0, 2 * p + 1] = sd[p][HEAD_DIM:, HEAD_DIM:]


def _rwkv_scan(at, rt, bt, kt, vb, wtot, bonus, g, s0, lng, lnb, *, chunk, rows, t_seq):
    m, w = at.shape
    n_b = m // t_seq
    n_steps = t_seq // rows
    n_heads = w // HEAD_DIM
    blk = pl.BlockSpec((rows, w), lambda bi, si: (bi * n_steps + si, 0))
    wt = pl.BlockSpec((rows // chunk, 1, w), lambda bi, si: (bi * n_steps + si, 0, 0))
    st = pl.BlockSpec((1, n_heads, HEAD_DIM, HEAD_DIM), lambda bi, si: (bi, 0, 0, 0))
    return pl.pallas_call(
        functools.partial(_rwkv_scan_kernel, chunk=chunk, n_steps=n_steps),
        grid=(n_b, n_steps),
        in_specs=[blk] * 5 + [wt, blk, blk, st, _const_spec(lng.shape), _const_spec(lnb.shape)],
        out_specs=[blk, st],
        out_shape=[jax.ShapeDtypeStruct((m, w), BF16),
                   jax.ShapeDtypeStruct((n_b, n_heads, HEAD_DIM, HEAD_DIM), F32)],
        scratch_shapes=[pltpu.VMEM((w // LANES, LANES, LANES), F32)],
        compiler_params=_params(("arbitrary", "arbitrary")),
        name="rwkv_scan",
    )(at, rt, bt, kt, vb, wtot, bonus, g, s0, lng, lnb)


def _rms_gate(o0, o1, gate, lane):
    first = lane < HEAD_DIM
    o = jnp.where(first, o0, o1)
    sq = o * o
    ms0 = jnp.sum(jnp.where(first, sq, 0.0), axis=-1, keepdims=True)
    ms1 = jnp.sum(jnp.where(first, 0.0, sq), axis=-1, keepdims=True)
    ms = jnp.where(first, ms0, ms1) * (1.0 / HEAD_DIM)
    return o * lax.rsqrt(ms + RMS_EPS) * gate


def _lane_cumsum(x, block):
    tri = jnp.where(_iota((block, block), 0) <= _iota((block, block), 1), 1.0, 0.0)
    carry = jnp.zeros((x.shape[0], 1), F32)
    outs = []
    for j in range(x.shape[1] // block):
        c = _dot(x[:, j * block:(j + 1) * block], tri, HI) + carry
        outs.append(c)
        carry = c[:, block - 1:block]
    return outs[0] if len(outs) == 1 else jnp.concatenate(outs, axis=1)


def _fox_prompt_kernel(q_ref, k_ref, caug_ref, vt_ref, gate_ref, y_ref,
                       s0_ref, s1_ref, m_ref, l_ref, acc_ref, *, tq):
    hp = pl.program_id(1)
    qi = pl.program_id(2)
    lane = _iota((tq, LANES), 1)
    first = lane < HEAD_DIM
    q = q_ref[...]
    zero = jnp.zeros_like(q)
    hslot = jnp.where(lane < 3 * FOX_HEADS, lax.rem(lane, FOX_HEADS), -1)
    qsa = []
    for h in range(2):
        ones = jnp.where(hslot == 2 * hp + h, 1.0, 0.0).astype(BF16)
        qh = jnp.where(first, q, zero) if h == 0 else jnp.where(first, zero, q)
        qsa.append(jnp.concatenate([qh, ones], axis=1))
    qsa = jnp.concatenate(qsa, axis=0)

    def scores(j):
        k0 = pl.multiple_of(j * tq, tq)
        kk = jnp.concatenate([k_ref[pl.ds(k0, tq), :], caug_ref[pl.ds(k0, tq), :]], axis=1)
        return _dg(kk, qsa, NT)

    def block(j, st_ref, masked):
        nv = tq // vt_ref.shape[2]
        vt = jnp.concatenate([vt_ref[nv * j + i] for i in range(nv)], axis=1)
        for h in range(2):
            m_old = m_ref[h]
            sh = st_ref[:, h * tq:(h + 1) * tq]
            if masked:
                sh = jnp.where(_iota((tq, tq), 0) <= _iota((tq, tq), 1), sh, NEG_BIG)
            m_new = jnp.maximum(m_old, jnp.max(sh, axis=0, keepdims=True))
            alpha = jnp.exp2(m_old - m_new)
            p = jnp.exp2(sh - m_new)
            m_ref[h] = m_new
            l_ref[h] = alpha * l_ref[h] + jnp.sum(p, axis=0, keepdims=True)
            acc_ref[h] = alpha * acc_ref[h] + _dot(vt[h * HEAD_DIM:(h + 1) * HEAD_DIM], p.astype(BF16))

    m_ref[...] = jnp.full(m_ref.shape, NEG_BIG, F32)
    l_ref[...] = jnp.zeros(l_ref.shape, F32)
    acc_ref[...] = jnp.zeros(acc_ref.shape, F32)
    s0_ref[...] = scores(0)

    def two_blocks(jj, _):
        j = 2 * jj
        s1_ref[...] = scores(j + 1)
        block(j, s0_ref, False)
        s0_ref[...] = scores(j + 2)
        block(j + 1, s1_ref, False)
        return _

    lax.fori_loop(0, lax.div(qi, 2), two_blocks, 0)

    @pl.when(lax.rem(qi, 2) == 0)
    def _():
        block(qi, s0_ref, True)

    @pl.when(lax.rem(qi, 2) == 1)
    def _():
        s1_ref[...] = scores(qi)
        block(qi - 1, s0_ref, False)
        block(qi, s1_ref, True)

    yts = []
    for h in range(2):
        o = acc_ref[h] * (1.0 / l_ref[h])
        ms = jnp.mean(o * o, axis=0, keepdims=True)
        yts.append(o * lax.rsqrt(ms + RMS_EPS))
    yt = jnp.concatenate(yts, axis=0)
    y_ref[...] = (yt.T * gate_ref[...]).astype(y_ref.dtype)


def _fox_prompt_attn(qb, kb, caug, vbt, gate, *, n_b, t_seq, tq):
    m, w = qb.shape
    n_pairs = w // LANES
    nq = t_seq // tq
    vblk = vbt.shape[2]
    assert vbt.shape == (m // vblk, w, vblk) and tq % vblk == 0
    qspec = pl.BlockSpec((tq, LANES), lambda b, hp, qi: (b * nq + qi, hp))
    return pl.pallas_call(
        functools.partial(_fox_prompt_kernel, tq=tq),
        grid=(n_b, n_pairs, nq),
        in_specs=[qspec,
                  pl.BlockSpec((t_seq, LANES), lambda b, hp, qi: (b, hp)),
                  pl.BlockSpec((t_seq, LANES), lambda b, hp, qi: (b, 0)),
                  pl.BlockSpec((t_seq // vblk, LANES, vblk), lambda b, hp, qi: (b, hp, 0)),
                  qspec],
        out_specs=qspec,
        out_shape=jax.ShapeDtypeStruct((m, w), BF16),
        scratch_shapes=[pltpu.VMEM((tq, 2 * tq), F32), pltpu.VMEM((tq, 2 * tq), F32),
                        pltpu.VMEM((2, 1, tq), F32), pltpu.VMEM((2, 1, tq), F32),
                        pltpu.VMEM((2, HEAD_DIM, tq), F32)],
        compiler_params=_params(("arbitrary", "arbitrary", "arbitrary")),
        name="fox_prompt_attn",
    )(qb, kb, caug, vbt, gate)


def _fox_sample_kernel(q_ref, kn_ref, vn_ref, ck_ref, cv_ref, clf_ref, lf_ref, gate_ref, y_ref):
    t = q_ref.shape[0]
    n_heads = lf_ref.shape[1]
    eye_h = jnp.where(_iota((n_heads, n_heads), 0) == _iota((n_heads, n_heads), 1), 1.0, 0.0)
    tril = jnp.where(_iota((t, t), 0) >= _iota((t, t), 1), 1.0, 0.0)
    clf = clf_ref[0]
    c_tot = jnp.sum(clf, axis=0, keepdims=True)
    cn_col = _dot(tril, lf_ref[...], HI)
    cn_row = _dg(eye_h, cn_col, NT, HI)
    cq_col = cn_col + c_tot
    cc_row = _lane_cumsum(_dg(eye_h, clf, NT, HI), 256)
    lane = _iota((t, LANES), 1)
    first = lane < HEAD_DIM
    causal = _iota((t, t), 1) <= _iota((t, t), 0)
    ys = []
    for p in range(q_ref.shape[1] // LANES):
        sl = slice(p * LANES, (p + 1) * LANES)
        q = q_ref[:, sl]
        zero = jnp.zeros_like(q)
        qs = jnp.concatenate([jnp.where(first, q, zero), jnp.where(first, zero, q)], axis=0)
        kc = ck_ref[0, :, sl].astype(BF16)
        vc = cv_ref[0, :, sl].astype(BF16)
        kn = kn_ref[:, sl]
        vn = vn_ref[:, sl]
        s_c = _dg(qs, kc, NT)
        s_n = _dg(qs, kn, NT)
        o = []
        for h in range(2):
            hd = 2 * p + h
            sc = s_c[h * t:(h + 1) * t] + (cq_col[:, hd:hd + 1] - cc_row[hd:hd + 1]) * LOG2E
            sn = s_n[h * t:(h + 1) * t] + (cn_col[:, hd:hd + 1] - cn_row[hd:hd + 1]) * LOG2E
            sn = jnp.where(causal, sn, NEG_BIG)
            mx = jnp.maximum(jnp.max(sc, axis=-1, keepdims=True), jnp.max(sn, axis=-1, keepdims=True))
            pc = jnp.exp2(sc - mx)
            pn = jnp.exp2(sn - mx)
            den = jnp.sum(pc, axis=-1, keepdims=True) + jnp.sum(pn, axis=-1, keepdims=True)
            o.append((_dot(pc.astype(BF16), vc) + _dot(pn.astype(BF16), vn)) / den)
        ys.append(_rms_gate(o[0], o[1], gate_ref[:, sl], lane))
    y_ref[...] = jnp.concatenate(ys, axis=1).astype(y_ref.dtype)


def _fox_sample_attn(qb, kb, vb, cache_k, cache_v, cache_lf, lf, gate, *, t_seq):
    m, w = qb.shape
    n_b = m // t_seq
    past = cache_k.shape[1]
    row = pl.BlockSpec((t_seq, w), lambda b: (b, 0))
    cache = pl.BlockSpec((1, past, w), lambda b: (b, 0, 0))
    return pl.pallas_call(
        _fox_sample_kernel,
        grid=(n_b,),
        in_specs=[row, row, row, cache, cache,
                  pl.BlockSpec((1, past, lf.shape[1]), lambda b: (b, 0, 0)),
                  pl.BlockSpec((t_seq, lf.shape[1]), lambda b: (b, 0)), row],
        out_specs=row,
        out_shape=jax.ShapeDtypeStruct((m, w), BF16),
        compiler_params=_params(("arbitrary",)),
        name="fox_sample_attn",
    )(qb, kb, vb, cache_k, cache_v, cache_lf, lf, gate)


def _out_ln_kernel(yr_ref, yf_ref, x_ref, wo_ref, g_ref, b_ref, *rest, alpha, n_cast):
    cast_in, h_ref, cast_out, wob_ref = rest[:n_cast], rest[n_cast], rest[n_cast + 1:-1], rest[-1]

    @pl.when(pl.program_id(0) == 0)
    def _():
        wob_ref[...] = wo_ref[...].astype(BF16)

    half = yr_ref.shape[1]
    mix = _dot(yr_ref[...], wob_ref[0:half, :]) + _dot(yf_ref[...], wob_ref[half:, :])
    h_ref[...] = _layer_norm(alpha * x_ref[...] + mix, g_ref[...], b_ref[...])
    for src, dst in zip(cast_in, cast_out):
        dst[...] = src[...].astype(BF16)


def _out_ln(yr, yf, x, wo, g, b, *, tm, alpha, cast_cols=(), cast_rows=()):
    m, d = x.shape
    n = m // tm
    row = lambda i: (i, 0)
    slabs = ([(a, pl.BlockSpec((a.shape[0], a.shape[1] // n), lambda i: (0, i))) for a in cast_cols]
             + [(a, pl.BlockSpec((a.shape[0] // n, a.shape[1]), lambda i: (i, 0))) for a in cast_rows])
    outs = pl.pallas_call(
        functools.partial(_out_ln_kernel, alpha=alpha, n_cast=len(slabs)),
        grid=(n,),
        in_specs=[pl.BlockSpec((tm, yr.shape[1]), row), pl.BlockSpec((tm, yf.shape[1]), row),
                  pl.BlockSpec((tm, d), row), _const_spec(wo.shape), _const_spec(g.shape), _const_spec(b.shape)]
                 + [spec for _, spec in slabs],
        out_specs=[pl.BlockSpec((tm, d), row)] + [spec for _, spec in slabs],
        out_shape=[jax.ShapeDtypeStruct((m, d), F32)] + [jax.ShapeDtypeStruct(a.shape, BF16) for a, _ in slabs],
        scratch_shapes=[pltpu.VMEM(wo.shape, BF16)],
        compiler_params=_params(("arbitrary",)),
        name="out_ln",
    )(yr, yf, x, wo, g, b, *[a for a, _ in slabs])
    return outs[0], outs[1:]


def _ffn_ln_kernel(h_ref, wu_ref, wd_ref, g_ref, b_ref, k2_ref, v2_ref, o_ref, k5_ref, v5_ref, hb_ref,
                   *, alpha, n_f):
    j = pl.program_id(1)

    @pl.when(j == 0)
    def _():
        h = h_ref[...]
        hb_ref[...] = h.astype(BF16)
        o_ref[...] = alpha * h

    u = jnp.maximum(_dot(hb_ref[...], wu_ref[...]), 0.0)
    o_ref[...] += _dot((u * u).astype(BF16), wd_ref[...])
    for src, dst in ((k2_ref, k5_ref), (v2_ref, v5_ref)):
        for hd in range(dst.shape[1]):
            dst[:, hd, :] = src[:, hd * HEAD_DIM:(hd + 1) * HEAD_DIM]

    @pl.when(j == n_f - 1)
    def _():
        o_ref[...] = _layer_norm(o_ref[...], g_ref[...], b_ref[...])


def _ffn_ln(h, wu, wd, g, b, k2, v2, *, tm, tf, alpha):
    m, d = h.shape
    n_f = wu.shape[1] // tf
    rows = tm // n_f
    heads = k2.shape[1] // HEAD_DIM
    slab_in = pl.BlockSpec((rows, k2.shape[1]), lambda i, j: (i * n_f + j, 0))
    slab_out = pl.BlockSpec((rows, heads, HEAD_DIM), lambda i, j: (i * n_f + j, 0, 0))
    kv5 = jax.ShapeDtypeStruct((m, heads, HEAD_DIM), F32)
    return pl.pallas_call(
        functools.partial(_ffn_ln_kernel, alpha=alpha, n_f=n_f),
        grid=(m // tm, n_f),
        in_specs=[pl.BlockSpec((tm, d), lambda i, j: (i, 0), pipeline_mode=pl.Buffered(1)),
                  pl.BlockSpec((d, tf), lambda i, j: (0, j)),
                  pl.BlockSpec((tf, d), lambda i, j: (j, 0)),
                  _const_spec(g.shape), _const_spec(b.shape), slab_in, slab_in],
        out_specs=[pl.BlockSpec((tm, d), lambda i, j: (i, 0)), slab_out, slab_out],
        out_shape=[jax.ShapeDtypeStruct((m, d), F32), kv5, kv5],
        scratch_shapes=[pltpu.VMEM((tm, d), BF16)],
        compiler_params=_params(("arbitrary", "arbitrary")),
        name="ffn_ln",
    )(h, wu, wd, g, b, k2, v2)


def _pad_cols(x, n):
    return jnp.pad(x, [(0, 0)] * (x.ndim - 1) + [(0, n - x.shape[-1])])


def _rwkv_cols(x, lora):
    o = 3 * RW
    dl, al = lora
    return jnp.concatenate([x[..., :o], _pad_cols(x[..., o:o + dl], 128), _pad_cols(x[..., o + dl:o + dl + al], 128),
                            _pad_cols(x[..., o + dl + al:], 256)], axis=-1)


def _rwkv_cols_inv(x, lora):
    o = 3 * RW
    dl, al, gl = lora
    return jnp.concatenate([x[..., :o], x[..., o:o + dl], x[..., o + 128:o + 128 + al],
                            x[..., o + 256:o + 256 + gl]], axis=-1)


def _stream(x, shift_prev, s0, wts, *, t_seq, tm, tm_ffn, chunk, cache=None):
    n_b, _, d = x.shape
    m = n_b * t_seq
    x2 = x.reshape(m, d)
    lora = wts["lora"]
    fprev = _rwkv_cols(shift_prev, lora[:2])
    at, rt, bt, kt, vr, bonus, g, wtot, sh = _rwkv_proj(
        x2, wts["w_r"], wts["mu"], fprev, wts["w0"], wts["w2p"], wts["a0"], wts["a2p"], wts["g2p"], wts["k_k"],
        wts["k_a"], wts["r_k"], tm=tm, t_seq=t_seq, chunk=chunk)
    y_r, s_new = _rwkv_scan(at, rt, bt, kt, vr, wtot, bonus, g, s0, wts["lnx_g"], wts["lnx_b"],
                            chunk=chunk, rows=min(tm, t_seq), t_seq=t_seq)
    qb, k32, v32, kb, vb, vbt, gate, lf, caug = _fox_proj(x2, wts["w_f"], wts["bf_row"], wts["og_g"],
                                                          tm=tm, t_seq=t_seq)
    if cache is None:
        y_f = _fox_prompt_attn(qb, kb, caug, vbt, gate, n_b=n_b, t_seq=t_seq, tq=512)
    else:
        ck, cv, clf = cache
        y_f = _fox_sample_attn(qb, kb, vb, ck.reshape(n_b, ck.shape[1], FW), cv.reshape(n_b, cv.shape[1], FW),
                               clf, lf, gate, t_seq=t_seq)
    if "ffn_bf16" in wts:
        h, _ = _out_ln(y_r, y_f, x2, wts["w_o"], wts["ln1_g"], wts["ln1_b"], tm=tm, alpha=wts["alpha"])
    else:
        h, wts["ffn_bf16"] = _out_ln(y_r, y_f, x2, wts["w_o"], wts["ln1_g"], wts["ln1_b"], tm=tm, alpha=wts["alpha"],
                                     cast_cols=(wts["w_up"],), cast_rows=(wts["w_down"],))
    w_up_b, w_down_b = wts["ffn_bf16"]
    y, k5, v5 = _ffn_ln(h, w_up_b, w_down_b, wts["ln2_g"], wts["ln2_b"], k32, v32,
                        tm=tm_ffn, tf=1024, alpha=wts["alpha"])
    heads = FW // HEAD_DIM
    return (y.reshape(n_b, t_seq, d), k5.reshape(n_b, t_seq, heads, HEAD_DIM),
            v5.reshape(n_b, t_seq, heads, HEAD_DIM), lf.reshape(n_b, t_seq, heads), s_new,
            _rwkv_cols_inv(sh, lora))


def kernel(x_prompt, x_sample, cache_fox_k, cache_fox_v, cache_fox_logf, state_rwkv_wkv, state_rwkv_shift,
           w_in, rwkv_mu, rwkv_w0, rwkv_w2, rwkv_a0, rwkv_a2, rwkv_g2, rwkv_k_k, rwkv_k_a, rwkv_r_k,
           rwkv_lnx_g, rwkv_lnx_b, fox_b_f, fox_out_g, w_o, ln1_g, ln1_b, w_up, w_down, ln2_g, ln2_b):
    depth = w_in.shape[0]
    assert depth == 1, "single-layer problem"
    d_model = x_prompt.shape[-1]
    rwkv_proj = rwkv_mu.shape[-1]
    lora = (rwkv_w2.shape[1], rwkv_a2.shape[1], rwkv_g2.shape[1])
    assert rwkv_w0.shape[-1] == RW and fox_out_g.shape[-1] == FW and rwkv_proj == 3 * RW + sum(lora)
    alpha = (2 * depth) ** 0.25
    l = 0
    w = w_in[l]
    fo = rwkv_proj
    row = lambda z: z.reshape(1, -1)
    pad_rows = lambda z, n: jnp.pad(z, ((0, n - z.shape[0]), (0, 0)))
    wts = dict(
        lora=lora, alpha=alpha,
        w_r=_rwkv_cols(w[:, :rwkv_proj], lora[:2]).astype(BF16),
        w_f=jnp.concatenate([w[:, fo:fo + 3 * FW], w[:, fo + 3 * FW + FOX_HEADS:],
                             _pad_cols(jnp.tile(w[:, fo + 3 * FW:fo + 3 * FW + FOX_HEADS], (1, 3)), LANES)],
                            axis=-1).astype(BF16),
        mu=_rwkv_cols(row(rwkv_mu[l]), lora[:2]),
        w0=row(rwkv_w0[l]), w2p=pad_rows(rwkv_w2[l], 128).astype(BF16),
        a0=row(rwkv_a0[l]), a2p=pad_rows(rwkv_a2[l], 128).astype(BF16),
        g2p=pad_rows(rwkv_g2[l], 256).astype(BF16),
        k_k=row(rwkv_k_k[l]), k_a=row(rwkv_k_a[l]), r_k=row(rwkv_r_k[l]),
        lnx_g=row(rwkv_lnx_g[l]), lnx_b=row(rwkv_lnx_b[l]),
        bf_row=_pad_cols(jnp.tile(row(fox_b_f[l]), (1, 3)), LANES), og_g=row(fox_out_g[l]),
        w_o=w_o[l], ln1_g=row(ln1_g[l]), ln1_b=row(ln1_b[l]),
        w_up=w_up[l], w_down=w_down[l], ln2_g=row(ln2_g[l]), ln2_b=row(ln2_b[l]),
    )
    n_p, t_p, _ = x_prompt.shape
    n_s, t_s, _ = x_sample.shape
    heads = RW // HEAD_DIM
    shift0 = jnp.zeros((n_p, 1, rwkv_proj), F32)
    s_zero = jnp.zeros((n_p, heads, HEAD_DIM, HEAD_DIM), F32)
    yp, kp, vp, fp, sp, shp = _stream(x_prompt, shift0, s_zero, wts, t_seq=t_p, tm=256, tm_ffn=1024, chunk=64)
    ys, ks, vs, fs, ss, shs = _stream(x_sample, state_rwkv_shift[l], state_rwkv_wkv[l], wts, t_seq=t_s,
                                      tm=n_s * t_s, tm_ffn=n_s * t_s, chunk=t_s,
                                      cache=(cache_fox_k[l], cache_fox_v[l], cache_fox_logf[l]))
    return (yp, ys, kp[None], vp[None], fp[None], sp[None], shp[None],
            ks[None], vs[None], fs[None], ss[None], shs[None])
```

```python
import functools
import math

import jax
import jax.numpy as jnp
from jax import lax
from jax.experimental import pallas as pl
from jax.experimental.pallas import tpu as pltpu

F32 = jnp.float32
BF16 = jnp.bfloat16
HI = lax.Precision.HIGHEST

HEAD_DIM = 64
LANES = 128
LN_EPS = 1e-5
GN_EPS = 64e-5
RMS_EPS = 1e-6
ATTN_SCALE = HEAD_DIM ** -0.5
EXP_NEG_HALF = math.exp(-0.5)
LOG2E = math.log2(math.e)
NEG_BIG = -1e30
VMEM_LIMIT = 60 * 1024 * 1024

NT = (((1,), (1,)), ((), ()))
TN = (((0,), (0,)), ((), ()))


def _sigmoid(x):
    return 1.0 / (1.0 + jnp.exp(-x))


def _log_sigmoid(x):
    return jnp.minimum(x, 0.0) - jnp.log1p(jnp.exp(-jnp.abs(x)))


def _dot(a, b, precision=None):
    return jnp.dot(a, b, preferred_element_type=F32, precision=precision)


def _dg(a, b, dims, precision=None):
    return lax.dot_general(a, b, dims, preferred_element_type=F32, precision=precision)


def _split3(x):
    hi = x.astype(BF16)
    rem = x - hi.astype(F32)
    mid = rem.astype(BF16)
    return hi, mid, (rem - mid.astype(F32)).astype(BF16)


def _dot01(m01, x):
    hi, mid, lo = _split3(x)
    return _dot(m01, hi) + _dot(m01, mid) + _dot(m01, lo)


def _iota(shape, axis):
    return lax.broadcasted_iota(jnp.int32, shape, axis)


def _head_sum(x):
    r = _iota((LANES, LANES), 0) // HEAD_DIM
    c = _iota((LANES, LANES), 1) // HEAD_DIM
    ones_blk = jnp.where(r == c, 1.0, 0.0).astype(BF16)
    outs = []
    for g in range(x.shape[1] // LANES):
        xs = x[:, g * LANES:(g + 1) * LANES]
        hi = xs.astype(BF16)
        lo = (xs - hi.astype(F32)).astype(BF16)
        outs.append(_dot(hi, ones_blk) + _dot(lo, ones_blk))
    return outs[0] if len(outs) == 1 else jnp.concatenate(outs, axis=1)


def _layer_norm(z, g, b):
    mu = jnp.mean(z, axis=-1, keepdims=True)
    zc = z - mu
    var = jnp.mean(zc * zc, axis=-1, keepdims=True)
    return zc * lax.rsqrt(var + LN_EPS) * g + b


def _const_spec(shape):
    nd = len(shape)
    return pl.BlockSpec(shape, lambda *_: (0,) * nd, pipeline_mode=pl.Buffered(1))


def _params(sem):
    return pltpu.CompilerParams(dimension_semantics=sem, vmem_limit_bytes=VMEM_LIMIT)


RW = 1024
RP_PAD = 3 * RW + 128 + 256


def _rwkv_proj_kernel(x_ref, w_ref, mu_ref, fp_ref, w0_ref, w2_ref, a0_ref, a2_ref, g2_ref, kk_ref, ka_ref, rk_ref,
                      at_ref, rt_ref, bt_ref, kt_ref, vb_ref, bonus_ref, g_ref, wtot_ref, sh_ref, carry_ref,
                      *, tm, t_seq, chunk):
    i = pl.program_id(0)
    xb = x_ref[...].astype(BF16)
    rows = _iota((tm, 1), 0)

    def proj(c0, n):
        return _dot(xb, w_ref[:, c0:c0 + n])

    def shift(p, c0):
        n = p.shape[1]
        prev = pltpu.roll(p, 1, 0)
        if t_seq >= tm:
            tiles = t_seq // tm
            pos = lax.rem(i, tiles)
            sidx = lax.div(i, tiles)
            row0 = jnp.where(pos == 0, fp_ref[sidx, :, c0:c0 + n], carry_ref[:, c0:c0 + n])
            prev = jnp.where(rows == 0, row0, prev)
            carry_ref[:, c0:c0 + n] = p[tm - 1:tm, :]

            @pl.when(pos == tiles - 1)
            def _():
                sh_ref[sidx, :, c0:c0 + n] = p[tm - 1:tm, :]
        else:
            per_tile = tm // t_seq
            for j in range(per_tile):
                prev = jnp.where(rows == j * t_seq, fp_ref[i * per_tile + j, :, c0:c0 + n], prev)
                sh_ref[i * per_tile + j, :, c0:c0 + n] = p[(j + 1) * t_seq - 1:(j + 1) * t_seq, :]
        return p + (prev - p) * mu_ref[:, c0:c0 + n]

    p_wa = proj(3 * RW, 128)
    p_g = proj(3 * RW + 128, 256)
    p_k = proj(RW, RW)
    xwa = shift(p_wa, 3 * RW)
    xg = shift(p_g, 3 * RW + 128)
    wl = w0_ref[...] + _dot(jnp.tanh(xwa).astype(BF16), w2_ref[...])
    alr = _sigmoid(a0_ref[...] + _dot(xwa.astype(BF16), a2_ref[...]))
    g_ref[...] = _dot(_sigmoid(xg).astype(BF16), g2_ref[...])
    k = shift(p_k, RW)
    kk = k * kk_ref[...]
    kk_ss = _head_sum(kk * kk)
    p_r = proj(0, RW)
    lw = -EXP_NEG_HALF * _sigmoid(wl)
    rr = _iota((tm, tm), 0)
    cc = _iota((tm, tm), 1)
    same_chunk_tri = jnp.where(rr // chunk == cc // chunk, jnp.where(rr >= cc, 1.0, 0.0), 0.0).astype(BF16)
    lwc = _dot01(same_chunk_tri, lw)
    p_v = proj(2 * RW, RW)
    kkn = kk / jnp.maximum(jnp.sqrt(kk_ss), 1e-12)
    kh = k * (1.0 + (alr - 1.0) * ka_ref[...])
    e_in = jnp.exp(lwc)
    e_out = jnp.exp(-lwc)
    at_ref[...] = (-kkn * jnp.exp(lwc - lw)).astype(BF16)
    bt_ref[...] = (kkn * alr * e_out).astype(BF16)
    kt_ref[...] = (kh * e_out).astype(BF16)
    for c in range(tm // chunk):
        wtot_ref[c] = e_in[(c + 1) * chunk - 1:(c + 1) * chunk, :]
    r = shift(p_r, 0)
    rt_ref[...] = (r * e_in).astype(BF16)
    rkk = _head_sum(r * kh * rk_ref[...])
    v = shift(p_v, 2 * RW)
    vb_ref[...] = v.astype(BF16)
    bonus_ref[...] = (rkk * v).astype(BF16)


def _rwkv_proj(x, w, mu, fprev, w0, w2p, a0, a2p, g2p, k_k, k_a, r_k, *, tm, t_seq, chunk):
    m, d = x.shape
    n_seq = m // t_seq
    row = lambda i: (i, 0)
    big = lambda dt: jax.ShapeDtypeStruct((m, RW), dt)
    return pl.pallas_call(
        functools.partial(_rwkv_proj_kernel, tm=tm, t_seq=t_seq, chunk=chunk),
        grid=(m // tm,),
        in_specs=[pl.BlockSpec((tm, d), row),
                  _const_spec(w.shape), _const_spec(mu.shape), _const_spec(fprev.shape),
                  _const_spec(w0.shape), _const_spec(w2p.shape), _const_spec(a0.shape), _const_spec(a2p.shape),
                  _const_spec(g2p.shape), _const_spec(k_k.shape), _const_spec(k_a.shape), _const_spec(r_k.shape)],
        out_specs=[pl.BlockSpec((tm, RW), row)] * 7
                  + [pl.BlockSpec((tm // chunk, 1, RW), lambda i: (i, 0, 0)),
                     pl.BlockSpec((n_seq, 1, RP_PAD), lambda i: (0, 0, 0))],
        out_shape=[big(BF16)] * 6 + [big(F32), jax.ShapeDtypeStruct((m // chunk, 1, RW), F32),
                                     jax.ShapeDtypeStruct((n_seq, 1, RP_PAD), F32)],
        scratch_shapes=[pltpu.VMEM((1, RP_PAD), F32)],
        compiler_params=_params(("arbitrary",)),
        name="rwkv_proj",
    )(x, w, mu, fprev, w0, w2p, a0, a2p, g2p, k_k, k_a, r_k)


FW = 1024
FOX_HEADS = FW // HEAD_DIM


def _fox_proj_kernel(x_ref, w_ref, bf_ref, og_ref,
                     qb_ref, k_ref, v_ref, kb_ref, vb_ref, vbt_ref, gate_ref, lf_ref, caug_ref, carry_ref,
                     *, tm, t_seq):
    i = pl.program_id(0)
    xb = x_ref[...].astype(BF16)
    qb_ref[...] = (_dot(xb, w_ref[:, 0:FW]) * (ATTN_SCALE * LOG2E)).astype(BF16)
    k = _dot(xb, w_ref[:, FW:2 * FW])
    k_ref[...] = k
    kb_ref[...] = k.astype(BF16)
    v = _dot(xb, w_ref[:, 2 * FW:3 * FW])
    v_ref[...] = v
    vb_ref[...] = v.astype(BF16)
    vbt_ref[0] = v.T.astype(BF16)
    og = _dot(xb, w_ref[:, 3 * FW:4 * FW])
    gate_ref[...] = _sigmoid(og) * og_ref[...]
    logf = _log_sigmoid(_dot(xb, w_ref[:, 4 * FW:4 * FW + LANES]) + bf_ref[...])
    lf_ref[...] = logf[:, :FOX_HEADS]
    r = _iota((tm, tm), 0)
    c = _iota((tm, tm), 1)
    if t_seq >= tm:
        cs = _dot01(jnp.where(r >= c, 1.0, 0.0).astype(BF16), logf)
        cs = cs + jnp.where(lax.rem(i, t_seq // tm) == 0, 0.0, carry_ref[...])
        carry_ref[...] = cs[tm - 1:tm, :]
    else:
        same_seq = (r // t_seq) == (c // t_seq)
        cs = _dot01(jnp.where(same_seq, jnp.where(r >= c, 1.0, 0.0), 0.0).astype(BF16), logf)
    xs = -LOG2E * cs
    hi, mid, lo = _split3(xs)
    lane = _iota((tm, LANES), 1)
    zero = jnp.zeros_like(hi)
    caug_ref[...] = jnp.where(lane < FOX_HEADS, hi,
                              jnp.where(lane < 2 * FOX_HEADS, mid, jnp.where(lane < 3 * FOX_HEADS, lo, zero)))


def _fox_proj(x, w, bf_row, og_g, *, tm, t_seq):
    m, d = x.shape
    row = lambda i: (i, 0)
    big = lambda dt: jax.ShapeDtypeStruct((m, FW), dt)
    return pl.pallas_call(
        functools.partial(_fox_proj_kernel, tm=tm, t_seq=t_seq),
        grid=(m // tm,),
        in_specs=[pl.BlockSpec((tm, d), row), _const_spec(w.shape), _const_spec(bf_row.shape),
                  _const_spec(og_g.shape)],
        out_specs=[pl.BlockSpec((tm, FW), row)] * 5
                  + [pl.BlockSpec((1, FW, tm), lambda i: (i, 0, 0)), pl.BlockSpec((tm, FW), row),
                     pl.BlockSpec((tm, FOX_HEADS), row), pl.BlockSpec((tm, LANES), row)],
        out_shape=[big(BF16), big(F32), big(F32), big(BF16), big(BF16),
                   jax.ShapeDtypeStruct((m // tm, FW, tm), BF16), big(F32),
                   jax.ShapeDtypeStruct((m, FOX_HEADS), F32), jax.ShapeDtypeStruct((m, LANES), BF16)],
        scratch_shapes=[pltpu.VMEM((1, LANES), F32)],
        compiler_params=_params(("arbitrary",)),
        name="fox_proj",
    )(x, w, bf_row, og_g)


def _stack_heads(x, mask0):
    return jnp.concatenate([jnp.where(mask0, x, 0.0), jnp.where(mask0, 0.0, x)], axis=0)


def _rwkv_scan_kernel(at_ref, rt_ref, bt_ref, kt_ref, vb_ref, wtot_ref, bonus_ref, g_ref, s0_ref, lng_ref, lnb_ref,
                      y_ref, sout_ref, state_ref, *, chunk, n_steps):
    step = pl.program_id(1)
    n_pairs = at_ref.shape[1] // LANES
    C = chunk
    n_ch = at_ref.shape[0] // C
    pairs = range(n_pairs)

    @pl.when(step == 0)
    def _():
        zero = jnp.zeros((HEAD_DIM, HEAD_DIM), F32)
        for p in pairs:
            top = jnp.concatenate([s0_ref[0, 2 * p], zero], axis=1)
            bot = jnp.concatenate([zero, s0_ref[0, 2 * p + 1]], axis=1)
            state_ref[p] = jnp.concatenate([top, bot], axis=0)

    fmask0 = _iota((1, LANES), 1) < HEAD_DIM
    tcol = _iota((C, 2 * C), 1)
    trow = _iota((C, 2 * C), 0)
    tmask0 = tcol < C
    tj = jnp.where(tmask0, tcol, tcol - C)
    strict = tj < trow
    incl = tj <= trow
    eye_pair = jnp.where(tj == trow, 1.0, 0.0)
    blk = (_iota((LANES, LANES), 0) // HEAD_DIM) == (_iota((LANES, LANES), 1) // HEAD_DIM)

    def stack_t(x):
        return _stack_heads(x, tmask0)

    def tile(ref, c, p):
        return ref[c * C:(c + 1) * C, p * LANES:(p + 1) * LANES]

    probs = [(c, p) for c in range(n_ch) for p in pairs]
    at = {cp: tile(at_ref, *cp) for cp in probs}
    rt = {cp: tile(rt_ref, *cp) for cp in probs}
    bt = {cp: tile(bt_ref, *cp) for cp in probs}
    kt = {cp: tile(kt_ref, *cp) for cp in probs}
    vb = {cp: tile(vb_ref, *cp) for cp in probs}
    gm = {cp: _dg(jnp.concatenate([at[cp], rt[cp]], axis=0),
                  jnp.concatenate([_stack_heads(bt[cp], fmask0), _stack_heads(kt[cp], fmask0)], axis=0), NT)
          for cp in probs}
    lab = {cp: jnp.where(strict, gm[cp][:C, :2 * C], 0.0) for cp in probs}
    lak_b = {cp: jnp.where(strict, gm[cp][:C, 2 * C:], 0.0).astype(BF16) for cp in probs}
    mr_b = {cp: jnp.concatenate([jnp.where(incl, gm[cp][C:, :2 * C], 0.0),
                                 jnp.where(incl, gm[cp][C:, 2 * C:], 0.0)], axis=1).astype(BF16) for cp in probs}
    kmax = int(math.log2(C)) - 1
    tinv = {cp: eye_pair + lab[cp] for cp in probs}
    pw = {cp: lab[cp].astype(BF16) for cp in probs}
    pw = {cp: _dot(pw[cp], stack_t(pw[cp])).astype(BF16) for cp in probs}
    for _ in range(1, kmax):
        res = {cp: _dot(jnp.concatenate([pw[cp], tinv[cp].astype(BF16)], axis=0), stack_t(pw[cp])) for cp in probs}
        pw = {cp: res[cp][:C].astype(BF16) for cp in probs}
        tinv = {cp: tinv[cp] + res[cp][C:] for cp in probs}
    tinv_b = {cp: (tinv[cp] + _dot(tinv[cp].astype(BF16), stack_t(pw[cp]))).astype(BF16) for cp in probs}

    sd = [state_ref[p] for p in pairs]
    ys = []
    for c in range(n_ch):
        sd_b = [z.astype(BF16) for z in sd]
        vd_b = [_stack_heads(vb[c, p], fmask0) for p in pairs]
        x = [_dg(at[c, p], sd_b[p], NT) + _dot(lak_b[c, p], vd_b[p]) for p in pairs]
        u_b = [_dot(tinv_b[c, p], _stack_heads(x[p].astype(BF16), fmask0)).astype(BF16) for p in pairs]
        ys.append(jnp.concatenate(
            [_dg(rt[c, p], sd_b[p], NT)
             + _dot(mr_b[c, p], jnp.concatenate([_stack_heads(u_b[p], fmask0), vd_b[p]], axis=0)) for p in pairs],
            axis=1))
        w_tot = wtot_ref[c]
        for p in pairs:
            ds = _dg(jnp.concatenate([u_b[p], vb[c, p]], axis=0), jnp.concatenate([bt[c, p], kt[c, p]], axis=0), TN)
            sd[p] = (sd[p] + jnp.where(blk, ds, 0.0)) * w_tot[:, p * LANES:(p + 1) * LANES]
    for p in pairs:
        state_ref[p] = sd[p]

    y = ys[0] if n_ch == 1 else jnp.concatenate(ys, axis=0)
    inv_n = 1.0 / HEAD_DIM
    mu = _head_sum(y) * inv_n
    yc = y - mu
    var = _head_sum(yc * yc) * inv_n
    yn = yc * lax.rsqrt(var + GN_EPS) * lng_ref[...] + lnb_ref[...]
    y_ref[...] = ((yn + bonus_ref[...].astype(F32)) * g_ref[...]).astype(y_ref.dtype)

    @pl.when(step == n_steps - 1)
    def _():
        for p in pairs:
            sout_ref[0, 2 * p] = sd[p][:HEAD_DIM, :HEAD_DIM]
            sout_ref[0, 2 * p + 1] = sd[p][HEAD_DIM:, HEAD_DIM:]


def _rwkv_scan(at, rt, bt, kt, vb, wtot, bonus, g, s0, lng, lnb, *, chunk, rows, t_seq):
    m, w = at.shape
    n_b = m // t_seq
    n_steps = t_seq // rows
    n_heads = w // HEAD_DIM
    blk = pl.BlockSpec((rows, w), lambda bi, si: (bi * n_steps + si, 0))
    wt = pl.BlockSpec((rows // chunk, 1, w), lambda bi, si: (bi * n_steps + si, 0, 0))
    st = pl.BlockSpec((1, n_heads, HEAD_DIM, HEAD_DIM), lambda bi, si: (bi, 0, 0, 0))
    return pl.pallas_call(
        functools.partial(_rwkv_scan_kernel, chunk=chunk, n_steps=n_steps),
        grid=(n_b, n_steps),
        in_specs=[blk] * 5 + [wt, blk, blk, st, _const_spec(lng.shape), _const_spec(lnb.shape)],
        out_specs=[blk, st],
        out_shape=[jax.ShapeDtypeStruct((m, w), BF16),
                   jax.ShapeDtypeStruct((n_b, n_heads, HEAD_DIM, HEAD_DIM), F32)],
        scratch_shapes=[pltpu.VMEM((w // LANES, LANES, LANES), F32)],
        compiler_params=_params(("arbitrary", "arbitrary")),
        name="rwkv_scan",
    )(at, rt, bt, kt, vb, wtot, bonus, g, s0, lng, lnb)


def _rms_gate(o0, o1, gate, lane):
    first = lane < HEAD_DIM
    o = jnp.where(first, o0, o1)
    sq = o * o
    ms0 = jnp.sum(jnp.where(first, sq, 0.0), axis=-1, keepdims=True)
    ms1 = jnp.sum(jnp.where(first, 0.0, sq), axis=-1, keepdims=True)
    ms = jnp.where(first, ms0, ms1) * (1.0 / HEAD_DIM)
    return o * lax.rsqrt(ms + RMS_EPS) * gate


def _lane_cumsum(x, block):
    tri = jnp.where(_iota((block, block), 0) <= _iota((block, block), 1), 1.0, 0.0)
    carry = jnp.zeros((x.shape[0], 1), F32)
    outs = []
    for j in range(x.shape[1] // block):
        c = _dot(x[:, j * block:(j + 1) * block], tri, HI) + carry
        outs.append(c)
        carry = c[:, block - 1:block]
    return outs[0] if len(outs) == 1 else jnp.concatenate(outs, axis=1)


def _fox_prompt_kernel(q_ref, k_ref, caug_ref, vt_ref, gate_ref, y_ref, *, tq, tk):
    hp = pl.program_id(1)
    qi = pl.program_id(2)
    nq = k_ref.shape[0] // tq
    vblk = vt_ref.shape[2]
    lane = _iota((tq, LANES), 1)
    first = lane < HEAD_DIM
    q = q_ref[...]
    zero = jnp.zeros_like(q)
    hslot = jnp.where(lane < 3 * FOX_HEADS, lax.rem(lane, FOX_HEADS), -1)
    qsa = []
    for h in range(2):
        ones = jnp.where(hslot == 2 * hp + h, 1.0, 0.0).astype(BF16)
        qh = jnp.where(first, q, zero) if h == 0 else jnp.where(first, zero, q)
        qsa.append(jnp.concatenate([qh, ones], axis=1))

    def scores(item):
        k0, q_lo, _ = item
        kk = jnp.concatenate([k_ref[k0:k0 + tk, :], caug_ref[k0:k0 + tk, :]], axis=1)
        rhs = jnp.concatenate([qsa[0][q_lo:], qsa[1][q_lo:]], axis=0)
        return _dg(kk, rhs, NT)

    def update(st, item, state):
        k0, q_lo, masked = item
        nqv = tq - q_lo
        vt = jnp.concatenate([vt_ref[k0 // vblk + i] for i in range(tk // vblk)], axis=1)
        out = []
        for h in range(2):
            m_all, l_all, acc_all = state[h]
            sh = st[:, h * nqv:(h + 1) * nqv]
            if masked:
                sh = jnp.where(_iota((tk, nqv), 0) <= _iota((tk, nqv), 1), sh, NEG_BIG)
            m_old = m_all[:, q_lo:]
            m_new = jnp.maximum(m_old, jnp.max(sh, axis=0, keepdims=True))
            alpha = jnp.exp2(m_old - m_new)
            p = jnp.exp2(sh - m_new)
            l_new = alpha * l_all[:, q_lo:] + jnp.sum(p, axis=0, keepdims=True)
            acc_new = alpha * acc_all[:, q_lo:] + _dot(vt[h * HEAD_DIM:(h + 1) * HEAD_DIM], p.astype(BF16))
            if q_lo:
                m_new = jnp.concatenate([m_all[:, :q_lo], m_new], axis=1)
                l_new = jnp.concatenate([l_all[:, :q_lo], l_new], axis=1)
                acc_new = jnp.concatenate([acc_all[:, :q_lo], acc_new], axis=1)
            out.append((m_new, l_new, acc_new))
        return out

    def program(c):
        items = [(kb * tk, 0, False) for kb in range(c * tq // tk)]
        items += [(c * tq + d * tk, d * tk, True) for d in range(tq // tk)]
        state = [(jnp.full((1, tq), NEG_BIG, F32), jnp.zeros((1, tq), F32), jnp.zeros((HEAD_DIM, tq), F32))
                 for _ in range(2)]
        st_next = scores(items[0])
        for n, item in enumerate(items):
            st = st_next
            if n + 1 < len(items):
                st_next = scores(items[n + 1])
            state = update(st, item, state)
        yts = []
        for _, l_fin, acc in state:
            o = acc * (1.0 / l_fin)
            ms = jnp.mean(o * o, axis=0, keepdims=True)
            yts.append(o * lax.rsqrt(ms + RMS_EPS))
        yt = jnp.concatenate(yts, axis=0)
        y_ref[...] = (yt.T * gate_ref[...]).astype(y_ref.dtype)

    for c in range(nq):
        pl.when(qi == c)(functools.partial(program, c))


def _fox_prompt_attn(qb, kb, caug, vbt, gate, *, n_b, t_seq, tq, tk):
    m, w = qb.shape
    n_pairs = w // LANES
    nq = t_seq // tq
    vblk = vbt.shape[2]
    assert vbt.shape == (m // vblk, w, vblk) and tk % vblk == 0 and tq % tk == 0
    qspec = pl.BlockSpec((tq, LANES), lambda b, hp, qi: (b * nq + qi, hp))
    return pl.pallas_call(
        functools.partial(_fox_prompt_kernel, tq=tq, tk=tk),
        grid=(n_b, n_pairs, nq),
        in_specs=[qspec,
                  pl.BlockSpec((t_seq, LANES), lambda b, hp, qi: (b, hp)),
                  pl.BlockSpec((t_seq, LANES), lambda b, hp, qi: (b, 0)),
                  pl.BlockSpec((t_seq // vblk, LANES, vblk), lambda b, hp, qi: (b, hp, 0)),
                  qspec],
        out_specs=qspec,
        out_shape=jax.ShapeDtypeStruct((m, w), BF16),
        compiler_params=_params(("arbitrary", "arbitrary", "arbitrary")),
        name="fox_prompt_attn",
    )(qb, kb, caug, vbt, gate)


def _fox_sample_kernel(q_ref, kn_ref, vn_ref, ck_ref, cv_ref, clf_ref, lf_ref, gate_ref, y_ref):
    t = q_ref.shape[0]
    n_heads = lf_ref.shape[1]
    eye_h = jnp.where(_iota((n_heads, n_heads), 0) == _iota((n_heads, n_heads), 1), 1.0, 0.0)
    tril = jnp.where(_iota((t, t), 0) >= _iota((t, t), 1), 1.0, 0.0)
    clf = clf_ref[0]
    c_tot = jnp.sum(clf, axis=0, keepdims=True)
    cn_col = _dot(tril, lf_ref[...], HI)
    cn_row = _dg(eye_h, cn_col, NT, HI)
    cq_col = cn_col + c_tot
    cc_row = _lane_cumsum(_dg(eye_h, clf, NT, HI), 256)
    lane = _iota((t, LANES), 1)
    first = lane < HEAD_DIM
    causal = _iota((t, t), 1) <= _iota((t, t), 0)
    ys = []
    for p in range(q_ref.shape[1] // LANES):
        sl = slice(p * LANES, (p + 1) * LANES)
        q = q_ref[:, sl]
        zero = jnp.zeros_like(q)
        qs = jnp.concatenate([jnp.where(first, q, zero), jnp.where(first, zero, q)], axis=0)
        kc = ck_ref[0, :, sl]
        vc = cv_ref[0, :, sl]
        kn = kn_ref[:, sl]
        vn = vn_ref[:, sl]
        s_c = _dg(qs, kc, NT)
        s_n = _dg(qs, kn, NT)
        o = []
        for h in range(2):
            hd = 2 * p + h
            sc = s_c[h * t:(h + 1) * t] + (cq_col[:, hd:hd + 1] - cc_row[hd:hd + 1]) * LOG2E
            sn = s_n[h * t:(h + 1) * t] + (cn_col[:, hd:hd + 1] - cn_row[hd:hd + 1]) * LOG2E
            sn = jnp.where(causal, sn, NEG_BIG)
            mx = jnp.maximum(jnp.max(sc, axis=-1, keepdims=True), jnp.max(sn, axis=-1, keepdims=True))
            pc = jnp.exp2(sc - mx)
            pn = jnp.exp2(sn - mx)
            den = jnp.sum(pc, axis=-1, keepdims=True) + jnp.sum(pn, axis=-1, keepdims=True)
            o.append((_dot(pc.astype(BF16), vc) + _dot(pn.astype(BF16), vn)) / den)
        ys.append(_rms_gate(o[0], o[1], gate_ref[:, sl], lane))
    y_ref[...] = jnp.concatenate(ys, axis=1).astype(y_ref.dtype)


def _fox_sample_attn(qb, kb, vb, cache_k, cache_v, cache_lf, lf, gate, *, t_seq):
    m, w = qb.shape
    n_b = m // t_seq
    past = cache_k.shape[1]
    row = pl.BlockSpec((t_seq, w), lambda b: (b, 0))
    cache = pl.BlockSpec((1, past, w), lambda b: (b, 0, 0))
    return pl.pallas_call(
        _fox_sample_kernel,
        grid=(n_b,),
        in_specs=[row, row, row, cache, cache,
                  pl.BlockSpec((1, past, lf.shape[1]), lambda b: (b, 0, 0)),
                  pl.BlockSpec((t_seq, lf.shape[1]), lambda b: (b, 0)), row],
        out_specs=row,
        out_shape=jax.ShapeDtypeStruct((m, w), BF16),
        compiler_params=_params(("arbitrary",)),
        name="fox_sample_attn",
    )(qb, kb, vb, cache_k, cache_v, cache_lf, lf, gate)


def _out_ln_kernel(yr_ref, yf_ref, x_ref, wo_ref, g_ref, b_ref, *rest, alpha, n_cast):
    cast_in, h_ref, cast_out, wob_ref = rest[:n_cast], rest[n_cast], rest[n_cast + 1:-1], rest[-1]

    @pl.when(pl.program_id(0) == 0)
    def _():
        wob_ref[...] = wo_ref[...].astype(BF16)

    half = yr_ref.shape[1]
    mix = _dot(yr_ref[...], wob_ref[0:half, :]) + _dot(yf_ref[...], wob_ref[half:, :])
    h_ref[...] = _layer_norm(alpha * x_ref[...] + mix, g_ref[...], b_ref[...])
    for src, dst in zip(cast_in, cast_out):
        dst[...] = src[...].astype(BF16)


def _out_ln(yr, yf, x, wo, g, b, *, tm, alpha, cast_cols=(), cast_rows=()):
    m, d = x.shape
    n = m // tm
    row = lambda i: (i, 0)
    slabs = ([(a, pl.BlockSpec((a.shape[0], a.shape[1] // n), lambda i: (0, i))) for a in cast_cols]
             + [(a, pl.BlockSpec((a.shape[0] // n, a.shape[1]), lambda i: (i, 0))) for a in cast_rows])
    outs = pl.pallas_call(
        functools.partial(_out_ln_kernel, alpha=alpha, n_cast=len(slabs)),
        grid=(n,),
        in_specs=[pl.BlockSpec((tm, yr.shape[1]), row), pl.BlockSpec((tm, yf.shape[1]), row),
                  pl.BlockSpec((tm, d), row), _const_spec(wo.shape), _const_spec(g.shape), _const_spec(b.shape)]
                 + [spec for _, spec in slabs],
        out_specs=[pl.BlockSpec((tm, d), row)] + [spec for _, spec in slabs],
        out_shape=[jax.ShapeDtypeStruct((m, d), F32)] + [jax.ShapeDtypeStruct(a.shape, BF16) for a, _ in slabs],
        scratch_shapes=[pltpu.VMEM(wo.shape, BF16)],
        compiler_params=_params(("arbitrary",)),
        name="out_ln",
    )(yr, yf, x, wo, g, b, *[a for a, _ in slabs])
    return outs[0], outs[1:]


def _ffn_ln_kernel(h_ref, wu_ref, wd_ref, g_ref, b_ref, k2_ref, v2_ref, o_ref, k5_ref, v5_ref, hb_ref,
                   *, alpha, n_f):
    j = pl.program_id(1)

    @pl.when(j == 0)
    def _():
        h = h_ref[...]
        hb_ref[...] = h.astype(BF16)
        o_ref[...] = alpha * h

    u = jnp.maximum(_dot(hb_ref[...], wu_ref[...]), 0.0)
    o_ref[...] += _dot((u * u).astype(BF16), wd_ref[...])
    for src, dst in ((k2_ref, k5_ref), (v2_ref, v5_ref)):
        for hd in range(dst.shape[1]):
            dst[:, hd, :] = src[:, hd * HEAD_DIM:(hd + 1) * HEAD_DIM]

    @pl.when(j == n_f - 1)
    def _():
        o_ref[...] = _layer_norm(o_ref[...], g_ref[...], b_ref[...])


def _ffn_ln(h, wu, wd, g, b, k2, v2, *, tm, tf, alpha):
    m, d = h.shape
    n_f = wu.shape[1] // tf
    rows = tm // n_f
    heads = k2.shape[1] // HEAD_DIM
    slab_in = pl.BlockSpec((rows, k2.shape[1]), lambda i, j: (i * n_f + j, 0))
    slab_out = pl.BlockSpec((rows, heads, HEAD_DIM), lambda i, j: (i * n_f + j, 0, 0))
    kv5 = jax.ShapeDtypeStruct((m, heads, HEAD_DIM), F32)
    return pl.pallas_call(
        functools.partial(_ffn_ln_kernel, alpha=alpha, n_f=n_f),
        grid=(m // tm, n_f),
        in_specs=[pl.BlockSpec((tm, d), lambda i, j: (i, 0), pipeline_mode=pl.Buffered(1)),
                  pl.BlockSpec((d, tf), lambda i, j: (0, j)),
                  pl.BlockSpec((tf, d), lambda i, j: (j, 0)),
                  _const_spec(g.shape), _const_spec(b.shape), slab_in, slab_in],
        out_specs=[pl.BlockSpec((tm, d), lambda i, j: (i, 0)), slab_out, slab_out],
        out_shape=[jax.ShapeDtypeStruct((m, d), F32), kv5, kv5],
        scratch_shapes=[pltpu.VMEM((tm, d), BF16)],
        compiler_params=_params(("arbitrary", "arbitrary")),
        name="ffn_ln",
    )(h, wu, wd, g, b, k2, v2)


def _pad_cols(x, n):
    return jnp.pad(x, [(0, 0)] * (x.ndim - 1) + [(0, n - x.shape[-1])])


def _stream(x, shift_prev, s0, wts, *, t_seq, tm, tm_ffn, chunk, cache=None):
    n_b, _, d = x.shape
    m = n_b * t_seq
    x2 = x.reshape(m, d)
    fprev = _pad_cols(shift_prev, RP_PAD)
    at, rt, bt, kt, vr, bonus, g, wtot, sh = _rwkv_proj(
        x2, wts["w_r"], wts["mu"], fprev, wts["w0"], wts["w2p"], wts["a0"], wts["a2p"], wts["g2p"], wts["k_k"],
        wts["k_a"], wts["r_k"], tm=tm, t_seq=t_seq, chunk=chunk)
    y_r, s_new = _rwkv_scan(at, rt, bt, kt, vr, wtot, bonus, g, s0, wts["lnx_g"], wts["lnx_b"],
                            chunk=chunk, rows=min(tm, t_seq), t_seq=t_seq)
    qb, k32, v32, kb, vb, vbt, gate, lf, caug = _fox_proj(x2, wts["w_f"], wts["bf_row"], wts["og_g"],
                                                          tm=tm, t_seq=t_seq)
    if cache is None:
        y_f = _fox_prompt_attn(qb, kb, caug, vbt, gate, n_b=n_b, t_seq=t_seq, tq=1024, tk=512)
    else:
        ck, cv, clf = cache
        y_f = _fox_sample_attn(qb, kb, vb, ck.astype(BF16).reshape(n_b, ck.shape[1], FW),
                               cv.astype(BF16).reshape(n_b, cv.shape[1], FW),
                               clf, lf, gate, t_seq=t_seq)
    if "ffn_bf16" in wts:
        h, _ = _out_ln(y_r, y_f, x2, wts["w_o"], wts["ln1_g"], wts["ln1_b"], tm=tm, alpha=wts["alpha"])
    else:
        h, wts["ffn_bf16"] = _out_ln(y_r, y_f, x2, wts["w_o"], wts["ln1_g"], wts["ln1_b"], tm=tm, alpha=wts["alpha"],
                                     cast_cols=(wts["w_up"],), cast_rows=(wts["w_down"],))
    w_up_b, w_down_b = wts["ffn_bf16"]
    y, k5, v5 = _ffn_ln(h, w_up_b, w_down_b, wts["ln2_g"], wts["ln2_b"], k32, v32,
                        tm=tm_ffn, tf=1024, alpha=wts["alpha"])
    heads = FW // HEAD_DIM
    return (y.reshape(n_b, t_seq, d), k5.reshape(n_b, t_seq, heads, HEAD_DIM),
            v5.reshape(n_b, t_seq, heads, HEAD_DIM), lf.reshape(n_b, t_seq, heads), s_new,
            sh[..., :shift_prev.shape[-1]])


def kernel(x_prompt, x_sample, cache_fox_k, cache_fox_v, cache_fox_logf, state_rwkv_wkv, state_rwkv_shift,
           w_in, rwkv_mu, rwkv_w0, rwkv_w2, rwkv_a0, rwkv_a2, rwkv_g2, rwkv_k_k, rwkv_k_a, rwkv_r_k,
           rwkv_lnx_g, rwkv_lnx_b, fox_b_f, fox_out_g, w_o, ln1_g, ln1_b, w_up, w_down, ln2_g, ln2_b):
    depth = w_in.shape[0]
    assert depth == 1, "single-layer problem"
    d_model = x_prompt.shape[-1]
    rwkv_proj = rwkv_mu.shape[-1]
    lora = (rwkv_w2.shape[1], rwkv_a2.shape[1], rwkv_g2.shape[1])
    assert rwkv_w0.shape[-1] == RW and fox_out_g.shape[-1] == FW and rwkv_proj == 3 * RW + sum(lora)
    assert lora[0] + lora[1] == LANES and lora[2] <= 2 * LANES and RP_PAD <= w_in.shape[-1]
    alpha = (2 * depth) ** 0.25
    l = 0
    w = w_in[l]
    fo = rwkv_proj
    row = lambda z: z.reshape(1, -1)
    pad_rows = lambda z, n: jnp.pad(z, ((0, n - z.shape[0]), (0, 0)))
    wts = dict(
        alpha=alpha,
        w_r=w[:, :RP_PAD].astype(BF16),
        w_f=jnp.concatenate([w[:, fo:fo + 3 * FW], w[:, fo + 3 * FW + FOX_HEADS:],
                             _pad_cols(jnp.tile(w[:, fo + 3 * FW:fo + 3 * FW + FOX_HEADS], (1, 3)), LANES)],
                            axis=-1).astype(BF16),
        mu=_pad_cols(row(rwkv_mu[l]), RP_PAD),
        w0=row(rwkv_w0[l]), w2p=jnp.pad(rwkv_w2[l], ((0, lora[1]), (0, 0))).astype(BF16),
        a0=row(rwkv_a0[l]), a2p=jnp.pad(rwkv_a2[l], ((lora[0], 0), (0, 0))).astype(BF16),
        g2p=pad_rows(rwkv_g2[l], 256).astype(BF16),
        k_k=row(rwkv_k_k[l]), k_a=row(rwkv_k_a[l]), r_k=row(rwkv_r_k[l]),
        lnx_g=row(rwkv_lnx_g[l]), lnx_b=row(rwkv_lnx_b[l]),
        bf_row=_pad_cols(jnp.tile(row(fox_b_f[l]), (1, 3)), LANES), og_g=row(fox_out_g[l]),
        w_o=w_o[l], ln1_g=row(ln1_g[l]), ln1_b=row(ln1_b[l]),
        w_up=w_up[l], w_down=w_down[l], ln2_g=row(ln2_g[l]), ln2_b=row(ln2_b[l]),
    )
    n_p, t_p, _ = x_prompt.shape
    n_s, t_s, _ = x_sample.shape
    heads = RW // HEAD_DIM
    shift0 = jnp.zeros((n_p, 1, rwkv_proj), F32)
    s_zero = jnp.zeros((n_p, heads, HEAD_DIM, HEAD_DIM), F32)
    yp, kp, vp, fp, sp, shp = _stream(x_prompt, shift0, s_zero, wts, t_seq=t_p, tm=256, tm_ffn=1024, chunk=64)
    ys, ks, vs, fs, ss, shs = _stream(x_sample, state_rwkv_shift[l], state_rwkv_wkv[l], wts, t_seq=t_s,
                                      tm=n_s * t_s, tm_ffn=n_s * t_s, chunk=t_s,
                                      cache=(cache_fox_k[l], cache_fox_v[l], cache_fox_logf[l]))
    return (yp, ys, kp[None], vp[None], fp[None], sp[None], shp[None],
            ks[None], vs[None], fs[None], ss[None], shs[None])
```

```python
import functools
import math

import jax
import jax.numpy as jnp
from jax import lax
from jax.experimental import pallas as pl
from jax.experimental.pallas import tpu as pltpu

F32 = jnp.float32
BF16 = jnp.bfloat16
HI = lax.Precision.HIGHEST

HEAD_DIM = 64
LANES = 128
LN_EPS = 1e-5
GN_EPS = 64e-5
RMS_EPS = 1e-6
ATTN_SCALE = HEAD_DIM ** -0.5
EXP_NEG_HALF = math.exp(-0.5)
LOG2E = math.log2(math.e)
NEG_BIG = -1e30
VMEM_LIMIT = 60 * 1024 * 1024

NT = (((1,), (1,)), ((), ()))
TN = (((0,), (0,)), ((), ()))


def _sigmoid(x):
    return 1.0 / (1.0 + jnp.exp(-x))


def _log_sigmoid(x):
    return jnp.minimum(x, 0.0) - jnp.log1p(jnp.exp(-jnp.abs(x)))


def _dot(a, b, precision=None):
    return jnp.dot(a, b, preferred_element_type=F32, precision=precision)


def _dg(a, b, dims, precision=None):
    return lax.dot_general(a, b, dims, preferred_element_type=F32, precision=precision)


def _split3(x):
    hi = x.astype(BF16)
    rem = x - hi.astype(F32)
    mid = rem.astype(BF16)
    return hi, mid, (rem - mid.astype(F32)).astype(BF16)


def _dot01(m01, x):
    hi, mid, lo = _split3(x)
    return _dot(m01, hi) + _dot(m01, mid) + _dot(m01, lo)


def _iota(shape, axis):
    return lax.broadcasted_iota(jnp.int32, shape, axis)


def _head_sum(x):
    r = _iota((LANES, LANES), 0) // HEAD_DIM
    c = _iota((LANES, LANES), 1) // HEAD_DIM
    ones_blk = jnp.where(r == c, 1.0, 0.0).astype(BF16)
    outs = []
    for g in range(x.shape[1] // LANES):
        xs = x[:, g * LANES:(g + 1) * LANES]
        hi = xs.astype(BF16)
        lo = (xs - hi.astype(F32)).astype(BF16)
        outs.append(_dot(hi, ones_blk) + _dot(lo, ones_blk))
    return outs[0] if len(outs) == 1 else jnp.concatenate(outs, axis=1)


def _layer_norm(z, g, b):
    mu = jnp.mean(z, axis=-1, keepdims=True)
    zc = z - mu
    var = jnp.mean(zc * zc, axis=-1, keepdims=True)
    return zc * lax.rsqrt(var + LN_EPS) * g + b


def _const_spec(shape):
    nd = len(shape)
    return pl.BlockSpec(shape, lambda *_: (0,) * nd, pipeline_mode=pl.Buffered(1))


def _params(sem):
    return pltpu.CompilerParams(dimension_semantics=sem, vmem_limit_bytes=VMEM_LIMIT)


RW = 1024
RP_PAD = 3 * RW + 128 + 256


def _rwkv_proj_kernel(x_ref, w_ref, mu_ref, fp_ref, w0_ref, w2_ref, a0_ref, a2_ref, g2_ref, kk_ref, ka_ref, rk_ref,
                      at_ref, rt_ref, bt_ref, kt_ref, vb_ref, bonus_ref, g_ref, wtot_ref, sh_ref, carry_ref,
                      *, tm, t_seq, chunk):
    i = pl.program_id(0)
    xb = x_ref[...].astype(BF16)
    rows = _iota((tm, 1), 0)

    def proj(c0, n):
        return _dot(xb, w_ref[:, c0:c0 + n])

    def shift(p, c0):
        n = p.shape[1]
        prev = pltpu.roll(p, 1, 0)
        if t_seq >= tm:
            tiles = t_seq // tm
            pos = lax.rem(i, tiles)
            sidx = lax.div(i, tiles)
            row0 = jnp.where(pos == 0, fp_ref[sidx, :, c0:c0 + n], carry_ref[:, c0:c0 + n])
            prev = jnp.where(rows == 0, row0, prev)
            carry_ref[:, c0:c0 + n] = p[tm - 1:tm, :]

            @pl.when(pos == tiles - 1)
            def _():
                sh_ref[sidx, :, c0:c0 + n] = p[tm - 1:tm, :]
        else:
            per_tile = tm // t_seq
            for j in range(per_tile):
                prev = jnp.where(rows == j * t_seq, fp_ref[i * per_tile + j, :, c0:c0 + n], prev)
                sh_ref[i * per_tile + j, :, c0:c0 + n] = p[(j + 1) * t_seq - 1:(j + 1) * t_seq, :]
        return p + (prev - p) * mu_ref[:, c0:c0 + n]

    p_wa = proj(3 * RW, 128)
    p_g = proj(3 * RW + 128, 256)
    p_k = proj(RW, RW)
    xwa = shift(p_wa, 3 * RW)
    xg = shift(p_g, 3 * RW + 128)
    wl = w0_ref[...] + _dot(jnp.tanh(xwa).astype(BF16), w2_ref[...])
    alr = _sigmoid(a0_ref[...] + _dot(xwa.astype(BF16), a2_ref[...]))
    g_ref[...] = _dot(_sigmoid(xg).astype(BF16), g2_ref[...])
    k = shift(p_k, RW)
    kk = k * kk_ref[...]
    kk_ss = _head_sum(kk * kk)
    p_r = proj(0, RW)
    lw = -EXP_NEG_HALF * _sigmoid(wl)
    rr = _iota((tm, tm), 0)
    cc = _iota((tm, tm), 1)
    same_chunk_tri = jnp.where(rr // chunk == cc // chunk, jnp.where(rr >= cc, 1.0, 0.0), 0.0).astype(BF16)
    lwc = _dot01(same_chunk_tri, lw)
    p_v = proj(2 * RW, RW)
    kkn = kk / jnp.maximum(jnp.sqrt(kk_ss), 1e-12)
    kh = k * (1.0 + (alr - 1.0) * ka_ref[...])
    e_in = jnp.exp(lwc)
    e_out = jnp.exp(-lwc)
    at_ref[...] = (-kkn * jnp.exp(lwc - lw)).astype(BF16)
    bt_ref[...] = (kkn * alr * e_out).astype(BF16)
    kt_ref[...] = (kh * e_out).astype(BF16)
    for c in range(tm // chunk):
        wtot_ref[c] = e_in[(c + 1) * chunk - 1:(c + 1) * chunk, :]
    r = shift(p_r, 0)
    rt_ref[...] = (r * e_in).astype(BF16)
    rkk = _head_sum(r * kh * rk_ref[...])
    v = shift(p_v, 2 * RW)
    vb_ref[...] = v.astype(BF16)
    bonus_ref[...] = (rkk * v).astype(BF16)


def _rwkv_proj(x, w, mu, fprev, w0, w2p, a0, a2p, g2p, k_k, k_a, r_k, *, tm, t_seq, chunk):
    m, d = x.shape
    n_seq = m // t_seq
    row = lambda i: (i, 0)
    big = lambda dt: jax.ShapeDtypeStruct((m, RW), dt)
    return pl.pallas_call(
        functools.partial(_rwkv_proj_kernel, tm=tm, t_seq=t_seq, chunk=chunk),
        grid=(m // tm,),
        in_specs=[pl.BlockSpec((tm, d), row),
                  _const_spec(w.shape), _const_spec(mu.shape), _const_spec(fprev.shape),
                  _const_spec(w0.shape), _const_spec(w2p.shape), _const_spec(a0.shape), _const_spec(a2p.shape),
                  _const_spec(g2p.shape), _const_spec(k_k.shape), _const_spec(k_a.shape), _const_spec(r_k.shape)],
        out_specs=[pl.BlockSpec((tm, RW), row)] * 7
                  + [pl.BlockSpec((tm // chunk, 1, RW), lambda i: (i, 0, 0)),
                     pl.BlockSpec((n_seq, 1, RP_PAD), lambda i: (0, 0, 0))],
        out_shape=[big(BF16)] * 6 + [big(F32), jax.ShapeDtypeStruct((m // chunk, 1, RW), F32),
                                     jax.ShapeDtypeStruct((n_seq, 1, RP_PAD), F32)],
        scratch_shapes=[pltpu.VMEM((1, RP_PAD), F32)],
        compiler_params=_params(("arbitrary",)),
        name="rwkv_proj",
    )(x, w, mu, fprev, w0, w2p, a0, a2p, g2p, k_k, k_a, r_k)


FW = 1024
FOX_HEADS = FW // HEAD_DIM


def _fox_proj_kernel(x_ref, w_ref, bf_ref, og_ref, *rest, tm, t_seq, n_cast):
    cast_in, cast_out, carry_ref = rest[:n_cast], rest[n_cast + 9:-1], rest[-1]
    qb_ref, k_ref, v_ref, kb_ref, vb_ref, vbt_ref, gate_ref, lf_ref, caug_ref = rest[n_cast:n_cast + 9]
    i = pl.program_id(0)
    xb = x_ref[...].astype(BF16)
    for src, dst in zip(cast_in, cast_out):
        dst[...] = src[...].astype(BF16)
    qb_ref[...] = (_dot(xb, w_ref[:, 0:FW]) * (ATTN_SCALE * LOG2E)).astype(BF16)
    k = _dot(xb, w_ref[:, FW:2 * FW])
    k_ref[...] = k
    kb_ref[...] = k.astype(BF16)
    v = _dot(xb, w_ref[:, 2 * FW:3 * FW])
    v_ref[...] = v
    vb_ref[...] = v.astype(BF16)
    vbt_ref[0] = v.T.astype(BF16)
    og = _dot(xb, w_ref[:, 3 * FW:4 * FW])
    gate_ref[...] = _sigmoid(og) * og_ref[...]
    logf = _log_sigmoid(_dot(xb, w_ref[:, 4 * FW:4 * FW + LANES]) + bf_ref[...])
    lf_ref[...] = logf[:, :FOX_HEADS]
    r = _iota((tm, tm), 0)
    c = _iota((tm, tm), 1)
    if t_seq >= tm:
        cs = _dot01(jnp.where(r >= c, 1.0, 0.0).astype(BF16), logf)
        cs = cs + jnp.where(lax.rem(i, t_seq // tm) == 0, 0.0, carry_ref[...])
        carry_ref[...] = cs[tm - 1:tm, :]
    else:
        same_seq = (r // t_seq) == (c // t_seq)
        cs = _dot01(jnp.where(same_seq, jnp.where(r >= c, 1.0, 0.0), 0.0).astype(BF16), logf)
    xs = -LOG2E * cs
    hi, mid, lo = _split3(xs)
    lane = _iota((tm, LANES), 1)
    zero = jnp.zeros_like(hi)
    caug_ref[...] = jnp.where(lane < FOX_HEADS, hi,
                              jnp.where(lane < 2 * FOX_HEADS, mid, jnp.where(lane < 3 * FOX_HEADS, lo, zero)))


def _cast_job(a, block, index_map, out_shape=None):
    spec = pl.BlockSpec(block, index_map)
    return a, spec, jax.ShapeDtypeStruct(out_shape or a.shape, BF16)


def _fox_proj(x, w, bf_row, og_g, *, tm, t_seq, casts=()):
    m, d = x.shape
    row = lambda i: (i, 0)
    big = lambda dt: jax.ShapeDtypeStruct((m, FW), dt)
    return pl.pallas_call(
        functools.partial(_fox_proj_kernel, tm=tm, t_seq=t_seq, n_cast=len(casts)),
        grid=(m // tm,),
        in_specs=[pl.BlockSpec((tm, d), row), _const_spec(w.shape), _const_spec(bf_row.shape),
                  _const_spec(og_g.shape)] + [spec for _, spec, _ in casts],
        out_specs=[pl.BlockSpec((tm, FW), row)] * 5
                  + [pl.BlockSpec((1, FW, tm), lambda i: (i, 0, 0)), pl.BlockSpec((tm, FW), row),
                     pl.BlockSpec((tm, FOX_HEADS), row), pl.BlockSpec((tm, LANES), row)]
                  + [spec for _, spec, _ in casts],
        out_shape=[big(BF16), big(F32), big(F32), big(BF16), big(BF16),
                   jax.ShapeDtypeStruct((m // tm, FW, tm), BF16), big(F32),
                   jax.ShapeDtypeStruct((m, FOX_HEADS), F32), jax.ShapeDtypeStruct((m, LANES), BF16)]
                  + [sds for _, _, sds in casts],
        scratch_shapes=[pltpu.VMEM((1, LANES), F32)],
        compiler_params=_params(("arbitrary",)),
        name="fox_proj",
    )(x, w, bf_row, og_g, *[a for a, _, _ in casts])


def _stack_heads(x, mask0):
    return jnp.concatenate([jnp.where(mask0, x, 0.0), jnp.where(mask0, 0.0, x)], axis=0)


def _rwkv_scan_kernel(at_ref, rt_ref, bt_ref, kt_ref, vb_ref, wtot_ref, bonus_ref, g_ref, s0_ref, lng_ref, lnb_ref,
                      y_ref, sout_ref, state_ref, *, chunk, n_steps):
    step = pl.program_id(1)
    n_pairs = at_ref.shape[1] // LANES
    C = chunk
    n_ch = at_ref.shape[0] // C
    pairs = range(n_pairs)

    @pl.when(step == 0)
    def _():
        zero = jnp.zeros((HEAD_DIM, HEAD_DIM), F32)
        for p in pairs:
            top = jnp.concatenate([s0_ref[0, 2 * p], zero], axis=1)
            bot = jnp.concatenate([zero, s0_ref[0, 2 * p + 1]], axis=1)
            state_ref[p] = jnp.concatenate([top, bot], axis=0)

    fmask0 = _iota((1, LANES), 1) < HEAD_DIM
    tcol = _iota((C, 2 * C), 1)
    trow = _iota((C, 2 * C), 0)
    tmask0 = tcol < C
    tj = jnp.where(tmask0, tcol, tcol - C)
    strict = tj < trow
    incl = tj <= trow
    eye_pair = jnp.where(tj == trow, 1.0, 0.0)
    blk = (_iota((LANES, LANES), 0) // HEAD_DIM) == (_iota((LANES, LANES), 1) // HEAD_DIM)

    def stack_t(x):
        return _stack_heads(x, tmask0)

    def tile(ref, c, p):
        return ref[c * C:(c + 1) * C, p * LANES:(p + 1) * LANES]

    probs = [(c, p) for c in range(n_ch) for p in pairs]
    at = {cp: tile(at_ref, *cp) for cp in probs}
    rt = {cp: tile(rt_ref, *cp) for cp in probs}
    bt = {cp: tile(bt_ref, *cp) for cp in probs}
    kt = {cp: tile(kt_ref, *cp) for cp in probs}
    vb = {cp: tile(vb_ref, *cp) for cp in probs}
    gm = {cp: _dg(jnp.concatenate([at[cp], rt[cp]], axis=0),
                  jnp.concatenate([_stack_heads(bt[cp], fmask0), _stack_heads(kt[cp], fmask0)], axis=0), NT)
          for cp in probs}
    lab = {cp: jnp.where(strict, gm[cp][:C, :2 * C], 0.0) for cp in probs}
    lak_b = {cp: jnp.where(strict, gm[cp][:C, 2 * C:], 0.0).astype(BF16) for cp in probs}
    mr_b = {cp: jnp.concatenate([jnp.where(incl, gm[cp][C:, :2 * C], 0.0),
                                 jnp.where(incl, gm[cp][C:, 2 * C:], 0.0)], axis=1).astype(BF16) for cp in probs}
    kmax = int(math.log2(C)) - 1
    tinv = {cp: eye_pair + lab[cp] for cp in probs}
    pw = {cp: lab[cp].astype(BF16) for cp in probs}
    pw = {cp: _dot(pw[cp], stack_t(pw[cp])).astype(BF16) for cp in probs}
    for _ in range(1, kmax):
        res = {cp: _dot(jnp.concatenate([pw[cp], tinv[cp].astype(BF16)], axis=0), stack_t(pw[cp])) for cp in probs}
        pw = {cp: res[cp][:C].astype(BF16) for cp in probs}
        tinv = {cp: tinv[cp] + res[cp][C:] for cp in probs}
    tinv_b = {cp: (tinv[cp] + _dot(tinv[cp].astype(BF16), stack_t(pw[cp]))).astype(BF16) for cp in probs}

    sd = [state_ref[p] for p in pairs]
    ys = []
    for c in range(n_ch):
        sd_b = [z.astype(BF16) for z in sd]
        vd_b = [_stack_heads(vb[c, p], fmask0) for p in pairs]
        x = [_dg(at[c, p], sd_b[p], NT) + _dot(lak_b[c, p], vd_b[p]) for p in pairs]
        u_b = [_dot(tinv_b[c, p], _stack_heads(x[p].astype(BF16), fmask0)).astype(BF16) for p in pairs]
        ys.append(jnp.concatenate(
            [_dg(rt[c, p], sd_b[p], NT)
             + _dot(mr_b[c, p], jnp.concatenate([_stack_heads(u_b[p], fmask0), vd_b[p]], axis=0)) for p in pairs],
            axis=1))
        w_tot = wtot_ref[c]
        for p in pairs:
            ds = _dg(jnp.concatenate([u_b[p], vb[c, p]], axis=0), jnp.concatenate([bt[c, p], kt[c, p]], axis=0), TN)
            sd[p] = (sd[p] + jnp.where(blk, ds, 0.0)) * w_tot[:, p * LANES:(p + 1) * LANES]
    for p in pairs:
        state_ref[p] = sd[p]

    y = ys[0] if n_ch == 1 else jnp.concatenate(ys, axis=0)
    inv_n = 1.0 / HEAD_DIM
    mu = _head_sum(y) * inv_n
    yc = y - mu
    var = _head_sum(yc * yc) * inv_n
    yn = yc * lax.rsqrt(var + GN_EPS) * lng_ref[...] + lnb_ref[...]
    y_ref[...] = ((yn + bonus_ref[...].astype(F32)) * g_ref[...]).astype(y_ref.dtype)

    @pl.when(step == n_steps - 1)
    def _():
        for p in pairs:
            sout_ref[0, 2 * p] = sd[p][:HEAD_DIM, :HEAD_DIM]
            sout_ref[0, 2 * p + 1] = sd[p][HEAD_DIM:, HEAD_DIM:]


def _rwkv_scan(at, rt, bt, kt, vb, wtot, bonus, g, s0, lng, lnb, *, chunk, rows, t_seq):
    m, w = at.shape
    n_b = m // t_seq
    n_steps = t_seq // rows
    n_heads = w // HEAD_DIM
    blk = pl.BlockSpec((rows, w), lambda bi, si: (bi * n_steps + si, 0))
    wt = pl.BlockSpec((rows // chunk, 1, w), lambda bi, si: (bi * n_steps + si, 0, 0))
    st = pl.BlockSpec((1, n_heads, HEAD_DIM, HEAD_DIM), lambda bi, si: (bi, 0, 0, 0))
    return pl.pallas_call(
        functools.partial(_rwkv_scan_kernel, chunk=chunk, n_steps=n_steps),
        grid=(n_b, n_steps),
        in_specs=[blk] * 5 + [wt, blk, blk, st, _const_spec(lng.shape), _const_spec(lnb.shape)],
        out_specs=[blk, st],
        out_shape=[jax.ShapeDtypeStruct((m, w), BF16),
                   jax.ShapeDtypeStruct((n_b, n_heads, HEAD_DIM, HEAD_DIM), F32)],
        scratch_shapes=[pltpu.VMEM((w // LANES, LANES, LANES), F32)],
        compiler_params=_params(("arbitrary", "arbitrary")),
        name="rwkv_scan",
    )(at, rt, bt, kt, vb, wtot, bonus, g, s0, lng, lnb)


def _rms_gate(o0, o1, gate, lane):
    first = lane < HEAD_DIM
    o = jnp.where(first, o0, o1)
    sq = o * o
    ms0 = jnp.sum(jnp.where(first, sq, 0.0), axis=-1, keepdims=True)
    ms1 = jnp.sum(jnp.where(first, 0.0, sq), axis=-1, keepdims=True)
    ms = jnp.where(first, ms0, ms1) * (1.0 / HEAD_DIM)
    return o * lax.rsqrt(ms + RMS_EPS) * gate


def _lane_cumsum(x, block):
    tri = jnp.where(_iota((block, block), 0) <= _iota((block, block), 1), 1.0, 0.0)
    carry = jnp.zeros((x.shape[0], 1), F32)
    outs = []
    for j in range(x.shape[1] // block):
        c = _dot(x[:, j * block:(j + 1) * block], tri, HI) + carry
        outs.append(c)
        carry = c[:, block - 1:block]
    return outs[0] if len(outs) == 1 else jnp.concatenate(outs, axis=1)


def _fox_prompt_kernel(q_ref, k_ref, caug_ref, vt_ref, gate_ref, y_ref, *, tq, tk):
    hp = pl.program_id(1)
    qi = pl.program_id(2)
    nq = k_ref.shape[0] // tq
    vblk = vt_ref.shape[2]
    lane = _iota((tq, LANES), 1)
    first = lane < HEAD_DIM
    q = q_ref[...]
    zero = jnp.zeros_like(q)
    hslot = jnp.where(lane < 3 * FOX_HEADS, lax.rem(lane, FOX_HEADS), -1)
    qsa = []
    for h in range(2):
        ones = jnp.where(hslot == 2 * hp + h, 1.0, 0.0).astype(BF16)
        qh = jnp.where(first, q, zero) if h == 0 else jnp.where(first, zero, q)
        qsa.append(jnp.concatenate([qh, ones], axis=1))

    def scores(item):
        k0, q_lo, _ = item
        kk = jnp.concatenate([k_ref[k0:k0 + tk, :], caug_ref[k0:k0 + tk, :]], axis=1)
        rhs = jnp.concatenate([qsa[0][q_lo:], qsa[1][q_lo:]], axis=0)
        return _dg(kk, rhs, NT)

    def update(st, item, state):
        k0, q_lo, masked = item
        nqv = tq - q_lo
        vt = jnp.concatenate([vt_ref[k0 // vblk + i] for i in range(tk // vblk)], axis=1)
        out = []
        for h in range(2):
            m_all, l_all, acc_all = state[h]
            sh = st[:, h * nqv:(h + 1) * nqv]
            if masked:
                sh = jnp.where(_iota((tk, nqv), 0) <= _iota((tk, nqv), 1), sh, NEG_BIG)
            m_old = m_all[:, q_lo:]
            m_new = jnp.maximum(m_old, jnp.max(sh, axis=0, keepdims=True))
            alpha = jnp.exp2(m_old - m_new)
            p = jnp.exp2(sh - m_new)
            l_new = alpha * l_all[:, q_lo:] + jnp.sum(p, axis=0, keepdims=True)
            acc_new = alpha * acc_all[:, q_lo:] + _dot(vt[h * HEAD_DIM:(h + 1) * HEAD_DIM], p.astype(BF16))
            if q_lo:
                m_new = jnp.concatenate([m_all[:, :q_lo], m_new], axis=1)
                l_new = jnp.concatenate([l_all[:, :q_lo], l_new], axis=1)
                acc_new = jnp.concatenate([acc_all[:, :q_lo], acc_new], axis=1)
            out.append((m_new, l_new, acc_new))
        return out

    def program(c):
        items = [(kb * tk, 0, False) for kb in range(c * tq // tk)]
        items += [(c * tq + d * tk, d * tk, True) for d in range(tq // tk)]
        state = [(jnp.full((1, tq), NEG_BIG, F32), jnp.zeros((1, tq), F32), jnp.zeros((HEAD_DIM, tq), F32))
                 for _ in range(2)]
        st_next = scores(items[0])
        for n, item in enumerate(items):
            st = st_next
            if n + 1 < len(items):
                st_next = scores(items[n + 1])
            state = update(st, item, state)
        yts = []
        for _, l_fin, acc in state:
            o = acc * (1.0 / l_fin)
            ms = jnp.mean(o * o, axis=0, keepdims=True)
            yts.append(o * lax.rsqrt(ms + RMS_EPS))
        yt = jnp.concatenate(yts, axis=0)
        y_ref[...] = (yt.T * gate_ref[...]).astype(y_ref.dtype)

    for c in range(nq):
        pl.when(qi == c)(functools.partial(program, c))


def _fox_prompt_attn(qb, kb, caug, vbt, gate, *, n_b, t_seq, tq, tk):
    m, w = qb.shape
    n_pairs = w // LANES
    nq = t_seq // tq
    vblk = vbt.shape[2]
    assert vbt.shape == (m // vblk, w, vblk) and tk % vblk == 0 and tq % tk == 0
    qspec = pl.BlockSpec((tq, LANES), lambda b, hp, qi: (b * nq + qi, hp))
    return pl.pallas_call(
        functools.partial(_fox_prompt_kernel, tq=tq, tk=tk),
        grid=(n_b, n_pairs, nq),
        in_specs=[qspec,
                  pl.BlockSpec((t_seq, LANES), lambda b, hp, qi: (b, hp)),
                  pl.BlockSpec((t_seq, LANES), lambda b, hp, qi: (b, 0)),
                  pl.BlockSpec((t_seq // vblk, LANES, vblk), lambda b, hp, qi: (b, hp, 0)),
                  qspec],
        out_specs=qspec,
        out_shape=jax.ShapeDtypeStruct((m, w), BF16),
        compiler_params=_params(("arbitrary", "arbitrary", "arbitrary")),
        name="fox_prompt_attn",
    )(qb, kb, caug, vbt, gate)


def _fox_sample_kernel(q_ref, kn_ref, vn_ref, ck_ref, cv_ref, clf_ref, lf_ref, gate_ref, y_ref):
    t = q_ref.shape[0]
    n_heads = lf_ref.shape[1]
    eye_h = jnp.where(_iota((n_heads, n_heads), 0) == _iota((n_heads, n_heads), 1), 1.0, 0.0)
    tril = jnp.where(_iota((t, t), 0) >= _iota((t, t), 1), 1.0, 0.0)
    clf = clf_ref[0]
    c_tot = jnp.sum(clf, axis=0, keepdims=True)
    cn_col = _dot(tril, lf_ref[...], HI)
    cn_row = _dg(eye_h, cn_col, NT, HI)
    cq_col = cn_col + c_tot
    cc_row = _lane_cumsum(_dg(eye_h, clf, NT, HI), 256)
    lane = _iota((t, LANES), 1)
    first = lane < HEAD_DIM
    causal = _iota((t, t), 1) <= _iota((t, t), 0)
    ys = []
    for p in range(q_ref.shape[1] // LANES):
        sl = slice(p * LANES, (p + 1) * LANES)
        q = q_ref[:, sl]
        zero = jnp.zeros_like(q)
        qs = jnp.concatenate([jnp.where(first, q, zero), jnp.where(first, zero, q)], axis=0)
        kc = ck_ref[0, :, sl].astype(BF16)
        vc = cv_ref[0, :, sl].astype(BF16)
        kn = kn_ref[:, sl]
        vn = vn_ref[:, sl]
        s_c = _dg(qs, kc, NT)
        s_n = _dg(qs, kn, NT)
        o = []
        for h in range(2):
            hd = 2 * p + h
            sc = s_c[h * t:(h + 1) * t] + (cq_col[:, hd:hd + 1] - cc_row[hd:hd + 1]) * LOG2E
            sn = s_n[h * t:(h + 1) * t] + (cn_col[:, hd:hd + 1] - cn_row[hd:hd + 1]) * LOG2E
            sn = jnp.where(causal, sn, NEG_BIG)
            mx = jnp.maximum(jnp.max(sc, axis=-1, keepdims=True), jnp.max(sn, axis=-1, keepdims=True))
            pc = jnp.exp2(sc - mx)
            pn = jnp.exp2(sn - mx)
            den = jnp.sum(pc, axis=-1, keepdims=True) + jnp.sum(pn, axis=-1, keepdims=True)
            o.append((_dot(pc.astype(BF16), vc) + _dot(pn.astype(BF16), vn)) / den)
        ys.append(_rms_gate(o[0], o[1], gate_ref[:, sl], lane))
    y_ref[...] = jnp.concatenate(ys, axis=1).astype(y_ref.dtype)


def _fox_sample_attn(qb, kb, vb, cache_k, cache_v, cache_lf, lf, gate, *, t_seq):
    m, w = qb.shape
    n_b = m // t_seq
    past = cache_k.shape[1]
    row = pl.BlockSpec((t_seq, w), lambda b: (b, 0))
    cache = pl.BlockSpec((1, past, w), lambda b: (b, 0, 0))
    return pl.pallas_call(
        _fox_sample_kernel,
        grid=(n_b,),
        in_specs=[row, row, row, cache, cache,
                  pl.BlockSpec((1, past, lf.shape[1]), lambda b: (b, 0, 0)),
                  pl.BlockSpec((t_seq, lf.shape[1]), lambda b: (b, 0)), row],
        out_specs=row,
        out_shape=jax.ShapeDtypeStruct((m, w), BF16),
        compiler_params=_params(("arbitrary",)),
        name="fox_sample_attn",
    )(qb, kb, vb, cache_k, cache_v, cache_lf, lf, gate)


def _out_ln_kernel(yr_ref, yf_ref, x_ref, wo_ref, g_ref, b_ref, *rest, alpha, n_cast):
    cast_in, h_ref, cast_out, wob_ref = rest[:n_cast], rest[n_cast], rest[n_cast + 1:-1], rest[-1]

    @pl.when(pl.program_id(0) == 0)
    def _():
        wob_ref[...] = wo_ref[...].astype(BF16)

    half = yr_ref.shape[1]
    mix = _dot(yr_ref[...], wob_ref[0:half, :]) + _dot(yf_ref[...], wob_ref[half:, :])
    h_ref[...] = _layer_norm(alpha * x_ref[...] + mix, g_ref[...], b_ref[...])
    for src, dst in zip(cast_in, cast_out):
        dst[...] = src[...].astype(BF16)


def _out_ln(yr, yf, x, wo, g, b, *, tm, alpha, cast_cols=(), cast_rows=()):
    m, d = x.shape
    n = m // tm
    row = lambda i: (i, 0)
    slabs = ([(a, pl.BlockSpec((a.shape[0], a.shape[1] // n), lambda i: (0, i))) for a in cast_cols]
             + [(a, pl.BlockSpec((a.shape[0] // n, a.shape[1]), lambda i: (i, 0))) for a in cast_rows])
    outs = pl.pallas_call(
        functools.partial(_out_ln_kernel, alpha=alpha, n_cast=len(slabs)),
        grid=(n,),
        in_specs=[pl.BlockSpec((tm, yr.shape[1]), row), pl.BlockSpec((tm, yf.shape[1]), row),
                  pl.BlockSpec((tm, d), row), _const_spec(wo.shape), _const_spec(g.shape), _const_spec(b.shape)]
                 + [spec for _, spec in slabs],
        out_specs=[pl.BlockSpec((tm, d), row)] + [spec for _, spec in slabs],
        out_shape=[jax.ShapeDtypeStruct((m, d), F32)] + [jax.ShapeDtypeStruct(a.shape, BF16) for a, _ in slabs],
        scratch_shapes=[pltpu.VMEM(wo.shape, BF16)],
        compiler_params=_params(("arbitrary",)),
        name="out_ln",
    )(yr, yf, x, wo, g, b, *[a for a, _ in slabs])
    return outs[0], outs[1:]


def _ffn_ln_kernel(h_ref, wu_ref, wd_ref, g_ref, b_ref, k2_ref, v2_ref, o_ref, k5_ref, v5_ref, hb_ref,
                   *, alpha, n_f):
    j = pl.program_id(1)

    @pl.when(j == 0)
    def _():
        h = h_ref[...]
        hb_ref[...] = h.astype(BF16)
        o_ref[...] = alpha * h

    u = jnp.maximum(_dot(hb_ref[...], wu_ref[...]), 0.0)
    o_ref[...] += _dot((u * u).astype(BF16), wd_ref[...])
    for src, dst in ((k2_ref, k5_ref), (v2_ref, v5_ref)):
        for hd in range(dst.shape[1]):
            dst[:, hd, :] = src[:, hd * HEAD_DIM:(hd + 1) * HEAD_DIM]

    @pl.when(j == n_f - 1)
    def _():
        o_ref[...] = _layer_norm(o_ref[...], g_ref[...], b_ref[...])


def _ffn_ln(h, wu, wd, g, b, k2, v2, *, tm, tf, alpha):
    m, d = h.shape
    n_f = wu.shape[1] // tf
    rows = tm // n_f
    heads = k2.shape[1] // HEAD_DIM
    slab_in = pl.BlockSpec((rows, k2.shape[1]), lambda i, j: (i * n_f + j, 0))
    slab_out = pl.BlockSpec((rows, heads, HEAD_DIM), lambda i, j: (i * n_f + j, 0, 0))
    kv5 = jax.ShapeDtypeStruct((m, heads, HEAD_DIM), F32)
    return pl.pallas_call(
        functools.partial(_ffn_ln_kernel, alpha=alpha, n_f=n_f),
        grid=(m // tm, n_f),
        in_specs=[pl.BlockSpec((tm, d), lambda i, j: (i, 0), pipeline_mode=pl.Buffered(1)),
                  pl.BlockSpec((d, tf), lambda i, j: (0, j)),
                  pl.BlockSpec((tf, d), lambda i, j: (j, 0)),
                  _const_spec(g.shape), _const_spec(b.shape), slab_in, slab_in],
        out_specs=[pl.BlockSpec((tm, d), lambda i, j: (i, 0)), slab_out, slab_out],
        out_shape=[jax.ShapeDtypeStruct((m, d), F32), kv5, kv5],
        scratch_shapes=[pltpu.VMEM((tm, d), BF16)],
        compiler_params=_params(("arbitrary", "arbitrary")),
        name="ffn_ln",
    )(h, wu, wd, g, b, k2, v2)


def _pad_cols(x, n):
    return jnp.pad(x, [(0, 0)] * (x.ndim - 1) + [(0, n - x.shape[-1])])


def _stream(x, shift_prev, s0, wts, *, t_seq, tm, tm_ffn, chunk, cache=None):
    n_b, _, d = x.shape
    m = n_b * t_seq
    x2 = x.reshape(m, d)
    casts = ()
    if "w_r" not in wts:
        n = m // tm
        w_all, w_down = wts["w_in"], wts["w_down"]
        n_blk = RP_PAD // LANES
        assert n >= n_blk and w_down.shape[0] % n == 0
        casts = (_cast_job(w_all, (w_all.shape[0], LANES), lambda i: (0, jnp.minimum(i, n_blk - 1)),
                           (w_all.shape[0], RP_PAD)),
                 _cast_job(w_down, (w_down.shape[0] // n, w_down.shape[1]), lambda i: (i, 0)))
    fox = _fox_proj(x2, wts["w_f"], wts["bf_row"], wts["og_g"], tm=tm, t_seq=t_seq, casts=casts)
    qb, k32, v32, kb, vb, vbt, gate, lf, caug = fox[:9]
    if casts:
        wts["w_r"], wts["w_down_b"] = fox[9:]
    fprev = _pad_cols(shift_prev, RP_PAD)
    at, rt, bt, kt, vr, bonus, g, wtot, sh = _rwkv_proj(
        x2, wts["w_r"], wts["mu"], fprev, wts["w0"], wts["w2p"], wts["a0"], wts["a2p"], wts["g2p"], wts["k_k"],
        wts["k_a"], wts["r_k"], tm=tm, t_seq=t_seq, chunk=chunk)
    y_r, s_new = _rwkv_scan(at, rt, bt, kt, vr, wtot, bonus, g, s0, wts["lnx_g"], wts["lnx_b"],
                            chunk=chunk, rows=min(4 * chunk, t_seq), t_seq=t_seq)
    if cache is None:
        y_f = _fox_prompt_attn(qb, kb, caug, vbt, gate, n_b=n_b, t_seq=t_seq, tq=1024, tk=512)
    else:
        ck, cv, clf = cache
        y_f = _fox_sample_attn(qb, kb, vb, ck.reshape(n_b, ck.shape[1], FW), cv.reshape(n_b, cv.shape[1], FW),
                               clf, lf, gate, t_seq=t_seq)
    if "w_up_b" in wts:
        h, _ = _out_ln(y_r, y_f, x2, wts["w_o"], wts["ln1_g"], wts["ln1_b"], tm=tm, alpha=wts["alpha"])
    else:
        h, (wts["w_up_b"],) = _out_ln(y_r, y_f, x2, wts["w_o"], wts["ln1_g"], wts["ln1_b"], tm=tm,
                                      alpha=wts["alpha"], cast_cols=(wts["w_up"],))
    y, k5, v5 = _ffn_ln(h, wts["w_up_b"], wts["w_down_b"], wts["ln2_g"], wts["ln2_b"], k32, v32,
                        tm=tm_ffn, tf=1024, alpha=wts["alpha"])
    heads = FW // HEAD_DIM
    return (y.reshape(n_b, t_seq, d), k5.reshape(n_b, t_seq, heads, HEAD_DIM),
            v5.reshape(n_b, t_seq, heads, HEAD_DIM), lf.reshape(n_b, t_seq, heads), s_new,
            sh[..., :shift_prev.shape[-1]])


def kernel(x_prompt, x_sample, cache_fox_k, cache_fox_v, cache_fox_logf, state_rwkv_wkv, state_rwkv_shift,
           w_in, rwkv_mu, rwkv_w0, rwkv_w2, rwkv_a0, rwkv_a2, rwkv_g2, rwkv_k_k, rwkv_k_a, rwkv_r_k,
           rwkv_lnx_g, rwkv_lnx_b, fox_b_f, fox_out_g, w_o, ln1_g, ln1_b, w_up, w_down, ln2_g, ln2_b):
    depth = w_in.shape[0]
    assert depth == 1, "single-layer problem"
    d_model = x_prompt.shape[-1]
    rwkv_proj = rwkv_mu.shape[-1]
    lora = (rwkv_w2.shape[1], rwkv_a2.shape[1], rwkv_g2.shape[1])
    assert rwkv_w0.shape[-1] == RW and fox_out_g.shape[-1] == FW and rwkv_proj == 3 * RW + sum(lora)
    assert lora[0] + lora[1] == LANES and lora[2] <= 2 * LANES and RP_PAD <= w_in.shape[-1]
    alpha = (2 * depth) ** 0.25
    l = 0
    w = w_in[l]
    fo = rwkv_proj
    row = lambda z: z.reshape(1, -1)
    pad_rows = lambda z, n: jnp.pad(z, ((0, n - z.shape[0]), (0, 0)))
    wts = dict(
        alpha=alpha,
        w_in=w,
        w_f=jnp.concatenate([w[:, fo:fo + 3 * FW], w[:, fo + 3 * FW + FOX_HEADS:],
                             _pad_cols(jnp.tile(w[:, fo + 3 * FW:fo + 3 * FW + FOX_HEADS], (1, 3)), LANES)],
                            axis=-1).astype(BF16),
        mu=_pad_cols(row(rwkv_mu[l]), RP_PAD),
        w0=row(rwkv_w0[l]), w2p=jnp.pad(rwkv_w2[l], ((0, lora[1]), (0, 0))).astype(BF16),
        a0=row(rwkv_a0[l]), a2p=jnp.pad(rwkv_a2[l], ((lora[0], 0), (0, 0))).astype(BF16),
        g2p=pad_rows(rwkv_g2[l], 256).astype(BF16),
        k_k=row(rwkv_k_k[l]), k_a=row(rwkv_k_a[l]), r_k=row(rwkv_r_k[l]),
        lnx_g=row(rwkv_lnx_g[l]), lnx_b=row(rwkv_lnx_b[l]),
        bf_row=_pad_cols(jnp.tile(row(fox_b_f[l]), (1, 3)), LANES), og_g=row(fox_out_g[l]),
        w_o=w_o[l], ln1_g=row(ln1_g[l]), ln1_b=row(ln1_b[l]),
        w_up=w_up[l], w_down=w_down[l], ln2_g=row(ln2_g[l]), ln2_b=row(ln2_b[l]),
    )
    n_p, t_p, _ = x_prompt.shape
    n_s, t_s, _ = x_sample.shape
    heads = RW // HEAD_DIM
    shift0 = jnp.zeros((n_p, 1, rwkv_proj), F32)
    s_zero = jnp.zeros((n_p, heads, HEAD_DIM, HEAD_DIM), F32)
    yp, kp, vp, fp, sp, shp = _stream(x_prompt, shift0, s_zero, wts, t_seq=t_p, tm=256, tm_ffn=1024, chunk=64)
    ys, ks, vs, fs, ss, shs = _stream(x_sample, state_rwkv_shift[l], state_rwkv_wkv[l], wts, t_seq=t_s,
                                      tm=n_s * t_s, tm_ffn=n_s * t_s, chunk=t_s,
                                      cache=(cache_fox_k[l], cache_fox_v[l], cache_fox_logf[l]))
    return (yp, ys, kp[None], vp[None], fp[None], sp[None], shp[None],
            ks[None], vs[None], fs[None], ss[None], shs[None])
```

```python
import functools
import math

import jax
import jax.numpy as jnp
from jax import lax
from jax.experimental import pallas as pl
from jax.experimental.pallas import tpu as pltpu

F32 = jnp.float32
BF16 = jnp.bfloat16
HI = lax.Precision.HIGHEST

HEAD_DIM = 64
LANES = 128
LN_EPS = 1e-5
GN_EPS = 64e-5
RMS_EPS = 1e-6
ATTN_SCALE = HEAD_DIM ** -0.5
EXP_NEG_HALF = math.exp(-0.5)
LOG2E = math.log2(math.e)
NEG_BIG = -1e30
VMEM_LIMIT = 60 * 1024 * 1024

NT = (((1,), (1,)), ((), ()))
TN = (((0,), (0,)), ((), ()))


def _sigmoid(x):
    return 1.0 / (1.0 + jnp.exp(-x))


def _log_sigmoid(x):
    return jnp.minimum(x, 0.0) - jnp.log1p(jnp.exp(-jnp.abs(x)))


def _dot(a, b, precision=None):
    return jnp.dot(a, b, preferred_element_type=F32, precision=precision)


def _dg(a, b, dims, precision=None):
    return lax.dot_general(a, b, dims, preferred_element_type=F32, precision=precision)


def _split3(x):
    hi = x.astype(BF16)
    rem = x - hi.astype(F32)
    mid = rem.astype(BF16)
    return hi, mid, (rem - mid.astype(F32)).astype(BF16)


def _dot01(m01, x):
    hi, mid, lo = _split3(x)
    return _dot(m01, hi) + _dot(m01, mid) + _dot(m01, lo)


def _iota(shape, axis):
    return lax.broadcasted_iota(jnp.int32, shape, axis)


def _head_sum(x):
    r = _iota((LANES, LANES), 0) // HEAD_DIM
    c = _iota((LANES, LANES), 1) // HEAD_DIM
    ones_blk = jnp.where(r == c, 1.0, 0.0).astype(BF16)
    outs = []
    for g in range(x.shape[1] // LANES):
        xs = x[:, g * LANES:(g + 1) * LANES]
        hi = xs.astype(BF16)
        lo = (xs - hi.astype(F32)).astype(BF16)
        outs.append(_dot(hi, ones_blk) + _dot(lo, ones_blk))
    return outs[0] if len(outs) == 1 else jnp.concatenate(outs, axis=1)


def _layer_norm(z, g, b):
    mu = jnp.mean(z, axis=-1, keepdims=True)
    zc = z - mu
    var = jnp.mean(zc * zc, axis=-1, keepdims=True)
    return zc * lax.rsqrt(var + LN_EPS) * g + b


def _const_spec(shape):
    nd = len(shape)
    return pl.BlockSpec(shape, lambda *_: (0,) * nd, pipeline_mode=pl.Buffered(1))


def _params(sem):
    return pltpu.CompilerParams(dimension_semantics=sem, vmem_limit_bytes=VMEM_LIMIT)


RW = 1024
RP_PAD = 3 * RW + 128 + 256


def _rwkv_proj_kernel(x_ref, w_ref, mu_ref, fp_ref, w0_ref, w2_ref, a0_ref, a2_ref, g2_ref, kk_ref, ka_ref, rk_ref,
                      at_ref, rt_ref, bt_ref, kt_ref, vb_ref, bonus_ref, g_ref, wtot_ref, sh_ref, carry_ref,
                      *, tm, t_seq, chunk):
    i = pl.program_id(0)
    xb = x_ref[...].astype(BF16)
    rows = _iota((tm, 1), 0)

    def proj(c0, n):
        return _dot(xb, w_ref[:, c0:c0 + n])

    def shift(p, c0):
        n = p.shape[1]
        prev = pltpu.roll(p, 1, 0)
        if t_seq >= tm:
            tiles = t_seq // tm
            pos = lax.rem(i, tiles)
            sidx = lax.div(i, tiles)
            row0 = jnp.where(pos == 0, fp_ref[sidx, :, c0:c0 + n], carry_ref[:, c0:c0 + n])
            prev = jnp.where(rows == 0, row0, prev)
            carry_ref[:, c0:c0 + n] = p[tm - 1:tm, :]

            @pl.when(pos == tiles - 1)
            def _():
                sh_ref[sidx, :, c0:c0 + n] = p[tm - 1:tm, :]
        else:
            per_tile = tm // t_seq
            for j in range(per_tile):
                prev = jnp.where(rows == j * t_seq, fp_ref[i * per_tile + j, :, c0:c0 + n], prev)
                sh_ref[i * per_tile + j, :, c0:c0 + n] = p[(j + 1) * t_seq - 1:(j + 1) * t_seq, :]
        return p + (prev - p) * mu_ref[:, c0:c0 + n]

    p_wa = proj(3 * RW, 128)
    p_g = proj(3 * RW + 128, 256)
    p_k = proj(RW, RW)
    xwa = shift(p_wa, 3 * RW)
    xg = shift(p_g, 3 * RW + 128)
    wl = w0_ref[...] + _dot(jnp.tanh(xwa).astype(BF16), w2_ref[...])
    alr = _sigmoid(a0_ref[...] + _dot(xwa.astype(BF16), a2_ref[...]))
    g_ref[...] = _dot(_sigmoid(xg).astype(BF16), g2_ref[...])
    k = shift(p_k, RW)
    kk = k * kk_ref[...]
    kk_ss = _head_sum(kk * kk)
    p_r = proj(0, RW)
    lw = -EXP_NEG_HALF * _sigmoid(wl)
    rr = _iota((tm, tm), 0)
    cc = _iota((tm, tm), 1)
    same_chunk_tri = jnp.where(rr // chunk == cc // chunk, jnp.where(rr >= cc, 1.0, 0.0), 0.0).astype(BF16)
    lwc = _dot01(same_chunk_tri, lw)
    p_v = proj(2 * RW, RW)
    kkn = kk / jnp.maximum(jnp.sqrt(kk_ss), 1e-12)
    kh = k * (1.0 + (alr - 1.0) * ka_ref[...])
    e_in = jnp.exp(lwc)
    e_out = jnp.exp(-lwc)
    at_ref[...] = (-kkn * jnp.exp(lwc - lw)).astype(BF16)
    bt_ref[...] = (kkn * alr * e_out).astype(BF16)
    kt_ref[...] = (kh * e_out).astype(BF16)
    for c in range(tm // chunk):
        wtot_ref[c] = e_in[(c + 1) * chunk - 1:(c + 1) * chunk, :]
    r = shift(p_r, 0)
    rt_ref[...] = (r * e_in).astype(BF16)
    rkk = _head_sum(r * kh * rk_ref[...])
    v = shift(p_v, 2 * RW)
    vb_ref[...] = v.astype(BF16)
    bonus_ref[...] = (rkk * v).astype(BF16)


def _rwkv_proj(x, w, mu, fprev, w0, w2p, a0, a2p, g2p, k_k, k_a, r_k, *, tm, t_seq, chunk):
    m, d = x.shape
    n_seq = m // t_seq
    row = lambda i: (i, 0)
    big = lambda dt: jax.ShapeDtypeStruct((m, RW), dt)
    return pl.pallas_call(
        functools.partial(_rwkv_proj_kernel, tm=tm, t_seq=t_seq, chunk=chunk),
        grid=(m // tm,),
        in_specs=[pl.BlockSpec((tm, d), row),
                  _const_spec(w.shape), _const_spec(mu.shape), _const_spec(fprev.shape),
                  _const_spec(w0.shape), _const_spec(w2p.shape), _const_spec(a0.shape), _const_spec(a2p.shape),
                  _const_spec(g2p.shape), _const_spec(k_k.shape), _const_spec(k_a.shape), _const_spec(r_k.shape)],
        out_specs=[pl.BlockSpec((tm, RW), row)] * 7
                  + [pl.BlockSpec((tm // chunk, 1, RW), lambda i: (i, 0, 0)),
                     pl.BlockSpec((n_seq, 1, RP_PAD), lambda i: (0, 0, 0))],
        out_shape=[big(BF16)] * 6 + [big(F32), jax.ShapeDtypeStruct((m // chunk, 1, RW), F32),
                                     jax.ShapeDtypeStruct((n_seq, 1, RP_PAD), F32)],
        scratch_shapes=[pltpu.VMEM((1, RP_PAD), F32)],
        compiler_params=_params(("arbitrary",)),
        name="rwkv_proj",
    )(x, w, mu, fprev, w0, w2p, a0, a2p, g2p, k_k, k_a, r_k)


FW = 1024
FOX_HEADS = FW // HEAD_DIM


def _fox_proj_kernel(x_ref, w_ref, bf_ref, og_ref,
                     qb_ref, k_ref, v_ref, kb_ref, vb_ref, vbt_ref, gate_ref, lf_ref, caug_ref, carry_ref,
                     *, tm, t_seq):
    i = pl.program_id(0)
    xb = x_ref[...].astype(BF16)
    qb_ref[...] = (_dot(xb, w_ref[:, 0:FW]) * (ATTN_SCALE * LOG2E)).astype(BF16)
    k = _dot(xb, w_ref[:, FW:2 * FW])
    k_ref[...] = k
    kb_ref[...] = k.astype(BF16)
    v = _dot(xb, w_ref[:, 2 * FW:3 * FW])
    v_ref[...] = v
    vb_ref[...] = v.astype(BF16)
    vbt_ref[0] = v.T.astype(BF16)
    og = _dot(xb, w_ref[:, 3 * FW:4 * FW])
    gate_ref[...] = _sigmoid(og) * og_ref[...]
    logf = _log_sigmoid(_dot(xb, w_ref[:, 4 * FW:4 * FW + LANES]) + bf_ref[...])
    lf_ref[...] = logf[:, :FOX_HEADS]
    r = _iota((tm, tm), 0)
    c = _iota((tm, tm), 1)
    if t_seq >= tm:
        cs = _dot01(jnp.where(r >= c, 1.0, 0.0).astype(BF16), logf)
        cs = cs + jnp.where(lax.rem(i, t_seq // tm) == 0, 0.0, carry_ref[...])
        carry_ref[...] = cs[tm - 1:tm, :]
    else:
        same_seq = (r // t_seq) == (c // t_seq)
        cs = _dot01(jnp.where(same_seq, jnp.where(r >= c, 1.0, 0.0), 0.0).astype(BF16), logf)
    xs = -LOG2E * cs
    hi, mid, lo = _split3(xs)
    lane = _iota((tm, LANES), 1)
    zero = jnp.zeros_like(hi)
    caug_ref[...] = jnp.where(lane < FOX_HEADS, hi,
                              jnp.where(lane < 2 * FOX_HEADS, mid, jnp.where(lane < 3 * FOX_HEADS, lo, zero)))


def _cast_job(a, block, index_map, out_shape=None):
    spec = pl.BlockSpec(block, index_map)
    return a, spec, jax.ShapeDtypeStruct(out_shape or a.shape, BF16)


def _fox_proj(x, w, bf_row, og_g, *, tm, t_seq):
    m, d = x.shape
    row = lambda i: (i, 0)
    big = lambda dt: jax.ShapeDtypeStruct((m, FW), dt)
    return pl.pallas_call(
        functools.partial(_fox_proj_kernel, tm=tm, t_seq=t_seq),
        grid=(m // tm,),
        in_specs=[pl.BlockSpec((tm, d), row), _const_spec(w.shape), _const_spec(bf_row.shape),
                  _const_spec(og_g.shape)],
        out_specs=[pl.BlockSpec((tm, FW), row)] * 5
                  + [pl.BlockSpec((1, FW, tm), lambda i: (i, 0, 0)), pl.BlockSpec((tm, FW), row),
                     pl.BlockSpec((tm, FOX_HEADS), row), pl.BlockSpec((tm, LANES), row)],
        out_shape=[big(BF16), big(F32), big(F32), big(BF16), big(BF16),
                   jax.ShapeDtypeStruct((m // tm, FW, tm), BF16), big(F32),
                   jax.ShapeDtypeStruct((m, FOX_HEADS), F32), jax.ShapeDtypeStruct((m, LANES), BF16)],
        scratch_shapes=[pltpu.VMEM((1, LANES), F32)],
        compiler_params=_params(("arbitrary",)),
        name="fox_proj",
    )(x, w, bf_row, og_g)


def _stack_heads(x, mask0):
    return jnp.concatenate([jnp.where(mask0, x, 0.0), jnp.where(mask0, 0.0, x)], axis=0)


def _rwkv_scan_kernel(at_ref, rt_ref, bt_ref, kt_ref, vb_ref, wtot_ref, bonus_ref, g_ref, s0_ref, lng_ref, lnb_ref,
                      y_ref, sout_ref, state_ref, *, chunk, n_steps):
    step = pl.program_id(1)
    n_pairs = at_ref.shape[1] // LANES
    C = chunk
    n_ch = at_ref.shape[0] // C
    pairs = range(n_pairs)

    @pl.when(step == 0)
    def _():
        zero = jnp.zeros((HEAD_DIM, HEAD_DIM), F32)
        for p in pairs:
            top = jnp.concatenate([s0_ref[0, 2 * p], zero], axis=1)
            bot = jnp.concatenate([zero, s0_ref[0, 2 * p + 1]], axis=1)
            state_ref[p] = jnp.concatenate([top, bot], axis=0)

    fmask0 = _iota((1, LANES), 1) < HEAD_DIM
    tcol = _iota((C, 2 * C), 1)
    trow = _iota((C, 2 * C), 0)
    tmask0 = tcol < C
    tj = jnp.where(tmask0, tcol, tcol - C)
    strict = tj < trow
    incl = tj <= trow
    eye_pair = jnp.where(tj == trow, 1.0, 0.0)
    blk = (_iota((LANES, LANES), 0) // HEAD_DIM) == (_iota((LANES, LANES), 1) // HEAD_DIM)

    def stack_t(x):
        return _stack_heads(x, tmask0)

    def tile(ref, c, p):
        return ref[c * C:(c + 1) * C, p * LANES:(p + 1) * LANES]

    probs = [(c, p) for c in range(n_ch) for p in pairs]
    at = {cp: tile(at_ref, *cp) for cp in probs}
    rt = {cp: tile(rt_ref, *cp) for cp in probs}
    bt = {cp: tile(bt_ref, *cp) for cp in probs}
    kt = {cp: tile(kt_ref, *cp) for cp in probs}
    vb = {cp: tile(vb_ref, *cp) for cp in probs}
    gm = {cp: _dg(jnp.concatenate([at[cp], rt[cp]], axis=0),
                  jnp.concatenate([_stack_heads(bt[cp], fmask0), _stack_heads(kt[cp], fmask0)], axis=0), NT)
          for cp in probs}
    lab = {cp: jnp.where(strict, gm[cp][:C, :2 * C], 0.0) for cp in probs}
    lak_b = {cp: jnp.where(strict, gm[cp][:C, 2 * C:], 0.0).astype(BF16) for cp in probs}
    mr_b = {cp: jnp.concatenate([jnp.where(incl, gm[cp][C:, :2 * C], 0.0),
                                 jnp.where(incl, gm[cp][C:, 2 * C:], 0.0)], axis=1).astype(BF16) for cp in probs}
    kmax = int(math.log2(C)) - 1
    tinv = {cp: eye_pair + lab[cp] for cp in probs}
    pw = {cp: lab[cp].astype(BF16) for cp in probs}
    pw = {cp: _dot(pw[cp], stack_t(pw[cp])).astype(BF16) for cp in probs}
    for _ in range(1, kmax):
        res = {cp: _dot(jnp.concatenate([pw[cp], tinv[cp].astype(BF16)], axis=0), stack_t(pw[cp])) for cp in probs}
        pw = {cp: res[cp][:C].astype(BF16) for cp in probs}
        tinv = {cp: tinv[cp] + res[cp][C:] for cp in probs}
    tinv_b = {cp: (tinv[cp] + _dot(tinv[cp].astype(BF16), stack_t(pw[cp]))).astype(BF16) for cp in probs}

    sd = [state_ref[p] for p in pairs]
    ys = []
    for c in range(n_ch):
        sd_b = [z.astype(BF16) for z in sd]
        vd_b = [_stack_heads(vb[c, p], fmask0) for p in pairs]
        x = [_dg(at[c, p], sd_b[p], NT) + _dot(lak_b[c, p], vd_b[p]) for p in pairs]
        u_b = [_dot(tinv_b[c, p], _stack_heads(x[p].astype(BF16), fmask0)).astype(BF16) for p in pairs]
        ys.append(jnp.concatenate(
            [_dg(rt[c, p], sd_b[p], NT)
             + _dot(mr_b[c, p], jnp.concatenate([_stack_heads(u_b[p], fmask0), vd_b[p]], axis=0)) for p in pairs],
            axis=1))
        w_tot = wtot_ref[c]
        for p in pairs:
            ds = _dg(jnp.concatenate([u_b[p], vb[c, p]], axis=0), jnp.concatenate([bt[c, p], kt[c, p]], axis=0), TN)
            sd[p] = (sd[p] + jnp.where(blk, ds, 0.0)) * w_tot[:, p * LANES:(p + 1) * LANES]
    for p in pairs:
        state_ref[p] = sd[p]

    y = ys[0] if n_ch == 1 else jnp.concatenate(ys, axis=0)
    inv_n = 1.0 / HEAD_DIM
    mu = _head_sum(y) * inv_n
    yc = y - mu
    var = _head_sum(yc * yc) * inv_n
    yn = yc * lax.rsqrt(var + GN_EPS) * lng_ref[...] + lnb_ref[...]
    y_ref[...] = ((yn + bonus_ref[...].astype(F32)) * g_ref[...]).astype(y_ref.dtype)

    @pl.when(step == n_steps - 1)
    def _():
        for p in pairs:
            sout_ref[0, 2 * p] = sd[p][:HEAD_DIM, :HEAD_DIM]
            sout_ref[0, 2 * p + 1] = sd[p][HEAD_DIM:, HEAD_DIM:]


def _rwkv_scan(at, rt, bt, kt, vb, wtot, bonus, g, s0, lng, lnb, *, chunk, rows, t_seq):
    m, w = at.shape
    n_b = m // t_seq
    n_steps = t_seq // rows
    n_heads = w // HEAD_DIM
    blk = pl.BlockSpec((rows, w), lambda bi, si: (bi * n_steps + si, 0))
    wt = pl.BlockSpec((rows // chunk, 1, w), lambda bi, si: (bi * n_steps + si, 0, 0))
    st = pl.BlockSpec((1, n_heads, HEAD_DIM, HEAD_DIM), lambda bi, si: (bi, 0, 0, 0))
    return pl.pallas_call(
        functools.partial(_rwkv_scan_kernel, chunk=chunk, n_steps=n_steps),
        grid=(n_b, n_steps),
        in_specs=[blk] * 5 + [wt, blk, blk, st, _const_spec(lng.shape), _const_spec(lnb.shape)],
        out_specs=[blk, st],
        out_shape=[jax.ShapeDtypeStruct((m, w), BF16),
                   jax.ShapeDtypeStruct((n_b, n_heads, HEAD_DIM, HEAD_DIM), F32)],
        scratch_shapes=[pltpu.VMEM((w // LANES, LANES, LANES), F32)],
        compiler_params=_params(("arbitrary", "arbitrary")),
        name="rwkv_scan",
    )(at, rt, bt, kt, vb, wtot, bonus, g, s0, lng, lnb)


def _lane_cumsum(x, block):
    tri = jnp.where(_iota((block, block), 0) <= _iota((block, block), 1), 1.0, 0.0)
    carry = jnp.zeros((x.shape[0], 1), F32)
    outs = []
    for j in range(x.shape[1] // block):
        c = _dot(x[:, j * block:(j + 1) * block], tri, HI) + carry
        outs.append(c)
        carry = c[:, block - 1:block]
    return outs[0] if len(outs) == 1 else jnp.concatenate(outs, axis=1)


def _fox_prompt_kernel(q_ref, k_ref, caug_ref, vt_ref, gate_ref, *rest, tq, tk, n_cast):
    cast_in, y_ref, cast_out = rest[:n_cast], rest[n_cast], rest[n_cast + 1:]
    hp = pl.program_id(1)
    qi = pl.program_id(2)
    nq = k_ref.shape[0] // tq
    vblk = vt_ref.shape[2]
    lane = _iota((tq, LANES), 1)
    first = lane < HEAD_DIM
    q = q_ref[...]
    zero = jnp.zeros_like(q)
    hslot = jnp.where(lane < 3 * FOX_HEADS, lax.rem(lane, FOX_HEADS), -1)
    qsa = []
    for h in range(2):
        ones = jnp.where(hslot == 2 * hp + h, 1.0, 0.0).astype(BF16)
        qh = jnp.where(first, q, zero) if h == 0 else jnp.where(first, zero, q)
        qsa.append(jnp.concatenate([qh, ones], axis=1))

    def scores(item):
        k0, q_lo, _ = item
        kk = jnp.concatenate([k_ref[k0:k0 + tk, :], caug_ref[k0:k0 + tk, :]], axis=1)
        rhs = jnp.concatenate([qsa[0][q_lo:], qsa[1][q_lo:]], axis=0)
        return _dg(kk, rhs, NT)

    def update(st, item, state):
        k0, q_lo, masked = item
        nqv = tq - q_lo
        vt = jnp.concatenate([vt_ref[k0 // vblk + i] for i in range(tk // vblk)], axis=1)
        out = []
        for h in range(2):
            m_all, l_all, acc_all = state[h]
            sh = st[:, h * nqv:(h + 1) * nqv]
            if masked:
                sh = jnp.where(_iota((tk, nqv), 0) <= _iota((tk, nqv), 1), sh, NEG_BIG)
            m_old = m_all[:, q_lo:]
            m_new = jnp.maximum(m_old, jnp.max(sh, axis=0, keepdims=True))
            alpha = jnp.exp2(m_old - m_new)
            p = jnp.exp2(sh - m_new)
            l_new = alpha * l_all[:, q_lo:] + jnp.sum(p, axis=0, keepdims=True)
            acc_new = alpha * acc_all[:, q_lo:] + _dot(vt[h * HEAD_DIM:(h + 1) * HEAD_DIM], p.astype(BF16))
            if q_lo:
                m_new = jnp.concatenate([m_all[:, :q_lo], m_new], axis=1)
                l_new = jnp.concatenate([l_all[:, :q_lo], l_new], axis=1)
                acc_new = jnp.concatenate([acc_all[:, :q_lo], acc_new], axis=1)
            out.append((m_new, l_new, acc_new))
        return out

    def program(c):
        items = [(kb * tk, 0, False) for kb in range(c * tq // tk)]
        items += [(c * tq + d * tk, d * tk, True) for d in range(tq // tk)]
        state = [(jnp.full((1, tq), NEG_BIG, F32), jnp.zeros((1, tq), F32), jnp.zeros((HEAD_DIM, tq), F32))
                 for _ in range(2)]
        st_next = scores(items[0])
        for n, item in enumerate(items):
            st = st_next
            if n + 1 < len(items):
                st_next = scores(items[n + 1])
            state = update(st, item, state)
        yts = []
        for _, l_fin, acc in state:
            o = acc * (1.0 / l_fin)
            ms = jnp.mean(o * o, axis=0, keepdims=True)
            yts.append(o * lax.rsqrt(ms + RMS_EPS))
        yt = jnp.concatenate(yts, axis=0)
        y_ref[...] = (yt.T * gate_ref[...]).astype(y_ref.dtype)
        for src, dst in zip(cast_in, cast_out):
            dst[...] = src[...].astype(BF16)

    for c in range(nq):
        pl.when(qi == c)(functools.partial(program, c))


def _fox_prompt_attn(qb, kb, caug, vbt, gate, *, n_b, t_seq, tq, tk, cast_cols=(), cast_rows=()):
    m, w = qb.shape
    n_pairs = w // LANES
    nq = t_seq // tq
    n_steps = n_b * n_pairs * nq
    vblk = vbt.shape[2]
    assert vbt.shape == (m // vblk, w, vblk) and tk % vblk == 0 and tq % tk == 0
    qspec = pl.BlockSpec((tq, LANES), lambda b, hp, qi: (b * nq + qi, hp))
    step = lambda b, hp, qi: (b * n_pairs + hp) * nq + qi
    casts = ([_cast_job(a, (a.shape[0], a.shape[1] // n_steps), lambda b, hp, qi: (0, step(b, hp, qi)))
              for a in cast_cols]
             + [_cast_job(a, (a.shape[0] // n_steps, a.shape[1]), lambda b, hp, qi: (step(b, hp, qi), 0))
                for a in cast_rows])
    outs = pl.pallas_call(
        functools.partial(_fox_prompt_kernel, tq=tq, tk=tk, n_cast=len(casts)),
        grid=(n_b, n_pairs, nq),
        in_specs=[qspec,
                  pl.BlockSpec((t_seq, LANES), lambda b, hp, qi: (b, hp)),
                  pl.BlockSpec((t_seq, LANES), lambda b, hp, qi: (b, 0)),
                  pl.BlockSpec((t_seq // vblk, LANES, vblk), lambda b, hp, qi: (b, hp, 0)),
                  qspec] + [spec for _, spec, _ in casts],
        out_specs=[qspec] + [spec for _, spec, _ in casts],
        out_shape=[jax.ShapeDtypeStruct((m, w), BF16)] + [sds for _, _, sds in casts],
        compiler_params=_params(("arbitrary", "arbitrary", "arbitrary")),
        name="fox_prompt_attn",
    )(qb, kb, caug, vbt, gate, *[a for a, _, _ in casts])
    return outs[0], outs[1:]


def _fox_sample_kernel(q_ref, kn_ref, vn_ref, ck_ref, cv_ref, clf_ref, lf_ref, gate_ref, y_ref):
    t = q_ref.shape[0]
    n_heads = lf_ref.shape[1]
    eye_h = jnp.where(_iota((n_heads, n_heads), 0) == _iota((n_heads, n_heads), 1), 1.0, 0.0)
    tril = jnp.where(_iota((t, t), 0) >= _iota((t, t), 1), 1.0, 0.0)
    clf = clf_ref[0]
    c_tot = jnp.sum(clf, axis=0, keepdims=True)
    cn_col = _dot(tril, lf_ref[...], HI)
    cn_row = _dg(eye_h, cn_col, NT, HI)
    cq_col = cn_col + c_tot
    cc_row = _lane_cumsum(_dg(eye_h, clf, NT, HI), 256)
    past = clf.shape[0]
    causal = _iota((t, t), 1) <= _iota((t, t), 0)
    heads = range(n_heads)
    sls = [slice(hd * HEAD_DIM, (hd + 1) * HEAD_DIM) for hd in heads]
    kc = [ck_ref[0, pl.ds(hd, past, stride=n_heads), :].astype(BF16) for hd in heads]
    sc = [_dg(q_ref[:, sls[hd]], kc[hd], NT) + (cq_col[:, hd:hd + 1] - cc_row[hd:hd + 1]) * LOG2E for hd in heads]
    sn = [jnp.where(causal, _dg(q_ref[:, sls[hd]], kn_ref[:, sls[hd]], NT)
                    + (cn_col[:, hd:hd + 1] - cn_row[hd:hd + 1]) * LOG2E, NEG_BIG) for hd in heads]
    mx = [jnp.maximum(jnp.max(sc[hd], axis=-1, keepdims=True), jnp.max(sn[hd], axis=-1, keepdims=True))
          for hd in heads]
    pc = [jnp.exp2(sc[hd] - mx[hd]) for hd in heads]
    pn = [jnp.exp2(sn[hd] - mx[hd]) for hd in heads]
    vc = [cv_ref[0, pl.ds(hd, past, stride=n_heads), :].astype(BF16) for hd in heads]
    ys = []
    for hd in heads:
        den = jnp.sum(pc[hd], axis=-1, keepdims=True) + jnp.sum(pn[hd], axis=-1, keepdims=True)
        o = (_dot(pc[hd].astype(BF16), vc[hd]) + _dot(pn[hd].astype(BF16), vn_ref[:, sls[hd]])) / den
        ms = jnp.mean(o * o, axis=-1, keepdims=True)
        ys.append(o * lax.rsqrt(ms + RMS_EPS) * gate_ref[:, sls[hd]])
    y_ref[...] = jnp.concatenate(ys, axis=1).astype(y_ref.dtype)


def _fox_sample_attn(qb, kb, vb, cache_k, cache_v, cache_lf, lf, gate, *, t_seq):
    m, w = qb.shape
    n_b = m // t_seq
    row = pl.BlockSpec((t_seq, w), lambda b: (b, 0))
    cache = pl.BlockSpec((1,) + cache_k.shape[1:], lambda b: (b, 0, 0))
    return pl.pallas_call(
        _fox_sample_kernel,
        grid=(n_b,),
        in_specs=[row, row, row, cache, cache,
                  pl.BlockSpec((1,) + cache_lf.shape[1:], lambda b: (b, 0, 0)),
                  pl.BlockSpec((t_seq, lf.shape[1]), lambda b: (b, 0)), row],
        out_specs=row,
        out_shape=jax.ShapeDtypeStruct((m, w), BF16),
        compiler_params=_params(("arbitrary",)),
        name="fox_sample_attn",
    )(qb, kb, vb, cache_k, cache_v, cache_lf, lf, gate)


def _out_ln_kernel(yr_ref, yf_ref, x_ref, wo_ref, g_ref, b_ref, h_ref, wob_ref, *, alpha):
    @pl.when(pl.program_id(0) == 0)
    def _():
        wob_ref[...] = wo_ref[...].astype(BF16)

    half = yr_ref.shape[1]
    mix = _dot(yr_ref[...], wob_ref[0:half, :]) + _dot(yf_ref[...], wob_ref[half:, :])
    h_ref[...] = _layer_norm(alpha * x_ref[...] + mix, g_ref[...], b_ref[...])


def _out_ln(yr, yf, x, wo, g, b, *, tm, alpha):
    m, d = x.shape
    row = lambda i: (i, 0)
    return pl.pallas_call(
        functools.partial(_out_ln_kernel, alpha=alpha),
        grid=(m // tm,),
        in_specs=[pl.BlockSpec((tm, yr.shape[1]), row), pl.BlockSpec((tm, yf.shape[1]), row),
                  pl.BlockSpec((tm, d), row), _const_spec(wo.shape), _const_spec(g.shape), _const_spec(b.shape)],
        out_specs=pl.BlockSpec((tm, d), row),
        out_shape=jax.ShapeDtypeStruct((m, d), F32),
        scratch_shapes=[pltpu.VMEM(wo.shape, BF16)],
        compiler_params=_params(("arbitrary",)),
        name="out_ln",
    )(yr, yf, x, wo, g, b)


def _ffn_ln_kernel(h_ref, wu_ref, wd_ref, g_ref, b_ref, k2_ref, v2_ref, o_ref, k5_ref, v5_ref, hb_ref,
                   *, alpha, n_f):
    j = pl.program_id(1)

    @pl.when(j == 0)
    def _():
        h = h_ref[...]
        hb_ref[...] = h.astype(BF16)
        o_ref[...] = alpha * h

    u = jnp.maximum(_dot(hb_ref[...], wu_ref[...]), 0.0)
    o_ref[...] += _dot((u * u).astype(BF16), wd_ref[...])
    for src, dst in ((k2_ref, k5_ref), (v2_ref, v5_ref)):
        for hd in range(dst.shape[1]):
            dst[:, hd, :] = src[:, hd * HEAD_DIM:(hd + 1) * HEAD_DIM]

    @pl.when(j == n_f - 1)
    def _():
        o_ref[...] = _layer_norm(o_ref[...], g_ref[...], b_ref[...])


def _ffn_ln(h, wu, wd, g, b, k2, v2, *, tm, tf, alpha):
    m, d = h.shape
    n_f = wu.shape[1] // tf
    rows = tm // n_f
    heads = k2.shape[1] // HEAD_DIM
    slab_in = pl.BlockSpec((rows, k2.shape[1]), lambda i, j: (i * n_f + j, 0))
    slab_out = pl.BlockSpec((rows, heads, HEAD_DIM), lambda i, j: (i * n_f + j, 0, 0))
    kv5 = jax.ShapeDtypeStruct((m, heads, HEAD_DIM), F32)
    return pl.pallas_call(
        functools.partial(_ffn_ln_kernel, alpha=alpha, n_f=n_f),
        grid=(m // tm, n_f),
        in_specs=[pl.BlockSpec((tm, d), lambda i, j: (i, 0), pipeline_mode=pl.Buffered(1)),
                  pl.BlockSpec((d, tf), lambda i, j: (0, j)),
                  pl.BlockSpec((tf, d), lambda i, j: (j, 0)),
                  _const_spec(g.shape), _const_spec(b.shape), slab_in, slab_in],
        out_specs=[pl.BlockSpec((tm, d), lambda i, j: (i, 0)), slab_out, slab_out],
        out_shape=[jax.ShapeDtypeStruct((m, d), F32), kv5, kv5],
        scratch_shapes=[pltpu.VMEM((tm, d), BF16)],
        compiler_params=_params(("arbitrary", "arbitrary")),
        name="ffn_ln",
    )(h, wu, wd, g, b, k2, v2)


def _pad_cols(x, n):
    return jnp.pad(x, [(0, 0)] * (x.ndim - 1) + [(0, n - x.shape[-1])])


def _stream(x, shift_prev, s0, wts, *, t_seq, tm, tm_ffn, chunk, cache=None):
    n_b, _, d = x.shape
    m = n_b * t_seq
    x2 = x.reshape(m, d)
    qb, k32, v32, kb, vb, vbt, gate, lf, caug = _fox_proj(x2, wts["w_f"], wts["bf_row"], wts["og_g"],
                                                          tm=tm, t_seq=t_seq)
    fprev = _pad_cols(shift_prev, RP_PAD)
    at, rt, bt, kt, vr, bonus, g, wtot, sh = _rwkv_proj(
        x2, wts["w_r"], wts["mu"], fprev, wts["w0"], wts["w2p"], wts["a0"], wts["a2p"], wts["g2p"], wts["k_k"],
        wts["k_a"], wts["r_k"], tm=tm, t_seq=t_seq, chunk=chunk)
    y_r, s_new = _rwkv_scan(at, rt, bt, kt, vr, wtot, bonus, g, s0, wts["lnx_g"], wts["lnx_b"],
                            chunk=chunk, rows=min(4 * chunk, t_seq), t_seq=t_seq)
    if cache is None:
        y_f, (wts["w_up_b"], wts["w_down_b"]) = _fox_prompt_attn(
            qb, kb, caug, vbt, gate, n_b=n_b, t_seq=t_seq, tq=1024, tk=512,
            cast_cols=(wts["w_up"],), cast_rows=(wts["w_down"],))
    else:
        ck, cv, clf = cache
        flat = lambda z: z.reshape(n_b, z.shape[1] * z.shape[2], z.shape[3])
        y_f = _fox_sample_attn(qb, kb, vb, flat(ck), flat(cv), clf, lf, gate, t_seq=t_seq)
    h = _out_ln(y_r, y_f, x2, wts["w_o"], wts["ln1_g"], wts["ln1_b"], tm=tm, alpha=wts["alpha"])
    y, k5, v5 = _ffn_ln(h, wts["w_up_b"], wts["w_down_b"], wts["ln2_g"], wts["ln2_b"], k32, v32,
                        tm=tm_ffn, tf=1024, alpha=wts["alpha"])
    heads = FW // HEAD_DIM
    return (y.reshape(n_b, t_seq, d), k5.reshape(n_b, t_seq, heads, HEAD_DIM),
            v5.reshape(n_b, t_seq, heads, HEAD_DIM), lf.reshape(n_b, t_seq, heads), s_new,
            sh[..., :shift_prev.shape[-1]])


def kernel(x_prompt, x_sample, cache_fox_k, cache_fox_v, cache_fox_logf, state_rwkv_wkv, state_rwkv_shift,
           w_in, rwkv_mu, rwkv_w0, rwkv_w2, rwkv_a0, rwkv_a2, rwkv_g2, rwkv_k_k, rwkv_k_a, rwkv_r_k,
           rwkv_lnx_g, rwkv_lnx_b, fox_b_f, fox_out_g, w_o, ln1_g, ln1_b, w_up, w_down, ln2_g, ln2_b):
    depth = w_in.shape[0]
    assert depth == 1, "single-layer problem"
    d_model = x_prompt.shape[-1]
    rwkv_proj = rwkv_mu.shape[-1]
    lora = (rwkv_w2.shape[1], rwkv_a2.shape[1], rwkv_g2.shape[1])
    assert rwkv_w0.shape[-1] == RW and fox_out_g.shape[-1] == FW and rwkv_proj == 3 * RW + sum(lora)
    assert lora[0] + lora[1] == LANES and lora[2] <= 2 * LANES and RP_PAD <= w_in.shape[-1]
    alpha = (2 * depth) ** 0.25
    l = 0
    w = w_in[l]
    fo = rwkv_proj
    row = lambda z: z.reshape(1, -1)
    pad_rows = lambda z, n: jnp.pad(z, ((0, n - z.shape[0]), (0, 0)))
    wts = dict(
        alpha=alpha,
        w_r=w[:, :RP_PAD].astype(BF16),
        w_f=jnp.concatenate([w[:, fo:fo + 3 * FW], w[:, fo + 3 * FW + FOX_HEADS:],
                             _pad_cols(jnp.tile(w[:, fo + 3 * FW:fo + 3 * FW + FOX_HEADS], (1, 3)), LANES)],
                            axis=-1).astype(BF16),
        mu=_pad_cols(row(rwkv_mu[l]), RP_PAD),
        w0=row(rwkv_w0[l]), w2p=jnp.pad(rwkv_w2[l], ((0, lora[1]), (0, 0))).astype(BF16),
        a0=row(rwkv_a0[l]), a2p=jnp.pad(rwkv_a2[l], ((lora[0], 0), (0, 0))).astype(BF16),
        g2p=pad_rows(rwkv_g2[l], 256).astype(BF16),
        k_k=row(rwkv_k_k[l]), k_a=row(rwkv_k_a[l]), r_k=row(rwkv_r_k[l]),
        lnx_g=row(rwkv_lnx_g[l]), lnx_b=row(rwkv_lnx_b[l]),
        bf_row=_pad_cols(jnp.tile(row(fox_b_f[l]), (1, 3)), LANES), og_g=row(fox_out_g[l]),
        w_o=w_o[l], ln1_g=row(ln1_g[l]), ln1_b=row(ln1_b[l]),
        w_up=w_up[l], w_down=w_down[l], ln2_g=row(ln2_g[l]), ln2_b=row(ln2_b[l]),
    )
    n_p, t_p, _ = x_prompt.shape
    n_s, t_s, _ = x_sample.shape
    heads = RW // HEAD_DIM
    shift0 = jnp.zeros((n_p, 1, rwkv_proj), F32)
    s_zero = jnp.zeros((n_p, heads, HEAD_DIM, HEAD_DIM), F32)
    yp, kp, vp, fp, sp, shp = _stream(x_prompt, shift0, s_zero, wts, t_seq=t_p, tm=256, tm_ffn=1024, chunk=64)
    ys, ks, vs, fs, ss, shs = _stream(x_sample, state_rwkv_shift[l], state_rwkv_wkv[l], wts, t_seq=t_s,
                                      tm=n_s * t_s, tm_ffn=n_s * t_s, chunk=t_s,
                                      cache=(cache_fox_k[l], cache_fox_v[l], cache_fox_logf[l]))
    return (yp, ys, kp[None], vp[None], fp[None], sp[None], shp[None],
            ks[None], vs[None], fs[None], ss[None], shs[None])
```

```python
import functools
import math

import jax
import jax.numpy as jnp
from jax import lax
from jax.experimental import pallas as pl
from jax.experimental.pallas import tpu as pltpu

F32 = jnp.float32
BF16 = jnp.bfloat16
HI = lax.Precision.HIGHEST

HEAD_DIM = 64
LANES = 128
LN_EPS = 1e-5
GN_EPS = 64e-5
RMS_EPS = 1e-6
ATTN_SCALE = HEAD_DIM ** -0.5
EXP_NEG_HALF = math.exp(-0.5)
LOG2E = math.log2(math.e)
NEG_BIG = -1e30
VMEM_LIMIT = 60 * 1024 * 1024

NT = (((1,), (1,)), ((), ()))
TN = (((0,), (0,)), ((), ()))


def _sigmoid(x):
    return 1.0 / (1.0 + jnp.exp(-x))


def _log_sigmoid(x):
    return jnp.minimum(x, 0.0) - jnp.log1p(jnp.exp(-jnp.abs(x)))


def _dot(a, b, precision=None):
    return jnp.dot(a, b, preferred_element_type=F32, precision=precision)


def _dg(a, b, dims, precision=None):
    return lax.dot_general(a, b, dims, preferred_element_type=F32, precision=precision)


def _split3(x):
    hi = x.astype(BF16)
    rem = x - hi.astype(F32)
    mid = rem.astype(BF16)
    return hi, mid, (rem - mid.astype(F32)).astype(BF16)


def _dot01(m01, x):
    hi, mid, lo = _split3(x)
    return _dot(m01, hi) + _dot(m01, mid) + _dot(m01, lo)


def _iota(shape, axis):
    return lax.broadcasted_iota(jnp.int32, shape, axis)


def _head_sum(x):
    r = _iota((LANES, LANES), 0) // HEAD_DIM
    c = _iota((LANES, LANES), 1) // HEAD_DIM
    ones_blk = jnp.where(r == c, 1.0, 0.0).astype(BF16)
    outs = []
    for g in range(x.shape[1] // LANES):
        xs = x[:, g * LANES:(g + 1) * LANES]
        hi = xs.astype(BF16)
        lo = (xs - hi.astype(F32)).astype(BF16)
        outs.append(_dot(hi, ones_blk) + _dot(lo, ones_blk))
    return outs[0] if len(outs) == 1 else jnp.concatenate(outs, axis=1)


def _layer_norm(z, g, b):
    mu = jnp.mean(z, axis=-1, keepdims=True)
    zc = z - mu
    var = jnp.mean(zc * zc, axis=-1, keepdims=True)
    return zc * lax.rsqrt(var + LN_EPS) * g + b


def _const_spec(shape):
    nd = len(shape)
    return pl.BlockSpec(shape, lambda *_: (0,) * nd, pipeline_mode=pl.Buffered(1))


def _params(sem):
    return pltpu.CompilerParams(dimension_semantics=sem, vmem_limit_bytes=VMEM_LIMIT)


RW = 1024
RP_PAD = 3 * RW + 128 + 256


def _rwkv_proj_kernel(x_ref, w_ref, mu_ref, fp_ref, w0_ref, w2_ref, a0_ref, a2_ref, g2_ref, kk_ref, ka_ref,
                      at_ref, rt_ref, bt_ref, kt_ref, vb_ref, g_ref, wtot_ref, sh_ref, carry_ref,
                      *, tm, t_seq, chunk):
    i = pl.program_id(0)
    xb = x_ref[...].astype(BF16)
    rows = _iota((tm, 1), 0)

    def proj(c0, n):
        return _dot(xb, w_ref[:, c0:c0 + n])

    def shift(p, c0):
        n = p.shape[1]
        prev = pltpu.roll(p, 1, 0)
        if t_seq >= tm:
            tiles = t_seq // tm
            pos = lax.rem(i, tiles)
            sidx = lax.div(i, tiles)
            row0 = jnp.where(pos == 0, fp_ref[sidx, :, c0:c0 + n], carry_ref[:, c0:c0 + n])
            prev = jnp.where(rows == 0, row0, prev)
            carry_ref[:, c0:c0 + n] = p[tm - 1:tm, :]

            @pl.when(pos == tiles - 1)
            def _():
                sh_ref[sidx, :, c0:c0 + n] = p[tm - 1:tm, :]
        else:
            per_tile = tm // t_seq
            for j in range(per_tile):
                prev = jnp.where(rows == j * t_seq, fp_ref[i * per_tile + j, :, c0:c0 + n], prev)
                sh_ref[i * per_tile + j, :, c0:c0 + n] = p[(j + 1) * t_seq - 1:(j + 1) * t_seq, :]
        return p + (prev - p) * mu_ref[:, c0:c0 + n]

    p_wa = proj(3 * RW, 128)
    p_g = proj(3 * RW + 128, 256)
    p_k = proj(RW, RW)
    xwa = shift(p_wa, 3 * RW)
    xg = shift(p_g, 3 * RW + 128)
    wl = w0_ref[...] + _dot(jnp.tanh(xwa).astype(BF16), w2_ref[...])
    alr = _sigmoid(a0_ref[...] + _dot(xwa.astype(BF16), a2_ref[...]))
    g_ref[...] = _dot(_sigmoid(xg).astype(BF16), g2_ref[...]).astype(BF16)
    k = shift(p_k, RW)
    kk = k * kk_ref[...]
    kk_ss = _head_sum(kk * kk)
    p_r = proj(0, RW)
    lw = -EXP_NEG_HALF * _sigmoid(wl)
    rr = _iota((tm, tm), 0)
    cc = _iota((tm, tm), 1)
    same_chunk_tri = jnp.where(rr // chunk == cc // chunk, jnp.where(rr >= cc, 1.0, 0.0), 0.0).astype(BF16)
    lwc = _dot01(same_chunk_tri, lw)
    p_v = proj(2 * RW, RW)
    kkn = kk / jnp.maximum(jnp.sqrt(kk_ss), 1e-12)
    kh = k * (1.0 + (alr - 1.0) * ka_ref[...])
    e_in = jnp.exp(lwc)
    e_out = jnp.exp(-lwc)
    at_ref[...] = (-kkn * jnp.exp(lwc - lw)).astype(BF16)
    bt_ref[...] = (kkn * alr * e_out).astype(BF16)
    kt_ref[...] = (kh * e_out).astype(BF16)
    for c in range(tm // chunk):
        wtot_ref[c] = e_in[(c + 1) * chunk - 1:(c + 1) * chunk, :]
    r = shift(p_r, 0)
    rt_ref[...] = (r * e_in).astype(BF16)
    vb_ref[...] = shift(p_v, 2 * RW).astype(BF16)


def _rwkv_proj(x, w, mu, fprev, w0, w2p, a0, a2p, g2p, k_k, k_a, *, tm, t_seq, chunk):
    m, d = x.shape
    n_seq = m // t_seq
    row = lambda i: (i, 0)
    big = lambda dt: jax.ShapeDtypeStruct((m, RW), dt)
    return pl.pallas_call(
        functools.partial(_rwkv_proj_kernel, tm=tm, t_seq=t_seq, chunk=chunk),
        grid=(m // tm,),
        in_specs=[pl.BlockSpec((tm, d), row),
                  _const_spec(w.shape), _const_spec(mu.shape), _const_spec(fprev.shape),
                  _const_spec(w0.shape), _const_spec(w2p.shape), _const_spec(a0.shape), _const_spec(a2p.shape),
                  _const_spec(g2p.shape), _const_spec(k_k.shape), _const_spec(k_a.shape)],
        out_specs=[pl.BlockSpec((tm, RW), row)] * 6
                  + [pl.BlockSpec((tm // chunk, 1, RW), lambda i: (i, 0, 0)),
                     pl.BlockSpec((n_seq, 1, RP_PAD), lambda i: (0, 0, 0))],
        out_shape=[big(BF16)] * 6 + [jax.ShapeDtypeStruct((m // chunk, 1, RW), F32),
                                     jax.ShapeDtypeStruct((n_seq, 1, RP_PAD), F32)],
        scratch_shapes=[pltpu.VMEM((1, RP_PAD), F32)],
        compiler_params=_params(("arbitrary",)),
        name="rwkv_proj",
    )(x, w, mu, fprev, w0, w2p, a0, a2p, g2p, k_k, k_a)


FW = 1024
FOX_HEADS = FW // HEAD_DIM


def _fox_proj_kernel(x_ref, w_ref, bf_ref, og_ref,
                     qb_ref, k_ref, v_ref, kb_ref, vb_ref, vbt_ref, gate_ref, lf_ref, caug_ref, carry_ref,
                     *, tm, t_seq):
    i = pl.program_id(0)
    xb = x_ref[...].astype(BF16)
    qb_ref[...] = (_dot(xb, w_ref[:, 0:FW]) * (ATTN_SCALE * LOG2E)).astype(BF16)
    k = _dot(xb, w_ref[:, FW:2 * FW])
    k_ref[...] = k
    kb_ref[...] = k.astype(BF16)
    v = _dot(xb, w_ref[:, 2 * FW:3 * FW])
    v_ref[...] = v
    vb_ref[...] = v.astype(BF16)
    vbt_ref[0] = v.T.astype(BF16)
    og = _dot(xb, w_ref[:, 3 * FW:4 * FW])
    gate_ref[...] = _sigmoid(og) * og_ref[...]
    logf = _log_sigmoid(_dot(xb, w_ref[:, 4 * FW:4 * FW + LANES]) + bf_ref[...])
    lf_ref[...] = logf[:, :FOX_HEADS]
    r = _iota((tm, tm), 0)
    c = _iota((tm, tm), 1)
    if t_seq >= tm:
        cs = _dot01(jnp.where(r >= c, 1.0, 0.0).astype(BF16), logf)
        cs = cs + jnp.where(lax.rem(i, t_seq // tm) == 0, 0.0, carry_ref[...])
        carry_ref[...] = cs[tm - 1:tm, :]
    else:
        same_seq = (r // t_seq) == (c // t_seq)
        cs = _dot01(jnp.where(same_seq, jnp.where(r >= c, 1.0, 0.0), 0.0).astype(BF16), logf)
    xs = -LOG2E * cs
    hi, mid, lo = _split3(xs)
    lane = _iota((tm, LANES), 1)
    zero = jnp.zeros_like(hi)
    caug_ref[...] = jnp.where(lane < FOX_HEADS, hi,
                              jnp.where(lane < 2 * FOX_HEADS, mid, jnp.where(lane < 3 * FOX_HEADS, lo, zero)))


def _cast_job(a, block, index_map, out_shape=None):
    spec = pl.BlockSpec(block, index_map)
    return a, spec, jax.ShapeDtypeStruct(out_shape or a.shape, BF16)


def _fox_proj(x, w, bf_row, og_g, *, tm, t_seq):
    m, d = x.shape
    row = lambda i: (i, 0)
    big = lambda dt: jax.ShapeDtypeStruct((m, FW), dt)
    return pl.pallas_call(
        functools.partial(_fox_proj_kernel, tm=tm, t_seq=t_seq),
        grid=(m // tm,),
        in_specs=[pl.BlockSpec((tm, d), row), _const_spec(w.shape), _const_spec(bf_row.shape),
                  _const_spec(og_g.shape)],
        out_specs=[pl.BlockSpec((tm, FW), row)] * 5
                  + [pl.BlockSpec((1, FW, tm), lambda i: (i, 0, 0)), pl.BlockSpec((tm, FW), row),
                     pl.BlockSpec((tm, FOX_HEADS), row), pl.BlockSpec((tm, LANES), row)],
        out_shape=[big(BF16), big(F32), big(F32), big(BF16), big(BF16),
                   jax.ShapeDtypeStruct((m // tm, FW, tm), BF16), big(F32),
                   jax.ShapeDtypeStruct((m, FOX_HEADS), F32), jax.ShapeDtypeStruct((m, LANES), BF16)],
        scratch_shapes=[pltpu.VMEM((1, LANES), F32)],
        compiler_params=_params(("arbitrary",)),
        name="fox_proj",
    )(x, w, bf_row, og_g)


def _stack_heads(x, mask0):
    return jnp.concatenate([jnp.where(mask0, x, 0.0), jnp.where(mask0, 0.0, x)], axis=0)


def _rwkv_scan_kernel(at_ref, rt_ref, bt_ref, kt_ref, vb_ref, wtot_ref, g_ref, s0_ref, lng_ref, lnb_ref, rk_ref,
                      y_ref, sout_ref, state_ref, *, chunk, n_steps):
    step = pl.program_id(1)
    n_pairs = at_ref.shape[1] // LANES
    C = chunk
    n_ch = at_ref.shape[0] // C
    pairs = range(n_pairs)

    @pl.when(step == 0)
    def _():
        zero = jnp.zeros((HEAD_DIM, HEAD_DIM), F32)
        for p in pairs:
            top = jnp.concatenate([s0_ref[0, 2 * p], zero], axis=1)
            bot = jnp.concatenate([zero, s0_ref[0, 2 * p + 1]], axis=1)
            state_ref[p] = jnp.concatenate([top, bot], axis=0)

    fmask0 = _iota((1, LANES), 1) < HEAD_DIM
    tcol = _iota((C, 2 * C), 1)
    trow = _iota((C, 2 * C), 0)
    tmask0 = tcol < C
    tj = jnp.where(tmask0, tcol, tcol - C)
    strict = tj < trow
    incl = tj <= trow
    eye_pair = jnp.where(tj == trow, 1.0, 0.0)
    blk = (_iota((LANES, LANES), 0) // HEAD_DIM) == (_iota((LANES, LANES), 1) // HEAD_DIM)

    def stack_t(x):
        return _stack_heads(x, tmask0)

    def tile(ref, c, p):
        return ref[c * C:(c + 1) * C, p * LANES:(p + 1) * LANES]

    probs = [(c, p) for c in range(n_ch) for p in pairs]
    at = {cp: tile(at_ref, *cp) for cp in probs}
    rt = {cp: tile(rt_ref, *cp) for cp in probs}
    bt = {cp: tile(bt_ref, *cp) for cp in probs}
    kt = {cp: tile(kt_ref, *cp) for cp in probs}
    vb = {cp: tile(vb_ref, *cp) for cp in probs}
    gm = {cp: _dg(jnp.concatenate([at[cp], rt[cp]], axis=0),
                  jnp.concatenate([_stack_heads(bt[cp], fmask0), _stack_heads(kt[cp], fmask0)], axis=0), NT)
          for cp in probs}
    lab = {cp: jnp.where(strict, gm[cp][:C, :2 * C], 0.0) for cp in probs}
    lak_b = {cp: jnp.where(strict, gm[cp][:C, 2 * C:], 0.0).astype(BF16) for cp in probs}
    mr_b = {cp: jnp.concatenate([jnp.where(incl, gm[cp][C:, :2 * C], 0.0),
                                 jnp.where(incl, gm[cp][C:, 2 * C:], 0.0)], axis=1).astype(BF16) for cp in probs}
    kmax = int(math.log2(C)) - 1
    tinv = {cp: eye_pair + lab[cp] for cp in probs}
    pw = {cp: lab[cp].astype(BF16) for cp in probs}
    pw = {cp: _dot(pw[cp], stack_t(pw[cp])).astype(BF16) for cp in probs}
    for _ in range(1, kmax):
        res = {cp: _dot(jnp.concatenate([pw[cp], tinv[cp].astype(BF16)], axis=0), stack_t(pw[cp])) for cp in probs}
        pw = {cp: res[cp][:C].astype(BF16) for cp in probs}
        tinv = {cp: tinv[cp] + res[cp][C:] for cp in probs}
    tinv_b = {cp: (tinv[cp] + _dot(tinv[cp].astype(BF16), stack_t(pw[cp]))).astype(BF16) for cp in probs}

    sd = [state_ref[p] for p in pairs]
    ys = []
    for c in range(n_ch):
        sd_b = [z.astype(BF16) for z in sd]
        vd_b = [_stack_heads(vb[c, p], fmask0) for p in pairs]
        x = [_dg(at[c, p], sd_b[p], NT) + _dot(lak_b[c, p], vd_b[p]) for p in pairs]
        u_b = [_dot(tinv_b[c, p], _stack_heads(x[p].astype(BF16), fmask0)).astype(BF16) for p in pairs]
        ys.append(jnp.concatenate(
            [_dg(rt[c, p], sd_b[p], NT)
             + _dot(mr_b[c, p], jnp.concatenate([_stack_heads(u_b[p], fmask0), vd_b[p]], axis=0)) for p in pairs],
            axis=1))
        w_tot = wtot_ref[c]
        for p in pairs:
            ds = _dg(jnp.concatenate([u_b[p], vb[c, p]], axis=0), jnp.concatenate([bt[c, p], kt[c, p]], axis=0), TN)
            sd[p] = (sd[p] + jnp.where(blk, ds, 0.0)) * w_tot[:, p * LANES:(p + 1) * LANES]
    for p in pairs:
        state_ref[p] = sd[p]

    y = ys[0] if n_ch == 1 else jnp.concatenate(ys, axis=0)
    inv_n = 1.0 / HEAD_DIM
    mu = _head_sum(y) * inv_n
    yc = y - mu
    var = _head_sum(yc * yc) * inv_n
    yn = yc * lax.rsqrt(var + GN_EPS) * lng_ref[...] + lnb_ref[...]
    rk = rt_ref[...].astype(F32) * kt_ref[...].astype(F32) * rk_ref[...]
    bonus = _head_sum(rk) * vb_ref[...].astype(F32)
    y_ref[...] = ((yn + bonus) * g_ref[...].astype(F32)).astype(y_ref.dtype)

    @pl.when(step == n_steps - 1)
    def _():
        for p in pairs:
            sout_ref[0, 2 * p] = sd[p][:HEAD_DIM, :HEAD_DIM]
            sout_ref[0, 2 * p + 1] = sd[p][HEAD_DIM:, HEAD_DIM:]


def _rwkv_scan(at, rt, bt, kt, vb, wtot, g, s0, lng, lnb, rk, *, chunk, rows, t_seq):
    m, w = at.shape
    n_b = m // t_seq
    n_steps = t_seq // rows
    n_heads = w // HEAD_DIM
    blk = pl.BlockSpec((rows, w), lambda bi, si: (bi * n_steps + si, 0))
    wt = pl.BlockSpec((rows // chunk, 1, w), lambda bi, si: (bi * n_steps + si, 0, 0))
    st = pl.BlockSpec((1, n_heads, HEAD_DIM, HEAD_DIM), lambda bi, si: (bi, 0, 0, 0))
    return pl.pallas_call(
        functools.partial(_rwkv_scan_kernel, chunk=chunk, n_steps=n_steps),
        grid=(n_b, n_steps),
        in_specs=[blk] * 5 + [wt, blk, st, _const_spec(lng.shape), _const_spec(lnb.shape), _const_spec(rk.shape)],
        out_specs=[blk, st],
        out_shape=[jax.ShapeDtypeStruct((m, w), BF16),
                   jax.ShapeDtypeStruct((n_b, n_heads, HEAD_DIM, HEAD_DIM), F32)],
        scratch_shapes=[pltpu.VMEM((w // LANES, LANES, LANES), F32)],
        compiler_params=_params(("arbitrary", "arbitrary")),
        name="rwkv_scan",
    )(at, rt, bt, kt, vb, wtot, g, s0, lng, lnb, rk)


def _lane_cumsum(x, block):
    tri = jnp.where(_iota((block, block), 0) <= _iota((block, block), 1), 1.0, 0.0)
    carry = jnp.zeros((x.shape[0], 1), F32)
    outs = []
    for j in range(x.shape[1] // block):
        c = _dot(x[:, j * block:(j + 1) * block], tri, HI) + carry
        outs.append(c)
        carry = c[:, block - 1:block]
    return outs[0] if len(outs) == 1 else jnp.concatenate(outs, axis=1)


def _fox_prompt_kernel(q_ref, k_ref, caug_ref, vt_ref, gate_ref, *rest, tq, tk, n_cast):
    cast_in, y_ref, cast_out = rest[:n_cast], rest[n_cast], rest[n_cast + 1:]
    hp = pl.program_id(1)
    qi = pl.program_id(2)
    nq = k_ref.shape[0] // tq
    vblk = vt_ref.shape[2]
    lane = _iota((tq, LANES), 1)
    first = lane < HEAD_DIM
    q = q_ref[...]
    zero = jnp.zeros_like(q)
    hslot = jnp.where(lane < 3 * FOX_HEADS, lax.rem(lane, FOX_HEADS), -1)
    qsa = []
    for h in range(2):
        ones = jnp.where(hslot == 2 * hp + h, 1.0, 0.0).astype(BF16)
        qh = jnp.where(first, q, zero) if h == 0 else jnp.where(first, zero, q)
        qsa.append(jnp.concatenate([qh, ones], axis=1))

    def scores(item):
        k0, q_lo, _ = item
        kk = jnp.concatenate([k_ref[k0:k0 + tk, :], caug_ref[k0:k0 + tk, :]], axis=1)
        rhs = jnp.concatenate([qsa[0][q_lo:], qsa[1][q_lo:]], axis=0)
        return _dg(kk, rhs, NT)

    def update(st, item, state):
        k0, q_lo, masked = item
        nqv = tq - q_lo
        vt = jnp.concatenate([vt_ref[k0 // vblk + i] for i in range(tk // vblk)], axis=1)
        out = []
        for h in range(2):
            m_all, l_all, acc_all = state[h]
            sh = st[:, h * nqv:(h + 1) * nqv]
            if masked:
                sh = jnp.where(_iota((tk, nqv), 0) <= _iota((tk, nqv), 1), sh, NEG_BIG)
            m_old = m_all[:, q_lo:]
            m_new = jnp.maximum(m_old, jnp.max(sh, axis=0, keepdims=True))
            alpha = jnp.exp2(m_old - m_new)
            p = jnp.exp2(sh - m_new)
            l_new = alpha * l_all[:, q_lo:] + jnp.sum(p, axis=0, keepdims=True)
            acc_new = alpha * acc_all[:, q_lo:] + _dot(vt[h * HEAD_DIM:(h + 1) * HEAD_DIM], p.astype(BF16))
            if q_lo:
                m_new = jnp.concatenate([m_all[:, :q_lo], m_new], axis=1)
                l_new = jnp.concatenate([l_all[:, :q_lo], l_new], axis=1)
                acc_new = jnp.concatenate([acc_all[:, :q_lo], acc_new], axis=1)
            out.append((m_new, l_new, acc_new))
        return out

    def program(c):
        items = [(kb * tk, 0, False) for kb in range(c * tq // tk)]
        items += [(c * tq + d * tk, d * tk, True) for d in range(tq // tk)]
        state = [(jnp.full((1, tq), NEG_BIG, F32), jnp.zeros((1, tq), F32), jnp.zeros((HEAD_DIM, tq), F32))
                 for _ in range(2)]
        st_next = scores(items[0])
        for n, item in enumerate(items):
            st = st_next
            if n + 1 < len(items):
                st_next = scores(items[n + 1])
            state = update(st, item, state)
        yts = []
        for _, l_fin, acc in state:
            o = acc * (1.0 / l_fin)
            ms = jnp.mean(o * o, axis=0, keepdims=True)
            yts.append(o * lax.rsqrt(ms + RMS_EPS))
        yt = jnp.concatenate(yts, axis=0)
        y_ref[...] = (yt.T * gate_ref[...]).astype(y_ref.dtype)
        for src, dst in zip(cast_in, cast_out):
            dst[...] = src[...].astype(BF16)

    for c in range(nq):
        pl.when(qi == c)(functools.partial(program, c))


def _fox_prompt_attn(qb, kb, caug, vbt, gate, *, n_b, t_seq, tq, tk, cast_cols=(), cast_rows=()):
    m, w = qb.shape
    n_pairs = w // LANES
    nq = t_seq // tq
    n_steps = n_b * n_pairs * nq
    vblk = vbt.shape[2]
    assert vbt.shape == (m // vblk, w, vblk) and tk % vblk == 0 and tq % tk == 0
    qspec = pl.BlockSpec((tq, LANES), lambda b, hp, qi: (b * nq + qi, hp))
    step = lambda b, hp, qi: (b * n_pairs + hp) * nq + qi
    casts = ([_cast_job(a, (a.shape[0], a.shape[1] // n_steps), lambda b, hp, qi: (0, step(b, hp, qi)))
              for a in cast_cols]
             + [_cast_job(a, (a.shape[0] // n_steps, a.shape[1]), lambda b, hp, qi: (step(b, hp, qi), 0))
                for a in cast_rows])
    outs = pl.pallas_call(
        functools.partial(_fox_prompt_kernel, tq=tq, tk=tk, n_cast=len(casts)),
        grid=(n_b, n_pairs, nq),
        in_specs=[qspec,
                  pl.BlockSpec((t_seq, LANES), lambda b, hp, qi: (b, hp)),
                  pl.BlockSpec((t_seq, LANES), lambda b, hp, qi: (b, 0)),
                  pl.BlockSpec((t_seq // vblk, LANES, vblk), lambda b, hp, qi: (b, hp, 0)),
                  qspec] + [spec for _, spec, _ in casts],
        out_specs=[qspec] + [spec for _, spec, _ in casts],
        out_shape=[jax.ShapeDtypeStruct((m, w), BF16)] + [sds for _, _, sds in casts],
        compiler_params=_params(("arbitrary", "arbitrary", "arbitrary")),
        name="fox_prompt_attn",
    )(qb, kb, caug, vbt, gate, *[a for a, _, _ in casts])
    return outs[0], outs[1:]


def _fox_sample_kernel(q_ref, kn_ref, vn_ref, ck_ref, cv_ref, clf_ref, lf_ref, gate_ref, y_ref):
    t = q_ref.shape[0]
    n_heads = lf_ref.shape[1]
    eye_h = jnp.where(_iota((n_heads, n_heads), 0) == _iota((n_heads, n_heads), 1), 1.0, 0.0)
    tril = jnp.where(_iota((t, t), 0) >= _iota((t, t), 1), 1.0, 0.0)
    clf = clf_ref[0]
    c_tot = jnp.sum(clf, axis=0, keepdims=True)
    cn_col = _dot(tril, lf_ref[...], HI)
    cn_row = _dg(eye_h, cn_col, NT, HI)
    cq_col = cn_col + c_tot
    cc_row = _lane_cumsum(_dg(eye_h, clf, NT, HI), 256)
    lane = _iota((t, LANES), 1)
    first = lane < HEAD_DIM
    causal = _iota((t, t), 1) <= _iota((t, t), 0)
    ys = []
    for p in range(q_ref.shape[1] // LANES):
        sl = slice(p * LANES, (p + 1) * LANES)
        q = q_ref[:, sl]
        zero = jnp.zeros_like(q)
        qs = jnp.concatenate([jnp.where(first, q, zero), jnp.where(first, zero, q)], axis=0)
        kc = ck_ref[0, :, sl].astype(BF16)
        vc = cv_ref[0, :, sl].astype(BF16)
        kn = kn_ref[:, sl]
        vn = vn_ref[:, sl]
        s_c = _dg(qs, kc, NT)
        s_n = _dg(qs, kn, NT)
        o = []
        for h in range(2):
            hd = 2 * p + h
            sc = s_c[h * t:(h + 1) * t] + (cq_col[:, hd:hd + 1] - cc_row[hd:hd + 1]) * LOG2E
            sn = s_n[h * t:(h + 1) * t] + (cn_col[:, hd:hd + 1] - cn_row[hd:hd + 1]) * LOG2E
            sn = jnp.where(causal, sn, NEG_BIG)
            mx = jnp.maximum(jnp.max(sc, axis=-1, keepdims=True), jnp.max(sn, axis=-1, keepdims=True))
            pc = jnp.exp2(sc - mx)
            pn = jnp.exp2(sn - mx)
            den = jnp.sum(pc, axis=-1, keepdims=True) + jnp.sum(pn, axis=-1, keepdims=True)
            o.append((_dot(pc.astype(BF16), vc) + _dot(pn.astype(BF16), vn)) / den)
        om = jnp.where(first, o[0], o[1])
        sq = om * om
        ms0 = jnp.sum(jnp.where(first, sq, 0.0), axis=-1, keepdims=True)
        ms1 = jnp.sum(jnp.where(first, 0.0, sq), axis=-1, keepdims=True)
        ms = jnp.where(first, ms0, ms1) * (1.0 / HEAD_DIM)
        ys.append(om * lax.rsqrt(ms + RMS_EPS) * gate_ref[:, sl])
    y_ref[...] = jnp.concatenate(ys, axis=1).astype(y_ref.dtype)


def _fox_sample_attn(qb, kb, vb, cache_k, cache_v, cache_lf, lf, gate, *, t_seq):
    m, w = qb.shape
    n_b = m // t_seq
    row = pl.BlockSpec((t_seq, w), lambda b: (b, 0))
    cache = pl.BlockSpec((1,) + cache_k.shape[1:], lambda b: (b, 0, 0))
    return pl.pallas_call(
        _fox_sample_kernel,
        grid=(n_b,),
        in_specs=[row, row, row, cache, cache,
                  pl.BlockSpec((1,) + cache_lf.shape[1:], lambda b: (b, 0, 0)),
                  pl.BlockSpec((t_seq, lf.shape[1]), lambda b: (b, 0)), row],
        out_specs=row,
        out_shape=jax.ShapeDtypeStruct((m, w), BF16),
        compiler_params=_params(("arbitrary",)),
        name="fox_sample_attn",
    )(qb, kb, vb, cache_k, cache_v, cache_lf, lf, gate)


def _out_ln_kernel(yr_ref, yf_ref, x_ref, wo_ref, g_ref, b_ref, h_ref, wob_ref, *, alpha):
    @pl.when(pl.program_id(0) == 0)
    def _():
        wob_ref[...] = wo_ref[...].astype(BF16)

    half = yr_ref.shape[1]
    mix = _dot(yr_ref[...], wob_ref[0:half, :]) + _dot(yf_ref[...], wob_ref[half:, :])
    h_ref[...] = _layer_norm(alpha * x_ref[...] + mix, g_ref[...], b_ref[...])


def _out_ln(yr, yf, x, wo, g, b, *, tm, alpha):
    m, d = x.shape
    row = lambda i: (i, 0)
    return pl.pallas_call(
        functools.partial(_out_ln_kernel, alpha=alpha),
        grid=(m // tm,),
        in_specs=[pl.BlockSpec((tm, yr.shape[1]), row), pl.BlockSpec((tm, yf.shape[1]), row),
                  pl.BlockSpec((tm, d), row), _const_spec(wo.shape), _const_spec(g.shape), _const_spec(b.shape)],
        out_specs=pl.BlockSpec((tm, d), row),
        out_shape=jax.ShapeDtypeStruct((m, d), F32),
        scratch_shapes=[pltpu.VMEM(wo.shape, BF16)],
        compiler_params=_params(("arbitrary",)),
        name="out_ln",
    )(yr, yf, x, wo, g, b)


def _ffn_ln_kernel(h_ref, wu_ref, wd_ref, g_ref, b_ref, k2_ref, v2_ref, o_ref, k5_ref, v5_ref, hb_ref,
                   *, alpha, n_f):
    j = pl.program_id(1)

    @pl.when(j == 0)
    def _():
        h = h_ref[...]
        hb_ref[...] = h.astype(BF16)
        o_ref[...] = alpha * h

    u = jnp.maximum(_dot(hb_ref[...], wu_ref[...]), 0.0)
    o_ref[...] += _dot((u * u).astype(BF16), wd_ref[...])
    for src, dst in ((k2_ref, k5_ref), (v2_ref, v5_ref)):
        for hd in range(dst.shape[1]):
            dst[:, hd, :] = src[:, hd * HEAD_DIM:(hd + 1) * HEAD_DIM]

    @pl.when(j == n_f - 1)
    def _():
        o_ref[...] = _layer_norm(o_ref[...], g_ref[...], b_ref[...])


def _ffn_ln(h, wu, wd, g, b, k2, v2, *, tm, tf, alpha):
    m, d = h.shape
    n_f = wu.shape[1] // tf
    rows = tm // n_f
    heads = k2.shape[1] // HEAD_DIM
    slab_in = pl.BlockSpec((rows, k2.shape[1]), lambda i, j: (i * n_f + j, 0))
    slab_out = pl.BlockSpec((rows, heads, HEAD_DIM), lambda i, j: (i * n_f + j, 0, 0))
    kv5 = jax.ShapeDtypeStruct((m, heads, HEAD_DIM), F32)
    return pl.pallas_call(
        functools.partial(_ffn_ln_kernel, alpha=alpha, n_f=n_f),
        grid=(m // tm, n_f),
        in_specs=[pl.BlockSpec((tm, d), lambda i, j: (i, 0), pipeline_mode=pl.Buffered(1)),
                  pl.BlockSpec((d, tf), lambda i, j: (0, j)),
                  pl.BlockSpec((tf, d), lambda i, j: (j, 0)),
                  _const_spec(g.shape), _const_spec(b.shape), slab_in, slab_in],
        out_specs=[pl.BlockSpec((tm, d), lambda i, j: (i, 0)), slab_out, slab_out],
        out_shape=[jax.ShapeDtypeStruct((m, d), F32), kv5, kv5],
        scratch_shapes=[pltpu.VMEM((tm, d), BF16)],
        compiler_params=_params(("arbitrary", "arbitrary")),
        name="ffn_ln",
    )(h, wu, wd, g, b, k2, v2)


def _pad_cols(x, n):
    return jnp.pad(x, [(0, 0)] * (x.ndim - 1) + [(0, n - x.shape[-1])])


def _stream(x, shift_prev, s0, wts, *, t_seq, tm, tm_ffn, chunk, cache=None):
    n_b, _, d = x.shape
    m = n_b * t_seq
    x2 = x.reshape(m, d)
    qb, k32, v32, kb, vb, vbt, gate, lf, caug = _fox_proj(x2, wts["w_f"], wts["bf_row"], wts["og_g"],
                                                          tm=tm, t_seq=t_seq)
    fprev = _pad_cols(shift_prev, RP_PAD)
    at, rt, bt, kt, vr, g, wtot, sh = _rwkv_proj(
        x2, wts["w_r"], wts["mu"], fprev, wts["w0"], wts["w2p"], wts["a0"], wts["a2p"], wts["g2p"], wts["k_k"],
        wts["k_a"], tm=tm, t_seq=t_seq, chunk=chunk)
    y_r, s_new = _rwkv_scan(at, rt, bt, kt, vr, wtot, g, s0, wts["lnx_g"], wts["lnx_b"], wts["r_k"],
                            chunk=chunk, rows=min(4 * chunk, t_seq), t_seq=t_seq)
    if cache is None:
        y_f, (wts["w_up_b"], wts["w_down_b"]) = _fox_prompt_attn(
            qb, kb, caug, vbt, gate, n_b=n_b, t_seq=t_seq, tq=1024, tk=512,
            cast_cols=(wts["w_up"],), cast_rows=(wts["w_down"],))
    else:
        ck, cv, clf = cache
        y_f = _fox_sample_attn(qb, kb, vb, ck.reshape(n_b, ck.shape[1], FW), cv.reshape(n_b, cv.shape[1], FW),
                               clf, lf, gate, t_seq=t_seq)
    h = _out_ln(y_r, y_f, x2, wts["w_o"], wts["ln1_g"], wts["ln1_b"], tm=tm, alpha=wts["alpha"])
    y, k5, v5 = _ffn_ln(h, wts["w_up_b"], wts["w_down_b"], wts["ln2_g"], wts["ln2_b"], k32, v32,
                        tm=tm_ffn, tf=1024, alpha=wts["alpha"])
    heads = FW // HEAD_DIM
    return (y.reshape(n_b, t_seq, d), k5.reshape(n_b, t_seq, heads, HEAD_DIM),
            v5.reshape(n_b, t_seq, heads, HEAD_DIM), lf.reshape(n_b, t_seq, heads), s_new,
            sh[..., :shift_prev.shape[-1]])


def kernel(x_prompt, x_sample, cache_fox_k, cache_fox_v, cache_fox_logf, state_rwkv_wkv, state_rwkv_shift,
           w_in, rwkv_mu, rwkv_w0, rwkv_w2, rwkv_a0, rwkv_a2, rwkv_g2, rwkv_k_k, rwkv_k_a, rwkv_r_k,
           rwkv_lnx_g, rwkv_lnx_b, fox_b_f, fox_out_g, w_o, ln1_g, ln1_b, w_up, w_down, ln2_g, ln2_b):
    depth = w_in.shape[0]
    assert depth == 1, "single-layer problem"
    d_model = x_prompt.shape[-1]
    rwkv_proj = rwkv_mu.shape[-1]
    lora = (rwkv_w2.shape[1], rwkv_a2.shape[1], rwkv_g2.shape[1])
    assert rwkv_w0.shape[-1] == RW and fox_out_g.shape[-1] == FW and rwkv_proj == 3 * RW + sum(lora)
    assert lora[0] + lora[1] == LANES and lora[2] <= 2 * LANES and RP_PAD <= w_in.shape[-1]
    alpha = (2 * depth) ** 0.25
    l = 0
    w = w_in[l]
    fo = rwkv_proj
    row = lambda z: z.reshape(1, -1)
    pad_rows = lambda z, n: jnp.pad(z, ((0, n - z.shape[0]), (0, 0)))
    wts = dict(
        alpha=alpha,
        w_r=w[:, :RP_PAD].astype(BF16),
        w_f=jnp.concatenate([w[:, fo:fo + 3 * FW], w[:, fo + 3 * FW + FOX_HEADS:],
                             _pad_cols(jnp.tile(w[:, fo + 3 * FW:fo + 3 * FW + FOX_HEADS], (1, 3)), LANES)],
                            axis=-1).astype(BF16),
        mu=_pad_cols(row(rwkv_mu[l]), RP_PAD),
        w0=row(rwkv_w0[l]), w2p=jnp.pad(rwkv_w2[l], ((0, lora[1]), (0, 0))).astype(BF16),
        a0=row(rwkv_a0[l]), a2p=jnp.pad(rwkv_a2[l], ((lora[0], 0), (0, 0))).astype(BF16),
        g2p=pad_rows(rwkv_g2[l], 256).astype(BF16),
        k_k=row(rwkv_k_k[l]), k_a=row(rwkv_k_a[l]), r_k=row(rwkv_r_k[l]),
        lnx_g=row(rwkv_lnx_g[l]), lnx_b=row(rwkv_lnx_b[l]),
        bf_row=_pad_cols(jnp.tile(row(fox_b_f[l]), (1, 3)), LANES), og_g=row(fox_out_g[l]),
        w_o=w_o[l], ln1_g=row(ln1_g[l]), ln1_b=row(ln1_b[l]),
        w_up=w_up[l], w_down=w_down[l], ln2_g=row(ln2_g[l]), ln2_b=row(ln2_b[l]),
    )
    n_p, t_p, _ = x_prompt.shape
    n_s, t_s, _ = x_sample.shape
    heads = RW // HEAD_DIM
    shift0 = jnp.zeros((n_p, 1, rwkv_proj), F32)
    s_zero = jnp.zeros((n_p, heads, HEAD_DIM, HEAD_DIM), F32)
    yp, kp, vp, fp, sp, shp = _stream(x_prompt, shift0, s_zero, wts, t_seq=t_p, tm=256, tm_ffn=1024, chunk=64)
    ys, ks, vs, fs, ss, shs = _stream(x_sample, state_rwkv_shift[l], state_rwkv_wkv[l], wts, t_seq=t_s,
                                      tm=n_s * t_s, tm_ffn=n_s * t_s, chunk=t_s,
                                      cache=(cache_fox_k[l], cache_fox_v[l], cache_fox_logf[l]))
    return (yp, ys, kp[None], vp[None], fp[None], sp[None], shp[None],
            ks[None], vs[None], fs[None], ss[None], shs[None])
```

```python
import functools
import math

import jax
import jax.numpy as jnp
from jax import lax
from jax.experimental import pallas as pl
from jax.experimental.pallas import tpu as pltpu

F32 = jnp.float32
BF16 = jnp.bfloat16

HEAD_DIM = 64
LANES = 128
LN_EPS = 1e-5
GN_EPS = 64e-5
RMS_EPS = 1e-6
ATTN_SCALE = HEAD_DIM ** -0.5
EXP_NEG_HALF = math.exp(-0.5)
LOG2E = math.log2(math.e)
NEG_BIG = -1e30
VMEM_LIMIT = 60 * 1024 * 1024

TM_PROJ = 256
TM_FFN = 1024
TF_FFN = 1024
TQ_ATTN = 1024
TK_ATTN = 512
CHUNK = 64
SCAN_CHUNKS = 4

NT = (((1,), (1,)), ((), ()))
TN = (((0,), (0,)), ((), ()))


def _sigmoid(x):
    return 1.0 / (1.0 + jnp.exp(-x))


def _log_sigmoid(x):
    return jnp.minimum(x, 0.0) - jnp.log1p(jnp.exp(-jnp.abs(x)))


def _dot(a, b):
    return jnp.dot(a, b, preferred_element_type=F32)


def _dg(a, b, dims):
    return lax.dot_general(a, b, dims, preferred_element_type=F32)


def _split3(x):
    hi = x.astype(BF16)
    rem = x - hi.astype(F32)
    mid = rem.astype(BF16)
    return hi, mid, (rem - mid.astype(F32)).astype(BF16)


def _dot01(m01, x):
    hi, mid, lo = _split3(x)
    return _dot(m01, hi) + _dot(m01, mid) + _dot(m01, lo)


def _iota(shape, axis):
    return lax.broadcasted_iota(jnp.int32, shape, axis)


def _head_sum(x):
    r = _iota((LANES, LANES), 0) // HEAD_DIM
    c = _iota((LANES, LANES), 1) // HEAD_DIM
    ones_blk = jnp.where(r == c, 1.0, 0.0).astype(BF16)
    outs = []
    for g in range(x.shape[1] // LANES):
        xs = x[:, g * LANES:(g + 1) * LANES]
        hi = xs.astype(BF16)
        lo = (xs - hi.astype(F32)).astype(BF16)
        outs.append(_dot(hi, ones_blk) + _dot(lo, ones_blk))
    return outs[0] if len(outs) == 1 else jnp.concatenate(outs, axis=1)


def _layer_norm(z, g, b):
    mu = jnp.mean(z, axis=-1, keepdims=True)
    zc = z - mu
    var = jnp.mean(zc * zc, axis=-1, keepdims=True)
    return zc * lax.rsqrt(var + LN_EPS) * g + b


def _const_spec(shape):
    nd = len(shape)
    return pl.BlockSpec(shape, lambda *_: (0,) * nd, pipeline_mode=pl.Buffered(1))


def _params(sem):
    return pltpu.CompilerParams(dimension_semantics=sem, vmem_limit_bytes=VMEM_LIMIT)


RW = 1024
LORA_WA = LANES
LORA_G = 2 * LANES
RP_PAD = 3 * RW + LORA_WA + LORA_G


def _rwkv_proj_kernel(x_ref, w_ref, mu_ref, fp_ref, w0_ref, w2_ref, a0_ref, a2_ref, g2_ref, kk_ref, ka_ref,
                      at_ref, rt_ref, bt_ref, kt_ref, vb_ref, g_ref, wtot_ref, sh_ref, carry_ref,
                      *, tm, t_seq, chunk):
    i = pl.program_id(0)
    xb = x_ref[...].astype(BF16)
    rows = _iota((tm, 1), 0)

    def proj(c0, n):
        return _dot(xb, w_ref[:, c0:c0 + n])

    def shift(p, c0):
        n = p.shape[1]
        prev = pltpu.roll(p, 1, 0)
        if t_seq >= tm:
            tiles = t_seq // tm
            pos = lax.rem(i, tiles)
            sidx = lax.div(i, tiles)
            row0 = jnp.where(pos == 0, fp_ref[sidx, :, c0:c0 + n], carry_ref[:, c0:c0 + n])
            prev = jnp.where(rows == 0, row0, prev)
            carry_ref[:, c0:c0 + n] = p[tm - 1:tm, :]

            @pl.when(pos == tiles - 1)
            def _():
                sh_ref[sidx, :, c0:c0 + n] = p[tm - 1:tm, :]
        else:
            per_tile = tm // t_seq
            for j in range(per_tile):
                prev = jnp.where(rows == j * t_seq, fp_ref[i * per_tile + j, :, c0:c0 + n], prev)
                sh_ref[i * per_tile + j, :, c0:c0 + n] = p[(j + 1) * t_seq - 1:(j + 1) * t_seq, :]
        return p + (prev - p) * mu_ref[:, c0:c0 + n]

    p_wa = proj(3 * RW, LORA_WA)
    p_g = proj(3 * RW + LORA_WA, LORA_G)
    xwa = shift(p_wa, 3 * RW)
    xg = shift(p_g, 3 * RW + LORA_WA)
    wl = w0_ref[...] + _dot(jnp.tanh(xwa).astype(BF16), w2_ref[...])
    alr = _sigmoid(a0_ref[...] + _dot(xwa.astype(BF16), a2_ref[...]))
    g_ref[...] = _dot(_sigmoid(xg).astype(BF16), g2_ref[...]).astype(BF16)
    p_k = proj(RW, RW)
    lw = -EXP_NEG_HALF * _sigmoid(wl)
    rr = _iota((tm, tm), 0)
    cc = _iota((tm, tm), 1)
    same_chunk_tri = jnp.where(rr // chunk == cc // chunk, jnp.where(rr >= cc, 1.0, 0.0), 0.0).astype(BF16)
    lwc = _dot01(same_chunk_tri, lw)
    k = shift(p_k, RW)
    kk = k * kk_ref[...]
    kk_ss = _head_sum(kk * kk)
    p_r = proj(0, RW)
    p_v = proj(2 * RW, RW)
    kkn = kk / jnp.maximum(jnp.sqrt(kk_ss), 1e-12)
    kh = k * (1.0 + (alr - 1.0) * ka_ref[...])
    e_in = jnp.exp(lwc)
    e_out = jnp.exp(-lwc)
    at_ref[...] = (-kkn * jnp.exp(lwc - lw)).astype(BF16)
    bt_ref[...] = (kkn * alr * e_out).astype(BF16)
    kt_ref[...] = (kh * e_out).astype(BF16)
    for c in range(tm // chunk):
        wtot_ref[c] = e_in[(c + 1) * chunk - 1:(c + 1) * chunk, :]
    r = shift(p_r, 0)
    rt_ref[...] = (r * e_in).astype(BF16)
    vb_ref[...] = shift(p_v, 2 * RW).astype(BF16)


def _rwkv_proj(x, w, mu, fprev, w0, w2p, a0, a2p, g2p, k_k, k_a, *, tm, t_seq, chunk):
    m, d = x.shape
    n_seq = m // t_seq
    row = lambda i: (i, 0)
    big = lambda dt: jax.ShapeDtypeStruct((m, RW), dt)
    return pl.pallas_call(
        functools.partial(_rwkv_proj_kernel, tm=tm, t_seq=t_seq, chunk=chunk),
        grid=(m // tm,),
        in_specs=[pl.BlockSpec((tm, d), row),
                  _const_spec(w.shape), _const_spec(mu.shape), _const_spec(fprev.shape),
                  _const_spec(w0.shape), _const_spec(w2p.shape), _const_spec(a0.shape), _const_spec(a2p.shape),
                  _const_spec(g2p.shape), _const_spec(k_k.shape), _const_spec(k_a.shape)],
        out_specs=[pl.BlockSpec((tm, RW), row)] * 6
                  + [pl.BlockSpec((tm // chunk, 1, RW), lambda i: (i, 0, 0)),
                     pl.BlockSpec((n_seq, 1, RP_PAD), lambda i: (0, 0, 0))],
        out_shape=[big(BF16)] * 6 + [jax.ShapeDtypeStruct((m // chunk, 1, RW), F32),
                                     jax.ShapeDtypeStruct((n_seq, 1, RP_PAD), F32)],
        scratch_shapes=[pltpu.VMEM((1, RP_PAD), F32)],
        compiler_params=_params(("arbitrary",)),
        name="rwkv_proj",
    )(x, w, mu, fprev, w0, w2p, a0, a2p, g2p, k_k, k_a)


FW = 1024
FOX_HEADS = FW // HEAD_DIM


def _fox_proj_kernel(x_ref, w_ref, bf_ref, og_ref,
                     qb_ref, k_ref, v_ref, kb_ref, vb_ref, vbt_ref, gate_ref, lf_ref, caug_ref, carry_ref,
                     *, tm, t_seq):
    i = pl.program_id(0)
    xb = x_ref[...].astype(BF16)
    qb_ref[...] = (_dot(xb, w_ref[:, 0:FW]) * (ATTN_SCALE * LOG2E)).astype(BF16)
    k = _dot(xb, w_ref[:, FW:2 * FW])
    k_ref[...] = k
    kb_ref[...] = k.astype(BF16)
    v = _dot(xb, w_ref[:, 2 * FW:3 * FW])
    v_ref[...] = v
    vb_ref[...] = v.astype(BF16)
    vbt_ref[0] = v.T.astype(BF16)
    og = _dot(xb, w_ref[:, 3 * FW:4 * FW])
    gate_ref[...] = _sigmoid(og) * og_ref[...]
    logf = _log_sigmoid(_dot(xb, w_ref[:, 4 * FW:4 * FW + LANES]) + bf_ref[...])
    lf_ref[...] = logf[:, :FOX_HEADS]
    r = _iota((tm, tm), 0)
    c = _iota((tm, tm), 1)
    if t_seq >= tm:
        cs = _dot01(jnp.where(r >= c, 1.0, 0.0).astype(BF16), logf)
        cs = cs + jnp.where(lax.rem(i, t_seq // tm) == 0, 0.0, carry_ref[...])
        carry_ref[...] = cs[tm - 1:tm, :]
    else:
        same_seq = (r // t_seq) == (c // t_seq)
        cs = _dot01(jnp.where(same_seq, jnp.where(r >= c, 1.0, 0.0), 0.0).astype(BF16), logf)
    xs = -LOG2E * cs
    hi, mid, lo = _split3(xs)
    lane = _iota((tm, LANES), 1)
    zero = jnp.zeros_like(hi)
    caug_ref[...] = jnp.where(lane < FOX_HEADS, hi,
                              jnp.where(lane < 2 * FOX_HEADS, mid, jnp.where(lane < 3 * FOX_HEADS, lo, zero)))


def _cast_job(a, block, index_map, out_shape=None):
    spec = pl.BlockSpec(block, index_map)
    return a, spec, jax.ShapeDtypeStruct(out_shape or a.shape, BF16)


def _fox_proj(x, w, bf_row, og_g, *, tm, t_seq):
    m, d = x.shape
    row = lambda i: (i, 0)
    big = lambda dt: jax.ShapeDtypeStruct((m, FW), dt)
    return pl.pallas_call(
        functools.partial(_fox_proj_kernel, tm=tm, t_seq=t_seq),
        grid=(m // tm,),
        in_specs=[pl.BlockSpec((tm, d), row), _const_spec(w.shape), _const_spec(bf_row.shape),
                  _const_spec(og_g.shape)],
        out_specs=[pl.BlockSpec((tm, FW), row)] * 5
                  + [pl.BlockSpec((1, FW, tm), lambda i: (i, 0, 0)), pl.BlockSpec((tm, FW), row),
                     pl.BlockSpec((tm, FOX_HEADS), row), pl.BlockSpec((tm, LANES), row)],
        out_shape=[big(BF16), big(F32), big(F32), big(BF16), big(BF16),
                   jax.ShapeDtypeStruct((m // tm, FW, tm), BF16), big(F32),
                   jax.ShapeDtypeStruct((m, FOX_HEADS), F32), jax.ShapeDtypeStruct((m, LANES), BF16)],
        scratch_shapes=[pltpu.VMEM((1, LANES), F32)],
        compiler_params=_params(("arbitrary",)),
        name="fox_proj",
    )(x, w, bf_row, og_g)


def _stack_heads(x, mask0):
    return jnp.concatenate([jnp.where(mask0, x, 0.0), jnp.where(mask0, 0.0, x)], axis=0)


def _rwkv_scan_kernel(at_ref, rt_ref, bt_ref, kt_ref, vb_ref, wtot_ref, g_ref, s0_ref, lng_ref, lnb_ref, rk_ref,
                      y_ref, sout_ref, state_ref, *, chunk, n_steps):
    step = pl.program_id(1)
    n_pairs = at_ref.shape[1] // LANES
    C = chunk
    n_ch = at_ref.shape[0] // C
    pairs = range(n_pairs)

    @pl.when(step == 0)
    def _():
        zero = jnp.zeros((HEAD_DIM, HEAD_DIM), F32)
        for p in pairs:
            top = jnp.concatenate([s0_ref[0, 2 * p], zero], axis=1)
            bot = jnp.concatenate([zero, s0_ref[0, 2 * p + 1]], axis=1)
            state_ref[p] = jnp.concatenate([top, bot], axis=0)

    fmask0 = _iota((1, LANES), 1) < HEAD_DIM
    tcol = _iota((C, 2 * C), 1)
    trow = _iota((C, 2 * C), 0)
    tmask0 = tcol < C
    tj = jnp.where(tmask0, tcol, tcol - C)
    strict = tj < trow
    incl = tj <= trow
    eye_pair = jnp.where(tj == trow, 1.0, 0.0)
    blk = (_iota((LANES, LANES), 0) // HEAD_DIM) == (_iota((LANES, LANES), 1) // HEAD_DIM)

    def stack_t(x):
        return _stack_heads(x, tmask0)

    def tile(ref, c, p):
        return ref[c * C:(c + 1) * C, p * LANES:(p + 1) * LANES]

    probs = [(c, p) for c in range(n_ch) for p in pairs]
    at = {cp: tile(at_ref, *cp) for cp in probs}
    rt = {cp: tile(rt_ref, *cp) for cp in probs}
    bt = {cp: tile(bt_ref, *cp) for cp in probs}
    kt = {cp: tile(kt_ref, *cp) for cp in probs}
    vb = {cp: tile(vb_ref, *cp) for cp in probs}
    gm = {cp: _dg(jnp.concatenate([at[cp], rt[cp]], axis=0),
                  jnp.concatenate([_stack_heads(bt[cp], fmask0), _stack_heads(kt[cp], fmask0)], axis=0), NT)
          for cp in probs}
    lab = {cp: jnp.where(strict, gm[cp][:C, :2 * C], 0.0) for cp in probs}
    lak_b = {cp: jnp.where(strict, gm[cp][:C, 2 * C:], 0.0).astype(BF16) for cp in probs}
    mr_b = {cp: jnp.concatenate([jnp.where(incl, gm[cp][C:, :2 * C], 0.0),
                                 jnp.where(incl, gm[cp][C:, 2 * C:], 0.0)], axis=1).astype(BF16) for cp in probs}
    kmax = int(math.log2(C)) - 1
    tinv = {cp: eye_pair + lab[cp] for cp in probs}
    pw = {cp: lab[cp].astype(BF16) for cp in probs}
    pw = {cp: _dot(pw[cp], stack_t(pw[cp])).astype(BF16) for cp in probs}
    for _ in range(1, kmax):
        res = {cp: _dot(jnp.concatenate([pw[cp], tinv[cp].astype(BF16)], axis=0), stack_t(pw[cp])) for cp in probs}
        pw = {cp: res[cp][:C].astype(BF16) for cp in probs}
        tinv = {cp: tinv[cp] + res[cp][C:] for cp in probs}
    tinv_b = {cp: (tinv[cp] + _dot(tinv[cp].astype(BF16), stack_t(pw[cp]))).astype(BF16) for cp in probs}

    sd = [state_ref[p] for p in pairs]
    ys = []
    for c in range(n_ch):
        sd_b = [z.astype(BF16) for z in sd]
        vd_b = [_stack_heads(vb[c, p], fmask0) for p in pairs]
        x = [_dg(at[c, p], sd_b[p], NT) + _dot(lak_b[c, p], vd_b[p]) for p in pairs]
        u_b = [_dot(tinv_b[c, p], _stack_heads(x[p].astype(BF16), fmask0)).astype(BF16) for p in pairs]
        ys.append(jnp.concatenate(
            [_dg(rt[c, p], sd_b[p], NT)
             + _dot(mr_b[c, p], jnp.concatenate([_stack_heads(u_b[p], fmask0), vd_b[p]], axis=0)) for p in pairs],
            axis=1))
        w_tot = wtot_ref[c]
        for p in pairs:
            ds = _dg(jnp.concatenate([u_b[p], vb[c, p]], axis=0), jnp.concatenate([bt[c, p], kt[c, p]], axis=0), TN)
            sd[p] = (sd[p] + jnp.where(blk, ds, 0.0)) * w_tot[:, p * LANES:(p + 1) * LANES]
    for p in pairs:
        state_ref[p] = sd[p]

    y = ys[0] if n_ch == 1 else jnp.concatenate(ys, axis=0)
    inv_n = 1.0 / HEAD_DIM
    mu = _head_sum(y) * inv_n
    yc = y - mu
    var = _head_sum(yc * yc) * inv_n
    yn = yc * lax.rsqrt(var + GN_EPS) * lng_ref[...] + lnb_ref[...]
    rk = rt_ref[...].astype(F32) * kt_ref[...].astype(F32) * rk_ref[...]
    bonus = _head_sum(rk) * vb_ref[...].astype(F32)
    y_ref[...] = ((yn + bonus) * g_ref[...].astype(F32)).astype(y_ref.dtype)

    @pl.when(step == n_steps - 1)
    def _():
        for p in pairs:
            sout_ref[0, 2 * p] = sd[p][:HEAD_DIM, :HEAD_DIM]
            sout_ref[0, 2 * p + 1] = sd[p][HEAD_DIM:, HEAD_DIM:]


def _rwkv_scan(at, rt, bt, kt, vb, wtot, g, s0, lng, lnb, rk, *, chunk, rows, t_seq):
    m, w = at.shape
    n_b = m // t_seq
    n_steps = t_seq // rows
    n_heads = w // HEAD_DIM
    blk = pl.BlockSpec((rows, w), lambda bi, si: (bi * n_steps + si, 0))
    wt = pl.BlockSpec((rows // chunk, 1, w), lambda bi, si: (bi * n_steps + si, 0, 0))
    st = pl.BlockSpec((1, n_heads, HEAD_DIM, HEAD_DIM), lambda bi, si: (bi, 0, 0, 0))
    return pl.pallas_call(
        functools.partial(_rwkv_scan_kernel, chunk=chunk, n_steps=n_steps),
        grid=(n_b, n_steps),
        in_specs=[blk] * 5 + [wt, blk, st, _const_spec(lng.shape), _const_spec(lnb.shape), _const_spec(rk.shape)],
        out_specs=[blk, st],
        out_shape=[jax.ShapeDtypeStruct((m, w), BF16),
                   jax.ShapeDtypeStruct((n_b, n_heads, HEAD_DIM, HEAD_DIM), F32)],
        scratch_shapes=[pltpu.VMEM((w // LANES, LANES, LANES), F32)],
        compiler_params=_params(("arbitrary", "arbitrary")),
        name="rwkv_scan",
    )(at, rt, bt, kt, vb, wtot, g, s0, lng, lnb, rk)


def _transpose01(x):
    h = x.shape[1]
    eye = jnp.where(_iota((h, h), 0) == _iota((h, h), 1), 1.0, 0.0).astype(BF16)
    hi, mid, lo = _split3(x)
    return _dg(eye, hi, NT) + _dg(eye, mid, NT) + _dg(eye, lo, NT)


def _lane_cumsum(x, block):
    tri = jnp.where(_iota((block, block), 0) <= _iota((block, block), 1), 1.0, 0.0).astype(BF16)
    hi, mid, lo = _split3(x)
    outs = []
    offset = jnp.zeros((x.shape[0], 1), F32)
    for j in range(x.shape[1] // block):
        sl = slice(j * block, (j + 1) * block)
        c = _dot(hi[:, sl], tri) + _dot(mid[:, sl], tri) + _dot(lo[:, sl], tri)
        outs.append(c + offset)
        offset = offset + c[:, block - 1:block]
    return outs[0] if len(outs) == 1 else jnp.concatenate(outs, axis=1)


def _fox_prompt_kernel(q_ref, k_ref, caug_ref, vt_ref, gate_ref, *rest, tq, tk, n_cast):
    cast_in, y_ref, cast_out = rest[:n_cast], rest[n_cast], rest[n_cast + 1:]
    hp = pl.program_id(1)
    qi = pl.program_id(2)
    nq = k_ref.shape[0] // tq
    vblk = vt_ref.shape[2]
    lane = _iota((tq, LANES), 1)
    first = lane < HEAD_DIM
    q = q_ref[...]
    zero = jnp.zeros_like(q)
    hslot = jnp.where(lane < 3 * FOX_HEADS, lax.rem(lane, FOX_HEADS), -1)
    qsa = []
    for h in range(2):
        ones = jnp.where(hslot == 2 * hp + h, 1.0, 0.0).astype(BF16)
        qh = jnp.where(first, q, zero) if h == 0 else jnp.where(first, zero, q)
        qsa.append(jnp.concatenate([qh, ones], axis=1))

    def scores(item):
        k0, nk, q_lo, _ = item
        kk = jnp.concatenate([k_ref[k0:k0 + nk, :], caug_ref[k0:k0 + nk, :]], axis=1)
        rhs = jnp.concatenate([qsa[0][q_lo:], qsa[1][q_lo:]], axis=0)
        return _dg(kk, rhs, NT)

    def update(st, item, state):
        k0, nk, q_lo, masked = item
        nqv = tq - q_lo
        vt = jnp.concatenate([vt_ref[k0 // vblk + i] for i in range(nk // vblk)], axis=1)
        out = []
        for h in range(2):
            m_all, l_all, acc_all = state[h]
            sh = st[:, h * nqv:(h + 1) * nqv]
            if masked:
                sh = jnp.where(_iota((nk, nqv), 0) <= _iota((nk, nqv), 1), sh, NEG_BIG)
            m_old = m_all[:, q_lo:]
            m_new = jnp.maximum(m_old, jnp.max(sh, axis=0, keepdims=True))
            alpha = jnp.exp2(m_old - m_new)
            p = jnp.exp2(sh - m_new)
            l_new = alpha * l_all[:, q_lo:] + jnp.sum(p, axis=0, keepdims=True)
            acc_new = alpha * acc_all[:, q_lo:] + _dot(vt[h * HEAD_DIM:(h + 1) * HEAD_DIM], p.astype(BF16))
            if q_lo:
                m_new = jnp.concatenate([m_all[:, :q_lo], m_new], axis=1)
                l_new = jnp.concatenate([l_all[:, :q_lo], l_new], axis=1)
                acc_new = jnp.concatenate([acc_all[:, :q_lo], acc_new], axis=1)
            out.append((m_new, l_new, acc_new))
        return out

    def program(c):
        items = [(kb * tk, tk, 0, False) for kb in range(c * tq // tk)]
        items += [(c * tq + d * tk, tk, d * tk, True) for d in range(tq // tk)]
        state = [(jnp.full((1, tq), NEG_BIG, F32), jnp.zeros((1, tq), F32), jnp.zeros((HEAD_DIM, tq), F32))
                 for _ in range(2)]
        st_next = scores(items[0])
        for n, item in enumerate(items):
            st = st_next
            if n + 1 < len(items):
                st_next = scores(items[n + 1])
            state = update(st, item, state)
        yts = []
        for _, l_fin, acc in state:
            o = acc * (1.0 / l_fin)
            ms = jnp.mean(o * o, axis=0, keepdims=True)
            yts.append(o * lax.rsqrt(ms + RMS_EPS))
        yt = jnp.concatenate(yts, axis=0)
        y_ref[...] = (yt.T * gate_ref[...]).astype(y_ref.dtype)
        for src, dst in zip(cast_in, cast_out):
            dst[...] = src[...].astype(BF16)

    for c in range(nq):
        pl.when(qi == c)(functools.partial(program, c))


def _fox_prompt_attn(qb, kb, caug, vbt, gate, *, n_b, t_seq, tq, tk, cast_cols=(), cast_rows=()):
    m, w = qb.shape
    n_pairs = w // LANES
    nq = t_seq // tq
    n_steps = n_b * n_pairs * nq
    vblk = vbt.shape[2]
    assert vbt.shape == (m // vblk, w, vblk) and tk % vblk == 0 and tq % tk == 0
    qspec = pl.BlockSpec((tq, LANES), lambda b, hp, qi: (b * nq + qi, hp))
    step = lambda b, hp, qi: (b * n_pairs + hp) * nq + qi
    casts = ([_cast_job(a, (a.shape[0], a.shape[1] // n_steps), lambda b, hp, qi: (0, step(b, hp, qi)))
              for a in cast_cols]
             + [_cast_job(a, (a.shape[0] // n_steps, a.shape[1]), lambda b, hp, qi: (step(b, hp, qi), 0))
                for a in cast_rows])
    outs = pl.pallas_call(
        functools.partial(_fox_prompt_kernel, tq=tq, tk=tk, n_cast=len(casts)),
        grid=(n_b, n_pairs, nq),
        in_specs=[qspec,
                  pl.BlockSpec((t_seq, LANES), lambda b, hp, qi: (b, hp)),
                  pl.BlockSpec((t_seq, LANES), lambda b, hp, qi: (b, 0)),
                  pl.BlockSpec((t_seq // vblk, LANES, vblk), lambda b, hp, qi: (b, hp, 0)),
                  qspec] + [spec for _, spec, _ in casts],
        out_specs=[qspec] + [spec for _, spec, _ in casts],
        out_shape=[jax.ShapeDtypeStruct((m, w), BF16)] + [sds for _, _, sds in casts],
        compiler_params=_params(("arbitrary", "arbitrary", "arbitrary")),
        name="fox_prompt_attn",
    )(qb, kb, caug, vbt, gate, *[a for a, _, _ in casts])
    return outs[0], outs[1:]


def _fox_sample_kernel(q_ref, kn_ref, vn_ref, ck_ref, cv_ref, clf_ref, lf_ref, gate_ref, y_ref):
    t = q_ref.shape[0]
    n_heads = lf_ref.shape[1]
    tril = jnp.where(_iota((t, t), 0) >= _iota((t, t), 1), 1.0, 0.0).astype(BF16)
    clf = clf_ref[0]
    c_tot = jnp.sum(clf, axis=0, keepdims=True)
    cn_col = _dot01(tril, lf_ref[...])
    cn_row = _transpose01(cn_col)
    cq_col = cn_col + c_tot
    cc_row = _lane_cumsum(_transpose01(clf), 2 * LANES)
    lane = _iota((t, LANES), 1)
    first = lane < HEAD_DIM
    causal = _iota((t, t), 1) <= _iota((t, t), 0)
    ys = []
    for p in range(q_ref.shape[1] // LANES):
        sl = slice(p * LANES, (p + 1) * LANES)
        q = q_ref[:, sl]
        zero = jnp.zeros_like(q)
        qs = jnp.concatenate([jnp.where(first, q, zero), jnp.where(first, zero, q)], axis=0)
        kc = ck_ref[0, :, sl].astype(BF16)
        vc = cv_ref[0, :, sl].astype(BF16)
        kn = kn_ref[:, sl]
        vn = vn_ref[:, sl]
        s_c = _dg(qs, kc, NT)
        s_n = _dg(qs, kn, NT)
        o = []
        for h in range(2):
            hd = 2 * p + h
            sc = s_c[h * t:(h + 1) * t] + (cq_col[:, hd:hd + 1] - cc_row[hd:hd + 1]) * LOG2E
            sn = s_n[h * t:(h + 1) * t] + (cn_col[:, hd:hd + 1] - cn_row[hd:hd + 1]) * LOG2E
            sn = jnp.where(causal, sn, NEG_BIG)
            mx = jnp.maximum(jnp.max(sc, axis=-1, keepdims=True), jnp.max(sn, axis=-1, keepdims=True))
            pc = jnp.exp2(sc - mx)
            pn = jnp.exp2(sn - mx)
            den = jnp.sum(pc, axis=-1, keepdims=True) + jnp.sum(pn, axis=-1, keepdims=True)
            o.append((_dot(pc.astype(BF16), vc) + _dot(pn.astype(BF16), vn)) / den)
        om = jnp.where(first, o[0], o[1])
        sq = om * om
        ms0 = jnp.sum(jnp.where(first, sq, 0.0), axis=-1, keepdims=True)
        ms1 = jnp.sum(jnp.where(first, 0.0, sq), axis=-1, keepdims=True)
        ms = jnp.where(first, ms0, ms1) * (1.0 / HEAD_DIM)
        ys.append(om * lax.rsqrt(ms + RMS_EPS) * gate_ref[:, sl])
    y_ref[...] = jnp.concatenate(ys, axis=1).astype(y_ref.dtype)


def _fox_sample_attn(qb, kb, vb, cache_k, cache_v, cache_lf, lf, gate, *, t_seq):
    m, w = qb.shape
    n_b = m // t_seq
    row = pl.BlockSpec((t_seq, w), lambda b: (b, 0))
    cache = pl.BlockSpec((1,) + cache_k.shape[1:], lambda b: (b, 0, 0))
    return pl.pallas_call(
        _fox_sample_kernel,
        grid=(n_b,),
        in_specs=[row, row, row, cache, cache,
                  pl.BlockSpec((1,) + cache_lf.shape[1:], lambda b: (b, 0, 0)),
                  pl.BlockSpec((t_seq, lf.shape[1]), lambda b: (b, 0)), row],
        out_specs=row,
        out_shape=jax.ShapeDtypeStruct((m, w), BF16),
        compiler_params=_params(("arbitrary",)),
        name="fox_sample_attn",
    )(qb, kb, vb, cache_k, cache_v, cache_lf, lf, gate)


def _out_ln_kernel(yr_ref, yf_ref, x_ref, wo_ref, g_ref, b_ref, h_ref, wob_ref, *, alpha):
    @pl.when(pl.program_id(0) == 0)
    def _():
        wob_ref[...] = wo_ref[...].astype(BF16)

    half = yr_ref.shape[1]
    mix = _dot(yr_ref[...], wob_ref[0:half, :]) + _dot(yf_ref[...], wob_ref[half:, :])
    h_ref[...] = _layer_norm(alpha * x_ref[...] + mix, g_ref[...], b_ref[...])


def _out_ln(yr, yf, x, wo, g, b, *, tm, alpha):
    m, d = x.shape
    row = lambda i: (i, 0)
    return pl.pallas_call(
        functools.partial(_out_ln_kernel, alpha=alpha),
        grid=(m // tm,),
        in_specs=[pl.BlockSpec((tm, yr.shape[1]), row), pl.BlockSpec((tm, yf.shape[1]), row),
                  pl.BlockSpec((tm, d), row), _const_spec(wo.shape), _const_spec(g.shape), _const_spec(b.shape)],
        out_specs=pl.BlockSpec((tm, d), row),
        out_shape=jax.ShapeDtypeStruct((m, d), F32),
        scratch_shapes=[pltpu.VMEM(wo.shape, BF16)],
        compiler_params=_params(("arbitrary",)),
        name="out_ln",
    )(yr, yf, x, wo, g, b)


def _ffn_ln_kernel(h_ref, wu_ref, wd_ref, g_ref, b_ref, k2_ref, v2_ref, o_ref, k5_ref, v5_ref, hb_ref,
                   *, alpha, n_f):
    j = pl.program_id(1)

    @pl.when(j == 0)
    def _():
        h = h_ref[...]
        hb_ref[...] = h.astype(BF16)
        o_ref[...] = alpha * h

    u = jnp.maximum(_dot(hb_ref[...], wu_ref[...]), 0.0)
    o_ref[...] += _dot((u * u).astype(BF16), wd_ref[...])
    for src, dst in ((k2_ref, k5_ref), (v2_ref, v5_ref)):
        for hd in range(dst.shape[1]):
            dst[:, hd, :] = src[:, hd * HEAD_DIM:(hd + 1) * HEAD_DIM]

    @pl.when(j == n_f - 1)
    def _():
        o_ref[...] = _layer_norm(o_ref[...], g_ref[...], b_ref[...])


def _ffn_ln(h, wu, wd, g, b, k2, v2, *, tm, tf, alpha):
    m, d = h.shape
    n_f = wu.shape[1] // tf
    rows = tm // n_f
    heads = k2.shape[1] // HEAD_DIM
    slab_in = pl.BlockSpec((rows, k2.shape[1]), lambda i, j: (i * n_f + j, 0))
    slab_out = pl.BlockSpec((rows, heads, HEAD_DIM), lambda i, j: (i * n_f + j, 0, 0))
    kv5 = jax.ShapeDtypeStruct((m, heads, HEAD_DIM), F32)
    return pl.pallas_call(
        functools.partial(_ffn_ln_kernel, alpha=alpha, n_f=n_f),
        grid=(m // tm, n_f),
        in_specs=[pl.BlockSpec((tm, d), lambda i, j: (i, 0), pipeline_mode=pl.Buffered(1)),
                  pl.BlockSpec((d, tf), lambda i, j: (0, j)),
                  pl.BlockSpec((tf, d), lambda i, j: (j, 0)),
                  _const_spec(g.shape), _const_spec(b.shape), slab_in, slab_in],
        out_specs=[pl.BlockSpec((tm, d), lambda i, j: (i, 0)), slab_out, slab_out],
        out_shape=[jax.ShapeDtypeStruct((m, d), F32), kv5, kv5],
        scratch_shapes=[pltpu.VMEM((tm, d), BF16)],
        compiler_params=_params(("arbitrary", "arbitrary")),
        name="ffn_ln",
    )(h, wu, wd, g, b, k2, v2)


def _pad_cols(x, n):
    return jnp.pad(x, [(0, 0)] * (x.ndim - 1) + [(0, n - x.shape[-1])])


def _stream(x, shift_prev, s0, wts, *, t_seq, tm, tm_ffn, chunk, cache=None):
    n_b, _, d = x.shape
    m = n_b * t_seq
    x2 = x.reshape(m, d)
    qb, k32, v32, kb, vb, vbt, gate, lf, caug = _fox_proj(x2, wts["w_f"], wts["bf_row"], wts["og_g"],
                                                          tm=tm, t_seq=t_seq)
    fprev = _pad_cols(shift_prev, RP_PAD)
    at, rt, bt, kt, vr, g, wtot, sh = _rwkv_proj(
        x2, wts["w_r"], wts["mu"], fprev, wts["w0"], wts["w2p"], wts["a0"], wts["a2p"], wts["g2p"], wts["k_k"],
        wts["k_a"], tm=tm, t_seq=t_seq, chunk=chunk)
    y_r, s_new = _rwkv_scan(at, rt, bt, kt, vr, wtot, g, s0, wts["lnx_g"], wts["lnx_b"], wts["r_k"],
                            chunk=chunk, rows=min(SCAN_CHUNKS * chunk, t_seq), t_seq=t_seq)
    if cache is None:
        y_f, (wts["w_up_b"], wts["w_down_b"]) = _fox_prompt_attn(
            qb, kb, caug, vbt, gate, n_b=n_b, t_seq=t_seq, tq=TQ_ATTN, tk=TK_ATTN,
            cast_cols=(wts["w_up"],), cast_rows=(wts["w_down"],))
    else:
        ck, cv, clf = cache
        y_f = _fox_sample_attn(qb, kb, vb, ck.reshape(n_b, ck.shape[1], FW), cv.reshape(n_b, cv.shape[1], FW),
                               clf, lf, gate, t_seq=t_seq)
    h = _out_ln(y_r, y_f, x2, wts["w_o"], wts["ln1_g"], wts["ln1_b"], tm=tm, alpha=wts["alpha"])
    y, k5, v5 = _ffn_ln(h, wts["w_up_b"], wts["w_down_b"], wts["ln2_g"], wts["ln2_b"], k32, v32,
                        tm=tm_ffn, tf=TF_FFN, alpha=wts["alpha"])
    heads = FW // HEAD_DIM
    return (y.reshape(n_b, t_seq, d), k5.reshape(n_b, t_seq, heads, HEAD_DIM),
            v5.reshape(n_b, t_seq, heads, HEAD_DIM), lf.reshape(n_b, t_seq, heads), s_new,
            sh[..., :shift_prev.shape[-1]])


def kernel(x_prompt, x_sample, cache_fox_k, cache_fox_v, cache_fox_logf, state_rwkv_wkv, state_rwkv_shift,
           w_in, rwkv_mu, rwkv_w0, rwkv_w2, rwkv_a0, rwkv_a2, rwkv_g2, rwkv_k_k, rwkv_k_a, rwkv_r_k,
           rwkv_lnx_g, rwkv_lnx_b, fox_b_f, fox_out_g, w_o, ln1_g, ln1_b, w_up, w_down, ln2_g, ln2_b):
    depth = w_in.shape[0]
    assert depth == 1, "single-layer problem"
    rwkv_proj = rwkv_mu.shape[-1]
    lora = (rwkv_w2.shape[1], rwkv_a2.shape[1], rwkv_g2.shape[1])
    assert rwkv_w0.shape[-1] == RW and fox_out_g.shape[-1] == FW and rwkv_proj == 3 * RW + sum(lora)
    assert lora[0] + lora[1] == LORA_WA and lora[2] <= LORA_G and RP_PAD <= w_in.shape[-1]
    alpha = (2 * depth) ** 0.25
    l = 0
    w = w_in[l]
    fo = rwkv_proj
    row = lambda z: z.reshape(1, -1)
    pad_rows = lambda z, n: jnp.pad(z, ((0, n - z.shape[0]), (0, 0)))
    wts = dict(
        alpha=alpha,
        w_r=w[:, :RP_PAD].astype(BF16),
        w_f=jnp.concatenate([w[:, fo:fo + 3 * FW], w[:, fo + 3 * FW + FOX_HEADS:],
                             _pad_cols(jnp.tile(w[:, fo + 3 * FW:fo + 3 * FW + FOX_HEADS], (1, 3)), LANES)],
                            axis=-1).astype(BF16),
        mu=_pad_cols(row(rwkv_mu[l]), RP_PAD),
        w0=row(rwkv_w0[l]), w2p=jnp.pad(rwkv_w2[l], ((0, lora[1]), (0, 0))).astype(BF16),
        a0=row(rwkv_a0[l]), a2p=jnp.pad(rwkv_a2[l], ((lora[0], 0), (0, 0))).astype(BF16),
        g2p=pad_rows(rwkv_g2[l], LORA_G).astype(BF16),
        k_k=row(rwkv_k_k[l]), k_a=row(rwkv_k_a[l]), r_k=row(rwkv_r_k[l]),
        lnx_g=row(rwkv_lnx_g[l]), lnx_b=row(rwkv_lnx_b[l]),
        bf_row=_pad_cols(jnp.tile(row(fox_b_f[l]), (1, 3)), LANES), og_g=row(fox_out_g[l]),
        w_o=w_o[l], ln1_g=row(ln1_g[l]), ln1_b=row(ln1_b[l]),
        w_up=w_up[l], w_down=w_down[l], ln2_g=row(ln2_g[l]), ln2_b=row(ln2_b[l]),
    )
    n_p, t_p, _ = x_prompt.shape
    n_s, t_s, _ = x_sample.shape
    heads = RW // HEAD_DIM
    shift0 = jnp.zeros((n_p, 1, rwkv_proj), F32)
    s_zero = jnp.zeros((n_p, heads, HEAD_DIM, HEAD_DIM), F32)
    yp, kp, vp, fp, sp, shp = _stream(x_prompt, shift0, s_zero, wts, t_seq=t_p, tm=TM_PROJ, tm_ffn=TM_FFN, chunk=CHUNK)
    ys, ks, vs, fs, ss, shs = _stream(x_sample, state_rwkv_shift[l], state_rwkv_wkv[l], wts, t_seq=t_s,
                                      tm=n_s * t_s, tm_ffn=n_s * t_s, chunk=t_s,
                                      cache=(cache_fox_k[l], cache_fox_v[l], cache_fox_logf[l]))
    return (yp, ys, kp[None], vp[None], fp[None], sp[None], shp[None],
            ks[None], vs[None], fs[None], ss[None], shs[None])
```

```python
import functools
import math

import jax
import jax.numpy as jnp
from jax import lax
from jax.experimental import pallas as pl
from jax.experimental.pallas import tpu as pltpu

F32 = jnp.float32
BF16 = jnp.bfloat16

HEAD_DIM = 64
LANES = 128
LN_EPS = 1e-5
GN_EPS = 64e-5
RMS_EPS = 1e-6
ATTN_SCALE = HEAD_DIM ** -0.5
EXP_NEG_HALF = math.exp(-0.5)
LOG2E = math.log2(math.e)
NEG_BIG = -1e30
VMEM_LIMIT = 60 * 1024 * 1024

TM_PROJ = 256
TM_FFN = 1024
TF_FFN = 1024
TQ_ATTN = 1024
TK_ATTN = 512
CHUNK = 64
SCAN_CHUNKS = 4

NT = (((1,), (1,)), ((), ()))
TN = (((0,), (0,)), ((), ()))


def _sigmoid(x):
    return 1.0 / (1.0 + jnp.exp(-x))


def _log_sigmoid(x):
    return jnp.minimum(x, 0.0) - jnp.log1p(jnp.exp(-jnp.abs(x)))


def _dot(a, b):
    return jnp.dot(a, b, preferred_element_type=F32)


def _dg(a, b, dims):
    return lax.dot_general(a, b, dims, preferred_element_type=F32)


def _split3(x):
    hi = x.astype(BF16)
    rem = x - hi.astype(F32)
    mid = rem.astype(BF16)
    return hi, mid, (rem - mid.astype(F32)).astype(BF16)


def _dot01(m01, x):
    hi, mid, lo = _split3(x)
    return _dot(m01, hi) + _dot(m01, mid) + _dot(m01, lo)


def _iota(shape, axis):
    return lax.broadcasted_iota(jnp.int32, shape, axis)


def _head_sum(x):
    r = _iota((LANES, LANES), 0) // HEAD_DIM
    c = _iota((LANES, LANES), 1) // HEAD_DIM
    ones_blk = jnp.where(r == c, 1.0, 0.0).astype(BF16)
    outs = []
    for g in range(x.shape[1] // LANES):
        xs = x[:, g * LANES:(g + 1) * LANES]
        hi = xs.astype(BF16)
        lo = (xs - hi.astype(F32)).astype(BF16)
        outs.append(_dot(hi, ones_blk) + _dot(lo, ones_blk))
    return outs[0] if len(outs) == 1 else jnp.concatenate(outs, axis=1)


def _layer_norm(z, g, b):
    mu = jnp.mean(z, axis=-1, keepdims=True)
    zc = z - mu
    var = jnp.mean(zc * zc, axis=-1, keepdims=True)
    return zc * lax.rsqrt(var + LN_EPS) * g + b


def _const_spec(shape):
    nd = len(shape)
    return pl.BlockSpec(shape, lambda *_: (0,) * nd, pipeline_mode=pl.Buffered(1))


def _params(sem):
    return pltpu.CompilerParams(dimension_semantics=sem, vmem_limit_bytes=VMEM_LIMIT)


RW = 1024
LORA_WA = LANES
LORA_G = 2 * LANES
RP_PAD = 3 * RW + LORA_WA + LORA_G


def _rwkv_proj_kernel(x_ref, w_ref, mu_ref, fp_ref, w0_ref, w2_ref, a0_ref, a2_ref, g2_ref, kk_ref, ka_ref,
                      at_ref, rt_ref, bt_ref, kt_ref, vb_ref, g_ref, wtot_ref, sh_ref, carry_ref,
                      *, tm, t_seq, chunk):
    i = pl.program_id(0)
    xb = x_ref[...].astype(BF16)
    rows = _iota((tm, 1), 0)

    def proj(c0, n):
        return _dot(xb, w_ref[:, c0:c0 + n])

    def shift(p, c0):
        n = p.shape[1]
        prev = pltpu.roll(p, 1, 0)
        if t_seq >= tm:
            tiles = t_seq // tm
            pos = lax.rem(i, tiles)
            sidx = lax.div(i, tiles)
            row0 = jnp.where(pos == 0, fp_ref[sidx, :, c0:c0 + n], carry_ref[:, c0:c0 + n])
            prev = jnp.where(rows == 0, row0, prev)
            carry_ref[:, c0:c0 + n] = p[tm - 1:tm, :]

            @pl.when(pos == tiles - 1)
            def _():
                sh_ref[sidx, :, c0:c0 + n] = p[tm - 1:tm, :]
        else:
            per_tile = tm // t_seq
            for j in range(per_tile):
                prev = jnp.where(rows == j * t_seq, fp_ref[i * per_tile + j, :, c0:c0 + n], prev)
                sh_ref[i * per_tile + j, :, c0:c0 + n] = p[(j + 1) * t_seq - 1:(j + 1) * t_seq, :]
        return p + (prev - p) * mu_ref[:, c0:c0 + n]

    p_wa = proj(3 * RW, LORA_WA)
    p_g = proj(3 * RW + LORA_WA, LORA_G)
    xwa = shift(p_wa, 3 * RW)
    xg = shift(p_g, 3 * RW + LORA_WA)
    wl = w0_ref[...] + _dot(jnp.tanh(xwa).astype(BF16), w2_ref[...])
    alr = _sigmoid(a0_ref[...] + _dot(xwa.astype(BF16), a2_ref[...]))
    g_ref[...] = _dot(_sigmoid(xg).astype(BF16), g2_ref[...]).astype(BF16)
    p_k = proj(RW, RW)
    lw = -EXP_NEG_HALF * _sigmoid(wl)
    rr = _iota((tm, tm), 0)
    cc = _iota((tm, tm), 1)
    same_chunk_tri = jnp.where(rr // chunk == cc // chunk, jnp.where(rr >= cc, 1.0, 0.0), 0.0).astype(BF16)
    lwc = _dot01(same_chunk_tri, lw)
    k = shift(p_k, RW)
    kk = k * kk_ref[...]
    kk_ss = _head_sum(kk * kk)
    p_r = proj(0, RW)
    p_v = proj(2 * RW, RW)
    kkn = kk / jnp.maximum(jnp.sqrt(kk_ss), 1e-12)
    kh = k * (1.0 + (alr - 1.0) * ka_ref[...])
    e_in = jnp.exp(lwc)
    e_out = jnp.exp(-lwc)
    at_ref[...] = (-kkn * jnp.exp(lwc - lw)).astype(BF16)
    bt_ref[...] = (kkn * alr * e_out).astype(BF16)
    kt_ref[...] = (kh * e_out).astype(BF16)
    for c in range(tm // chunk):
        wtot_ref[c] = e_in[(c + 1) * chunk - 1:(c + 1) * chunk, :]
    r = shift(p_r, 0)
    rt_ref[...] = (r * e_in).astype(BF16)
    vb_ref[...] = shift(p_v, 2 * RW).astype(BF16)


def _rwkv_proj(x, w, mu, fprev, w0, w2p, a0, a2p, g2p, k_k, k_a, *, tm, t_seq, chunk):
    m, d = x.shape
    n_seq = m // t_seq
    row = lambda i: (i, 0)
    big = lambda dt: jax.ShapeDtypeStruct((m, RW), dt)
    return pl.pallas_call(
        functools.partial(_rwkv_proj_kernel, tm=tm, t_seq=t_seq, chunk=chunk),
        grid=(m // tm,),
        in_specs=[pl.BlockSpec((tm, d), row),
                  _const_spec(w.shape), _const_spec(mu.shape), _const_spec(fprev.shape),
                  _const_spec(w0.shape), _const_spec(w2p.shape), _const_spec(a0.shape), _const_spec(a2p.shape),
                  _const_spec(g2p.shape), _const_spec(k_k.shape), _const_spec(k_a.shape)],
        out_specs=[pl.BlockSpec((tm, RW), row)] * 6
                  + [pl.BlockSpec((tm // chunk, 1, RW), lambda i: (i, 0, 0)),
                     pl.BlockSpec((n_seq, 1, RP_PAD), lambda i: (0, 0, 0))],
        out_shape=[big(BF16)] * 6 + [jax.ShapeDtypeStruct((m // chunk, 1, RW), F32),
                                     jax.ShapeDtypeStruct((n_seq, 1, RP_PAD), F32)],
        scratch_shapes=[pltpu.VMEM((1, RP_PAD), F32)],
        compiler_params=_params(("arbitrary",)),
        name="rwkv_proj",
    )(x, w, mu, fprev, w0, w2p, a0, a2p, g2p, k_k, k_a)


FW = 1024
FOX_HEADS = FW // HEAD_DIM


def _fox_proj_kernel(x_ref, w_ref, bf_ref, og_ref,
                     qb_ref, k_ref, v_ref, kb_ref, vb_ref, vbt_ref, gate_ref, lf_ref, caug_ref, carry_ref,
                     *, tm, t_seq):
    i = pl.program_id(0)
    xb = x_ref[...].astype(BF16)
    qb_ref[...] = (_dot(xb, w_ref[:, 0:FW]) * (ATTN_SCALE * LOG2E)).astype(BF16)
    k = _dot(xb, w_ref[:, FW:2 * FW])
    k_ref[...] = k
    kb_ref[...] = k.astype(BF16)
    v = _dot(xb, w_ref[:, 2 * FW:3 * FW])
    v_ref[...] = v
    vb_ref[...] = v.astype(BF16)
    vbt_ref[0] = v.T.astype(BF16)
    og = _dot(xb, w_ref[:, 3 * FW:4 * FW])
    gate_ref[...] = _sigmoid(og) * og_ref[...]
    logf = _log_sigmoid(_dot(xb, w_ref[:, 4 * FW:4 * FW + LANES]) + bf_ref[...])
    lf_ref[...] = logf[:, :FOX_HEADS]
    r = _iota((tm, tm), 0)
    c = _iota((tm, tm), 1)
    if t_seq >= tm:
        cs = _dot01(jnp.where(r >= c, 1.0, 0.0).astype(BF16), logf)
        cs = cs + jnp.where(lax.rem(i, t_seq // tm) == 0, 0.0, carry_ref[...])
        carry_ref[...] = cs[tm - 1:tm, :]
    else:
        same_seq = (r // t_seq) == (c // t_seq)
        cs = _dot01(jnp.where(same_seq, jnp.where(r >= c, 1.0, 0.0), 0.0).astype(BF16), logf)
    xs = -LOG2E * cs
    hi, mid, lo = _split3(xs)
    lane = _iota((tm, LANES), 1)
    zero = jnp.zeros_like(hi)
    caug_ref[...] = jnp.where(lane < FOX_HEADS, hi,
                              jnp.where(lane < 2 * FOX_HEADS, mid, jnp.where(lane < 3 * FOX_HEADS, lo, zero)))


def _cast_job(a, block, index_map, out_shape=None):
    spec = pl.BlockSpec(block, index_map)
    return a, spec, jax.ShapeDtypeStruct(out_shape or a.shape, BF16)


def _fox_proj(x, w, bf_row, og_g, *, tm, t_seq):
    m, d = x.shape
    row = lambda i: (i, 0)
    big = lambda dt: jax.ShapeDtypeStruct((m, FW), dt)
    return pl.pallas_call(
        functools.partial(_fox_proj_kernel, tm=tm, t_seq=t_seq),
        grid=(m // tm,),
        in_specs=[pl.BlockSpec((tm, d), row), _const_spec(w.shape), _const_spec(bf_row.shape),
                  _const_spec(og_g.shape)],
        out_specs=[pl.BlockSpec((tm, FW), row)] * 5
                  + [pl.BlockSpec((1, FW, tm), lambda i: (i, 0, 0)), pl.BlockSpec((tm, FW), row),
                     pl.BlockSpec((tm, FOX_HEADS), row), pl.BlockSpec((tm, LANES), row)],
        out_shape=[big(BF16), big(F32), big(F32), big(BF16), big(BF16),
                   jax.ShapeDtypeStruct((m // tm, FW, tm), BF16), big(F32),
                   jax.ShapeDtypeStruct((m, FOX_HEADS), F32), jax.ShapeDtypeStruct((m, LANES), BF16)],
        scratch_shapes=[pltpu.VMEM((1, LANES), F32)],
        compiler_params=_params(("arbitrary",)),
        name="fox_proj",
    )(x, w, bf_row, og_g)


def _stack_heads(x, mask0):
    return jnp.concatenate([jnp.where(mask0, x, 0.0), jnp.where(mask0, 0.0, x)], axis=0)


def _rwkv_scan_kernel(at_ref, rt_ref, bt_ref, kt_ref, vb_ref, wtot_ref, g_ref, s0_ref, lng_ref, lnb_ref, rk_ref,
                      y_ref, sout_ref, state_ref, *, chunk, n_steps):
    step = pl.program_id(1)
    n_pairs = at_ref.shape[1] // LANES
    C = chunk
    n_ch = at_ref.shape[0] // C
    pairs = range(n_pairs)

    @pl.when(step == 0)
    def _():
        zero = jnp.zeros((HEAD_DIM, HEAD_DIM), F32)
        for p in pairs:
            top = jnp.concatenate([s0_ref[0, 2 * p], zero], axis=1)
            bot = jnp.concatenate([zero, s0_ref[0, 2 * p + 1]], axis=1)
            state_ref[p] = jnp.concatenate([top, bot], axis=0)

    fmask0 = _iota((1, LANES), 1) < HEAD_DIM
    tcol = _iota((C, 2 * C), 1)
    trow = _iota((C, 2 * C), 0)
    tmask0 = tcol < C
    tj = jnp.where(tmask0, tcol, tcol - C)
    strict = tj < trow
    incl = tj <= trow
    eye_pair = jnp.where(tj == trow, 1.0, 0.0)
    blk = (_iota((LANES, LANES), 0) // HEAD_DIM) == (_iota((LANES, LANES), 1) // HEAD_DIM)

    def stack_t(x):
        return _stack_heads(x, tmask0)

    def tile(ref, c, p):
        return ref[c * C:(c + 1) * C, p * LANES:(p + 1) * LANES]

    probs = [(c, p) for c in range(n_ch) for p in pairs]
    at = {cp: tile(at_ref, *cp) for cp in probs}
    rt = {cp: tile(rt_ref, *cp) for cp in probs}
    bt = {cp: tile(bt_ref, *cp) for cp in probs}
    kt = {cp: tile(kt_ref, *cp) for cp in probs}
    vb = {cp: tile(vb_ref, *cp) for cp in probs}
    gm = {cp: _dg(jnp.concatenate([at[cp], rt[cp]], axis=0),
                  jnp.concatenate([_stack_heads(bt[cp], fmask0), _stack_heads(kt[cp], fmask0)], axis=0), NT)
          for cp in probs}
    lab = {cp: jnp.where(strict, gm[cp][:C, :2 * C], 0.0) for cp in probs}
    lak_b = {cp: jnp.where(strict, gm[cp][:C, 2 * C:], 0.0).astype(BF16) for cp in probs}
    mr_b = {cp: jnp.concatenate([jnp.where(incl, gm[cp][C:, :2 * C], 0.0),
                                 jnp.where(incl, gm[cp][C:, 2 * C:], 0.0)], axis=1).astype(BF16) for cp in probs}
    kmax = int(math.log2(C)) - 1
    tinv = {cp: eye_pair + lab[cp] for cp in probs}
    pw = {cp: lab[cp].astype(BF16) for cp in probs}
    pw = {cp: _dot(pw[cp], stack_t(pw[cp])).astype(BF16) for cp in probs}
    for _ in range(1, kmax):
        res = {cp: _dot(jnp.concatenate([pw[cp], tinv[cp].astype(BF16)], axis=0), stack_t(pw[cp])) for cp in probs}
        pw = {cp: res[cp][:C].astype(BF16) for cp in probs}
        tinv = {cp: tinv[cp] + res[cp][C:] for cp in probs}
    tinv_b = {cp: (tinv[cp] + _dot(tinv[cp].astype(BF16), stack_t(pw[cp]))).astype(BF16) for cp in probs}

    sd = [state_ref[p] for p in pairs]
    ys = []
    for c in range(n_ch):
        sd_b = [z.astype(BF16) for z in sd]
        vd_b = [_stack_heads(vb[c, p], fmask0) for p in pairs]
        x = [_dg(at[c, p], sd_b[p], NT) + _dot(lak_b[c, p], vd_b[p]) for p in pairs]
        u_b = [_dot(tinv_b[c, p], _stack_heads(x[p].astype(BF16), fmask0)).astype(BF16) for p in pairs]
        ys.append(jnp.concatenate(
            [_dg(rt[c, p], sd_b[p], NT)
             + _dot(mr_b[c, p], jnp.concatenate([_stack_heads(u_b[p], fmask0), vd_b[p]], axis=0)) for p in pairs],
            axis=1))
        w_tot = wtot_ref[c]
        for p in pairs:
            ds = _dg(jnp.concatenate([u_b[p], vb[c, p]], axis=0), jnp.concatenate([bt[c, p], kt[c, p]], axis=0), TN)
            sd[p] = (sd[p] + jnp.where(blk, ds, 0.0)) * w_tot[:, p * LANES:(p + 1) * LANES]
    for p in pairs:
        state_ref[p] = sd[p]

    y = ys[0] if n_ch == 1 else jnp.concatenate(ys, axis=0)
    inv_n = 1.0 / HEAD_DIM
    mu = _head_sum(y) * inv_n
    yc = y - mu
    var = _head_sum(yc * yc) * inv_n
    yn = yc * lax.rsqrt(var + GN_EPS) * lng_ref[...] + lnb_ref[...]
    rk = rt_ref[...].astype(F32) * kt_ref[...].astype(F32) * rk_ref[...]
    bonus = _head_sum(rk) * vb_ref[...].astype(F32)
    y_ref[...] = ((yn + bonus) * g_ref[...].astype(F32)).astype(y_ref.dtype)

    @pl.when(step == n_steps - 1)
    def _():
        for p in pairs:
            sout_ref[0, 2 * p] = sd[p][:HEAD_DIM, :HEAD_DIM]
            sout_ref[0, 2 * p + 1] = sd[p][HEAD_DIM:, HEAD_DIM:]


def _rwkv_scan(at, rt, bt, kt, vb, wtot, g, s0, lng, lnb, rk, *, chunk, rows, t_seq):
    m, w = at.shape
    n_b = m // t_seq
    n_steps = t_seq // rows
    n_heads = w // HEAD_DIM
    blk = pl.BlockSpec((rows, w), lambda bi, si: (bi * n_steps + si, 0))
    wt = pl.BlockSpec((rows // chunk, 1, w), lambda bi, si: (bi * n_steps + si, 0, 0))
    st = pl.BlockSpec((1, n_heads, HEAD_DIM, HEAD_DIM), lambda bi, si: (bi, 0, 0, 0))
    return pl.pallas_call(
        functools.partial(_rwkv_scan_kernel, chunk=chunk, n_steps=n_steps),
        grid=(n_b, n_steps),
        in_specs=[blk] * 5 + [wt, blk, st, _const_spec(lng.shape), _const_spec(lnb.shape), _const_spec(rk.shape)],
        out_specs=[blk, st],
        out_shape=[jax.ShapeDtypeStruct((m, w), BF16),
                   jax.ShapeDtypeStruct((n_b, n_heads, HEAD_DIM, HEAD_DIM), F32)],
        scratch_shapes=[pltpu.VMEM((w // LANES, LANES, LANES), F32)],
        compiler_params=_params(("arbitrary", "arbitrary")),
        name="rwkv_scan",
    )(at, rt, bt, kt, vb, wtot, g, s0, lng, lnb, rk)


def _transpose01(x):
    h = x.shape[1]
    eye = jnp.where(_iota((h, h), 0) == _iota((h, h), 1), 1.0, 0.0).astype(BF16)
    hi, mid, lo = _split3(x)
    return _dg(eye, hi, NT) + _dg(eye, mid, NT) + _dg(eye, lo, NT)


def _lane_cumsum(x, block):
    tri = jnp.where(_iota((block, block), 0) <= _iota((block, block), 1), 1.0, 0.0).astype(BF16)
    hi, mid, lo = _split3(x)
    outs = []
    offset = jnp.zeros((x.shape[0], 1), F32)
    for j in range(x.shape[1] // block):
        sl = slice(j * block, (j + 1) * block)
        c = _dot(hi[:, sl], tri) + _dot(mid[:, sl], tri) + _dot(lo[:, sl], tri)
        outs.append(c + offset)
        offset = offset + c[:, block - 1:block]
    return outs[0] if len(outs) == 1 else jnp.concatenate(outs, axis=1)


def _fox_prompt_kernel(q_ref, k_ref, caug_ref, vt_ref, gate_ref, *rest, tq, tk, n_cast):
    cast_in, y_ref, cast_out = rest[:n_cast], rest[n_cast], rest[n_cast + 1:]
    hp = pl.program_id(1)
    qi = pl.program_id(2)
    nq = k_ref.shape[0] // tq
    vblk = vt_ref.shape[2]
    lane = _iota((tq, LANES), 1)
    first = lane < HEAD_DIM
    q = q_ref[...]
    zero = jnp.zeros_like(q)
    hslot = jnp.where(lane < 3 * FOX_HEADS, lax.rem(lane, FOX_HEADS), -1)
    qsa = []
    for h in range(2):
        ones = jnp.where(hslot == 2 * hp + h, 1.0, 0.0).astype(BF16)
        qh = jnp.where(first, q, zero) if h == 0 else jnp.where(first, zero, q)
        qsa.append(jnp.concatenate([qh, ones], axis=1))

    def scores(item):
        k0, nk, q_lo, _ = item
        kk = jnp.concatenate([k_ref[k0:k0 + nk, :], caug_ref[k0:k0 + nk, :]], axis=1)
        rhs = jnp.concatenate([qsa[0][q_lo:], qsa[1][q_lo:]], axis=0)
        return _dg(kk, rhs, NT)

    def update(st, item, state):
        k0, nk, q_lo, masked = item
        nqv = tq - q_lo
        vt = jnp.concatenate([vt_ref[k0 // vblk + i] for i in range(nk // vblk)], axis=1)
        out = []
        for h in range(2):
            m_all, l_all, acc_all = state[h]
            sh = st[:, h * nqv:(h + 1) * nqv]
            if masked:
                sh = jnp.where(_iota((nk, nqv), 0) <= _iota((nk, nqv), 1), sh, NEG_BIG)
            m_old = m_all[:, q_lo:]
            m_new = jnp.maximum(m_old, jnp.max(sh, axis=0, keepdims=True))
            alpha = jnp.exp2(m_old - m_new)
            p = jnp.exp2(sh - m_new)
            l_new = alpha * l_all[:, q_lo:] + jnp.sum(p, axis=0, keepdims=True)
            acc_new = alpha * acc_all[:, q_lo:] + _dot(vt[h * HEAD_DIM:(h + 1) * HEAD_DIM], p.astype(BF16))
            if q_lo:
                m_new = jnp.concatenate([m_all[:, :q_lo], m_new], axis=1)
                l_new = jnp.concatenate([l_all[:, :q_lo], l_new], axis=1)
                acc_new = jnp.concatenate([acc_all[:, :q_lo], acc_new], axis=1)
            out.append((m_new, l_new, acc_new))
        return out

    def program(c):
        items = [(kb * tk, tk, 0, False) for kb in range(c * tq // tk)]
        items += [(c * tq + d * tk, tk, d * tk, True) for d in range(tq // tk)]
        state = [(jnp.full((1, tq), NEG_BIG, F32), jnp.zeros((1, tq), F32), jnp.zeros((HEAD_DIM, tq), F32))
                 for _ in range(2)]
        st_next = scores(items[0])
        for n, item in enumerate(items):
            st = st_next
            if n + 1 < len(items):
                st_next = scores(items[n + 1])
            state = update(st, item, state)
        yts = []
        for _, l_fin, acc in state:
            o = acc * (1.0 / l_fin)
            ms = jnp.mean(o * o, axis=0, keepdims=True)
            yts.append(o * lax.rsqrt(ms + RMS_EPS))
        yt = jnp.concatenate(yts, axis=0)
        y_ref[...] = (yt.T * gate_ref[...]).astype(y_ref.dtype)
        for src, dst in zip(cast_in, cast_out):
            dst[...] = src[...].astype(BF16)

    for c in range(nq):
        pl.when(qi == c)(functools.partial(program, c))


def _fox_prompt_attn(qb, kb, caug, vbt, gate, *, n_b, t_seq, tq, tk, cast_cols=(), cast_rows=()):
    m, w = qb.shape
    n_pairs = w // LANES
    nq = t_seq // tq
    n_steps = n_b * n_pairs * nq
    vblk = vbt.shape[2]
    assert vbt.shape == (m // vblk, w, vblk) and tk % vblk == 0 and tq % tk == 0
    qspec = pl.BlockSpec((tq, LANES), lambda b, hp, qi: (b * nq + qi, hp))
    step = lambda b, hp, qi: (b * n_pairs + hp) * nq + qi
    casts = ([_cast_job(a, (a.shape[0], a.shape[1] // n_steps), lambda b, hp, qi: (0, step(b, hp, qi)))
              for a in cast_cols]
             + [_cast_job(a, (a.shape[0] // n_steps, a.shape[1]), lambda b, hp, qi: (step(b, hp, qi), 0))
                for a in cast_rows])
    outs = pl.pallas_call(
        functools.partial(_fox_prompt_kernel, tq=tq, tk=tk, n_cast=len(casts)),
        grid=(n_b, n_pairs, nq),
        in_specs=[qspec,
                  pl.BlockSpec((t_seq, LANES), lambda b, hp, qi: (b, hp)),
                  pl.BlockSpec((t_seq, LANES), lambda b, hp, qi: (b, 0)),
                  pl.BlockSpec((t_seq // vblk, LANES, vblk), lambda b, hp, qi: (b, hp, 0)),
                  qspec] + [spec for _, spec, _ in casts],
        out_specs=[qspec] + [spec for _, spec, _ in casts],
        out_shape=[jax.ShapeDtypeStruct((m, w), BF16)] + [sds for _, _, sds in casts],
        compiler_params=_params(("arbitrary", "arbitrary", "arbitrary")),
        name="fox_prompt_attn",
    )(qb, kb, caug, vbt, gate, *[a for a, _, _ in casts])
    return outs[0], outs[1:]


def _fox_sample_kernel(q_ref, kn_ref, vn_ref, ck_ref, cv_ref, clf_ref, lf_ref, gate_ref, y_ref):
    t = q_ref.shape[0]
    n_heads = lf_ref.shape[1]
    tril = jnp.where(_iota((t, t), 0) >= _iota((t, t), 1), 1.0, 0.0).astype(BF16)
    clf = clf_ref[0]
    c_tot = jnp.sum(clf, axis=0, keepdims=True)
    cn_col = _dot01(tril, lf_ref[...])
    cn_row = _transpose01(cn_col)
    cq_col = cn_col + c_tot
    cc_row = _lane_cumsum(_transpose01(clf), 2 * LANES)
    lane = _iota((t, LANES), 1)
    first = lane < HEAD_DIM
    causal = _iota((t, t), 1) <= _iota((t, t), 0)
    ys = []
    for p in range(q_ref.shape[1] // LANES):
        sl = slice(p * LANES, (p + 1) * LANES)
        q = q_ref[:, sl]
        zero = jnp.zeros_like(q)
        qs = jnp.concatenate([jnp.where(first, q, zero), jnp.where(first, zero, q)], axis=0)
        kc = ck_ref[0, :, sl].astype(BF16)
        vc = cv_ref[0, :, sl].astype(BF16)
        kn = kn_ref[:, sl]
        vn = vn_ref[:, sl]
        s_c = _dg(qs, kc, NT)
        s_n = _dg(qs, kn, NT)
        o = []
        for h in range(2):
            hd = 2 * p + h
            sc = s_c[h * t:(h + 1) * t] + (cq_col[:, hd:hd + 1] - cc_row[hd:hd + 1]) * LOG2E
            sn = s_n[h * t:(h + 1) * t] + (cn_col[:, hd:hd + 1] - cn_row[hd:hd + 1]) * LOG2E
            sn = jnp.where(causal, sn, NEG_BIG)
            mx = jnp.maximum(jnp.max(sc, axis=-1, keepdims=True), jnp.max(sn, axis=-1, keepdims=True))
            pc = jnp.exp2(sc - mx)
            pn = jnp.exp2(sn - mx)
            den = jnp.sum(pc, axis=-1, keepdims=True) + jnp.sum(pn, axis=-1, keepdims=True)
            o.append((_dot(pc.astype(BF16), vc) + _dot(pn.astype(BF16), vn)) / den)
        om = jnp.where(first, o[0], o[1])
        sq = om * om
        ms0 = jnp.sum(jnp.where(first, sq, 0.0), axis=-1, keepdims=True)
        ms1 = jnp.sum(jnp.where(first, 0.0, sq), axis=-1, keepdims=True)
        ms = jnp.where(first, ms0, ms1) * (1.0 / HEAD_DIM)
        ys.append(om * lax.rsqrt(ms + RMS_EPS) * gate_ref[:, sl])
    y_ref[...] = jnp.concatenate(ys, axis=1).astype(y_ref.dtype)


def _fox_sample_attn(qb, kb, vb, cache_k, cache_v, cache_lf, lf, gate, *, t_seq):
    m, w = qb.shape
    n_b = m // t_seq
    row = pl.BlockSpec((t_seq, w), lambda b: (b, 0))
    cache = pl.BlockSpec((1,) + cache_k.shape[1:], lambda b: (b, 0, 0))
    return pl.pallas_call(
        _fox_sample_kernel,
        grid=(n_b,),
        in_specs=[row, row, row, cache, cache,
                  pl.BlockSpec((1,) + cache_lf.shape[1:], lambda b: (b, 0, 0)),
                  pl.BlockSpec((t_seq, lf.shape[1]), lambda b: (b, 0)), row],
        out_specs=row,
        out_shape=jax.ShapeDtypeStruct((m, w), BF16),
        compiler_params=_params(("arbitrary",)),
        name="fox_sample_attn",
    )(qb, kb, vb, cache_k, cache_v, cache_lf, lf, gate)


def _out_ln_kernel(yr_ref, yf_ref, x_ref, wo_ref, g_ref, b_ref, h_ref, wob_ref, *, alpha):
    @pl.when(pl.program_id(0) == 0)
    def _():
        wob_ref[...] = wo_ref[...].astype(BF16)

    half = yr_ref.shape[1]
    mix = _dot(yr_ref[...], wob_ref[0:half, :]) + _dot(yf_ref[...], wob_ref[half:, :])
    h_ref[...] = _layer_norm(alpha * x_ref[...] + mix, g_ref[...], b_ref[...])


def _out_ln(yr, yf, x, wo, g, b, *, tm, alpha):
    m, d = x.shape
    row = lambda i: (i, 0)
    return pl.pallas_call(
        functools.partial(_out_ln_kernel, alpha=alpha),
        grid=(m // tm,),
        in_specs=[pl.BlockSpec((tm, yr.shape[1]), row), pl.BlockSpec((tm, yf.shape[1]), row),
                  pl.BlockSpec((tm, d), row), _const_spec(wo.shape), _const_spec(g.shape), _const_spec(b.shape)],
        out_specs=pl.BlockSpec((tm, d), row),
        out_shape=jax.ShapeDtypeStruct((m, d), F32),
        scratch_shapes=[pltpu.VMEM(wo.shape, BF16)],
        compiler_params=_params(("arbitrary",)),
        name="out_ln",
    )(yr, yf, x, wo, g, b)


def _ffn_ln_kernel(h_ref, wu_ref, wd_ref, g_ref, b_ref, k2_ref, v2_ref, o_ref, k5_ref, v5_ref, hb_ref,
                   *, alpha, n_f):
    j = pl.program_id(1)

    @pl.when(j == 0)
    def _():
        h = h_ref[...]
        hb_ref[...] = h.astype(BF16)
        o_ref[...] = alpha * h

    u = jnp.maximum(_dot(hb_ref[...], wu_ref[...]), 0.0)
    o_ref[...] += _dot((u * u).astype(BF16), wd_ref[...])
    for src, dst in ((k2_ref, k5_ref), (v2_ref, v5_ref)):
        x = src[...]
        heads = jnp.stack([x[:, hd * HEAD_DIM:(hd + 1) * HEAD_DIM] for hd in range(dst.shape[1])], axis=0)
        dst[...] = jnp.swapaxes(heads, 0, 1)

    @pl.when(j == n_f - 1)
    def _():
        o_ref[...] = _layer_norm(o_ref[...], g_ref[...], b_ref[...])


def _ffn_ln(h, wu, wd, g, b, k2, v2, *, tm, tf, alpha):
    m, d = h.shape
    n_f = wu.shape[1] // tf
    rows = tm // n_f
    heads = k2.shape[1] // HEAD_DIM
    slab_in = pl.BlockSpec((rows, k2.shape[1]), lambda i, j: (i * n_f + j, 0))
    slab_out = pl.BlockSpec((rows, heads, HEAD_DIM), lambda i, j: (i * n_f + j, 0, 0))
    kv5 = jax.ShapeDtypeStruct((m, heads, HEAD_DIM), F32)
    return pl.pallas_call(
        functools.partial(_ffn_ln_kernel, alpha=alpha, n_f=n_f),
        grid=(m // tm, n_f),
        in_specs=[pl.BlockSpec((tm, d), lambda i, j: (i, 0), pipeline_mode=pl.Buffered(1)),
                  pl.BlockSpec((d, tf), lambda i, j: (0, j)),
                  pl.BlockSpec((tf, d), lambda i, j: (j, 0)),
                  _const_spec(g.shape), _const_spec(b.shape), slab_in, slab_in],
        out_specs=[pl.BlockSpec((tm, d), lambda i, j: (i, 0)), slab_out, slab_out],
        out_shape=[jax.ShapeDtypeStruct((m, d), F32), kv5, kv5],
        scratch_shapes=[pltpu.VMEM((tm, d), BF16)],
        compiler_params=_params(("arbitrary", "arbitrary")),
        name="ffn_ln",
    )(h, wu, wd, g, b, k2, v2)


def _pad_cols(x, n):
    return jnp.pad(x, [(0, 0)] * (x.ndim - 1) + [(0, n - x.shape[-1])])


def _stream(x, shift_prev, s0, wts, *, t_seq, tm, tm_ffn, chunk, cache=None):
    n_b, _, d = x.shape
    m = n_b * t_seq
    x2 = x.reshape(m, d)
    qb, k32, v32, kb, vb, vbt, gate, lf, caug = _fox_proj(x2, wts["w_f"], wts["bf_row"], wts["og_g"],
                                                          tm=tm, t_seq=t_seq)
    fprev = _pad_cols(shift_prev, RP_PAD)
    at, rt, bt, kt, vr, g, wtot, sh = _rwkv_proj(
        x2, wts["w_r"], wts["mu"], fprev, wts["w0"], wts["w2p"], wts["a0"], wts["a2p"], wts["g2p"], wts["k_k"],
        wts["k_a"], tm=tm, t_seq=t_seq, chunk=chunk)
    y_r, s_new = _rwkv_scan(at, rt, bt, kt, vr, wtot, g, s0, wts["lnx_g"], wts["lnx_b"], wts["r_k"],
                            chunk=chunk, rows=min(SCAN_CHUNKS * chunk, t_seq), t_seq=t_seq)
    if cache is None:
        y_f, (wts["w_up_b"], wts["w_down_b"]) = _fox_prompt_attn(
            qb, kb, caug, vbt, gate, n_b=n_b, t_seq=t_seq, tq=TQ_ATTN, tk=TK_ATTN,
            cast_cols=(wts["w_up"],), cast_rows=(wts["w_down"],))
    else:
        ck, cv, clf = cache
        y_f = _fox_sample_attn(qb, kb, vb, ck.reshape(n_b, ck.shape[1], FW), cv.reshape(n_b, cv.shape[1], FW),
                               clf, lf, gate, t_seq=t_seq)
    h = _out_ln(y_r, y_f, x2, wts["w_o"], wts["ln1_g"], wts["ln1_b"], tm=tm, alpha=wts["alpha"])
    y, k5, v5 = _ffn_ln(h, wts["w_up_b"], wts["w_down_b"], wts["ln2_g"], wts["ln2_b"], k32, v32,
                        tm=tm_ffn, tf=TF_FFN, alpha=wts["alpha"])
    heads = FW // HEAD_DIM
    return (y.reshape(n_b, t_seq, d), k5.reshape(n_b, t_seq, heads, HEAD_DIM),
            v5.reshape(n_b, t_seq, heads, HEAD_DIM), lf.reshape(n_b, t_seq, heads), s_new,
            sh[..., :shift_prev.shape[-1]])


def kernel(x_prompt, x_sample, cache_fox_k, cache_fox_v, cache_fox_logf, state_rwkv_wkv, state_rwkv_shift,
           w_in, rwkv_mu, rwkv_w0, rwkv_w2, rwkv_a0, rwkv_a2, rwkv_g2, rwkv_k_k, rwkv_k_a, rwkv_r_k,
           rwkv_lnx_g, rwkv_lnx_b, fox_b_f, fox_out_g, w_o, ln1_g, ln1_b, w_up, w_down, ln2_g, ln2_b):
    depth = w_in.shape[0]
    assert depth == 1, "single-layer problem"
    rwkv_proj = rwkv_mu.shape[-1]
    lora = (rwkv_w2.shape[1], rwkv_a2.shape[1], rwkv_g2.shape[1])
    assert rwkv_w0.shape[-1] == RW and fox_out_g.shape[-1] == FW and rwkv_proj == 3 * RW + sum(lora)
    assert lora[0] + lora[1] == LORA_WA and lora[2] <= LORA_G and RP_PAD <= w_in.shape[-1]
    alpha = (2 * depth) ** 0.25
    l = 0
    w = w_in[l]
    fo = rwkv_proj
    row = lambda z: z.reshape(1, -1)
    pad_rows = lambda z, n: jnp.pad(z, ((0, n - z.shape[0]), (0, 0)))
    wts = dict(
        alpha=alpha,
        w_r=w[:, :RP_PAD].astype(BF16),
        w_f=jnp.concatenate([w[:, fo:fo + 3 * FW], w[:, fo + 3 * FW + FOX_HEADS:],
                             _pad_cols(jnp.tile(w[:, fo + 3 * FW:fo + 3 * FW + FOX_HEADS], (1, 3)), LANES)],
                            axis=-1).astype(BF16),
        mu=_pad_cols(row(rwkv_mu[l]), RP_PAD),
        w0=row(rwkv_w0[l]), w2p=jnp.pad(rwkv_w2[l], ((0, lora[1]), (0, 0))).astype(BF16),
        a0=row(rwkv_a0[l]), a2p=jnp.pad(rwkv_a2[l], ((lora[0], 0), (0, 0))).astype(BF16),
        g2p=pad_rows(rwkv_g2[l], LORA_G).astype(BF16),
        k_k=row(rwkv_k_k[l]), k_a=row(rwkv_k_a[l]), r_k=row(rwkv_r_k[l]),
        lnx_g=row(rwkv_lnx_g[l]), lnx_b=row(rwkv_lnx_b[l]),
        bf_row=_pad_cols(jnp.tile(row(fox_b_f[l]), (1, 3)), LANES), og_g=row(fox_out_g[l]),
        w_o=w_o[l], ln1_g=row(ln1_g[l]), ln1_b=row(ln1_b[l]),
        w_up=w_up[l], w_down=w_down[l], ln2_g=row(ln2_g[l]), ln2_b=row(ln2_b[l]),
    )
    n_p, t_p, _ = x_prompt.shape
    n_s, t_s, _ = x_sample.shape
    heads = RW // HEAD_DIM
    shift0 = jnp.zeros((n_p, 1, rwkv_proj), F32)
    s_zero = jnp.zeros((n_p, heads, HEAD_DIM, HEAD_DIM), F32)
    yp, kp, vp, fp, sp, shp = _stream(x_prompt, shift0, s_zero, wts, t_seq=t_p, tm=TM_PROJ, tm_ffn=TM_FFN, chunk=CHUNK)
    ys, ks, vs, fs, ss, shs = _stream(x_sample, state_rwkv_shift[l], state_rwkv_wkv[l], wts, t_seq=t_s,
                                      tm=n_s * t_s, tm_ffn=n_s * t_s, chunk=t_s,
                                      cache=(cache_fox_k[l], cache_fox_v[l], cache_fox_logf[l]))
    return (yp, ys, kp[None], vp[None], fp[None], sp[None], shp[None],
            ks[None], vs[None], fs[None], ss[None], shs[None])
```

```python
import functools
import math

import jax
import jax.numpy as jnp
from jax import lax
from jax.experimental import pallas as pl
from jax.experimental.pallas import tpu as pltpu

F32 = jnp.float32
BF16 = jnp.bfloat16

HEAD_DIM = 64
LANES = 128
LN_EPS = 1e-5
GN_EPS = 64e-5
RMS_EPS = 1e-6
ATTN_SCALE = HEAD_DIM ** -0.5
EXP_NEG_HALF = math.exp(-0.5)
LOG2E = math.log2(math.e)
NEG_BIG = -1e30
VMEM_LIMIT = 60 * 1024 * 1024

TM_PROJ = 256
TM_FFN = 1024
TF_FFN = 1024
TQ_ATTN = 1024
TK_ATTN = 512
CHUNK = 64
SCAN_CHUNKS = 4

NT = (((1,), (1,)), ((), ()))
TN = (((0,), (0,)), ((), ()))


def _sigmoid(x):
    return 1.0 / (1.0 + jnp.exp(-x))


def _log_sigmoid(x):
    return jnp.minimum(x, 0.0) - jnp.log1p(jnp.exp(-jnp.abs(x)))


def _dot(a, b):
    return jnp.dot(a, b, preferred_element_type=F32)


def _dg(a, b, dims):
    return lax.dot_general(a, b, dims, preferred_element_type=F32)


def _split3(x):
    hi = x.astype(BF16)
    rem = x - hi.astype(F32)
    mid = rem.astype(BF16)
    return hi, mid, (rem - mid.astype(F32)).astype(BF16)


def _dot01(m01, x):
    hi, mid, lo = _split3(x)
    return _dot(m01, hi) + _dot(m01, mid) + _dot(m01, lo)


def _iota(shape, axis):
    return lax.broadcasted_iota(jnp.int32, shape, axis)


def _head_sum(x):
    r = _iota((LANES, LANES), 0) // HEAD_DIM
    c = _iota((LANES, LANES), 1) // HEAD_DIM
    ones_blk = jnp.where(r == c, 1.0, 0.0).astype(BF16)
    outs = []
    for g in range(x.shape[1] // LANES):
        xs = x[:, g * LANES:(g + 1) * LANES]
        hi = xs.astype(BF16)
        lo = (xs - hi.astype(F32)).astype(BF16)
        outs.append(_dot(hi, ones_blk) + _dot(lo, ones_blk))
    return outs[0] if len(outs) == 1 else jnp.concatenate(outs, axis=1)


def _layer_norm(z, g, b):
    mu = jnp.mean(z, axis=-1, keepdims=True)
    zc = z - mu
    var = jnp.mean(zc * zc, axis=-1, keepdims=True)
    return zc * lax.rsqrt(var + LN_EPS) * g + b


def _const_spec(shape):
    nd = len(shape)
    return pl.BlockSpec(shape, lambda *_: (0,) * nd, pipeline_mode=pl.Buffered(1))


def _params(sem):
    return pltpu.CompilerParams(dimension_semantics=sem, vmem_limit_bytes=VMEM_LIMIT)


RW = 1024
LORA_WA = LANES
LORA_G = 2 * LANES
RP_PAD = 3 * RW + LORA_WA + LORA_G


def _rwkv_proj_kernel(x_ref, w_ref, mu_ref, fp_ref, w0_ref, w2_ref, a0_ref, a2_ref, g2_ref, kk_ref, ka_ref,
                      at_ref, rt_ref, bt_ref, kt_ref, vb_ref, g_ref, wtot_ref, sh_ref, carry_ref,
                      *, tm, t_seq, chunk):
    i = pl.program_id(0)
    xb = x_ref[...].astype(BF16)
    rows = _iota((tm, 1), 0)

    def proj(c0, n):
        return _dot(xb, w_ref[:, c0:c0 + n])

    def shift(p, c0):
        n = p.shape[1]
        prev = pltpu.roll(p, 1, 0)
        if t_seq >= tm:
            tiles = t_seq // tm
            pos = lax.rem(i, tiles)
            sidx = lax.div(i, tiles)
            row0 = jnp.where(pos == 0, fp_ref[sidx, :, c0:c0 + n], carry_ref[:, c0:c0 + n])
            prev = jnp.where(rows == 0, row0, prev)
            carry_ref[:, c0:c0 + n] = p[tm - 1:tm, :]

            @pl.when(pos == tiles - 1)
            def _():
                sh_ref[sidx, :, c0:c0 + n] = p[tm - 1:tm, :]
        else:
            per_tile = tm // t_seq
            for j in range(per_tile):
                prev = jnp.where(rows == j * t_seq, fp_ref[i * per_tile + j, :, c0:c0 + n], prev)
                sh_ref[i * per_tile + j, :, c0:c0 + n] = p[(j + 1) * t_seq - 1:(j + 1) * t_seq, :]
        return p + (prev - p) * mu_ref[:, c0:c0 + n]

    p_wa = proj(3 * RW, LORA_WA)
    p_g = proj(3 * RW + LORA_WA, LORA_G)
    xwa = shift(p_wa, 3 * RW)
    xg = shift(p_g, 3 * RW + LORA_WA)
    wl = w0_ref[...] + _dot(jnp.tanh(xwa).astype(BF16), w2_ref[...])
    alr = _sigmoid(a0_ref[...] + _dot(xwa.astype(BF16), a2_ref[...]))
    g_ref[...] = _dot(_sigmoid(xg).astype(BF16), g2_ref[...]).astype(BF16)
    p_k = proj(RW, RW)
    lw = -EXP_NEG_HALF * _sigmoid(wl)
    rr = _iota((tm, tm), 0)
    cc = _iota((tm, tm), 1)
    same_chunk_tri = jnp.where(rr // chunk == cc // chunk, jnp.where(rr >= cc, 1.0, 0.0), 0.0).astype(BF16)
    lwc = _dot01(same_chunk_tri, lw)
    k = shift(p_k, RW)
    kk = k * kk_ref[...]
    kk_ss = _head_sum(kk * kk)
    p_r = proj(0, RW)
    p_v = proj(2 * RW, RW)
    kkn = kk / jnp.maximum(jnp.sqrt(kk_ss), 1e-12)
    kh = k * (1.0 + (alr - 1.0) * ka_ref[...])
    e_in = jnp.exp(lwc)
    e_out = jnp.exp(-lwc)
    at_ref[...] = (-kkn * jnp.exp(lwc - lw)).astype(BF16)
    bt_ref[...] = (kkn * alr * e_out).astype(BF16)
    kt_ref[...] = (kh * e_out).astype(BF16)
    for c in range(tm // chunk):
        wtot_ref[c] = e_in[(c + 1) * chunk - 1:(c + 1) * chunk, :]
    r = shift(p_r, 0)
    rt_ref[...] = (r * e_in).astype(BF16)
    vb_ref[...] = shift(p_v, 2 * RW).astype(BF16)


def _rwkv_proj(x, w, mu, fprev, w0, w2p, a0, a2p, g2p, k_k, k_a, *, tm, t_seq, chunk):
    m, d = x.shape
    n_seq = m // t_seq
    row = lambda i: (i, 0)
    big = lambda dt: jax.ShapeDtypeStruct((m, RW), dt)
    return pl.pallas_call(
        functools.partial(_rwkv_proj_kernel, tm=tm, t_seq=t_seq, chunk=chunk),
        grid=(m // tm,),
        in_specs=[pl.BlockSpec((tm, d), row),
                  _const_spec(w.shape), _const_spec(mu.shape), _const_spec(fprev.shape),
                  _const_spec(w0.shape), _const_spec(w2p.shape), _const_spec(a0.shape), _const_spec(a2p.shape),
                  _const_spec(g2p.shape), _const_spec(k_k.shape), _const_spec(k_a.shape)],
        out_specs=[pl.BlockSpec((tm, RW), row)] * 6
                  + [pl.BlockSpec((tm // chunk, 1, RW), lambda i: (i, 0, 0)),
                     pl.BlockSpec((n_seq, 1, RP_PAD), lambda i: (0, 0, 0))],
        out_shape=[big(BF16)] * 6 + [jax.ShapeDtypeStruct((m // chunk, 1, RW), F32),
                                     jax.ShapeDtypeStruct((n_seq, 1, RP_PAD), F32)],
        scratch_shapes=[pltpu.VMEM((1, RP_PAD), F32)],
        compiler_params=_params(("arbitrary",)),
        name="rwkv_proj",
    )(x, w, mu, fprev, w0, w2p, a0, a2p, g2p, k_k, k_a)


FW = 1024
FOX_HEADS = FW // HEAD_DIM


def _fox_proj_kernel(x_ref, w_ref, bf_ref, og_ref,
                     qb_ref, k_ref, v_ref, kb_ref, vb_ref, vbt_ref, gate_ref, lf_ref, caug_ref, carry_ref,
                     *, tm, t_seq):
    i = pl.program_id(0)
    xb = x_ref[...].astype(BF16)
    qb_ref[...] = (_dot(xb, w_ref[:, 0:FW]) * (ATTN_SCALE * LOG2E)).astype(BF16)
    k = _dot(xb, w_ref[:, FW:2 * FW])
    k_ref[...] = k
    kb_ref[...] = k.astype(BF16)
    v = _dot(xb, w_ref[:, 2 * FW:3 * FW])
    v_ref[...] = v
    vb_ref[...] = v.astype(BF16)
    vbt_ref[0] = v.T.astype(BF16)
    og = _dot(xb, w_ref[:, 3 * FW:4 * FW])
    gate_ref[...] = _sigmoid(og) * og_ref[...]
    logf = _log_sigmoid(_dot(xb, w_ref[:, 4 * FW:4 * FW + LANES]) + bf_ref[...])
    lf_ref[...] = logf[:, :FOX_HEADS]
    r = _iota((tm, tm), 0)
    c = _iota((tm, tm), 1)
    if t_seq >= tm:
        cs = _dot01(jnp.where(r >= c, 1.0, 0.0).astype(BF16), logf)
        cs = cs + jnp.where(lax.rem(i, t_seq // tm) == 0, 0.0, carry_ref[...])
        carry_ref[...] = cs[tm - 1:tm, :]
    else:
        same_seq = (r // t_seq) == (c // t_seq)
        cs = _dot01(jnp.where(same_seq, jnp.where(r >= c, 1.0, 0.0), 0.0).astype(BF16), logf)
    xs = -LOG2E * cs
    hi, mid, lo = _split3(xs)
    lane = _iota((tm, LANES), 1)
    zero = jnp.zeros_like(hi)
    caug_ref[...] = jnp.where(lane < FOX_HEADS, hi,
                              jnp.where(lane < 2 * FOX_HEADS, mid, jnp.where(lane < 3 * FOX_HEADS, lo, zero)))


def _cast_job(a, block, index_map, out_shape=None):
    spec = pl.BlockSpec(block, index_map)
    return a, spec, jax.ShapeDtypeStruct(out_shape or a.shape, BF16)


def _fox_proj(x, w, bf_row, og_g, *, tm, t_seq):
    m, d = x.shape
    row = lambda i: (i, 0)
    big = lambda dt: jax.ShapeDtypeStruct((m, FW), dt)
    return pl.pallas_call(
        functools.partial(_fox_proj_kernel, tm=tm, t_seq=t_seq),
        grid=(m // tm,),
        in_specs=[pl.BlockSpec((tm, d), row), _const_spec(w.shape), _const_spec(bf_row.shape),
                  _const_spec(og_g.shape)],
        out_specs=[pl.BlockSpec((tm, FW), row)] * 5
                  + [pl.BlockSpec((1, FW, tm), lambda i: (i, 0, 0)), pl.BlockSpec((tm, FW), row),
                     pl.BlockSpec((tm, FOX_HEADS), row), pl.BlockSpec((tm, LANES), row)],
        out_shape=[big(BF16), big(F32), big(F32), big(BF16), big(BF16),
                   jax.ShapeDtypeStruct((m // tm, FW, tm), BF16), big(F32),
                   jax.ShapeDtypeStruct((m, FOX_HEADS), F32), jax.ShapeDtypeStruct((m, LANES), BF16)],
        scratch_shapes=[pltpu.VMEM((1, LANES), F32)],
        compiler_params=_params(("arbitrary",)),
        name="fox_proj",
    )(x, w, bf_row, og_g)


def _stack_heads(x, mask0):
    return jnp.concatenate([jnp.where(mask0, x, 0.0), jnp.where(mask0, 0.0, x)], axis=0)


def _rwkv_scan_kernel(at_ref, rt_ref, bt_ref, kt_ref, vb_ref, wtot_ref, g_ref, s0_ref, lng_ref, lnb_ref, rk_ref,
                      y_ref, sout_ref, state_ref, *, chunk, n_steps):
    step = pl.program_id(1)
    n_pairs = at_ref.shape[1] // LANES
    C = chunk
    n_ch = at_ref.shape[0] // C
    pairs = range(n_pairs)

    @pl.when(step == 0)
    def _():
        zero = jnp.zeros((HEAD_DIM, HEAD_DIM), F32)
        for p in pairs:
            top = jnp.concatenate([s0_ref[0, 2 * p], zero], axis=1)
            bot = jnp.concatenate([zero, s0_ref[0, 2 * p + 1]], axis=1)
            state_ref[p] = jnp.concatenate([top, bot], axis=0)

    fmask0 = _iota((1, LANES), 1) < HEAD_DIM
    tcol = _iota((C, 2 * C), 1)
    trow = _iota((C, 2 * C), 0)
    tmask0 = tcol < C
    tj = jnp.where(tmask0, tcol, tcol - C)
    strict = tj < trow
    incl = tj <= trow
    eye_pair = jnp.where(tj == trow, 1.0, 0.0)
    blk = (_iota((LANES, LANES), 0) // HEAD_DIM) == (_iota((LANES, LANES), 1) // HEAD_DIM)

    def stack_t(x):
        return _stack_heads(x, tmask0)

    def tile(ref, c, p):
        return ref[c * C:(c + 1) * C, p * LANES:(p + 1) * LANES]

    probs = [(c, p) for c in range(n_ch) for p in pairs]
    at = {cp: tile(at_ref, *cp) for cp in probs}
    rt = {cp: tile(rt_ref, *cp) for cp in probs}
    bt = {cp: tile(bt_ref, *cp) for cp in probs}
    kt = {cp: tile(kt_ref, *cp) for cp in probs}
    vb = {cp: tile(vb_ref, *cp) for cp in probs}
    gm = {cp: _dg(jnp.concatenate([at[cp], rt[cp]], axis=0),
                  jnp.concatenate([_stack_heads(bt[cp], fmask0), _stack_heads(kt[cp], fmask0)], axis=0), NT)
          for cp in probs}
    lab = {cp: jnp.where(strict, gm[cp][:C, :2 * C], 0.0) for cp in probs}
    lak_b = {cp: jnp.where(strict, gm[cp][:C, 2 * C:], 0.0).astype(BF16) for cp in probs}
    mr_b = {cp: jnp.concatenate([jnp.where(incl, gm[cp][C:, :2 * C], 0.0),
                                 jnp.where(incl, gm[cp][C:, 2 * C:], 0.0)], axis=1).astype(BF16) for cp in probs}
    kmax = int(math.log2(C)) - 1
    tinv = {cp: eye_pair + lab[cp] for cp in probs}
    pw = {cp: lab[cp].astype(BF16) for cp in probs}
    pw = {cp: _dot(pw[cp], stack_t(pw[cp])).astype(BF16) for cp in probs}
    for _ in range(1, kmax):
        res = {cp: _dot(jnp.concatenate([pw[cp], tinv[cp].astype(BF16)], axis=0), stack_t(pw[cp])) for cp in probs}
        pw = {cp: res[cp][:C].astype(BF16) for cp in probs}
        tinv = {cp: tinv[cp] + res[cp][C:] for cp in probs}
    tinv_b = {cp: (tinv[cp] + _dot(tinv[cp].astype(BF16), stack_t(pw[cp]))).astype(BF16) for cp in probs}

    sd = [state_ref[p] for p in pairs]
    ys = []
    for c in range(n_ch):
        sd_b = [z.astype(BF16) for z in sd]
        vd_b = [_stack_heads(vb[c, p], fmask0) for p in pairs]
        x = [_dg(at[c, p], sd_b[p], NT) + _dot(lak_b[c, p], vd_b[p]) for p in pairs]
        u_b = [_dot(tinv_b[c, p], _stack_heads(x[p].astype(BF16), fmask0)).astype(BF16) for p in pairs]
        ys.append(jnp.concatenate(
            [_dg(rt[c, p], sd_b[p], NT)
             + _dot(mr_b[c, p], jnp.concatenate([_stack_heads(u_b[p], fmask0), vd_b[p]], axis=0)) for p in pairs],
            axis=1))
        w_tot = wtot_ref[c]
        for p in pairs:
            ds = _dg(jnp.concatenate([u_b[p], vb[c, p]], axis=0), jnp.concatenate([bt[c, p], kt[c, p]], axis=0), TN)
            sd[p] = (sd[p] + jnp.where(blk, ds, 0.0)) * w_tot[:, p * LANES:(p + 1) * LANES]
    for p in pairs:
        state_ref[p] = sd[p]

    y = ys[0] if n_ch == 1 else jnp.concatenate(ys, axis=0)
    inv_n = 1.0 / HEAD_DIM
    mu = _head_sum(y) * inv_n
    yc = y - mu
    var = _head_sum(yc * yc) * inv_n
    yn = yc * lax.rsqrt(var + GN_EPS) * lng_ref[...] + lnb_ref[...]
    rk = rt_ref[...].astype(F32) * kt_ref[...].astype(F32) * rk_ref[...]
    bonus = _head_sum(rk) * vb_ref[...].astype(F32)
    y_ref[...] = ((yn + bonus) * g_ref[...].astype(F32)).astype(y_ref.dtype)

    @pl.when(step == n_steps - 1)
    def _():
        for p in pairs:
            sout_ref[0, 2 * p] = sd[p][:HEAD_DIM, :HEAD_DIM]
            sout_ref[0, 2 * p + 1] = sd[p][HEAD_DIM:, HEAD_DIM:]


def _rwkv_scan(at, rt, bt, kt, vb, wtot, g, s0, lng, lnb, rk, *, chunk, rows, t_seq):
    m, w = at.shape
    n_b = m // t_seq
    n_steps = t_seq // rows
    n_heads = w // HEAD_DIM
    blk = pl.BlockSpec((rows, w), lambda bi, si: (bi * n_steps + si, 0))
    wt = pl.BlockSpec((rows // chunk, 1, w), lambda bi, si: (bi * n_steps + si, 0, 0))
    st = pl.BlockSpec((1, n_heads, HEAD_DIM, HEAD_DIM), lambda bi, si: (bi, 0, 0, 0))
    return pl.pallas_call(
        functools.partial(_rwkv_scan_kernel, chunk=chunk, n_steps=n_steps),
        grid=(n_b, n_steps),
        in_specs=[blk] * 5 + [wt, blk, st, _const_spec(lng.shape), _const_spec(lnb.shape), _const_spec(rk.shape)],
        out_specs=[blk, st],
        out_shape=[jax.ShapeDtypeStruct((m, w), BF16),
                   jax.ShapeDtypeStruct((n_b, n_heads, HEAD_DIM, HEAD_DIM), F32)],
        scratch_shapes=[pltpu.VMEM((w // LANES, LANES, LANES), F32)],
        compiler_params=_params(("arbitrary", "arbitrary")),
        name="rwkv_scan",
    )(at, rt, bt, kt, vb, wtot, g, s0, lng, lnb, rk)


def _transpose01(x):
    h = x.shape[1]
    eye = jnp.where(_iota((h, h), 0) == _iota((h, h), 1), 1.0, 0.0).astype(BF16)
    hi, mid, lo = _split3(x)
    return _dg(eye, hi, NT) + _dg(eye, mid, NT) + _dg(eye, lo, NT)


def _lane_cumsum(x, block):
    tri = jnp.where(_iota((block, block), 0) <= _iota((block, block), 1), 1.0, 0.0).astype(BF16)
    hi, mid, lo = _split3(x)
    outs = []
    offset = jnp.zeros((x.shape[0], 1), F32)
    for j in range(x.shape[1] // block):
        sl = slice(j * block, (j + 1) * block)
        c = _dot(hi[:, sl], tri) + _dot(mid[:, sl], tri) + _dot(lo[:, sl], tri)
        outs.append(c + offset)
        offset = offset + c[:, block - 1:block]
    return outs[0] if len(outs) == 1 else jnp.concatenate(outs, axis=1)


def _fox_prompt_kernel(q_ref, k_ref, caug_ref, vt_ref, gate_ref, *rest, tq, tk, n_cast):
    cast_in, y_ref, cast_out = rest[:n_cast], rest[n_cast], rest[n_cast + 1:]
    hp = pl.program_id(1)
    qi = pl.program_id(2)
    nq = k_ref.shape[0] // tq
    vblk = vt_ref.shape[2]
    lane = _iota((tq, LANES), 1)
    first = lane < HEAD_DIM
    q = q_ref[...]
    zero = jnp.zeros_like(q)
    hslot = jnp.where(lane < 3 * FOX_HEADS, lax.rem(lane, FOX_HEADS), -1)
    qsa = []
    for h in range(2):
        ones = jnp.where(hslot == 2 * hp + h, 1.0, 0.0).astype(BF16)
        qh = jnp.where(first, q, zero) if h == 0 else jnp.where(first, zero, q)
        qsa.append(jnp.concatenate([qh, ones], axis=1))

    def scores(item):
        k0, nk, q_lo, _ = item
        kk = jnp.concatenate([k_ref[k0:k0 + nk, :], caug_ref[k0:k0 + nk, :]], axis=1)
        rhs = jnp.concatenate([qsa[0][q_lo:], qsa[1][q_lo:]], axis=0)
        return _dg(kk, rhs, NT)

    def update(st, item, state):
        k0, nk, q_lo, masked = item
        nqv = tq - q_lo
        vt = jnp.concatenate([vt_ref[k0 // vblk + i] for i in range(nk // vblk)], axis=1)
        out = []
        for h in range(2):
            m_all, l_all, acc_all = state[h]
            sh = st[:, h * nqv:(h + 1) * nqv]
            if masked:
                sh = jnp.where(_iota((nk, nqv), 0) <= _iota((nk, nqv), 1), sh, NEG_BIG)
            m_old = m_all[:, q_lo:]
            m_new = jnp.maximum(m_old, jnp.max(sh, axis=0, keepdims=True))
            alpha = jnp.exp2(m_old - m_new)
            p = jnp.exp2(sh - m_new)
            l_new = alpha * l_all[:, q_lo:] + jnp.sum(p, axis=0, keepdims=True)
            acc_new = alpha * acc_all[:, q_lo:] + _dot(vt[h * HEAD_DIM:(h + 1) * HEAD_DIM], p.astype(BF16))
            if q_lo:
                m_new = jnp.concatenate([m_all[:, :q_lo], m_new], axis=1)
                l_new = jnp.concatenate([l_all[:, :q_lo], l_new], axis=1)
                acc_new = jnp.concatenate([acc_all[:, :q_lo], acc_new], axis=1)
            out.append((m_new, l_new, acc_new))
        return out

    def program(c):
        items = [(kb * tk, tk, 0, False) for kb in range(c * tq // tk)]
        items += [(c * tq + d * tk, tk, d * tk, True) for d in range(tq // tk)]
        state = [(jnp.full((1, tq), NEG_BIG, F32), jnp.zeros((1, tq), F32), jnp.zeros((HEAD_DIM, tq), F32))
                 for _ in range(2)]
        st_next = scores(items[0])
        for n, item in enumerate(items):
            st = st_next
            if n + 1 < len(items):
                st_next = scores(items[n + 1])
            state = update(st, item, state)
        yts = []
        for _, l_fin, acc in state:
            o = acc * (1.0 / l_fin)
            ms = jnp.mean(o * o, axis=0, keepdims=True)
            yts.append(o * lax.rsqrt(ms + RMS_EPS))
        yt = jnp.concatenate(yts, axis=0)
        y_ref[...] = (yt.T * gate_ref[...]).astype(y_ref.dtype)
        for src, dst in zip(cast_in, cast_out):
            dst[...] = src[...].astype(BF16)

    for c in range(nq):
        pl.when(qi == c)(functools.partial(program, c))


def _fox_prompt_attn(qb, kb, caug, vbt, gate, *, n_b, t_seq, tq, tk, cast_cols=(), cast_rows=()):
    m, w = qb.shape
    n_pairs = w // LANES
    nq = t_seq // tq
    n_steps = n_b * n_pairs * nq
    vblk = vbt.shape[2]
    assert vbt.shape == (m // vblk, w, vblk) and tk % vblk == 0 and tq % tk == 0
    qspec = pl.BlockSpec((tq, LANES), lambda b, hp, qi: (b * nq + qi, hp))
    step = lambda b, hp, qi: (b * n_pairs + hp) * nq + qi
    casts = ([_cast_job(a, (a.shape[0], a.shape[1] // n_steps), lambda b, hp, qi: (0, step(b, hp, qi)))
              for a in cast_cols]
             + [_cast_job(a, (a.shape[0] // n_steps, a.shape[1]), lambda b, hp, qi: (step(b, hp, qi), 0))
                for a in cast_rows])
    outs = pl.pallas_call(
        functools.partial(_fox_prompt_kernel, tq=tq, tk=tk, n_cast=len(casts)),
        grid=(n_b, n_pairs, nq),
        in_specs=[qspec,
                  pl.BlockSpec((t_seq, LANES), lambda b, hp, qi: (b, hp)),
                  pl.BlockSpec((t_seq, LANES), lambda b, hp, qi: (b, 0)),
                  pl.BlockSpec((t_seq // vblk, LANES, vblk), lambda b, hp, qi: (b, hp, 0)),
                  qspec] + [spec for _, spec, _ in casts],
        out_specs=[qspec] + [spec for _, spec, _ in casts],
        out_shape=[jax.ShapeDtypeStruct((m, w), BF16)] + [sds for _, _, sds in casts],
        compiler_params=_params(("arbitrary", "arbitrary", "arbitrary")),
        name="fox_prompt_attn",
    )(qb, kb, caug, vbt, gate, *[a for a, _, _ in casts])
    return outs[0], outs[1:]


def _fox_sample_kernel(q_ref, kn_ref, vn_ref, ck_ref, cv_ref, clf_ref, lf_ref, gate_ref, y_ref):
    t = q_ref.shape[0]
    past = clf_ref.shape[2]
    tril = jnp.where(_iota((t, t), 0) >= _iota((t, t), 1), 1.0, 0.0).astype(BF16)
    clf = clf_ref[0, 0]
    c_tot = jnp.sum(clf, axis=0, keepdims=True)
    cn_col = _dot01(tril, lf_ref[...])
    cn_row = _transpose01(cn_col)
    cq_col = cn_col + c_tot
    cc_row = _lane_cumsum(_transpose01(clf), 2 * LANES)
    def head_major(ref):
        chunks = [jnp.swapaxes(ref[0, 0, c * LANES:(c + 1) * LANES], 0, 1).astype(BF16) for c in range(past // LANES)]
        return jnp.concatenate(chunks, axis=1)

    k_heads = head_major(ck_ref)
    v_heads = head_major(cv_ref)
    lane = _iota((t, LANES), 1)
    first = lane < HEAD_DIM
    causal = _iota((t, t), 1) <= _iota((t, t), 0)
    ys = []
    for p in range(q_ref.shape[1] // LANES):
        sl = slice(p * LANES, (p + 1) * LANES)
        q = q_ref[:, sl]
        zero = jnp.zeros_like(q)
        qs = jnp.concatenate([jnp.where(first, q, zero), jnp.where(first, zero, q)], axis=0)
        kc = jnp.concatenate([k_heads[2 * p], k_heads[2 * p + 1]], axis=1)
        vc = jnp.concatenate([v_heads[2 * p], v_heads[2 * p + 1]], axis=1)
        kn = kn_ref[:, sl]
        vn = vn_ref[:, sl]
        s_c = _dg(qs, kc, NT)
        s_n = _dg(qs, kn, NT)
        o = []
        for h in range(2):
            hd = 2 * p + h
            sc = s_c[h * t:(h + 1) * t] + (cq_col[:, hd:hd + 1] - cc_row[hd:hd + 1]) * LOG2E
            sn = s_n[h * t:(h + 1) * t] + (cn_col[:, hd:hd + 1] - cn_row[hd:hd + 1]) * LOG2E
            sn = jnp.where(causal, sn, NEG_BIG)
            mx = jnp.maximum(jnp.max(sc, axis=-1, keepdims=True), jnp.max(sn, axis=-1, keepdims=True))
            pc = jnp.exp2(sc - mx)
            pn = jnp.exp2(sn - mx)
            den = jnp.sum(pc, axis=-1, keepdims=True) + jnp.sum(pn, axis=-1, keepdims=True)
            o.append((_dot(pc.astype(BF16), vc) + _dot(pn.astype(BF16), vn)) / den)
        om = jnp.where(first, o[0], o[1])
        sq = om * om
        ms0 = jnp.sum(jnp.where(first, sq, 0.0), axis=-1, keepdims=True)
        ms1 = jnp.sum(jnp.where(first, 0.0, sq), axis=-1, keepdims=True)
        ms = jnp.where(first, ms0, ms1) * (1.0 / HEAD_DIM)
        ys.append(om * lax.rsqrt(ms + RMS_EPS) * gate_ref[:, sl])
    y_ref[...] = jnp.concatenate(ys, axis=1).astype(y_ref.dtype)


def _fox_sample_attn(qb, kb, vb, cache_k, cache_v, cache_lf, lf, gate, *, t_seq, layer):
    m, w = qb.shape
    n_b = m // t_seq
    row = pl.BlockSpec((t_seq, w), lambda b: (b, 0))
    cache = pl.BlockSpec((1, 1) + cache_k.shape[2:], lambda b: (layer, b, 0, 0, 0))
    return pl.pallas_call(
        _fox_sample_kernel,
        grid=(n_b,),
        in_specs=[row, row, row, cache, cache,
                  pl.BlockSpec((1, 1) + cache_lf.shape[2:], lambda b: (layer, b, 0, 0)),
                  pl.BlockSpec((t_seq, lf.shape[1]), lambda b: (b, 0)), row],
        out_specs=row,
        out_shape=jax.ShapeDtypeStruct((m, w), BF16),
        compiler_params=_params(("arbitrary",)),
        name="fox_sample_attn",
    )(qb, kb, vb, cache_k, cache_v, cache_lf, lf, gate)


def _out_ln_kernel(yr_ref, yf_ref, x_ref, wo_ref, g_ref, b_ref, h_ref, wob_ref, *, alpha):
    @pl.when(pl.program_id(0) == 0)
    def _():
        wob_ref[...] = wo_ref[...].astype(BF16)

    half = yr_ref.shape[1]
    mix = _dot(yr_ref[...], wob_ref[0:half, :]) + _dot(yf_ref[...], wob_ref[half:, :])
    h_ref[...] = _layer_norm(alpha * x_ref[...] + mix, g_ref[...], b_ref[...])


def _out_ln(yr, yf, x, wo, g, b, *, tm, alpha):
    m, d = x.shape
    row = lambda i: (i, 0)
    return pl.pallas_call(
        functools.partial(_out_ln_kernel, alpha=alpha),
        grid=(m // tm,),
        in_specs=[pl.BlockSpec((tm, yr.shape[1]), row), pl.BlockSpec((tm, yf.shape[1]), row),
                  pl.BlockSpec((tm, d), row), _const_spec(wo.shape), _const_spec(g.shape), _const_spec(b.shape)],
        out_specs=pl.BlockSpec((tm, d), row),
        out_shape=jax.ShapeDtypeStruct((m, d), F32),
        scratch_shapes=[pltpu.VMEM(wo.shape, BF16)],
        compiler_params=_params(("arbitrary",)),
        name="out_ln",
    )(yr, yf, x, wo, g, b)


def _ffn_ln_kernel(h_ref, wu_ref, wd_ref, g_ref, b_ref, k2_ref, v2_ref, o_ref, k5_ref, v5_ref, hb_ref,
                   *, alpha, n_f):
    j = pl.program_id(1)

    @pl.when(j == 0)
    def _():
        h = h_ref[...]
        hb_ref[...] = h.astype(BF16)
        o_ref[...] = alpha * h

    u = jnp.maximum(_dot(hb_ref[...], wu_ref[...]), 0.0)
    o_ref[...] += _dot((u * u).astype(BF16), wd_ref[...])
    for src, dst in ((k2_ref, k5_ref), (v2_ref, v5_ref)):
        x = src[...]
        heads = jnp.stack([x[:, hd * HEAD_DIM:(hd + 1) * HEAD_DIM] for hd in range(dst.shape[1])], axis=0)
        dst[...] = jnp.swapaxes(heads, 0, 1)

    @pl.when(j == n_f - 1)
    def _():
        o_ref[...] = _layer_norm(o_ref[...], g_ref[...], b_ref[...])


def _ffn_ln(h, wu, wd, g, b, k2, v2, *, tm, tf, alpha):
    m, d = h.shape
    n_f = wu.shape[1] // tf
    rows = tm // n_f
    heads = k2.shape[1] // HEAD_DIM
    slab_in = pl.BlockSpec((rows, k2.shape[1]), lambda i, j: (i * n_f + j, 0))
    slab_out = pl.BlockSpec((rows, heads, HEAD_DIM), lambda i, j: (i * n_f + j, 0, 0))
    kv5 = jax.ShapeDtypeStruct((m, heads, HEAD_DIM), F32)
    return pl.pallas_call(
        functools.partial(_ffn_ln_kernel, alpha=alpha, n_f=n_f),
        grid=(m // tm, n_f),
        in_specs=[pl.BlockSpec((tm, d), lambda i, j: (i, 0), pipeline_mode=pl.Buffered(1)),
                  pl.BlockSpec((d, tf), lambda i, j: (0, j)),
                  pl.BlockSpec((tf, d), lambda i, j: (j, 0)),
                  _const_spec(g.shape), _const_spec(b.shape), slab_in, slab_in],
        out_specs=[pl.BlockSpec((tm, d), lambda i, j: (i, 0)), slab_out, slab_out],
        out_shape=[jax.ShapeDtypeStruct((m, d), F32), kv5, kv5],
        scratch_shapes=[pltpu.VMEM((tm, d), BF16)],
        compiler_params=_params(("arbitrary", "arbitrary")),
        name="ffn_ln",
    )(h, wu, wd, g, b, k2, v2)


def _pad_cols(x, n):
    return jnp.pad(x, [(0, 0)] * (x.ndim - 1) + [(0, n - x.shape[-1])])


def _stream(x, shift_prev, s0, wts, *, t_seq, tm, tm_ffn, chunk, cache=None):
    n_b, _, d = x.shape
    m = n_b * t_seq
    x2 = x.reshape(m, d)
    qb, k32, v32, kb, vb, vbt, gate, lf, caug = _fox_proj(x2, wts["w_f"], wts["bf_row"], wts["og_g"],
                                                          tm=tm, t_seq=t_seq)
    fprev = _pad_cols(shift_prev, RP_PAD)
    at, rt, bt, kt, vr, g, wtot, sh = _rwkv_proj(
        x2, wts["w_r"], wts["mu"], fprev, wts["w0"], wts["w2p"], wts["a0"], wts["a2p"], wts["g2p"], wts["k_k"],
        wts["k_a"], tm=tm, t_seq=t_seq, chunk=chunk)
    y_r, s_new = _rwkv_scan(at, rt, bt, kt, vr, wtot, g, s0, wts["lnx_g"], wts["lnx_b"], wts["r_k"],
                            chunk=chunk, rows=min(SCAN_CHUNKS * chunk, t_seq), t_seq=t_seq)
    if cache is None:
        y_f, (wts["w_up_b"], wts["w_down_b"]) = _fox_prompt_attn(
            qb, kb, caug, vbt, gate, n_b=n_b, t_seq=t_seq, tq=TQ_ATTN, tk=TK_ATTN,
            cast_cols=(wts["w_up"],), cast_rows=(wts["w_down"],))
    else:
        ck, cv, clf, layer = cache
        y_f = _fox_sample_attn(qb, kb, vb, ck, cv, clf, lf, gate, t_seq=t_seq, layer=layer)
    h = _out_ln(y_r, y_f, x2, wts["w_o"], wts["ln1_g"], wts["ln1_b"], tm=tm, alpha=wts["alpha"])
    y, k5, v5 = _ffn_ln(h, wts["w_up_b"], wts["w_down_b"], wts["ln2_g"], wts["ln2_b"], k32, v32,
                        tm=tm_ffn, tf=TF_FFN, alpha=wts["alpha"])
    heads = FW // HEAD_DIM
    return (y.reshape(n_b, t_seq, d), k5.reshape(n_b, t_seq, heads, HEAD_DIM),
            v5.reshape(n_b, t_seq, heads, HEAD_DIM), lf.reshape(n_b, t_seq, heads), s_new,
            sh[..., :shift_prev.shape[-1]])


def kernel(x_prompt, x_sample, cache_fox_k, cache_fox_v, cache_fox_logf, state_rwkv_wkv, state_rwkv_shift,
           w_in, rwkv_mu, rwkv_w0, rwkv_w2, rwkv_a0, rwkv_a2, rwkv_g2, rwkv_k_k, rwkv_k_a, rwkv_r_k,
           rwkv_lnx_g, rwkv_lnx_b, fox_b_f, fox_out_g, w_o, ln1_g, ln1_b, w_up, w_down, ln2_g, ln2_b):
    depth = w_in.shape[0]
    assert depth == 1, "single-layer problem"
    rwkv_proj = rwkv_mu.shape[-1]
    lora = (rwkv_w2.shape[1], rwkv_a2.shape[1], rwkv_g2.shape[1])
    assert rwkv_w0.shape[-1] == RW and fox_out_g.shape[-1] == FW and rwkv_proj == 3 * RW + sum(lora)
    assert lora[0] + lora[1] == LORA_WA and lora[2] <= LORA_G and RP_PAD <= w_in.shape[-1]
    alpha = (2 * depth) ** 0.25
    l = 0
    w = w_in[l]
    fo = rwkv_proj
    row = lambda z: z.reshape(1, -1)
    pad_rows = lambda z, n: jnp.pad(z, ((0, n - z.shape[0]), (0, 0)))
    wts = dict(
        alpha=alpha,
        w_r=w[:, :RP_PAD].astype(BF16),
        w_f=jnp.concatenate([w[:, fo:fo + 3 * FW], w[:, fo + 3 * FW + FOX_HEADS:],
                             _pad_cols(jnp.tile(w[:, fo + 3 * FW:fo + 3 * FW + FOX_HEADS], (1, 3)), LANES)],
                            axis=-1).astype(BF16),
        mu=_pad_cols(row(rwkv_mu[l]), RP_PAD),
        w0=row(rwkv_w0[l]), w2p=jnp.pad(rwkv_w2[l], ((0, lora[1]), (0, 0))).astype(BF16),
        a0=row(rwkv_a0[l]), a2p=jnp.pad(rwkv_a2[l], ((lora[0], 0), (0, 0))).astype(BF16),
        g2p=pad_rows(rwkv_g2[l], LORA_G).astype(BF16),
        k_k=row(rwkv_k_k[l]), k_a=row(rwkv_k_a[l]), r_k=row(rwkv_r_k[l]),
        lnx_g=row(rwkv_lnx_g[l]), lnx_b=row(rwkv_lnx_b[l]),
        bf_row=_pad_cols(jnp.tile(row(fox_b_f[l]), (1, 3)), LANES), og_g=row(fox_out_g[l]),
        w_o=w_o[l], ln1_g=row(ln1_g[l]), ln1_b=row(ln1_b[l]),
        w_up=w_up[l], w_down=w_down[l], ln2_g=row(ln2_g[l]), ln2_b=row(ln2_b[l]),
    )
    n_p, t_p, _ = x_prompt.shape
    n_s, t_s, _ = x_sample.shape
    heads = RW // HEAD_DIM
    shift0 = jnp.zeros((n_p, 1, rwkv_proj), F32)
    s_zero = jnp.zeros((n_p, heads, HEAD_DIM, HEAD_DIM), F32)
    yp, kp, vp, fp, sp, shp = _stream(x_prompt, shift0, s_zero, wts, t_seq=t_p, tm=TM_PROJ, tm_ffn=TM_FFN, chunk=CHUNK)
    ys, ks, vs, fs, ss, shs = _stream(x_sample, state_rwkv_shift[l], state_rwkv_wkv[l], wts, t_seq=t_s,
                                      tm=n_s * t_s, tm_ffn=n_s * t_s, chunk=t_s,
                                      cache=(cache_fox_k, cache_fox_v, cache_fox_logf, l))
    return (yp, ys, kp[None], vp[None], fp[None], sp[None], shp[None],
            ks[None], vs[None], fs[None], ss[None], shs[None])
```

```python
import functools
import math

import jax
import jax.numpy as jnp
from jax import lax
from jax.experimental import pallas as pl
from jax.experimental.pallas import tpu as pltpu

F32 = jnp.float32
BF16 = jnp.bfloat16

HEAD_DIM = 64
LANES = 128
LN_EPS = 1e-5
GN_EPS = 64e-5
RMS_EPS = 1e-6
ATTN_SCALE = HEAD_DIM ** -0.5
EXP_NEG_HALF = math.exp(-0.5)
LOG2E = math.log2(math.e)
NEG_BIG = -1e30
VMEM_LIMIT = 60 * 1024 * 1024

TM_PROJ = 256
TM_FFN = 1024
TF_FFN = 1024
TQ_ATTN = 1024
TK_ATTN = 512
CHUNK = 64
SCAN_CHUNKS = 4

NT = (((1,), (1,)), ((), ()))
TN = (((0,), (0,)), ((), ()))


def _sigmoid(x):
    return 1.0 / (1.0 + jnp.exp(-x))


def _log_sigmoid(x):
    return jnp.minimum(x, 0.0) - jnp.log1p(jnp.exp(-jnp.abs(x)))


def _dot(a, b):
    return jnp.dot(a, b, preferred_element_type=F32)


def _dg(a, b, dims):
    return lax.dot_general(a, b, dims, preferred_element_type=F32)


def _split3(x):
    hi = x.astype(BF16)
    rem = x - hi.astype(F32)
    mid = rem.astype(BF16)
    return hi, mid, (rem - mid.astype(F32)).astype(BF16)


def _dot01(m01, x):
    hi, mid, lo = _split3(x)
    return _dot(m01, hi) + _dot(m01, mid) + _dot(m01, lo)


def _iota(shape, axis):
    return lax.broadcasted_iota(jnp.int32, shape, axis)


def _head_sum(x):
    r = _iota((LANES, LANES), 0) // HEAD_DIM
    c = _iota((LANES, LANES), 1) // HEAD_DIM
    ones_blk = jnp.where(r == c, 1.0, 0.0).astype(BF16)
    outs = []
    for g in range(x.shape[1] // LANES):
        xs = x[:, g * LANES:(g + 1) * LANES]
        hi = xs.astype(BF16)
        lo = (xs - hi.astype(F32)).astype(BF16)
        outs.append(_dot(hi, ones_blk) + _dot(lo, ones_blk))
    return outs[0] if len(outs) == 1 else jnp.concatenate(outs, axis=1)


def _layer_norm(z, g, b):
    mu = jnp.mean(z, axis=-1, keepdims=True)
    zc = z - mu
    var = jnp.mean(zc * zc, axis=-1, keepdims=True)
    return zc * lax.rsqrt(var + LN_EPS) * g + b


def _const_spec(shape):
    nd = len(shape)
    return pl.BlockSpec(shape, lambda *_: (0,) * nd, pipeline_mode=pl.Buffered(1))


def _params(sem):
    return pltpu.CompilerParams(dimension_semantics=sem, vmem_limit_bytes=VMEM_LIMIT)


RW = 1024
LORA_WA = LANES
LORA_G = 2 * LANES
RP_PAD = 3 * RW + LORA_WA + LORA_G


def _rwkv_proj_kernel(x_ref, w_ref, mu_ref, fp_ref, w0_ref, w2_ref, a0_ref, a2_ref, g2_ref, kk_ref, ka_ref,
                      at_ref, rt_ref, bt_ref, kt_ref, vb_ref, g_ref, wtot_ref, sh_ref, carry_ref,
                      *, tm, t_seq, chunk):
    i = pl.program_id(0)
    xb = x_ref[...].astype(BF16)
    rows = _iota((tm, 1), 0)

    def proj(c0, n):
        return _dot(xb, w_ref[:, c0:c0 + n])

    def shift(p, c0):
        n = p.shape[1]
        prev = pltpu.roll(p, 1, 0)
        if t_seq >= tm:
            tiles = t_seq // tm
            pos = lax.rem(i, tiles)
            sidx = lax.div(i, tiles)
            row0 = jnp.where(pos == 0, fp_ref[sidx, :, c0:c0 + n], carry_ref[:, c0:c0 + n])
            prev = jnp.where(rows == 0, row0, prev)
            carry_ref[:, c0:c0 + n] = p[tm - 1:tm, :]

            @pl.when(pos == tiles - 1)
            def _():
                sh_ref[sidx, :, c0:c0 + n] = p[tm - 1:tm, :]
        else:
            per_tile = tm // t_seq
            for j in range(per_tile):
                prev = jnp.where(rows == j * t_seq, fp_ref[i * per_tile + j, :, c0:c0 + n], prev)
                sh_ref[i * per_tile + j, :, c0:c0 + n] = p[(j + 1) * t_seq - 1:(j + 1) * t_seq, :]
        return p + (prev - p) * mu_ref[:, c0:c0 + n]

    p_wa = proj(3 * RW, LORA_WA)
    p_g = proj(3 * RW + LORA_WA, LORA_G)
    xwa = shift(p_wa, 3 * RW)
    xg = shift(p_g, 3 * RW + LORA_WA)
    wl = w0_ref[...] + _dot(jnp.tanh(xwa).astype(BF16), w2_ref[...])
    alr = _sigmoid(a0_ref[...] + _dot(xwa.astype(BF16), a2_ref[...]))
    g_ref[...] = _dot(_sigmoid(xg).astype(BF16), g2_ref[...]).astype(BF16)
    p_k = proj(RW, RW)
    lw = -EXP_NEG_HALF * _sigmoid(wl)
    rr = _iota((tm, tm), 0)
    cc = _iota((tm, tm), 1)
    same_chunk_tri = jnp.where(rr // chunk == cc // chunk, jnp.where(rr >= cc, 1.0, 0.0), 0.0).astype(BF16)
    lwc = _dot01(same_chunk_tri, lw)
    k = shift(p_k, RW)
    kk = k * kk_ref[...]
    kk_ss = _head_sum(kk * kk)
    p_r = proj(0, RW)
    p_v = proj(2 * RW, RW)
    kkn = kk / jnp.maximum(jnp.sqrt(kk_ss), 1e-12)
    kh = k * (1.0 + (alr - 1.0) * ka_ref[...])
    e_in = jnp.exp(lwc)
    e_out = jnp.exp(-lwc)
    at_ref[...] = (-kkn * jnp.exp(lwc - lw)).astype(BF16)
    bt_ref[...] = (kkn * alr * e_out).astype(BF16)
    kt_ref[...] = (kh * e_out).astype(BF16)
    for c in range(tm // chunk):
        wtot_ref[c] = e_in[(c + 1) * chunk - 1:(c + 1) * chunk, :]
    r = shift(p_r, 0)
    rt_ref[...] = (r * e_in).astype(BF16)
    vb_ref[...] = shift(p_v, 2 * RW).astype(BF16)


def _rwkv_proj(x, w, mu, fprev, w0, w2p, a0, a2p, g2p, k_k, k_a, *, tm, t_seq, chunk):
    m, d = x.shape
    n_seq = m // t_seq
    row = lambda i: (i, 0)
    big = lambda dt: jax.ShapeDtypeStruct((m, RW), dt)
    return pl.pallas_call(
        functools.partial(_rwkv_proj_kernel, tm=tm, t_seq=t_seq, chunk=chunk),
        grid=(m // tm,),
        in_specs=[pl.BlockSpec((tm, d), row),
                  _const_spec(w.shape), _const_spec(mu.shape), _const_spec(fprev.shape),
                  _const_spec(w0.shape), _const_spec(w2p.shape), _const_spec(a0.shape), _const_spec(a2p.shape),
                  _const_spec(g2p.shape), _const_spec(k_k.shape), _const_spec(k_a.shape)],
        out_specs=[pl.BlockSpec((tm, RW), row)] * 6
                  + [pl.BlockSpec((tm // chunk, 1, RW), lambda i: (i, 0, 0)),
                     pl.BlockSpec((n_seq, 1, RP_PAD), lambda i: (0, 0, 0))],
        out_shape=[big(BF16)] * 6 + [jax.ShapeDtypeStruct((m // chunk, 1, RW), F32),
                                     jax.ShapeDtypeStruct((n_seq, 1, RP_PAD), F32)],
        scratch_shapes=[pltpu.VMEM((1, RP_PAD), F32)],
        compiler_params=_params(("arbitrary",)),
        name="rwkv_proj",
    )(x, w, mu, fprev, w0, w2p, a0, a2p, g2p, k_k, k_a)


FW = 1024
FOX_HEADS = FW // HEAD_DIM


def _fox_proj_kernel(x_ref, w_ref, bf_ref, og_ref,
                     qb_ref, kt_ref, vt_ref, kb_ref, vb_ref, vbt_ref, gate_ref, lf_ref, caug_ref, carry_ref,
                     *, tm, t_seq):
    i = pl.program_id(0)
    xb = x_ref[...].astype(BF16)

    def store_t(dst_ref, zt):
        if t_seq >= tm:
            dst_ref[0] = zt.reshape(FOX_HEADS, HEAD_DIM, tm)
        else:
            for j in range(tm // t_seq):
                dst_ref[j] = zt[:, j * t_seq:(j + 1) * t_seq].reshape(FOX_HEADS, HEAD_DIM, t_seq)

    qb_ref[...] = (_dot(xb, w_ref[:, 0:FW]) * (ATTN_SCALE * LOG2E)).astype(BF16)
    k = _dot(xb, w_ref[:, FW:2 * FW])
    store_t(kt_ref, k.T)
    kb_ref[...] = k.astype(BF16)
    v = _dot(xb, w_ref[:, 2 * FW:3 * FW])
    vt = v.T
    store_t(vt_ref, vt)
    vb_ref[...] = v.astype(BF16)
    vbt_ref[0] = vt.astype(BF16)
    og = _dot(xb, w_ref[:, 3 * FW:4 * FW])
    gate_ref[...] = _sigmoid(og) * og_ref[...]
    logf = _log_sigmoid(_dot(xb, w_ref[:, 4 * FW:4 * FW + LANES]) + bf_ref[...])
    lf_ref[...] = logf[:, :FOX_HEADS]
    r = _iota((tm, tm), 0)
    c = _iota((tm, tm), 1)
    if t_seq >= tm:
        cs = _dot01(jnp.where(r >= c, 1.0, 0.0).astype(BF16), logf)
        cs = cs + jnp.where(lax.rem(i, t_seq // tm) == 0, 0.0, carry_ref[...])
        carry_ref[...] = cs[tm - 1:tm, :]
    else:
        same_seq = (r // t_seq) == (c // t_seq)
        cs = _dot01(jnp.where(same_seq, jnp.where(r >= c, 1.0, 0.0), 0.0).astype(BF16), logf)
    xs = -LOG2E * cs
    hi, mid, lo = _split3(xs)
    lane = _iota((tm, LANES), 1)
    zero = jnp.zeros_like(hi)
    caug_ref[...] = jnp.where(lane < FOX_HEADS, hi,
                              jnp.where(lane < 2 * FOX_HEADS, mid, jnp.where(lane < 3 * FOX_HEADS, lo, zero)))


def _cast_job(a, block, index_map, out_shape=None):
    spec = pl.BlockSpec(block, index_map)
    return a, spec, jax.ShapeDtypeStruct(out_shape or a.shape, BF16)


def _fox_proj(x, w, bf_row, og_g, *, tm, t_seq):
    m, d = x.shape
    row = lambda i: (i, 0)
    big = lambda dt: jax.ShapeDtypeStruct((m, FW), dt)
    cache_t = jax.ShapeDtypeStruct((m // t_seq, FOX_HEADS, HEAD_DIM, t_seq), F32)
    if t_seq >= tm:
        tiles = t_seq // tm
        cache_spec = pl.BlockSpec((1, FOX_HEADS, HEAD_DIM, tm), lambda i: (i // tiles, 0, 0, i % tiles))
    else:
        cache_spec = pl.BlockSpec((tm // t_seq, FOX_HEADS, HEAD_DIM, t_seq), lambda i: (i, 0, 0, 0))
    return pl.pallas_call(
        functools.partial(_fox_proj_kernel, tm=tm, t_seq=t_seq),
        grid=(m // tm,),
        in_specs=[pl.BlockSpec((tm, d), row), _const_spec(w.shape), _const_spec(bf_row.shape),
                  _const_spec(og_g.shape)],
        out_specs=[pl.BlockSpec((tm, FW), row), cache_spec, cache_spec, pl.BlockSpec((tm, FW), row),
                   pl.BlockSpec((tm, FW), row),
                   pl.BlockSpec((1, FW, tm), lambda i: (i, 0, 0)), pl.BlockSpec((tm, FW), row),
                   pl.BlockSpec((tm, FOX_HEADS), row), pl.BlockSpec((tm, LANES), row)],
        out_shape=[big(BF16), cache_t, cache_t, big(BF16), big(BF16),
                   jax.ShapeDtypeStruct((m // tm, FW, tm), BF16), big(F32),
                   jax.ShapeDtypeStruct((m, FOX_HEADS), F32), jax.ShapeDtypeStruct((m, LANES), BF16)],
        scratch_shapes=[pltpu.VMEM((1, LANES), F32)],
        compiler_params=_params(("arbitrary",)),
        name="fox_proj",
    )(x, w, bf_row, og_g)


def _stack_heads(x, mask0):
    return jnp.concatenate([jnp.where(mask0, x, 0.0), jnp.where(mask0, 0.0, x)], axis=0)


def _rwkv_scan_kernel(at_ref, rt_ref, bt_ref, kt_ref, vb_ref, wtot_ref, g_ref, s0_ref, lng_ref, lnb_ref, rk_ref,
                      y_ref, sout_ref, state_ref, *, chunk, n_steps):
    step = pl.program_id(1)
    n_pairs = at_ref.shape[1] // LANES
    C = chunk
    n_ch = at_ref.shape[0] // C
    pairs = range(n_pairs)

    @pl.when(step == 0)
    def _():
        zero = jnp.zeros((HEAD_DIM, HEAD_DIM), F32)
        for p in pairs:
            top = jnp.concatenate([s0_ref[0, 2 * p], zero], axis=1)
            bot = jnp.concatenate([zero, s0_ref[0, 2 * p + 1]], axis=1)
            state_ref[p] = jnp.concatenate([top, bot], axis=0)

    fmask0 = _iota((1, LANES), 1) < HEAD_DIM
    tcol = _iota((C, 2 * C), 1)
    trow = _iota((C, 2 * C), 0)
    tmask0 = tcol < C
    tj = jnp.where(tmask0, tcol, tcol - C)
    strict = tj < trow
    incl = tj <= trow
    eye_pair = jnp.where(tj == trow, 1.0, 0.0)
    blk = (_iota((LANES, LANES), 0) // HEAD_DIM) == (_iota((LANES, LANES), 1) // HEAD_DIM)

    def stack_t(x):
        return _stack_heads(x, tmask0)

    def tile(ref, c, p):
        return ref[c * C:(c + 1) * C, p * LANES:(p + 1) * LANES]

    probs = [(c, p) for c in range(n_ch) for p in pairs]
    at = {cp: tile(at_ref, *cp) for cp in probs}
    rt = {cp: tile(rt_ref, *cp) for cp in probs}
    bt = {cp: tile(bt_ref, *cp) for cp in probs}
    kt = {cp: tile(kt_ref, *cp) for cp in probs}
    vb = {cp: tile(vb_ref, *cp) for cp in probs}
    gm = {cp: _dg(jnp.concatenate([at[cp], rt[cp]], axis=0),
                  jnp.concatenate([_stack_heads(bt[cp], fmask0), _stack_heads(kt[cp], fmask0)], axis=0), NT)
          for cp in probs}
    lab = {cp: jnp.where(strict, gm[cp][:C, :2 * C], 0.0) for cp in probs}
    lak_b = {cp: jnp.where(strict, gm[cp][:C, 2 * C:], 0.0).astype(BF16) for cp in probs}
    mr_b = {cp: jnp.concatenate([jnp.where(incl, gm[cp][C:, :2 * C], 0.0),
                                 jnp.where(incl, gm[cp][C:, 2 * C:], 0.0)], axis=1).astype(BF16) for cp in probs}
    kmax = int(math.log2(C)) - 1
    tinv = {cp: eye_pair + lab[cp] for cp in probs}
    pw = {cp: lab[cp].astype(BF16) for cp in probs}
    pw = {cp: _dot(pw[cp], stack_t(pw[cp])).astype(BF16) for cp in probs}
    for _ in range(1, kmax):
        res = {cp: _dot(jnp.concatenate([pw[cp], tinv[cp].astype(BF16)], axis=0), stack_t(pw[cp])) for cp in probs}
        pw = {cp: res[cp][:C].astype(BF16) for cp in probs}
        tinv = {cp: tinv[cp] + res[cp][C:] for cp in probs}
    tinv_b = {cp: (tinv[cp] + _dot(tinv[cp].astype(BF16), stack_t(pw[cp]))).astype(BF16) for cp in probs}

    sd = [state_ref[p] for p in pairs]
    ys = []
    for c in range(n_ch):
        sd_b = [z.astype(BF16) for z in sd]
        vd_b = [_stack_heads(vb[c, p], fmask0) for p in pairs]
        x = [_dg(at[c, p], sd_b[p], NT) + _dot(lak_b[c, p], vd_b[p]) for p in pairs]
        u_b = [_dot(tinv_b[c, p], _stack_heads(x[p].astype(BF16), fmask0)).astype(BF16) for p in pairs]
        ys.append(jnp.concatenate(
            [_dg(rt[c, p], sd_b[p], NT)
             + _dot(mr_b[c, p], jnp.concatenate([_stack_heads(u_b[p], fmask0), vd_b[p]], axis=0)) for p in pairs],
            axis=1))
        w_tot = wtot_ref[c]
        for p in pairs:
            ds = _dg(jnp.concatenate([u_b[p], vb[c, p]], axis=0), jnp.concatenate([bt[c, p], kt[c, p]], axis=0), TN)
            sd[p] = (sd[p] + jnp.where(blk, ds, 0.0)) * w_tot[:, p * LANES:(p + 1) * LANES]
    for p in pairs:
        state_ref[p] = sd[p]

    y = ys[0] if n_ch == 1 else jnp.concatenate(ys, axis=0)
    inv_n = 1.0 / HEAD_DIM
    mu = _head_sum(y) * inv_n
    yc = y - mu
    var = _head_sum(yc * yc) * inv_n
    yn = yc * lax.rsqrt(var + GN_EPS) * lng_ref[...] + lnb_ref[...]
    rk = rt_ref[...].astype(F32) * kt_ref[...].astype(F32) * rk_ref[...]
    bonus = _head_sum(rk) * vb_ref[...].astype(F32)
    y_ref[...] = ((yn + bonus) * g_ref[...].astype(F32)).astype(y_ref.dtype)

    @pl.when(step == n_steps - 1)
    def _():
        for p in pairs:
            sout_ref[0, 2 * p] = sd[p][:HEAD_DIM, :HEAD_DIM]
            sout_ref[0, 2 * p + 1] = sd[p][HEAD_DIM:, HEAD_DIM:]


def _rwkv_scan(at, rt, bt, kt, vb, wtot, g, s0, lng, lnb, rk, *, chunk, rows, t_seq):
    m, w = at.shape
    n_b = m // t_seq
    n_steps = t_seq // rows
    n_heads = w // HEAD_DIM
    blk = pl.BlockSpec((rows, w), lambda bi, si: (bi * n_steps + si, 0))
    wt = pl.BlockSpec((rows // chunk, 1, w), lambda bi, si: (bi * n_steps + si, 0, 0))
    st = pl.BlockSpec((1, n_heads, HEAD_DIM, HEAD_DIM), lambda bi, si: (bi, 0, 0, 0))
    return pl.pallas_call(
        functools.partial(_rwkv_scan_kernel, chunk=chunk, n_steps=n_steps),
        grid=(n_b, n_steps),
        in_specs=[blk] * 5 + [wt, blk, st, _const_spec(lng.shape), _const_spec(lnb.shape), _const_spec(rk.shape)],
        out_specs=[blk, st],
        out_shape=[jax.ShapeDtypeStruct((m, w), BF16),
                   jax.ShapeDtypeStruct((n_b, n_heads, HEAD_DIM, HEAD_DIM), F32)],
        scratch_shapes=[pltpu.VMEM((w // LANES, LANES, LANES), F32)],
        compiler_params=_params(("arbitrary", "arbitrary")),
        name="rwkv_scan",
    )(at, rt, bt, kt, vb, wtot, g, s0, lng, lnb, rk)


def _transpose01(x):
    h = x.shape[1]
    eye = jnp.where(_iota((h, h), 0) == _iota((h, h), 1), 1.0, 0.0).astype(BF16)
    hi, mid, lo = _split3(x)
    return _dg(eye, hi, NT) + _dg(eye, mid, NT) + _dg(eye, lo, NT)


def _lane_cumsum(x, block):
    tri = jnp.where(_iota((block, block), 0) <= _iota((block, block), 1), 1.0, 0.0).astype(BF16)
    hi, mid, lo = _split3(x)
    outs = []
    offset = jnp.zeros((x.shape[0], 1), F32)
    for j in range(x.shape[1] // block):
        sl = slice(j * block, (j + 1) * block)
        c = _dot(hi[:, sl], tri) + _dot(mid[:, sl], tri) + _dot(lo[:, sl], tri)
        outs.append(c + offset)
        offset = offset + c[:, block - 1:block]
    return outs[0] if len(outs) == 1 else jnp.concatenate(outs, axis=1)


def _fox_prompt_kernel(q_ref, k_ref, caug_ref, vt_ref, gate_ref, *rest, tq, tk, n_cast):
    cast_in, y_ref, cast_out = rest[:n_cast], rest[n_cast], rest[n_cast + 1:]
    hp = pl.program_id(1)
    qi = pl.program_id(2)
    nq = k_ref.shape[0] // tq
    vblk = vt_ref.shape[2]
    lane = _iota((tq, LANES), 1)
    first = lane < HEAD_DIM
    q = q_ref[...]
    zero = jnp.zeros_like(q)
    hslot = jnp.where(lane < 3 * FOX_HEADS, lax.rem(lane, FOX_HEADS), -1)
    qsa = []
    for h in range(2):
        ones = jnp.where(hslot == 2 * hp + h, 1.0, 0.0).astype(BF16)
        qh = jnp.where(first, q, zero) if h == 0 else jnp.where(first, zero, q)
        qsa.append(jnp.concatenate([qh, ones], axis=1))

    def scores(item):
        k0, nk, q_lo, _ = item
        kk = jnp.concatenate([k_ref[k0:k0 + nk, :], caug_ref[k0:k0 + nk, :]], axis=1)
        rhs = jnp.concatenate([qsa[0][q_lo:], qsa[1][q_lo:]], axis=0)
        return _dg(kk, rhs, NT)

    def update(st, item, state):
        k0, nk, q_lo, masked = item
        nqv = tq - q_lo
        vt = jnp.concatenate([vt_ref[k0 // vblk + i] for i in range(nk // vblk)], axis=1)
        out = []
        for h in range(2):
            m_all, l_all, acc_all = state[h]
            sh = st[:, h * nqv:(h + 1) * nqv]
            if masked:
                sh = jnp.where(_iota((nk, nqv), 0) <= _iota((nk, nqv), 1), sh, NEG_BIG)
            m_old = m_all[:, q_lo:]
            m_new = jnp.maximum(m_old, jnp.max(sh, axis=0, keepdims=True))
            alpha = jnp.exp2(m_old - m_new)
            p = jnp.exp2(sh - m_new)
            l_new = alpha * l_all[:, q_lo:] + jnp.sum(p, axis=0, keepdims=True)
            acc_new = alpha * acc_all[:, q_lo:] + _dot(vt[h * HEAD_DIM:(h + 1) * HEAD_DIM], p.astype(BF16))
            if q_lo:
                m_new = jnp.concatenate([m_all[:, :q_lo], m_new], axis=1)
                l_new = jnp.concatenate([l_all[:, :q_lo], l_new], axis=1)
                acc_new = jnp.concatenate([acc_all[:, :q_lo], acc_new], axis=1)
            out.append((m_new, l_new, acc_new))
        return out

    def program(c):
        items = [(kb * tk, tk, 0, False) for kb in range(c * tq // tk)]
        items += [(c * tq + d * tk, tk, d * tk, True) for d in range(tq // tk)]
        state = [(jnp.full((1, tq), NEG_BIG, F32), jnp.zeros((1, tq), F32), jnp.zeros((HEAD_DIM, tq), F32))
                 for _ in range(2)]
        st_next = scores(items[0])
        for n, item in enumerate(items):
            st = st_next
            if n + 1 < len(items):
                st_next = scores(items[n + 1])
            state = update(st, item, state)
        yts = []
        for _, l_fin, acc in state:
            o = acc * (1.0 / l_fin)
            ms = jnp.mean(o * o, axis=0, keepdims=True)
            yts.append(o * lax.rsqrt(ms + RMS_EPS))
        yt = jnp.concatenate(yts, axis=0)
        y_ref[...] = (yt.T * gate_ref[...]).astype(y_ref.dtype)
        for src, dst in zip(cast_in, cast_out):
            dst[...] = src[...].astype(BF16)

    for c in range(nq):
        pl.when(qi == c)(functools.partial(program, c))


def _fox_prompt_attn(qb, kb, caug, vbt, gate, *, n_b, t_seq, tq, tk, cast_cols=(), cast_rows=()):
    m, w = qb.shape
    n_pairs = w // LANES
    nq = t_seq // tq
    n_steps = n_b * n_pairs * nq
    vblk = vbt.shape[2]
    assert vbt.shape == (m // vblk, w, vblk) and tk % vblk == 0 and tq % tk == 0
    qspec = pl.BlockSpec((tq, LANES), lambda b, hp, qi: (b * nq + qi, hp))
    step = lambda b, hp, qi: (b * n_pairs + hp) * nq + qi
    casts = ([_cast_job(a, (a.shape[0], a.shape[1] // n_steps), lambda b, hp, qi: (0, step(b, hp, qi)))
              for a in cast_cols]
             + [_cast_job(a, (a.shape[0] // n_steps, a.shape[1]), lambda b, hp, qi: (step(b, hp, qi), 0))
                for a in cast_rows])
    outs = pl.pallas_call(
        functools.partial(_fox_prompt_kernel, tq=tq, tk=tk, n_cast=len(casts)),
        grid=(n_b, n_pairs, nq),
        in_specs=[qspec,
                  pl.BlockSpec((t_seq, LANES), lambda b, hp, qi: (b, hp)),
                  pl.BlockSpec((t_seq, LANES), lambda b, hp, qi: (b, 0)),
                  pl.BlockSpec((t_seq // vblk, LANES, vblk), lambda b, hp, qi: (b, hp, 0)),
                  qspec] + [spec for _, spec, _ in casts],
        out_specs=[qspec] + [spec for _, spec, _ in casts],
        out_shape=[jax.ShapeDtypeStruct((m, w), BF16)] + [sds for _, _, sds in casts],
        compiler_params=_params(("arbitrary", "arbitrary", "arbitrary")),
        name="fox_prompt_attn",
    )(qb, kb, caug, vbt, gate, *[a for a, _, _ in casts])
    return outs[0], outs[1:]


def _fox_sample_kernel(q_ref, kn_ref, vn_ref, ckt_ref, cvt_ref, clf_ref, lf_ref, gate_ref, y_ref):
    t = q_ref.shape[0]
    tril = jnp.where(_iota((t, t), 0) >= _iota((t, t), 1), 1.0, 0.0).astype(BF16)
    clf_row = clf_ref[0, 0]
    c_tot = jnp.sum(clf_row, axis=-1, keepdims=True)
    cn_col = _dot01(tril, lf_ref[...])
    cn_row = _transpose01(cn_col)
    cc_row = _lane_cumsum(clf_row, 2 * LANES)
    lane = _iota((t, LANES), 1)
    first = lane < HEAD_DIM
    causal = _iota((t, t), 1) <= _iota((t, t), 0)
    ys = []
    for p in range(q_ref.shape[1] // LANES):
        sl = slice(p * LANES, (p + 1) * LANES)
        q = q_ref[:, sl]
        zero = jnp.zeros_like(q)
        qs = jnp.concatenate([jnp.where(first, q, zero), jnp.where(first, zero, q)], axis=0)
        kct = jnp.concatenate([ckt_ref[0, 0, 2 * p], ckt_ref[0, 0, 2 * p + 1]], axis=0).astype(BF16)
        vct = jnp.concatenate([cvt_ref[0, 0, 2 * p], cvt_ref[0, 0, 2 * p + 1]], axis=0).astype(BF16)
        kn = kn_ref[:, sl]
        vn = vn_ref[:, sl]
        s_c = _dot(qs, kct)
        s_n = _dg(qs, kn, NT)
        o = []
        for h in range(2):
            hd = 2 * p + h
            cq = cn_col[:, hd:hd + 1] + c_tot[hd:hd + 1]
            sc = s_c[h * t:(h + 1) * t] + (cq - cc_row[hd:hd + 1]) * LOG2E
            sn = s_n[h * t:(h + 1) * t] + (cn_col[:, hd:hd + 1] - cn_row[hd:hd + 1]) * LOG2E
            sn = jnp.where(causal, sn, NEG_BIG)
            mx = jnp.maximum(jnp.max(sc, axis=-1, keepdims=True), jnp.max(sn, axis=-1, keepdims=True))
            pc = jnp.exp2(sc - mx)
            pn = jnp.exp2(sn - mx)
            den = jnp.sum(pc, axis=-1, keepdims=True) + jnp.sum(pn, axis=-1, keepdims=True)
            o.append((_dg(pc.astype(BF16), vct, NT) + _dot(pn.astype(BF16), vn)) / den)
        om = jnp.where(first, o[0], o[1])
        sq = om * om
        ms0 = jnp.sum(jnp.where(first, sq, 0.0), axis=-1, keepdims=True)
        ms1 = jnp.sum(jnp.where(first, 0.0, sq), axis=-1, keepdims=True)
        ms = jnp.where(first, ms0, ms1) * (1.0 / HEAD_DIM)
        ys.append(om * lax.rsqrt(ms + RMS_EPS) * gate_ref[:, sl])
    y_ref[...] = jnp.concatenate(ys, axis=1).astype(y_ref.dtype)


def _fox_sample_attn(qb, kb, vb, cache_kt, cache_vt, cache_lft, lf, gate, *, t_seq, layer):
    m, w = qb.shape
    n_b = m // t_seq
    row = pl.BlockSpec((t_seq, w), lambda b: (b, 0))
    cache = pl.BlockSpec((1, 1) + cache_kt.shape[2:], lambda b: (layer, b, 0, 0, 0))
    return pl.pallas_call(
        _fox_sample_kernel,
        grid=(n_b,),
        in_specs=[row, row, row, cache, cache,
                  pl.BlockSpec((1, 1) + cache_lft.shape[2:], lambda b: (layer, b, 0, 0)),
                  pl.BlockSpec((t_seq, lf.shape[1]), lambda b: (b, 0)), row],
        out_specs=row,
        out_shape=jax.ShapeDtypeStruct((m, w), BF16),
        compiler_params=_params(("arbitrary",)),
        name="fox_sample_attn",
    )(qb, kb, vb, cache_kt, cache_vt, cache_lft, lf, gate)


def _out_ln_kernel(yr_ref, yf_ref, x_ref, wo_ref, g_ref, b_ref, h_ref, wob_ref, *, alpha):
    @pl.when(pl.program_id(0) == 0)
    def _():
        wob_ref[...] = wo_ref[...].astype(BF16)

    half = yr_ref.shape[1]
    mix = _dot(yr_ref[...], wob_ref[0:half, :]) + _dot(yf_ref[...], wob_ref[half:, :])
    h_ref[...] = _layer_norm(alpha * x_ref[...] + mix, g_ref[...], b_ref[...])


def _out_ln(yr, yf, x, wo, g, b, *, tm, alpha):
    m, d = x.shape
    row = lambda i: (i, 0)
    return pl.pallas_call(
        functools.partial(_out_ln_kernel, alpha=alpha),
        grid=(m // tm,),
        in_specs=[pl.BlockSpec((tm, yr.shape[1]), row), pl.BlockSpec((tm, yf.shape[1]), row),
                  pl.BlockSpec((tm, d), row), _const_spec(wo.shape), _const_spec(g.shape), _const_spec(b.shape)],
        out_specs=pl.BlockSpec((tm, d), row),
        out_shape=jax.ShapeDtypeStruct((m, d), F32),
        scratch_shapes=[pltpu.VMEM(wo.shape, BF16)],
        compiler_params=_params(("arbitrary",)),
        name="out_ln",
    )(yr, yf, x, wo, g, b)


def _ffn_ln_kernel(h_ref, wu_ref, wd_ref, g_ref, b_ref, o_ref, hb_ref, *, alpha, n_f):
    j = pl.program_id(1)

    @pl.when(j == 0)
    def _():
        h = h_ref[...]
        hb_ref[...] = h.astype(BF16)
        o_ref[...] = alpha * h

    u = jnp.maximum(_dot(hb_ref[...], wu_ref[...]), 0.0)
    o_ref[...] += _dot((u * u).astype(BF16), wd_ref[...])

    @pl.when(j == n_f - 1)
    def _():
        o_ref[...] = _layer_norm(o_ref[...], g_ref[...], b_ref[...])


def _ffn_ln(h, wu, wd, g, b, *, tm, tf, alpha):
    m, d = h.shape
    n_f = wu.shape[1] // tf
    return pl.pallas_call(
        functools.partial(_ffn_ln_kernel, alpha=alpha, n_f=n_f),
        grid=(m // tm, n_f),
        in_specs=[pl.BlockSpec((tm, d), lambda i, j: (i, 0), pipeline_mode=pl.Buffered(1)),
                  pl.BlockSpec((d, tf), lambda i, j: (0, j)),
                  pl.BlockSpec((tf, d), lambda i, j: (j, 0)),
                  _const_spec(g.shape), _const_spec(b.shape)],
        out_specs=pl.BlockSpec((tm, d), lambda i, j: (i, 0)),
        out_shape=jax.ShapeDtypeStruct((m, d), F32),
        scratch_shapes=[pltpu.VMEM((tm, d), BF16)],
        compiler_params=_params(("arbitrary", "arbitrary")),
        name="ffn_ln",
    )(h, wu, wd, g, b)


def _pad_cols(x, n):
    return jnp.pad(x, [(0, 0)] * (x.ndim - 1) + [(0, n - x.shape[-1])])


def _stream(x, shift_prev, s0, wts, *, t_seq, tm, tm_ffn, chunk, cache=None):
    n_b, _, d = x.shape
    m = n_b * t_seq
    x2 = x.reshape(m, d)
    qb, k_t, v_t, kb, vb, vbt, gate, lf, caug = _fox_proj(x2, wts["w_f"], wts["bf_row"], wts["og_g"],
                                                          tm=tm, t_seq=t_seq)
    fprev = _pad_cols(shift_prev, RP_PAD)
    at, rt, bt, kt, vr, g, wtot, sh = _rwkv_proj(
        x2, wts["w_r"], wts["mu"], fprev, wts["w0"], wts["w2p"], wts["a0"], wts["a2p"], wts["g2p"], wts["k_k"],
        wts["k_a"], tm=tm, t_seq=t_seq, chunk=chunk)
    y_r, s_new = _rwkv_scan(at, rt, bt, kt, vr, wtot, g, s0, wts["lnx_g"], wts["lnx_b"], wts["r_k"],
                            chunk=chunk, rows=min(SCAN_CHUNKS * chunk, t_seq), t_seq=t_seq)
    if cache is None:
        y_f, (wts["w_up_b"], wts["w_down_b"]) = _fox_prompt_attn(
            qb, kb, caug, vbt, gate, n_b=n_b, t_seq=t_seq, tq=TQ_ATTN, tk=TK_ATTN,
            cast_cols=(wts["w_up"],), cast_rows=(wts["w_down"],))
    else:
        ck, cv, clf, layer = cache
        y_f = _fox_sample_attn(qb, kb, vb, jnp.transpose(ck, (0, 1, 3, 4, 2)), jnp.transpose(cv, (0, 1, 3, 4, 2)),
                               jnp.transpose(clf, (0, 1, 3, 2)), lf, gate, t_seq=t_seq, layer=layer)
    h = _out_ln(y_r, y_f, x2, wts["w_o"], wts["ln1_g"], wts["ln1_b"], tm=tm, alpha=wts["alpha"])
    y = _ffn_ln(h, wts["w_up_b"], wts["w_down_b"], wts["ln2_g"], wts["ln2_b"], tm=tm_ffn, tf=TF_FFN, alpha=wts["alpha"])
    heads = FW // HEAD_DIM
    to_cache = lambda z: jnp.transpose(z, (0, 3, 1, 2))
    return (y.reshape(n_b, t_seq, d), to_cache(k_t), to_cache(v_t), lf.reshape(n_b, t_seq, heads), s_new,
            sh[..., :shift_prev.shape[-1]])


def kernel(x_prompt, x_sample, cache_fox_k, cache_fox_v, cache_fox_logf, state_rwkv_wkv, state_rwkv_shift,
           w_in, rwkv_mu, rwkv_w0, rwkv_w2, rwkv_a0, rwkv_a2, rwkv_g2, rwkv_k_k, rwkv_k_a, rwkv_r_k,
           rwkv_lnx_g, rwkv_lnx_b, fox_b_f, fox_out_g, w_o, ln1_g, ln1_b, w_up, w_down, ln2_g, ln2_b):
    depth = w_in.shape[0]
    assert depth == 1, "single-layer problem"
    rwkv_proj = rwkv_mu.shape[-1]
    lora = (rwkv_w2.shape[1], rwkv_a2.shape[1], rwkv_g2.shape[1])
    assert rwkv_w0.shape[-1] == RW and fox_out_g.shape[-1] == FW and rwkv_proj == 3 * RW + sum(lora)
    assert lora[0] + lora[1] == LORA_WA and lora[2] <= LORA_G and RP_PAD <= w_in.shape[-1]
    alpha = (2 * depth) ** 0.25
    l = 0
    w = w_in[l]
    fo = rwkv_proj
    row = lambda z: z.reshape(1, -1)
    pad_rows = lambda z, n: jnp.pad(z, ((0, n - z.shape[0]), (0, 0)))
    wts = dict(
        alpha=alpha,
        w_r=w[:, :RP_PAD].astype(BF16),
        w_f=jnp.concatenate([w[:, fo:fo + 3 * FW], w[:, fo + 3 * FW + FOX_HEADS:],
                             _pad_cols(jnp.tile(w[:, fo + 3 * FW:fo + 3 * FW + FOX_HEADS], (1, 3)), LANES)],
                            axis=-1).astype(BF16),
        mu=_pad_cols(row(rwkv_mu[l]), RP_PAD),
        w0=row(rwkv_w0[l]), w2p=jnp.pad(rwkv_w2[l], ((0, lora[1]), (0, 0))).astype(BF16),
        a0=row(rwkv_a0[l]), a2p=jnp.pad(rwkv_a2[l], ((lora[0], 0), (0, 0))).astype(BF16),
        g2p=pad_rows(rwkv_g2[l], LORA_G).astype(BF16),
        k_k=row(rwkv_k_k[l]), k_a=row(rwkv_k_a[l]), r_k=row(rwkv_r_k[l]),
        lnx_g=row(rwkv_lnx_g[l]), lnx_b=row(rwkv_lnx_b[l]),
        bf_row=_pad_cols(jnp.tile(row(fox_b_f[l]), (1, 3)), LANES), og_g=row(fox_out_g[l]),
        w_o=w_o[l], ln1_g=row(ln1_g[l]), ln1_b=row(ln1_b[l]),
        w_up=w_up[l], w_down=w_down[l], ln2_g=row(ln2_g[l]), ln2_b=row(ln2_b[l]),
    )
    n_p, t_p, _ = x_prompt.shape
    n_s, t_s, _ = x_sample.shape
    heads = RW // HEAD_DIM
    shift0 = jnp.zeros((n_p, 1, rwkv_proj), F32)
    s_zero = jnp.zeros((n_p, heads, HEAD_DIM, HEAD_DIM), F32)
    yp, kp, vp, fp, sp, shp = _stream(x_prompt, shift0, s_zero, wts, t_seq=t_p, tm=TM_PROJ, tm_ffn=TM_FFN, chunk=CHUNK)
    ys, ks, vs, fs, ss, shs = _stream(x_sample, state_rwkv_shift[l], state_rwkv_wkv[l], wts, t_seq=t_s,
                                      tm=n_s * t_s, tm_ffn=n_s * t_s, chunk=t_s,
                                      cache=(cache_fox_k, cache_fox_v, cache_fox_logf, l))
    return (yp, ys, kp[None], vp[None], fp[None], sp[None], shp[None],
            ks[None], vs[None], fs[None], ss[None], shs[None])
```

```python
import functools
import math

import jax
import jax.numpy as jnp
from jax import lax
from jax.experimental import pallas as pl
from jax.experimental.pallas import tpu as pltpu

F32 = jnp.float32
BF16 = jnp.bfloat16

HEAD_DIM = 64
LANES = 128
LN_EPS = 1e-5
GN_EPS = 64e-5
RMS_EPS = 1e-6
ATTN_SCALE = HEAD_DIM ** -0.5
EXP_NEG_HALF = math.exp(-0.5)
LOG2E = math.log2(math.e)
NEG_BIG = -1e30
VMEM_LIMIT = 60 * 1024 * 1024

TM_PROJ = 256
TM_OUT = 512
TM_FFN = 1024
TF_FFN = 1024
TQ_ATTN = 1024
TK_ATTN = 512
CHUNK = 64
SCAN_CHUNKS = 4

NT = (((1,), (1,)), ((), ()))
TN = (((0,), (0,)), ((), ()))


def _sigmoid(x):
    return 1.0 / (1.0 + jnp.exp(-x))


def _log_sigmoid(x):
    return jnp.minimum(x, 0.0) - jnp.log1p(jnp.exp(-jnp.abs(x)))


def _dot(a, b):
    return jnp.dot(a, b, preferred_element_type=F32)


def _dg(a, b, dims):
    return lax.dot_general(a, b, dims, preferred_element_type=F32)


def _split3(x):
    hi = x.astype(BF16)
    rem = x - hi.astype(F32)
    mid = rem.astype(BF16)
    return hi, mid, (rem - mid.astype(F32)).astype(BF16)


def _dot01(m01, x):
    hi, mid, lo = _split3(x)
    return _dot(m01, hi) + _dot(m01, mid) + _dot(m01, lo)


def _iota(shape, axis):
    return lax.broadcasted_iota(jnp.int32, shape, axis)


def _head_sum(x):
    r = _iota((LANES, LANES), 0) // HEAD_DIM
    c = _iota((LANES, LANES), 1) // HEAD_DIM
    ones_blk = jnp.where(r == c, 1.0, 0.0).astype(BF16)
    outs = []
    for g in range(x.shape[1] // LANES):
        xs = x[:, g * LANES:(g + 1) * LANES]
        hi = xs.astype(BF16)
        lo = (xs - hi.astype(F32)).astype(BF16)
        outs.append(_dot(hi, ones_blk) + _dot(lo, ones_blk))
    return outs[0] if len(outs) == 1 else jnp.concatenate(outs, axis=1)


def _layer_norm(z, g, b):
    mu = jnp.mean(z, axis=-1, keepdims=True)
    zc = z - mu
    var = jnp.mean(zc * zc, axis=-1, keepdims=True)
    return zc * lax.rsqrt(var + LN_EPS) * g + b


def _const_spec(shape):
    nd = len(shape)
    return pl.BlockSpec(shape, lambda *_: (0,) * nd, pipeline_mode=pl.Buffered(1))


def _params(sem):
    return pltpu.CompilerParams(dimension_semantics=sem, vmem_limit_bytes=VMEM_LIMIT)


RW = 1024
LORA_WA = LANES
LORA_G = 2 * LANES
RP_PAD = 3 * RW + LORA_WA + LORA_G


def _rwkv_proj_kernel(x_ref, w_ref, mu_ref, fp_ref, w0_ref, w2_ref, a0_ref, a2_ref, g2_ref, kk_ref, ka_ref,
                      at_ref, rt_ref, bt_ref, kt_ref, vb_ref, g_ref, wtot_ref, sh_ref, carry_ref,
                      *, tm, t_seq, chunk):
    i = pl.program_id(0)
    xb = x_ref[...].astype(BF16)
    rows = _iota((tm, 1), 0)

    def proj(c0, n):
        return _dot(xb, w_ref[:, c0:c0 + n])

    def shift(p, c0):
        n = p.shape[1]
        prev = pltpu.roll(p, 1, 0)
        if t_seq >= tm:
            tiles = t_seq // tm
            pos = lax.rem(i, tiles)
            sidx = lax.div(i, tiles)
            row0 = jnp.where(pos == 0, fp_ref[sidx, :, c0:c0 + n], carry_ref[:, c0:c0 + n])
            prev = jnp.where(rows == 0, row0, prev)
            carry_ref[:, c0:c0 + n] = p[tm - 1:tm, :]

            @pl.when(pos == tiles - 1)
            def _():
                sh_ref[sidx, :, c0:c0 + n] = p[tm - 1:tm, :]
        else:
            per_tile = tm // t_seq
            for j in range(per_tile):
                prev = jnp.where(rows == j * t_seq, fp_ref[i * per_tile + j, :, c0:c0 + n], prev)
                sh_ref[i * per_tile + j, :, c0:c0 + n] = p[(j + 1) * t_seq - 1:(j + 1) * t_seq, :]
        return p + (prev - p) * mu_ref[:, c0:c0 + n]

    p_wa = proj(3 * RW, LORA_WA)
    p_g = proj(3 * RW + LORA_WA, LORA_G)
    xwa = shift(p_wa, 3 * RW)
    xg = shift(p_g, 3 * RW + LORA_WA)
    wl = w0_ref[...] + _dot(jnp.tanh(xwa).astype(BF16), w2_ref[...])
    alr = _sigmoid(a0_ref[...] + _dot(xwa.astype(BF16), a2_ref[...]))
    g_ref[...] = _dot(_sigmoid(xg).astype(BF16), g2_ref[...]).astype(BF16)
    p_k = proj(RW, RW)
    lw = -EXP_NEG_HALF * _sigmoid(wl)
    rr = _iota((tm, tm), 0)
    cc = _iota((tm, tm), 1)
    same_chunk_tri = jnp.where(rr // chunk == cc // chunk, jnp.where(rr >= cc, 1.0, 0.0), 0.0).astype(BF16)
    lwc = _dot01(same_chunk_tri, lw)
    k = shift(p_k, RW)
    kk = k * kk_ref[...]
    kk_ss = _head_sum(kk * kk)
    p_r = proj(0, RW)
    p_v = proj(2 * RW, RW)
    kkn = kk / jnp.maximum(jnp.sqrt(kk_ss), 1e-12)
    kh = k * (1.0 + (alr - 1.0) * ka_ref[...])
    e_in = jnp.exp(lwc)
    e_out = jnp.exp(-lwc)
    at_ref[...] = (-kkn * jnp.exp(lwc - lw)).astype(BF16)
    bt_ref[...] = (kkn * alr * e_out).astype(BF16)
    kt_ref[...] = (kh * e_out).astype(BF16)
    for c in range(tm // chunk):
        wtot_ref[c] = e_in[(c + 1) * chunk - 1:(c + 1) * chunk, :]
    r = shift(p_r, 0)
    rt_ref[...] = (r * e_in).astype(BF16)
    vb_ref[...] = shift(p_v, 2 * RW).astype(BF16)


def _rwkv_proj(x, w, mu, fprev, w0, w2p, a0, a2p, g2p, k_k, k_a, *, tm, t_seq, chunk):
    m, d = x.shape
    n_seq = m // t_seq
    row = lambda i: (i, 0)
    big = lambda dt: jax.ShapeDtypeStruct((m, RW), dt)
    return pl.pallas_call(
        functools.partial(_rwkv_proj_kernel, tm=tm, t_seq=t_seq, chunk=chunk),
        grid=(m // tm,),
        in_specs=[pl.BlockSpec((tm, d), row),
                  _const_spec(w.shape), _const_spec(mu.shape), _const_spec(fprev.shape),
                  _const_spec(w0.shape), _const_spec(w2p.shape), _const_spec(a0.shape), _const_spec(a2p.shape),
                  _const_spec(g2p.shape), _const_spec(k_k.shape), _const_spec(k_a.shape)],
        out_specs=[pl.BlockSpec((tm, RW), row)] * 6
                  + [pl.BlockSpec((tm // chunk, 1, RW), lambda i: (i, 0, 0)),
                     pl.BlockSpec((n_seq, 1, RP_PAD), lambda i: (0, 0, 0))],
        out_shape=[big(BF16)] * 6 + [jax.ShapeDtypeStruct((m // chunk, 1, RW), F32),
                                     jax.ShapeDtypeStruct((n_seq, 1, RP_PAD), F32)],
        scratch_shapes=[pltpu.VMEM((1, RP_PAD), F32)],
        compiler_params=_params(("arbitrary",)),
        name="rwkv_proj",
    )(x, w, mu, fprev, w0, w2p, a0, a2p, g2p, k_k, k_a)


FW = 1024
FOX_HEADS = FW // HEAD_DIM


def _fox_proj_kernel(x_ref, w_ref, bf_ref, og_ref,
                     qb_ref, kt_ref, vt_ref, kb_ref, vb_ref, vbt_ref, gate_ref, lf_ref, caug_ref, carry_ref,
                     *, tm, t_seq):
    i = pl.program_id(0)
    xb = x_ref[...].astype(BF16)

    def store_t(dst_ref, zt):
        if t_seq >= tm:
            dst_ref[0] = zt.reshape(FOX_HEADS, HEAD_DIM, tm)
        else:
            for j in range(tm // t_seq):
                dst_ref[j] = zt[:, j * t_seq:(j + 1) * t_seq].reshape(FOX_HEADS, HEAD_DIM, t_seq)

    qb_ref[...] = (_dot(xb, w_ref[:, 0:FW]) * (ATTN_SCALE * LOG2E)).astype(BF16)
    k = _dot(xb, w_ref[:, FW:2 * FW])
    store_t(kt_ref, k.T)
    kb_ref[...] = k.astype(BF16)
    v = _dot(xb, w_ref[:, 2 * FW:3 * FW])
    vt = v.T
    store_t(vt_ref, vt)
    vb_ref[...] = v.astype(BF16)
    vbt_ref[0] = vt.astype(BF16)
    og = _dot(xb, w_ref[:, 3 * FW:4 * FW])
    gate_ref[...] = _sigmoid(og) * og_ref[...]
    logf = _log_sigmoid(_dot(xb, w_ref[:, 4 * FW:4 * FW + LANES]) + bf_ref[...])
    lf_ref[...] = logf[:, :FOX_HEADS]
    r = _iota((tm, tm), 0)
    c = _iota((tm, tm), 1)
    if t_seq >= tm:
        cs = _dot01(jnp.where(r >= c, 1.0, 0.0).astype(BF16), logf)
        cs = cs + jnp.where(lax.rem(i, t_seq // tm) == 0, 0.0, carry_ref[...])
        carry_ref[...] = cs[tm - 1:tm, :]
    else:
        same_seq = (r // t_seq) == (c // t_seq)
        cs = _dot01(jnp.where(same_seq, jnp.where(r >= c, 1.0, 0.0), 0.0).astype(BF16), logf)
    xs = -LOG2E * cs
    hi, mid, lo = _split3(xs)
    lane = _iota((tm, LANES), 1)
    zero = jnp.zeros_like(hi)
    caug_ref[...] = jnp.where(lane < FOX_HEADS, hi,
                              jnp.where(lane < 2 * FOX_HEADS, mid, jnp.where(lane < 3 * FOX_HEADS, lo, zero)))


def _cast_job(a, block, index_map, out_shape=None):
    spec = pl.BlockSpec(block, index_map)
    return a, spec, jax.ShapeDtypeStruct(out_shape or a.shape, BF16)


def _fox_proj(x, w, bf_row, og_g, *, tm, t_seq):
    m, d = x.shape
    row = lambda i: (i, 0)
    big = lambda dt: jax.ShapeDtypeStruct((m, FW), dt)
    cache_t = jax.ShapeDtypeStruct((m // t_seq, FOX_HEADS, HEAD_DIM, t_seq), F32)
    if t_seq >= tm:
        tiles = t_seq // tm
        cache_spec = pl.BlockSpec((1, FOX_HEADS, HEAD_DIM, tm), lambda i: (i // tiles, 0, 0, i % tiles))
    else:
        cache_spec = pl.BlockSpec((tm // t_seq, FOX_HEADS, HEAD_DIM, t_seq), lambda i: (i, 0, 0, 0))
    return pl.pallas_call(
        functools.partial(_fox_proj_kernel, tm=tm, t_seq=t_seq),
        grid=(m // tm,),
        in_specs=[pl.BlockSpec((tm, d), row), _const_spec(w.shape), _const_spec(bf_row.shape),
                  _const_spec(og_g.shape)],
        out_specs=[pl.BlockSpec((tm, FW), row), cache_spec, cache_spec, pl.BlockSpec((tm, FW), row),
                   pl.BlockSpec((tm, FW), row),
                   pl.BlockSpec((1, FW, tm), lambda i: (i, 0, 0)), pl.BlockSpec((tm, FW), row),
                   pl.BlockSpec((tm, FOX_HEADS), row), pl.BlockSpec((tm, LANES), row)],
        out_shape=[big(BF16), cache_t, cache_t, big(BF16), big(BF16),
                   jax.ShapeDtypeStruct((m // tm, FW, tm), BF16), big(F32),
                   jax.ShapeDtypeStruct((m, FOX_HEADS), F32), jax.ShapeDtypeStruct((m, LANES), BF16)],
        scratch_shapes=[pltpu.VMEM((1, LANES), F32)],
        compiler_params=_params(("arbitrary",)),
        name="fox_proj",
    )(x, w, bf_row, og_g)


def _stack_heads(x, mask0):
    return jnp.concatenate([jnp.where(mask0, x, 0.0), jnp.where(mask0, 0.0, x)], axis=0)


def _rwkv_scan_kernel(at_ref, rt_ref, bt_ref, kt_ref, vb_ref, wtot_ref, g_ref, s0_ref, lng_ref, lnb_ref, rk_ref,
                      y_ref, sout_ref, state_ref, *, chunk, n_steps):
    step = pl.program_id(1)
    n_pairs = at_ref.shape[1] // LANES
    C = chunk
    n_ch = at_ref.shape[0] // C
    pairs = range(n_pairs)

    @pl.when(step == 0)
    def _():
        zero = jnp.zeros((HEAD_DIM, HEAD_DIM), F32)
        for p in pairs:
            top = jnp.concatenate([s0_ref[0, 2 * p], zero], axis=1)
            bot = jnp.concatenate([zero, s0_ref[0, 2 * p + 1]], axis=1)
            state_ref[p] = jnp.concatenate([top, bot], axis=0)

    fmask0 = _iota((1, LANES), 1) < HEAD_DIM
    tcol = _iota((C, 2 * C), 1)
    trow = _iota((C, 2 * C), 0)
    tmask0 = tcol < C
    tj = jnp.where(tmask0, tcol, tcol - C)
    strict = tj < trow
    incl = tj <= trow
    eye_pair = jnp.where(tj == trow, 1.0, 0.0)
    blk = (_iota((LANES, LANES), 0) // HEAD_DIM) == (_iota((LANES, LANES), 1) // HEAD_DIM)

    def stack_t(x):
        return _stack_heads(x, tmask0)

    def tile(ref, c, p):
        return ref[c * C:(c + 1) * C, p * LANES:(p + 1) * LANES]

    probs = [(c, p) for c in range(n_ch) for p in pairs]
    at = {cp: tile(at_ref, *cp) for cp in probs}
    rt = {cp: tile(rt_ref, *cp) for cp in probs}
    bt = {cp: tile(bt_ref, *cp) for cp in probs}
    kt = {cp: tile(kt_ref, *cp) for cp in probs}
    vb = {cp: tile(vb_ref, *cp) for cp in probs}
    gm = {cp: _dg(jnp.concatenate([at[cp], rt[cp]], axis=0),
                  jnp.concatenate([_stack_heads(bt[cp], fmask0), _stack_heads(kt[cp], fmask0)], axis=0), NT)
          for cp in probs}
    lab = {cp: jnp.where(strict, gm[cp][:C, :2 * C], 0.0) for cp in probs}
    lak_b = {cp: jnp.where(strict, gm[cp][:C, 2 * C:], 0.0).astype(BF16) for cp in probs}
    mr_b = {cp: jnp.concatenate([jnp.where(incl, gm[cp][C:, :2 * C], 0.0),
                                 jnp.where(incl, gm[cp][C:, 2 * C:], 0.0)], axis=1).astype(BF16) for cp in probs}
    kmax = int(math.log2(C)) - 1
    tinv = {cp: eye_pair + lab[cp] for cp in probs}
    pw = {cp: lab[cp].astype(BF16) for cp in probs}
    pw = {cp: _dot(pw[cp], stack_t(pw[cp])).astype(BF16) for cp in probs}
    for _ in range(1, kmax):
        res = {cp: _dot(jnp.concatenate([pw[cp], tinv[cp].astype(BF16)], axis=0), stack_t(pw[cp])) for cp in probs}
        pw = {cp: res[cp][:C].astype(BF16) for cp in probs}
        tinv = {cp: tinv[cp] + res[cp][C:] for cp in probs}
    tinv_b = {cp: (tinv[cp] + _dot(tinv[cp].astype(BF16), stack_t(pw[cp]))).astype(BF16) for cp in probs}

    sd = [state_ref[p] for p in pairs]
    ys = []
    for c in range(n_ch):
        sd_b = [z.astype(BF16) for z in sd]
        vd_b = [_stack_heads(vb[c, p], fmask0) for p in pairs]
        x = [_dg(at[c, p], sd_b[p], NT) + _dot(lak_b[c, p], vd_b[p]) for p in pairs]
        u_b = [_dot(tinv_b[c, p], _stack_heads(x[p].astype(BF16), fmask0)).astype(BF16) for p in pairs]
        ys.append(jnp.concatenate(
            [_dg(rt[c, p], sd_b[p], NT)
             + _dot(mr_b[c, p], jnp.concatenate([_stack_heads(u_b[p], fmask0), vd_b[p]], axis=0)) for p in pairs],
            axis=1))
        w_tot = wtot_ref[c]
        for p in pairs:
            ds = _dg(jnp.concatenate([u_b[p], vb[c, p]], axis=0), jnp.concatenate([bt[c, p], kt[c, p]], axis=0), TN)
            sd[p] = (sd[p] + jnp.where(blk, ds, 0.0)) * w_tot[:, p * LANES:(p + 1) * LANES]
    for p in pairs:
        state_ref[p] = sd[p]

    y = ys[0] if n_ch == 1 else jnp.concatenate(ys, axis=0)
    inv_n = 1.0 / HEAD_DIM
    mu = _head_sum(y) * inv_n
    yc = y - mu
    var = _head_sum(yc * yc) * inv_n
    yn = yc * lax.rsqrt(var + GN_EPS) * lng_ref[...] + lnb_ref[...]
    rk = rt_ref[...].astype(F32) * kt_ref[...].astype(F32) * rk_ref[...]
    bonus = _head_sum(rk) * vb_ref[...].astype(F32)
    y_ref[...] = ((yn + bonus) * g_ref[...].astype(F32)).astype(y_ref.dtype)

    @pl.when(step == n_steps - 1)
    def _():
        for p in pairs:
            sout_ref[0, 2 * p] = sd[p][:HEAD_DIM, :HEAD_DIM]
            sout_ref[0, 2 * p + 1] = sd[p][HEAD_DIM:, HEAD_DIM:]


def _rwkv_scan(at, rt, bt, kt, vb, wtot, g, s0, lng, lnb, rk, *, chunk, rows, t_seq):
    m, w = at.shape
    n_b = m // t_seq
    n_steps = t_seq // rows
    n_heads = w // HEAD_DIM
    blk = pl.BlockSpec((rows, w), lambda bi, si: (bi * n_steps + si, 0))
    wt = pl.BlockSpec((rows // chunk, 1, w), lambda bi, si: (bi * n_steps + si, 0, 0))
    st = pl.BlockSpec((1, n_heads, HEAD_DIM, HEAD_DIM), lambda bi, si: (bi, 0, 0, 0))
    return pl.pallas_call(
        functools.partial(_rwkv_scan_kernel, chunk=chunk, n_steps=n_steps),
        grid=(n_b, n_steps),
        in_specs=[blk] * 5 + [wt, blk, st, _const_spec(lng.shape), _const_spec(lnb.shape), _const_spec(rk.shape)],
        out_specs=[blk, st],
        out_shape=[jax.ShapeDtypeStruct((m, w), BF16),
                   jax.ShapeDtypeStruct((n_b, n_heads, HEAD_DIM, HEAD_DIM), F32)],
        scratch_shapes=[pltpu.VMEM((w // LANES, LANES, LANES), F32)],
        compiler_params=_params(("arbitrary", "arbitrary")),
        name="rwkv_scan",
    )(at, rt, bt, kt, vb, wtot, g, s0, lng, lnb, rk)


def _transpose01(x):
    h = x.shape[1]
    eye = jnp.where(_iota((h, h), 0) == _iota((h, h), 1), 1.0, 0.0).astype(BF16)
    hi, mid, lo = _split3(x)
    return _dg(eye, hi, NT) + _dg(eye, mid, NT) + _dg(eye, lo, NT)


def _lane_cumsum(x, block):
    tri = jnp.where(_iota((block, block), 0) <= _iota((block, block), 1), 1.0, 0.0).astype(BF16)
    hi, mid, lo = _split3(x)
    outs = []
    offset = jnp.zeros((x.shape[0], 1), F32)
    for j in range(x.shape[1] // block):
        sl = slice(j * block, (j + 1) * block)
        c = _dot(hi[:, sl], tri) + _dot(mid[:, sl], tri) + _dot(lo[:, sl], tri)
        outs.append(c + offset)
        offset = offset + c[:, block - 1:block]
    return outs[0] if len(outs) == 1 else jnp.concatenate(outs, axis=1)


def _fox_prompt_kernel(q_ref, k_ref, caug_ref, vt_ref, gate_ref, *rest, tq, tk, n_cast):
    cast_in, y_ref, cast_out = rest[:n_cast], rest[n_cast], rest[n_cast + 1:]
    hp = pl.program_id(1)
    qi = pl.program_id(2)
    nq = k_ref.shape[0] // tq
    vblk = vt_ref.shape[2]
    lane = _iota((tq, LANES), 1)
    first = lane < HEAD_DIM
    q = q_ref[...]
    zero = jnp.zeros_like(q)
    hslot = jnp.where(lane < 3 * FOX_HEADS, lax.rem(lane, FOX_HEADS), -1)
    qsa = []
    for h in range(2):
        ones = jnp.where(hslot == 2 * hp + h, 1.0, 0.0).astype(BF16)
        qh = jnp.where(first, q, zero) if h == 0 else jnp.where(first, zero, q)
        qsa.append(jnp.concatenate([qh, ones], axis=1))
    qsa_t = [z.astype(F32).T.astype(BF16) for z in qsa]

    def scores(item):
        k0, nk, q_lo, _ = item
        kk = jnp.concatenate([k_ref[k0:k0 + nk, :], caug_ref[k0:k0 + nk, :]], axis=1)
        rhs = jnp.concatenate([qsa_t[0][:, q_lo:], qsa_t[1][:, q_lo:]], axis=1)
        return _dot(kk, rhs)

    def update(st, item, state):
        k0, nk, q_lo, masked = item
        nqv = tq - q_lo
        vt = jnp.concatenate([vt_ref[k0 // vblk + i] for i in range(nk // vblk)], axis=1)
        out = []
        for h in range(2):
            m_all, l_all, acc_all = state[h]
            sh = st[:, h * nqv:(h + 1) * nqv]
            if masked:
                sh = jnp.where(_iota((nk, nqv), 0) <= _iota((nk, nqv), 1), sh, NEG_BIG)
            m_old = m_all[:, q_lo:]
            m_new = jnp.maximum(m_old, jnp.max(sh, axis=0, keepdims=True))
            alpha = jnp.exp2(m_old - m_new)
            p = jnp.exp2(sh - m_new)
            l_new = alpha * l_all[:, q_lo:] + jnp.sum(p, axis=0, keepdims=True)
            acc_new = alpha * acc_all[:, q_lo:] + _dot(vt[h * HEAD_DIM:(h + 1) * HEAD_DIM], p.astype(BF16))
            if q_lo:
                m_new = jnp.concatenate([m_all[:, :q_lo], m_new], axis=1)
                l_new = jnp.concatenate([l_all[:, :q_lo], l_new], axis=1)
                acc_new = jnp.concatenate([acc_all[:, :q_lo], acc_new], axis=1)
            out.append((m_new, l_new, acc_new))
        return out

    def program(c):
        items = [(kb * tk, tk, 0, False) for kb in range(c * tq // tk)]
        items += [(c * tq + d * tk, tk, d * tk, True) for d in range(tq // tk)]
        state = [(jnp.full((1, tq), NEG_BIG, F32), jnp.zeros((1, tq), F32), jnp.zeros((HEAD_DIM, tq), F32))
                 for _ in range(2)]
        st_next = scores(items[0])
        for n, item in enumerate(items):
            st = st_next
            if n + 1 < len(items):
                st_next = scores(items[n + 1])
            state = update(st, item, state)
        yts = []
        for _, l_fin, acc in state:
            o = acc * (1.0 / l_fin)
            ms = jnp.mean(o * o, axis=0, keepdims=True)
            yts.append(o * lax.rsqrt(ms + RMS_EPS))
        yt = jnp.concatenate(yts, axis=0)
        y_ref[...] = (yt.T * gate_ref[...]).astype(y_ref.dtype)
        for src, dst in zip(cast_in, cast_out):
            dst[...] = src[...].astype(BF16)

    for c in range(nq):
        pl.when(qi == c)(functools.partial(program, c))


def _fox_prompt_attn(qb, kb, caug, vbt, gate, *, n_b, t_seq, tq, tk, cast_cols=(), cast_rows=()):
    m, w = qb.shape
    n_pairs = w // LANES
    nq = t_seq // tq
    n_steps = n_b * n_pairs * nq
    vblk = vbt.shape[2]
    assert vbt.shape == (m // vblk, w, vblk) and tk % vblk == 0 and tq % tk == 0
    qspec = pl.BlockSpec((tq, LANES), lambda b, hp, qi: (b * nq + qi, hp))
    step = lambda b, hp, qi: (b * n_pairs + hp) * nq + qi
    casts = ([_cast_job(a, (a.shape[0], a.shape[1] // n_steps), lambda b, hp, qi: (0, step(b, hp, qi)))
              for a in cast_cols]
             + [_cast_job(a, (a.shape[0] // n_steps, a.shape[1]), lambda b, hp, qi: (step(b, hp, qi), 0))
                for a in cast_rows])
    outs = pl.pallas_call(
        functools.partial(_fox_prompt_kernel, tq=tq, tk=tk, n_cast=len(casts)),
        grid=(n_b, n_pairs, nq),
        in_specs=[qspec,
                  pl.BlockSpec((t_seq, LANES), lambda b, hp, qi: (b, hp)),
                  pl.BlockSpec((t_seq, LANES), lambda b, hp, qi: (b, 0)),
                  pl.BlockSpec((t_seq // vblk, LANES, vblk), lambda b, hp, qi: (b, hp, 0)),
                  qspec] + [spec for _, spec, _ in casts],
        out_specs=[qspec] + [spec for _, spec, _ in casts],
        out_shape=[jax.ShapeDtypeStruct((m, w), BF16)] + [sds for _, _, sds in casts],
        compiler_params=_params(("arbitrary", "arbitrary", "arbitrary")),
        name="fox_prompt_attn",
    )(qb, kb, caug, vbt, gate, *[a for a, _, _ in casts])
    return outs[0], outs[1:]


def _fox_sample_kernel(q_ref, kn_ref, vn_ref, ckt_ref, cvt_ref, clf_ref, lf_ref, gate_ref, y_ref):
    t = q_ref.shape[0]
    tril = jnp.where(_iota((t, t), 0) >= _iota((t, t), 1), 1.0, 0.0).astype(BF16)
    clf_row = clf_ref[0, 0]
    c_tot = jnp.sum(clf_row, axis=-1, keepdims=True)
    cn_col = _dot01(tril, lf_ref[...])
    cn_row = _transpose01(cn_col)
    cc_row = _lane_cumsum(clf_row, 2 * LANES)
    lane = _iota((t, LANES), 1)
    first = lane < HEAD_DIM
    causal = _iota((t, t), 1) <= _iota((t, t), 0)
    ys = []
    for p in range(q_ref.shape[1] // LANES):
        sl = slice(p * LANES, (p + 1) * LANES)
        q = q_ref[:, sl]
        zero = jnp.zeros_like(q)
        qs = jnp.concatenate([jnp.where(first, q, zero), jnp.where(first, zero, q)], axis=0)
        kct = jnp.concatenate([ckt_ref[0, 0, 2 * p], ckt_ref[0, 0, 2 * p + 1]], axis=0).astype(BF16)
        vct = jnp.concatenate([cvt_ref[0, 0, 2 * p], cvt_ref[0, 0, 2 * p + 1]], axis=0).astype(BF16)
        kn = kn_ref[:, sl]
        vn = vn_ref[:, sl]
        s_c = _dot(qs, kct)
        s_n = _dg(qs, kn, NT)
        o = []
        for h in range(2):
            hd = 2 * p + h
            cq = cn_col[:, hd:hd + 1] + c_tot[hd:hd + 1]
            sc = s_c[h * t:(h + 1) * t] + (cq - cc_row[hd:hd + 1]) * LOG2E
            sn = s_n[h * t:(h + 1) * t] + (cn_col[:, hd:hd + 1] - cn_row[hd:hd + 1]) * LOG2E
            sn = jnp.where(causal, sn, NEG_BIG)
            mx = jnp.maximum(jnp.max(sc, axis=-1, keepdims=True), jnp.max(sn, axis=-1, keepdims=True))
            pc = jnp.exp2(sc - mx)
            pn = jnp.exp2(sn - mx)
            den = jnp.sum(pc, axis=-1, keepdims=True) + jnp.sum(pn, axis=-1, keepdims=True)
            o.append((_dg(pc.astype(BF16), vct, NT) + _dot(pn.astype(BF16), vn)) / den)
        om = jnp.where(first, o[0], o[1])
        sq = om * om
        ms0 = jnp.sum(jnp.where(first, sq, 0.0), axis=-1, keepdims=True)
        ms1 = jnp.sum(jnp.where(first, 0.0, sq), axis=-1, keepdims=True)
        ms = jnp.where(first, ms0, ms1) * (1.0 / HEAD_DIM)
        ys.append(om * lax.rsqrt(ms + RMS_EPS) * gate_ref[:, sl])
    y_ref[...] = jnp.concatenate(ys, axis=1).astype(y_ref.dtype)


def _fox_sample_attn(qb, kb, vb, cache_kt, cache_vt, cache_lft, lf, gate, *, t_seq, layer):
    m, w = qb.shape
    n_b = m // t_seq
    row = pl.BlockSpec((t_seq, w), lambda b: (b, 0))
    cache = pl.BlockSpec((1, 1) + cache_kt.shape[2:], lambda b: (layer, b, 0, 0, 0))
    return pl.pallas_call(
        _fox_sample_kernel,
        grid=(n_b,),
        in_specs=[row, row, row, cache, cache,
                  pl.BlockSpec((1, 1) + cache_lft.shape[2:], lambda b: (layer, b, 0, 0)),
                  pl.BlockSpec((t_seq, lf.shape[1]), lambda b: (b, 0)), row],
        out_specs=row,
        out_shape=jax.ShapeDtypeStruct((m, w), BF16),
        compiler_params=_params(("arbitrary",)),
        name="fox_sample_attn",
    )(qb, kb, vb, cache_kt, cache_vt, cache_lft, lf, gate)


def _out_ln_kernel(yr_ref, yf_ref, x_ref, wo_ref, g_ref, b_ref, h_ref, wob_ref, *, alpha):
    @pl.when(pl.program_id(0) == 0)
    def _():
        wob_ref[...] = wo_ref[...].astype(BF16)

    half = yr_ref.shape[1]
    mix = _dot(yr_ref[...], wob_ref[0:half, :]) + _dot(yf_ref[...], wob_ref[half:, :])
    h_ref[...] = _layer_norm(alpha * x_ref[...] + mix, g_ref[...], b_ref[...])


def _out_ln(yr, yf, x, wo, g, b, *, tm, alpha):
    m, d = x.shape
    row = lambda i: (i, 0)
    return pl.pallas_call(
        functools.partial(_out_ln_kernel, alpha=alpha),
        grid=(m // tm,),
        in_specs=[pl.BlockSpec((tm, yr.shape[1]), row), pl.BlockSpec((tm, yf.shape[1]), row),
                  pl.BlockSpec((tm, d), row), _const_spec(wo.shape), _const_spec(g.shape), _const_spec(b.shape)],
        out_specs=pl.BlockSpec((tm, d), row),
        out_shape=jax.ShapeDtypeStruct((m, d), F32),
        scratch_shapes=[pltpu.VMEM(wo.shape, BF16)],
        compiler_params=_params(("arbitrary",)),
        name="out_ln",
    )(yr, yf, x, wo, g, b)


def _ffn_ln_kernel(h_ref, wu_ref, wd_ref, g_ref, b_ref, o_ref, hb_ref, *, alpha, n_f):
    j = pl.program_id(1)

    @pl.when(j == 0)
    def _():
        h = h_ref[...]
        hb_ref[...] = h.astype(BF16)
        o_ref[...] = alpha * h

    u = jnp.maximum(_dot(hb_ref[...], wu_ref[...]), 0.0)
    o_ref[...] += _dot((u * u).astype(BF16), wd_ref[...])

    @pl.when(j == n_f - 1)
    def _():
        o_ref[...] = _layer_norm(o_ref[...], g_ref[...], b_ref[...])


def _ffn_ln(h, wu, wd, g, b, *, tm, tf, alpha):
    m, d = h.shape
    n_f = wu.shape[1] // tf
    return pl.pallas_call(
        functools.partial(_ffn_ln_kernel, alpha=alpha, n_f=n_f),
        grid=(m // tm, n_f),
        in_specs=[pl.BlockSpec((tm, d), lambda i, j: (i, 0), pipeline_mode=pl.Buffered(1)),
                  pl.BlockSpec((d, tf), lambda i, j: (0, j)),
                  pl.BlockSpec((tf, d), lambda i, j: (j, 0)),
                  _const_spec(g.shape), _const_spec(b.shape)],
        out_specs=pl.BlockSpec((tm, d), lambda i, j: (i, 0)),
        out_shape=jax.ShapeDtypeStruct((m, d), F32),
        scratch_shapes=[pltpu.VMEM((tm, d), BF16)],
        compiler_params=_params(("arbitrary", "arbitrary")),
        name="ffn_ln",
    )(h, wu, wd, g, b)


def _pad_cols(x, n):
    return jnp.pad(x, [(0, 0)] * (x.ndim - 1) + [(0, n - x.shape[-1])])


def _stream(x, shift_prev, s0, wts, *, t_seq, tm, tm_ffn, chunk, cache=None):
    n_b, _, d = x.shape
    m = n_b * t_seq
    x2 = x.reshape(m, d)
    qb, k_t, v_t, kb, vb, vbt, gate, lf, caug = _fox_proj(x2, wts["w_f"], wts["bf_row"], wts["og_g"],
                                                          tm=tm, t_seq=t_seq)
    fprev = _pad_cols(shift_prev, RP_PAD)
    at, rt, bt, kt, vr, g, wtot, sh = _rwkv_proj(
        x2, wts["w_r"], wts["mu"], fprev, wts["w0"], wts["w2p"], wts["a0"], wts["a2p"], wts["g2p"], wts["k_k"],
        wts["k_a"], tm=tm, t_seq=t_seq, chunk=chunk)
    y_r, s_new = _rwkv_scan(at, rt, bt, kt, vr, wtot, g, s0, wts["lnx_g"], wts["lnx_b"], wts["r_k"],
                            chunk=chunk, rows=min(SCAN_CHUNKS * chunk, t_seq), t_seq=t_seq)
    if cache is None:
        y_f, (wts["w_up_b"], wts["w_down_b"]) = _fox_prompt_attn(
            qb, kb, caug, vbt, gate, n_b=n_b, t_seq=t_seq, tq=TQ_ATTN, tk=TK_ATTN,
            cast_cols=(wts["w_up"],), cast_rows=(wts["w_down"],))
    else:
        ck, cv, clf, layer = cache
        y_f = _fox_sample_attn(qb, kb, vb, jnp.transpose(ck, (0, 1, 3, 4, 2)), jnp.transpose(cv, (0, 1, 3, 4, 2)),
                               jnp.transpose(clf, (0, 1, 3, 2)), lf, gate, t_seq=t_seq, layer=layer)
    h = _out_ln(y_r, y_f, x2, wts["w_o"], wts["ln1_g"], wts["ln1_b"], tm=min(TM_OUT, m), alpha=wts["alpha"])
    y = _ffn_ln(h, wts["w_up_b"], wts["w_down_b"], wts["ln2_g"], wts["ln2_b"], tm=tm_ffn, tf=TF_FFN, alpha=wts["alpha"])
    heads = FW // HEAD_DIM
    to_cache = lambda z: jnp.transpose(z, (0, 3, 1, 2))
    return (y.reshape(n_b, t_seq, d), to_cache(k_t), to_cache(v_t), lf.reshape(n_b, t_seq, heads), s_new,
            sh[..., :shift_prev.shape[-1]])


def kernel(x_prompt, x_sample, cache_fox_k, cache_fox_v, cache_fox_logf, state_rwkv_wkv, state_rwkv_shift,
           w_in, rwkv_mu, rwkv_w0, rwkv_w2, rwkv_a0, rwkv_a2, rwkv_g2, rwkv_k_k, rwkv_k_a, rwkv_r_k,
           rwkv_lnx_g, rwkv_lnx_b, fox_b_f, fox_out_g, w_o, ln1_g, ln1_b, w_up, w_down, ln2_g, ln2_b):
    depth = w_in.shape[0]
    assert depth == 1, "single-layer problem"
    rwkv_proj = rwkv_mu.shape[-1]
    lora = (rwkv_w2.shape[1], rwkv_a2.shape[1], rwkv_g2.shape[1])
    assert rwkv_w0.shape[-1] == RW and fox_out_g.shape[-1] == FW and rwkv_proj == 3 * RW + sum(lora)
    assert lora[0] + lora[1] == LORA_WA and lora[2] <= LORA_G and RP_PAD <= w_in.shape[-1]
    alpha = (2 * depth) ** 0.25
    l = 0
    w = w_in[l]
    fo = rwkv_proj
    row = lambda z: z.reshape(1, -1)
    pad_rows = lambda z, n: jnp.pad(z, ((0, n - z.shape[0]), (0, 0)))
    wts = dict(
        alpha=alpha,
        w_r=w[:, :RP_PAD].astype(BF16),
        w_f=jnp.concatenate([w[:, fo:fo + 3 * FW], w[:, fo + 3 * FW + FOX_HEADS:],
                             _pad_cols(jnp.tile(w[:, fo + 3 * FW:fo + 3 * FW + FOX_HEADS], (1, 3)), LANES)],
                            axis=-1).astype(BF16),
        mu=_pad_cols(row(rwkv_mu[l]), RP_PAD),
        w0=row(rwkv_w0[l]), w2p=jnp.pad(rwkv_w2[l], ((0, lora[1]), (0, 0))).astype(BF16),
        a0=row(rwkv_a0[l]), a2p=jnp.pad(rwkv_a2[l], ((lora[0], 0), (0, 0))).astype(BF16),
        g2p=pad_rows(rwkv_g2[l], LORA_G).astype(BF16),
        k_k=row(rwkv_k_k[l]), k_a=row(rwkv_k_a[l]), r_k=row(rwkv_r_k[l]),
        lnx_g=row(rwkv_lnx_g[l]), lnx_b=row(rwkv_lnx_b[l]),
        bf_row=_pad_cols(jnp.tile(row(fox_b_f[l]), (1, 3)), LANES), og_g=row(fox_out_g[l]),
        w_o=w_o[l], ln1_g=row(ln1_g[l]), ln1_b=row(ln1_b[l]),
        w_up=w_up[l], w_down=w_down[l], ln2_g=row(ln2_g[l]), ln2_b=row(ln2_b[l]),
    )
    n_p, t_p, _ = x_prompt.shape
    n_s, t_s, _ = x_sample.shape
    heads = RW // HEAD_DIM
    shift0 = jnp.zeros((n_p, 1, rwkv_proj), F32)
    s_zero = jnp.zeros((n_p, heads, HEAD_DIM, HEAD_DIM), F32)
    yp, kp, vp, fp, sp, shp = _stream(x_prompt, shift0, s_zero, wts, t_seq=t_p, tm=TM_PROJ, tm_ffn=TM_FFN, chunk=CHUNK)
    ys, ks, vs, fs, ss, shs = _stream(x_sample, state_rwkv_shift[l], state_rwkv_wkv[l], wts, t_seq=t_s,
                                      tm=n_s * t_s, tm_ffn=n_s * t_s, chunk=t_s,
                                      cache=(cache_fox_k, cache_fox_v, cache_fox_logf, l))
    return (yp, ys, kp[None], vp[None], fp[None], sp[None], shp[None],
            ks[None], vs[None], fs[None], ss[None], shs[None])
```

```python
import functools
import math

import jax
import jax.numpy as jnp
from jax import lax
from jax.experimental import pallas as pl
from jax.experimental.pallas import tpu as pltpu

F32 = jnp.float32
BF16 = jnp.bfloat16

HEAD_DIM = 64
LANES = 128
LN_EPS = 1e-5
GN_EPS = 64e-5
RMS_EPS = 1e-6
ATTN_SCALE = HEAD_DIM ** -0.5
EXP_NEG_HALF = math.exp(-0.5)
LOG2E = math.log2(math.e)
NEG_BIG = -1e30
VMEM_LIMIT = 56 * 1024 * 1024

TM_PROJ = 256
TM_OUT = 512
TM_FFN = 1024
TF_FFN = 1024
TQ_ATTN = 1024
TK_ATTN = 512
CHUNK = 64
SCAN_CHUNKS = 4

NT = (((1,), (1,)), ((), ()))
TN = (((0,), (0,)), ((), ()))


def _sigmoid(x):
    return 1.0 / (1.0 + jnp.exp(-x))


def _log_sigmoid(x):
    return jnp.minimum(x, 0.0) - jnp.log1p(jnp.exp(-jnp.abs(x)))


def _dot(a, b):
    return jnp.dot(a, b, preferred_element_type=F32)


def _dg(a, b, dims):
    return lax.dot_general(a, b, dims, preferred_element_type=F32)


def _split3(x):
    hi = x.astype(BF16)
    rem = x - hi.astype(F32)
    mid = rem.astype(BF16)
    return hi, mid, (rem - mid.astype(F32)).astype(BF16)


def _dot01(m01, x):
    hi, mid, lo = _split3(x)
    return _dot(m01, hi) + _dot(m01, mid) + _dot(m01, lo)


def _iota(shape, axis):
    return lax.broadcasted_iota(jnp.int32, shape, axis)


def _head_sum(x):
    r = _iota((LANES, LANES), 0) // HEAD_DIM
    c = _iota((LANES, LANES), 1) // HEAD_DIM
    ones_blk = jnp.where(r == c, 1.0, 0.0).astype(BF16)
    outs = []
    for g in range(x.shape[1] // LANES):
        xs = x[:, g * LANES:(g + 1) * LANES]
        hi = xs.astype(BF16)
        lo = (xs - hi.astype(F32)).astype(BF16)
        outs.append(_dot(hi, ones_blk) + _dot(lo, ones_blk))
    return outs[0] if len(outs) == 1 else jnp.concatenate(outs, axis=1)


def _layer_norm(z, g, b):
    mu = jnp.mean(z, axis=-1, keepdims=True)
    zc = z - mu
    var = jnp.mean(zc * zc, axis=-1, keepdims=True)
    return zc * lax.rsqrt(var + LN_EPS) * g + b


def _const_spec(shape):
    nd = len(shape)
    return pl.BlockSpec(shape, lambda *_: (0,) * nd, pipeline_mode=pl.Buffered(1))


def _params(sem):
    return pltpu.CompilerParams(dimension_semantics=sem, vmem_limit_bytes=VMEM_LIMIT)


RW = 1024
LORA_WA = LANES
LORA_G = 2 * LANES
RP_PAD = 3 * RW + LORA_WA + LORA_G


def _rwkv_proj_kernel(x_ref, w_ref, mu_ref, fp_ref, w0_ref, w2_ref, a0_ref, a2_ref, g2_ref, kk_ref, ka_ref,
                      at_ref, rt_ref, bt_ref, kt_ref, vb_ref, g_ref, wtot_ref, sh_ref, carry_ref,
                      *, tm, t_seq, chunk):
    i = pl.program_id(0)
    xb = x_ref[...].astype(BF16)
    rows = _iota((tm, 1), 0)

    def proj(c0, n):
        return _dot(xb, w_ref[:, c0:c0 + n])

    def shift(p, c0):
        n = p.shape[1]
        prev = pltpu.roll(p, 1, 0)
        if t_seq >= tm:
            tiles = t_seq // tm
            pos = lax.rem(i, tiles)
            sidx = lax.div(i, tiles)
            row0 = jnp.where(pos == 0, fp_ref[sidx, :, c0:c0 + n], carry_ref[:, c0:c0 + n])
            prev = jnp.where(rows == 0, row0, prev)
            carry_ref[:, c0:c0 + n] = p[tm - 1:tm, :]

            @pl.when(pos == tiles - 1)
            def _():
                sh_ref[sidx, :, c0:c0 + n] = p[tm - 1:tm, :]
        else:
            per_tile = tm // t_seq
            for j in range(per_tile):
                prev = jnp.where(rows == j * t_seq, fp_ref[i * per_tile + j, :, c0:c0 + n], prev)
                sh_ref[i * per_tile + j, :, c0:c0 + n] = p[(j + 1) * t_seq - 1:(j + 1) * t_seq, :]
        return p + (prev - p) * mu_ref[:, c0:c0 + n]

    p_wa = proj(3 * RW, LORA_WA)
    p_g = proj(3 * RW + LORA_WA, LORA_G)
    xwa = shift(p_wa, 3 * RW)
    xg = shift(p_g, 3 * RW + LORA_WA)
    wl = w0_ref[...] + _dot(jnp.tanh(xwa).astype(BF16), w2_ref[...])
    alr = _sigmoid(a0_ref[...] + _dot(xwa.astype(BF16), a2_ref[...]))
    g_ref[...] = _dot(_sigmoid(xg).astype(BF16), g2_ref[...]).astype(BF16)
    p_k = proj(RW, RW)
    lw = -EXP_NEG_HALF * _sigmoid(wl)
    rr = _iota((tm, tm), 0)
    cc = _iota((tm, tm), 1)
    same_chunk_tri = jnp.where(rr // chunk == cc // chunk, jnp.where(rr >= cc, 1.0, 0.0), 0.0).astype(BF16)
    lwc = _dot01(same_chunk_tri, lw)
    k = shift(p_k, RW)
    kk = k * kk_ref[...]
    kk_ss = _head_sum(kk * kk)
    p_r = proj(0, RW)
    p_v = proj(2 * RW, RW)
    kkn = kk / jnp.maximum(jnp.sqrt(kk_ss), 1e-12)
    kh = k * (1.0 + (alr - 1.0) * ka_ref[...])
    e_in = jnp.exp(lwc)
    e_out = jnp.exp(-lwc)
    at_ref[...] = (-kkn * jnp.exp(lwc - lw)).astype(BF16)
    bt_ref[...] = (kkn * alr * e_out).astype(BF16)
    kt_ref[...] = (kh * e_out).astype(BF16)
    for c in range(tm // chunk):
        wtot_ref[c] = e_in[(c + 1) * chunk - 1:(c + 1) * chunk, :]
    r = shift(p_r, 0)
    rt_ref[...] = (r * e_in).astype(BF16)
    vb_ref[...] = shift(p_v, 2 * RW).astype(BF16)


def _rwkv_proj(x, w, mu, fprev, w0, w2p, a0, a2p, g2p, k_k, k_a, *, tm, t_seq, chunk):
    m, d = x.shape
    n_seq = m // t_seq
    row = lambda i: (i, 0)
    big = lambda dt: jax.ShapeDtypeStruct((m, RW), dt)
    return pl.pallas_call(
        functools.partial(_rwkv_proj_kernel, tm=tm, t_seq=t_seq, chunk=chunk),
        grid=(m // tm,),
        in_specs=[pl.BlockSpec((tm, d), row),
                  _const_spec(w.shape), _const_spec(mu.shape), _const_spec(fprev.shape),
                  _const_spec(w0.shape), _const_spec(w2p.shape), _const_spec(a0.shape), _const_spec(a2p.shape),
                  _const_spec(g2p.shape), _const_spec(k_k.shape), _const_spec(k_a.shape)],
        out_specs=[pl.BlockSpec((tm, RW), row)] * 6
                  + [pl.BlockSpec((tm // chunk, 1, RW), lambda i: (i, 0, 0)),
                     pl.BlockSpec((n_seq, 1, RP_PAD), lambda i: (0, 0, 0))],
        out_shape=[big(BF16)] * 6 + [jax.ShapeDtypeStruct((m // chunk, 1, RW), F32),
                                     jax.ShapeDtypeStruct((n_seq, 1, RP_PAD), F32)],
        scratch_shapes=[pltpu.VMEM((1, RP_PAD), F32)],
        compiler_params=_params(("arbitrary",)),
        name="rwkv_proj",
    )(x, w, mu, fprev, w0, w2p, a0, a2p, g2p, k_k, k_a)


FW = 1024
FOX_HEADS = FW // HEAD_DIM


def _fox_proj_kernel(x_ref, w_ref, bf_ref, og_ref,
                     qb_ref, kt_ref, vt_ref, kb_ref, vb_ref, vbt_ref, gate_ref, lf_ref, caug_ref, carry_ref,
                     *, tm, t_seq):
    i = pl.program_id(0)
    xb = x_ref[...].astype(BF16)

    def store_t(dst_ref, zt):
        if t_seq >= tm:
            dst_ref[0] = zt.reshape(FOX_HEADS, HEAD_DIM, tm)
        else:
            for j in range(tm // t_seq):
                dst_ref[j] = zt[:, j * t_seq:(j + 1) * t_seq].reshape(FOX_HEADS, HEAD_DIM, t_seq)

    qb_ref[...] = (_dot(xb, w_ref[:, 0:FW]) * (ATTN_SCALE * LOG2E)).astype(BF16)
    k = _dot(xb, w_ref[:, FW:2 * FW])
    store_t(kt_ref, k.T)
    kb_ref[...] = k.astype(BF16)
    v = _dot(xb, w_ref[:, 2 * FW:3 * FW])
    vt = v.T
    store_t(vt_ref, vt)
    vb_ref[...] = v.astype(BF16)
    vbt_ref[0] = vt.astype(BF16)
    og = _dot(xb, w_ref[:, 3 * FW:4 * FW])
    gate_ref[...] = _sigmoid(og) * og_ref[...]
    logf = _log_sigmoid(_dot(xb, w_ref[:, 4 * FW:4 * FW + LANES]) + bf_ref[...])
    lf_ref[...] = logf[:, :FOX_HEADS]
    r = _iota((tm, tm), 0)
    c = _iota((tm, tm), 1)
    if t_seq >= tm:
        cs = _dot01(jnp.where(r >= c, 1.0, 0.0).astype(BF16), logf)
        cs = cs + jnp.where(lax.rem(i, t_seq // tm) == 0, 0.0, carry_ref[...])
        carry_ref[...] = cs[tm - 1:tm, :]
    else:
        same_seq = (r // t_seq) == (c // t_seq)
        cs = _dot01(jnp.where(same_seq, jnp.where(r >= c, 1.0, 0.0), 0.0).astype(BF16), logf)
    xs = -LOG2E * cs
    hi, mid, lo = _split3(xs)
    lane = _iota((tm, LANES), 1)
    zero = jnp.zeros_like(hi)
    caug_ref[...] = jnp.where(lane < FOX_HEADS, hi,
                              jnp.where(lane < 2 * FOX_HEADS, mid, jnp.where(lane < 3 * FOX_HEADS, lo, zero)))


def _cast_job(a, block, index_map, out_shape=None):
    spec = pl.BlockSpec(block, index_map)
    return a, spec, jax.ShapeDtypeStruct(out_shape or a.shape, BF16)


def _fox_proj(x, w, bf_row, og_g, *, tm, t_seq):
    m, d = x.shape
    row = lambda i: (i, 0)
    big = lambda dt: jax.ShapeDtypeStruct((m, FW), dt)
    cache_t = jax.ShapeDtypeStruct((m // t_seq, FOX_HEADS, HEAD_DIM, t_seq), F32)
    if t_seq >= tm:
        tiles = t_seq // tm
        cache_spec = pl.BlockSpec((1, FOX_HEADS, HEAD_DIM, tm), lambda i: (i // tiles, 0, 0, i % tiles))
    else:
        cache_spec = pl.BlockSpec((tm // t_seq, FOX_HEADS, HEAD_DIM, t_seq), lambda i: (i, 0, 0, 0))
    return pl.pallas_call(
        functools.partial(_fox_proj_kernel, tm=tm, t_seq=t_seq),
        grid=(m // tm,),
        in_specs=[pl.BlockSpec((tm, d), row), _const_spec(w.shape), _const_spec(bf_row.shape),
                  _const_spec(og_g.shape)],
        out_specs=[pl.BlockSpec((tm, FW), row), cache_spec, cache_spec, pl.BlockSpec((tm, FW), row),
                   pl.BlockSpec((tm, FW), row),
                   pl.BlockSpec((1, FW, tm), lambda i: (i, 0, 0)), pl.BlockSpec((tm, FW), row),
                   pl.BlockSpec((tm, FOX_HEADS), row), pl.BlockSpec((tm, LANES), row)],
        out_shape=[big(BF16), cache_t, cache_t, big(BF16), big(BF16),
                   jax.ShapeDtypeStruct((m // tm, FW, tm), BF16), big(F32),
                   jax.ShapeDtypeStruct((m, FOX_HEADS), F32), jax.ShapeDtypeStruct((m, LANES), BF16)],
        scratch_shapes=[pltpu.VMEM((1, LANES), F32)],
        compiler_params=_params(("arbitrary",)),
        name="fox_proj",
    )(x, w, bf_row, og_g)


def _stack_heads(x, mask0):
    return jnp.concatenate([jnp.where(mask0, x, 0.0), jnp.where(mask0, 0.0, x)], axis=0)


def _rwkv_scan_kernel(at_ref, rt_ref, bt_ref, kt_ref, vb_ref, wtot_ref, g_ref, s0_ref, lng_ref, lnb_ref, rk_ref,
                      y_ref, sout_ref, state_ref, *, chunk, n_steps):
    step = pl.program_id(1)
    n_pairs = at_ref.shape[1] // LANES
    C = chunk
    n_ch = at_ref.shape[0] // C
    pairs = range(n_pairs)

    @pl.when(step == 0)
    def _():
        zero = jnp.zeros((HEAD_DIM, HEAD_DIM), F32)
        for p in pairs:
            top = jnp.concatenate([s0_ref[0, 2 * p], zero], axis=1)
            bot = jnp.concatenate([zero, s0_ref[0, 2 * p + 1]], axis=1)
            state_ref[p] = jnp.concatenate([top, bot], axis=0)

    fmask0 = _iota((1, LANES), 1) < HEAD_DIM
    tcol = _iota((C, 2 * C), 1)
    trow = _iota((C, 2 * C), 0)
    tmask0 = tcol < C
    tj = jnp.where(tmask0, tcol, tcol - C)
    strict = tj < trow
    incl = tj <= trow
    eye_pair = jnp.where(tj == trow, 1.0, 0.0)
    blk = (_iota((LANES, LANES), 0) // HEAD_DIM) == (_iota((LANES, LANES), 1) // HEAD_DIM)

    def stack_t(x):
        return _stack_heads(x, tmask0)

    def tile(ref, c, p):
        return ref[c * C:(c + 1) * C, p * LANES:(p + 1) * LANES]

    probs = [(c, p) for c in range(n_ch) for p in pairs]
    at = {cp: tile(at_ref, *cp) for cp in probs}
    rt = {cp: tile(rt_ref, *cp) for cp in probs}
    bt = {cp: tile(bt_ref, *cp) for cp in probs}
    kt = {cp: tile(kt_ref, *cp) for cp in probs}
    vb = {cp: tile(vb_ref, *cp) for cp in probs}
    gm = {cp: _dg(jnp.concatenate([at[cp], rt[cp]], axis=0),
                  jnp.concatenate([_stack_heads(bt[cp], fmask0), _stack_heads(kt[cp], fmask0)], axis=0), NT)
          for cp in probs}
    lab = {cp: jnp.where(strict, gm[cp][:C, :2 * C], 0.0) for cp in probs}
    lak_b = {cp: jnp.where(strict, gm[cp][:C, 2 * C:], 0.0).astype(BF16) for cp in probs}
    mr_b = {cp: jnp.concatenate([jnp.where(incl, gm[cp][C:, :2 * C], 0.0),
                                 jnp.where(incl, gm[cp][C:, 2 * C:], 0.0)], axis=1).astype(BF16) for cp in probs}
    kmax = int(math.log2(C)) - 1
    tinv = {cp: eye_pair + lab[cp] for cp in probs}
    pw = {cp: lab[cp].astype(BF16) for cp in probs}
    pw = {cp: _dot(pw[cp], stack_t(pw[cp])).astype(BF16) for cp in probs}
    for _ in range(1, kmax):
        res = {cp: _dot(jnp.concatenate([pw[cp], tinv[cp].astype(BF16)], axis=0), stack_t(pw[cp])) for cp in probs}
        pw = {cp: res[cp][:C].astype(BF16) for cp in probs}
        tinv = {cp: tinv[cp] + res[cp][C:] for cp in probs}
    tinv_b = {cp: (tinv[cp] + _dot(tinv[cp].astype(BF16), stack_t(pw[cp]))).astype(BF16) for cp in probs}

    sd = [state_ref[p] for p in pairs]
    ys = []
    for c in range(n_ch):
        sd_b = [z.astype(BF16) for z in sd]
        vd_b = [_stack_heads(vb[c, p], fmask0) for p in pairs]
        x = [_dg(at[c, p], sd_b[p], NT) + _dot(lak_b[c, p], vd_b[p]) for p in pairs]
        u_b = [_dot(tinv_b[c, p], _stack_heads(x[p].astype(BF16), fmask0)).astype(BF16) for p in pairs]
        ys.append(jnp.concatenate(
            [_dg(rt[c, p], sd_b[p], NT)
             + _dot(mr_b[c, p], jnp.concatenate([_stack_heads(u_b[p], fmask0), vd_b[p]], axis=0)) for p in pairs],
            axis=1))
        w_tot = wtot_ref[c]
        for p in pairs:
            ds = _dg(jnp.concatenate([u_b[p], vb[c, p]], axis=0), jnp.concatenate([bt[c, p], kt[c, p]], axis=0), TN)
            sd[p] = (sd[p] + jnp.where(blk, ds, 0.0)) * w_tot[:, p * LANES:(p + 1) * LANES]
    for p in pairs:
        state_ref[p] = sd[p]

    y = ys[0] if n_ch == 1 else jnp.concatenate(ys, axis=0)
    inv_n = 1.0 / HEAD_DIM
    mu = _head_sum(y) * inv_n
    yc = y - mu
    var = _head_sum(yc * yc) * inv_n
    yn = yc * lax.rsqrt(var + GN_EPS) * lng_ref[...] + lnb_ref[...]
    rk = rt_ref[...].astype(F32) * kt_ref[...].astype(F32) * rk_ref[...]
    bonus = _head_sum(rk) * vb_ref[...].astype(F32)
    y_ref[...] = ((yn + bonus) * g_ref[...].astype(F32)).astype(y_ref.dtype)

    @pl.when(step == n_steps - 1)
    def _():
        for p in pairs:
            sout_ref[0, 2 * p] = sd[p][:HEAD_DIM, :HEAD_DIM]
            sout_ref[0, 2 * p + 1] = sd[p][HEAD_DIM:, HEAD_DIM:]


def _rwkv_scan(at, rt, bt, kt, vb, wtot, g, s0, lng, lnb, rk, *, chunk, rows, t_seq):
    m, w = at.shape
    n_b = m // t_seq
    n_steps = t_seq // rows
    n_heads = w // HEAD_DIM
    blk = pl.BlockSpec((rows, w), lambda bi, si: (bi * n_steps + si, 0))
    wt = pl.BlockSpec((rows // chunk, 1, w), lambda bi, si: (bi * n_steps + si, 0, 0))
    st = pl.BlockSpec((1, n_heads, HEAD_DIM, HEAD_DIM), lambda bi, si: (bi, 0, 0, 0))
    return pl.pallas_call(
        functools.partial(_rwkv_scan_kernel, chunk=chunk, n_steps=n_steps),
        grid=(n_b, n_steps),
        in_specs=[blk] * 5 + [wt, blk, st, _const_spec(lng.shape), _const_spec(lnb.shape), _const_spec(rk.shape)],
        out_specs=[blk, st],
        out_shape=[jax.ShapeDtypeStruct((m, w), BF16),
                   jax.ShapeDtypeStruct((n_b, n_heads, HEAD_DIM, HEAD_DIM), F32)],
        scratch_shapes=[pltpu.VMEM((w // LANES, LANES, LANES), F32)],
        compiler_params=_params(("arbitrary", "arbitrary")),
        name="rwkv_scan",
    )(at, rt, bt, kt, vb, wtot, g, s0, lng, lnb, rk)


def _transpose01(x):
    h = x.shape[1]
    eye = jnp.where(_iota((h, h), 0) == _iota((h, h), 1), 1.0, 0.0).astype(BF16)
    hi, mid, lo = _split3(x)
    return _dg(eye, hi, NT) + _dg(eye, mid, NT) + _dg(eye, lo, NT)


def _lane_cumsum(x, block):
    tri = jnp.where(_iota((block, block), 0) <= _iota((block, block), 1), 1.0, 0.0).astype(BF16)
    hi, mid, lo = _split3(x)
    outs = []
    offset = jnp.zeros((x.shape[0], 1), F32)
    for j in range(x.shape[1] // block):
        sl = slice(j * block, (j + 1) * block)
        c = _dot(hi[:, sl], tri) + _dot(mid[:, sl], tri) + _dot(lo[:, sl], tri)
        outs.append(c + offset)
        offset = offset + c[:, block - 1:block]
    return outs[0] if len(outs) == 1 else jnp.concatenate(outs, axis=1)


def _fox_prompt_kernel(q_ref, k_ref, caug_ref, vt_ref, gate_ref, *rest, tq, tk, n_cast):
    cast_in, y_ref, cast_out = rest[:n_cast], rest[n_cast], rest[n_cast + 1:]
    hp = pl.program_id(1)
    qi = pl.program_id(2)
    nq = k_ref.shape[0] // tq
    vblk = vt_ref.shape[2]
    lane = _iota((tq, LANES), 1)
    first = lane < HEAD_DIM
    q = q_ref[...]
    zero = jnp.zeros_like(q)
    hslot = jnp.where(lane < 3 * FOX_HEADS, lax.rem(lane, FOX_HEADS), -1)
    qsa = []
    for h in range(2):
        ones = jnp.where(hslot == 2 * hp + h, 1.0, 0.0).astype(BF16)
        qh = jnp.where(first, q, zero) if h == 0 else jnp.where(first, zero, q)
        qsa.append(jnp.concatenate([qh, ones], axis=1))
    qsa_t = [z.astype(F32).T.astype(BF16) for z in qsa]

    def scores(item):
        k0, nk, q_lo, _ = item
        kk = jnp.concatenate([k_ref[k0:k0 + nk, :], caug_ref[k0:k0 + nk, :]], axis=1)
        rhs = jnp.concatenate([qsa_t[0][:, q_lo:], qsa_t[1][:, q_lo:]], axis=1)
        return _dot(kk, rhs)

    def update(st, item, state):
        k0, nk, q_lo, masked = item
        nqv = tq - q_lo
        vt = jnp.concatenate([vt_ref[k0 // vblk + i] for i in range(nk // vblk)], axis=1)
        out = []
        for h in range(2):
            m_all, l_all, acc_all = state[h]
            sh = st[:, h * nqv:(h + 1) * nqv]
            if masked:
                sh = jnp.where(_iota((nk, nqv), 0) <= _iota((nk, nqv), 1), sh, NEG_BIG)
            m_old = m_all[:, q_lo:]
            m_new = jnp.maximum(m_old, jnp.max(sh, axis=0, keepdims=True))
            alpha = jnp.exp2(m_old - m_new)
            p = jnp.exp2(sh - m_new)
            l_new = alpha * l_all[:, q_lo:] + jnp.sum(p, axis=0, keepdims=True)
            acc_new = alpha * acc_all[:, q_lo:] + _dot(vt[h * HEAD_DIM:(h + 1) * HEAD_DIM], p.astype(BF16))
            if q_lo:
                m_new = jnp.concatenate([m_all[:, :q_lo], m_new], axis=1)
                l_new = jnp.concatenate([l_all[:, :q_lo], l_new], axis=1)
                acc_new = jnp.concatenate([acc_all[:, :q_lo], acc_new], axis=1)
            out.append((m_new, l_new, acc_new))
        return out

    def program(c):
        items = [(kb * tk, tk, 0, False) for kb in range(c * tq // tk)]
        items += [(c * tq + d * tk, tk, d * tk, True) for d in range(tq // tk)]
        state = [(jnp.full((1, tq), NEG_BIG, F32), jnp.zeros((1, tq), F32), jnp.zeros((HEAD_DIM, tq), F32))
                 for _ in range(2)]
        st_next = scores(items[0])
        for n, item in enumerate(items):
            st = st_next
            if n + 1 < len(items):
                st_next = scores(items[n + 1])
            state = update(st, item, state)
        yts = []
        for _, l_fin, acc in state:
            o = acc * (1.0 / l_fin)
            ms = jnp.mean(o * o, axis=0, keepdims=True)
            yts.append(o * lax.rsqrt(ms + RMS_EPS))
        yt = jnp.concatenate(yts, axis=0)
        y_ref[...] = (yt.T * gate_ref[...]).astype(y_ref.dtype)
        for src, dst in zip(cast_in, cast_out):
            dst[...] = src[...].astype(BF16)

    for c in range(nq):
        pl.when(qi == c)(functools.partial(program, c))


def _fox_prompt_attn(qb, kb, caug, vbt, gate, *, n_b, t_seq, tq, tk, cast_cols=(), cast_rows=()):
    m, w = qb.shape
    n_pairs = w // LANES
    nq = t_seq // tq
    n_steps = n_b * n_pairs * nq
    vblk = vbt.shape[2]
    assert vbt.shape == (m // vblk, w, vblk) and tk % vblk == 0 and tq % tk == 0
    qspec = pl.BlockSpec((tq, LANES), lambda b, hp, qi: (b * nq + qi, hp))
    step = lambda b, hp, qi: (b * n_pairs + hp) * nq + qi
    casts = ([_cast_job(a, (a.shape[0], a.shape[1] // n_steps), lambda b, hp, qi: (0, step(b, hp, qi)))
              for a in cast_cols]
             + [_cast_job(a, (a.shape[0] // n_steps, a.shape[1]), lambda b, hp, qi: (step(b, hp, qi), 0))
                for a in cast_rows])
    outs = pl.pallas_call(
        functools.partial(_fox_prompt_kernel, tq=tq, tk=tk, n_cast=len(casts)),
        grid=(n_b, n_pairs, nq),
        in_specs=[qspec,
                  pl.BlockSpec((t_seq, LANES), lambda b, hp, qi: (b, hp)),
                  pl.BlockSpec((t_seq, LANES), lambda b, hp, qi: (b, 0)),
                  pl.BlockSpec((t_seq // vblk, LANES, vblk), lambda b, hp, qi: (b, hp, 0)),
                  qspec] + [spec for _, spec, _ in casts],
        out_specs=[qspec] + [spec for _, spec, _ in casts],
        out_shape=[jax.ShapeDtypeStruct((m, w), BF16)] + [sds for _, _, sds in casts],
        compiler_params=_params(("arbitrary", "arbitrary", "arbitrary")),
        name="fox_prompt_attn",
    )(qb, kb, caug, vbt, gate, *[a for a, _, _ in casts])
    return outs[0], outs[1:]


def _fox_sample_kernel(q_ref, kn_ref, vn_ref, ckt_ref, cvt_ref, clf_ref, lf_ref, gate_ref, y_ref):
    t = q_ref.shape[0]
    tril = jnp.where(_iota((t, t), 0) >= _iota((t, t), 1), 1.0, 0.0).astype(BF16)
    clf_row = clf_ref[0, 0]
    c_tot = jnp.sum(clf_row, axis=-1, keepdims=True)
    cn_col = _dot01(tril, lf_ref[...])
    cn_row = _transpose01(cn_col)
    cc_row = _lane_cumsum(clf_row, 2 * LANES)
    lane = _iota((t, LANES), 1)
    first = lane < HEAD_DIM
    causal = _iota((t, t), 1) <= _iota((t, t), 0)
    ys = []
    for p in range(q_ref.shape[1] // LANES):
        sl = slice(p * LANES, (p + 1) * LANES)
        q = q_ref[:, sl]
        zero = jnp.zeros_like(q)
        qs = jnp.concatenate([jnp.where(first, q, zero), jnp.where(first, zero, q)], axis=0)
        kct = jnp.concatenate([ckt_ref[0, 0, 2 * p], ckt_ref[0, 0, 2 * p + 1]], axis=0).astype(BF16)
        vct = jnp.concatenate([cvt_ref[0, 0, 2 * p], cvt_ref[0, 0, 2 * p + 1]], axis=0).astype(BF16)
        kn = kn_ref[:, sl]
        vn = vn_ref[:, sl]
        s_c = _dot(qs, kct)
        s_n = _dg(qs, kn, NT)
        o = []
        for h in range(2):
            hd = 2 * p + h
            cq = cn_col[:, hd:hd + 1] + c_tot[hd:hd + 1]
            sc = s_c[h * t:(h + 1) * t] + (cq - cc_row[hd:hd + 1]) * LOG2E
            sn = s_n[h * t:(h + 1) * t] + (cn_col[:, hd:hd + 1] - cn_row[hd:hd + 1]) * LOG2E
            sn = jnp.where(causal, sn, NEG_BIG)
            mx = jnp.maximum(jnp.max(sc, axis=-1, keepdims=True), jnp.max(sn, axis=-1, keepdims=True))
            pc = jnp.exp2(sc - mx)
            pn = jnp.exp2(sn - mx)
            den = jnp.sum(pc, axis=-1, keepdims=True) + jnp.sum(pn, axis=-1, keepdims=True)
            o.append((_dg(pc.astype(BF16), vct, NT) + _dot(pn.astype(BF16), vn)) / den)
        om = jnp.where(first, o[0], o[1])
        sq = om * om
        ms0 = jnp.sum(jnp.where(first, sq, 0.0), axis=-1, keepdims=True)
        ms1 = jnp.sum(jnp.where(first, 0.0, sq), axis=-1, keepdims=True)
        ms = jnp.where(first, ms0, ms1) * (1.0 / HEAD_DIM)
        ys.append(om * lax.rsqrt(ms + RMS_EPS) * gate_ref[:, sl])
    y_ref[...] = jnp.concatenate(ys, axis=1).astype(y_ref.dtype)


def _fox_sample_attn(qb, kb, vb, cache_kt, cache_vt, cache_lft, lf, gate, *, t_seq, layer):
    m, w = qb.shape
    n_b = m // t_seq
    row = pl.BlockSpec((t_seq, w), lambda b: (b, 0))
    cache = pl.BlockSpec((1, 1) + cache_kt.shape[2:], lambda b: (layer, b, 0, 0, 0))
    return pl.pallas_call(
        _fox_sample_kernel,
        grid=(n_b,),
        in_specs=[row, row, row, cache, cache,
                  pl.BlockSpec((1, 1) + cache_lft.shape[2:], lambda b: (layer, b, 0, 0)),
                  pl.BlockSpec((t_seq, lf.shape[1]), lambda b: (b, 0)), row],
        out_specs=row,
        out_shape=jax.ShapeDtypeStruct((m, w), BF16),
        compiler_params=_params(("arbitrary",)),
        name="fox_sample_attn",
    )(qb, kb, vb, cache_kt, cache_vt, cache_lft, lf, gate)


def _out_ln_kernel(yr_ref, yf_ref, x_ref, wo_ref, g_ref, b_ref, h_ref, wob_ref, *, alpha):
    @pl.when(pl.program_id(0) == 0)
    def _():
        wob_ref[...] = wo_ref[...].astype(BF16)

    half = yr_ref.shape[1]
    mix = _dot(yr_ref[...], wob_ref[0:half, :]) + _dot(yf_ref[...], wob_ref[half:, :])
    h_ref[...] = _layer_norm(alpha * x_ref[...] + mix, g_ref[...], b_ref[...])


def _out_ln(yr, yf, x, wo, g, b, *, tm, alpha):
    m, d = x.shape
    row = lambda i: (i, 0)
    return pl.pallas_call(
        functools.partial(_out_ln_kernel, alpha=alpha),
        grid=(m // tm,),
        in_specs=[pl.BlockSpec((tm, yr.shape[1]), row), pl.BlockSpec((tm, yf.shape[1]), row),
                  pl.BlockSpec((tm, d), row), _const_spec(wo.shape), _const_spec(g.shape), _const_spec(b.shape)],
        out_specs=pl.BlockSpec((tm, d), row),
        out_shape=jax.ShapeDtypeStruct((m, d), F32),
        scratch_shapes=[pltpu.VMEM(wo.shape, BF16)],
        compiler_params=_params(("arbitrary",)),
        name="out_ln",
    )(yr, yf, x, wo, g, b)


def _ffn_ln_kernel(h_ref, wu_ref, wd_ref, g_ref, b_ref, o_ref, hb_ref, *, alpha, n_f):
    j = pl.program_id(1)

    @pl.when(j == 0)
    def _():
        h = h_ref[...]
        hb_ref[...] = h.astype(BF16)
        o_ref[...] = alpha * h

    u = jnp.maximum(_dot(hb_ref[...], wu_ref[...]), 0.0)
    o_ref[...] += _dot((u * u).astype(BF16), wd_ref[...])

    @pl.when(j == n_f - 1)
    def _():
        o_ref[...] = _layer_norm(o_ref[...], g_ref[...], b_ref[...])


def _ffn_ln(h, wu, wd, g, b, *, tm, tf, alpha):
    m, d = h.shape
    n_f = wu.shape[1] // tf
    return pl.pallas_call(
        functools.partial(_ffn_ln_kernel, alpha=alpha, n_f=n_f),
        grid=(m // tm, n_f),
        in_specs=[pl.BlockSpec((tm, d), lambda i, j: (i, 0), pipeline_mode=pl.Buffered(1)),
                  pl.BlockSpec((d, tf), lambda i, j: (0, j)),
                  pl.BlockSpec((tf, d), lambda i, j: (j, 0)),
                  _const_spec(g.shape), _const_spec(b.shape)],
        out_specs=pl.BlockSpec((tm, d), lambda i, j: (i, 0)),
        out_shape=jax.ShapeDtypeStruct((m, d), F32),
        scratch_shapes=[pltpu.VMEM((tm, d), BF16)],
        compiler_params=_params(("arbitrary", "arbitrary")),
        name="ffn_ln",
    )(h, wu, wd, g, b)


def _pad_cols(x, n):
    return jnp.pad(x, [(0, 0)] * (x.ndim - 1) + [(0, n - x.shape[-1])])


def _stream(x, shift_prev, s0, wts, *, t_seq, tm, tm_ffn, chunk, cache=None):
    n_b, _, d = x.shape
    m = n_b * t_seq
    x2 = x.reshape(m, d)
    qb, k_t, v_t, kb, vb, vbt, gate, lf, caug = _fox_proj(x2, wts["w_f"], wts["bf_row"], wts["og_g"],
                                                          tm=tm, t_seq=t_seq)
    fprev = _pad_cols(shift_prev, RP_PAD)
    at, rt, bt, kt, vr, g, wtot, sh = _rwkv_proj(
        x2, wts["w_r"], wts["mu"], fprev, wts["w0"], wts["w2p"], wts["a0"], wts["a2p"], wts["g2p"], wts["k_k"],
        wts["k_a"], tm=tm, t_seq=t_seq, chunk=chunk)
    y_r, s_new = _rwkv_scan(at, rt, bt, kt, vr, wtot, g, s0, wts["lnx_g"], wts["lnx_b"], wts["r_k"],
                            chunk=chunk, rows=min(SCAN_CHUNKS * chunk, t_seq), t_seq=t_seq)
    if cache is None:
        y_f, (wts["w_up_b"], wts["w_down_b"]) = _fox_prompt_attn(
            qb, kb, caug, vbt, gate, n_b=n_b, t_seq=t_seq, tq=TQ_ATTN, tk=TK_ATTN,
            cast_cols=(wts["w_up"],), cast_rows=(wts["w_down"],))
    else:
        ck, cv, clf, layer = cache
        y_f = _fox_sample_attn(qb, kb, vb, jnp.transpose(ck, (0, 1, 3, 4, 2)), jnp.transpose(cv, (0, 1, 3, 4, 2)),
                               jnp.transpose(clf, (0, 1, 3, 2)), lf, gate, t_seq=t_seq, layer=layer)
    h = _out_ln(y_r, y_f, x2, wts["w_o"], wts["ln1_g"], wts["ln1_b"], tm=min(TM_OUT, m), alpha=wts["alpha"])
    y = _ffn_ln(h, wts["w_up_b"], wts["w_down_b"], wts["ln2_g"], wts["ln2_b"], tm=tm_ffn, tf=TF_FFN, alpha=wts["alpha"])
    heads = FW // HEAD_DIM
    to_cache = lambda z: jnp.transpose(z, (0, 3, 1, 2))
    return (y.reshape(n_b, t_seq, d), to_cache(k_t), to_cache(v_t), lf.reshape(n_b, t_seq, heads), s_new,
            sh[..., :shift_prev.shape[-1]])


def kernel(x_prompt, x_sample, cache_fox_k, cache_fox_v, cache_fox_logf, state_rwkv_wkv, state_rwkv_shift,
           w_in, rwkv_mu, rwkv_w0, rwkv_w2, rwkv_a0, rwkv_a2, rwkv_g2, rwkv_k_k, rwkv_k_a, rwkv_r_k,
           rwkv_lnx_g, rwkv_lnx_b, fox_b_f, fox_out_g, w_o, ln1_g, ln1_b, w_up, w_down, ln2_g, ln2_b):
    depth = w_in.shape[0]
    assert depth == 1, "single-layer problem"
    rwkv_proj = rwkv_mu.shape[-1]
    lora = (rwkv_w2.shape[1], rwkv_a2.shape[1], rwkv_g2.shape[1])
    assert rwkv_w0.shape[-1] == RW and fox_out_g.shape[-1] == FW and rwkv_proj == 3 * RW + sum(lora)
    assert lora[0] + lora[1] == LORA_WA and lora[2] <= LORA_G and RP_PAD <= w_in.shape[-1]
    alpha = (2 * depth) ** 0.25
    l = 0
    w = w_in[l]
    fo = rwkv_proj
    row = lambda z: z.reshape(1, -1)
    pad_rows = lambda z, n: jnp.pad(z, ((0, n - z.shape[0]), (0, 0)))
    wts = dict(
        alpha=alpha,
        w_r=w[:, :RP_PAD].astype(BF16),
        w_f=jnp.concatenate([w[:, fo:fo + 3 * FW], w[:, fo + 3 * FW + FOX_HEADS:],
                             _pad_cols(jnp.tile(w[:, fo + 3 * FW:fo + 3 * FW + FOX_HEADS], (1, 3)), LANES)],
                            axis=-1).astype(BF16),
        mu=_pad_cols(row(rwkv_mu[l]), RP_PAD),
        w0=row(rwkv_w0[l]), w2p=jnp.pad(rwkv_w2[l], ((0, lora[1]), (0, 0))).astype(BF16),
        a0=row(rwkv_a0[l]), a2p=jnp.pad(rwkv_a2[l], ((lora[0], 0), (0, 0))).astype(BF16),
        g2p=pad_rows(rwkv_g2[l], LORA_G).astype(BF16),
        k_k=row(rwkv_k_k[l]), k_a=row(rwkv_k_a[l]), r_k=row(rwkv_r_k[l]),
        lnx_g=row(rwkv_lnx_g[l]), lnx_b=row(rwkv_lnx_b[l]),
        bf_row=_pad_cols(jnp.tile(row(fox_b_f[l]), (1, 3)), LANES), og_g=row(fox_out_g[l]),
        w_o=w_o[l], ln1_g=row(ln1_g[l]), ln1_b=row(ln1_b[l]),
        w_up=w_up[l], w_down=w_down[l], ln2_g=row(ln2_g[l]), ln2_b=row(ln2_b[l]),
    )
    n_p, t_p, _ = x_prompt.shape
    n_s, t_s, _ = x_sample.shape
    heads = RW // HEAD_DIM
    shift0 = jnp.zeros((n_p, 1, rwkv_proj), F32)
    s_zero = jnp.zeros((n_p, heads, HEAD_DIM, HEAD_DIM), F32)
    yp, kp, vp, fp, sp, shp = _stream(x_prompt, shift0, s_zero, wts, t_seq=t_p, tm=TM_PROJ, tm_ffn=TM_FFN, chunk=CHUNK)
    ys, ks, vs, fs, ss, shs = _stream(x_sample, state_rwkv_shift[l], state_rwkv_wkv[l], wts, t_seq=t_s,
                                      tm=n_s * t_s, tm_ffn=n_s * t_s, chunk=t_s,
                                      cache=(cache_fox_k, cache_fox_v, cache_fox_logf, l))
    return (yp, ys, kp[None], vp[None], fp[None], sp[None], shp[None],
            ks[None], vs[None], fs[None], ss[None], shs[None])
```

```python
import functools
import math

import jax
import jax.numpy as jnp
from jax import lax
from jax.experimental import pallas as pl
from jax.experimental.pallas import tpu as pltpu

F32 = jnp.float32
BF16 = jnp.bfloat16

HEAD_DIM = 64
LANES = 128
LN_EPS = 1e-5
GN_EPS = 64e-5
RMS_EPS = 1e-6
ATTN_SCALE = HEAD_DIM ** -0.5
EXP_NEG_HALF = math.exp(-0.5)
LOG2E = math.log2(math.e)
NEG_BIG = -1e30
VMEM_LIMIT = 56 * 1024 * 1024

TM_PROJ = 256
TM_OUT = 512
TM_FFN = 1024
TF_FFN = 1024
TQ_ATTN = 1024
TK_ATTN = 512
CHUNK = 64
SCAN_CHUNKS = 4

NT = (((1,), (1,)), ((), ()))
TN = (((0,), (0,)), ((), ()))


def _sigmoid(x):
    return 1.0 / (1.0 + jnp.exp(-x))


def _log_sigmoid(x):
    return jnp.minimum(x, 0.0) - jnp.log1p(jnp.exp(-jnp.abs(x)))


def _dot(a, b):
    return jnp.dot(a, b, preferred_element_type=F32)


def _dg(a, b, dims):
    return lax.dot_general(a, b, dims, preferred_element_type=F32)


def _split3(x):
    hi = x.astype(BF16)
    rem = x - hi.astype(F32)
    mid = rem.astype(BF16)
    return hi, mid, (rem - mid.astype(F32)).astype(BF16)


def _dot01(m01, x):
    hi, mid, lo = _split3(x)
    return _dot(m01, hi) + _dot(m01, mid) + _dot(m01, lo)


def _iota(shape, axis):
    return lax.broadcasted_iota(jnp.int32, shape, axis)


def _head_sum(x):
    r = _iota((LANES, LANES), 0) // HEAD_DIM
    c = _iota((LANES, LANES), 1) // HEAD_DIM
    ones_blk = jnp.where(r == c, 1.0, 0.0).astype(BF16)
    outs = []
    for g in range(x.shape[1] // LANES):
        xs = x[:, g * LANES:(g + 1) * LANES]
        hi = xs.astype(BF16)
        lo = (xs - hi.astype(F32)).astype(BF16)
        outs.append(_dot(hi, ones_blk) + _dot(lo, ones_blk))
    return outs[0] if len(outs) == 1 else jnp.concatenate(outs, axis=1)


def _layer_norm(z, g, b):
    mu = jnp.mean(z, axis=-1, keepdims=True)
    zc = z - mu
    var = jnp.mean(zc * zc, axis=-1, keepdims=True)
    return zc * lax.rsqrt(var + LN_EPS) * g + b


def _const_spec(shape):
    nd = len(shape)
    return pl.BlockSpec(shape, lambda *_: (0,) * nd, pipeline_mode=pl.Buffered(1))


def _params(sem):
    return pltpu.CompilerParams(dimension_semantics=sem, vmem_limit_bytes=VMEM_LIMIT)


RW = 1024
LORA_WA = LANES
LORA_G = 2 * LANES
RP_PAD = 3 * RW + LORA_WA + LORA_G


def _rwkv_proj_kernel(x_ref, w_ref, mu_ref, fp_ref, w0_ref, w2_ref, a0_ref, a2_ref, g2_ref, kk_ref, ka_ref,
                      at_ref, rt_ref, bt_ref, kt_ref, vb_ref, g_ref, wtot_ref, sh_ref, carry_ref,
                      *, tm, t_seq, chunk):
    i = pl.program_id(0)
    xb = x_ref[...].astype(BF16)
    rows = _iota((tm, 1), 0)

    def proj(c0, n):
        return _dg(xb, w_ref[c0:c0 + n, :], NT)

    def shift(p, c0):
        n = p.shape[1]
        prev = pltpu.roll(p, 1, 0)
        if t_seq >= tm:
            tiles = t_seq // tm
            pos = lax.rem(i, tiles)
            sidx = lax.div(i, tiles)
            row0 = jnp.where(pos == 0, fp_ref[sidx, :, c0:c0 + n], carry_ref[:, c0:c0 + n])
            prev = jnp.where(rows == 0, row0, prev)
            carry_ref[:, c0:c0 + n] = p[tm - 1:tm, :]

            @pl.when(pos == tiles - 1)
            def _():
                sh_ref[sidx, :, c0:c0 + n] = p[tm - 1:tm, :]
        else:
            per_tile = tm // t_seq
            for j in range(per_tile):
                prev = jnp.where(rows == j * t_seq, fp_ref[i * per_tile + j, :, c0:c0 + n], prev)
                sh_ref[i * per_tile + j, :, c0:c0 + n] = p[(j + 1) * t_seq - 1:(j + 1) * t_seq, :]
        return p + (prev - p) * mu_ref[:, c0:c0 + n]

    p_wa = proj(3 * RW, LORA_WA)
    p_g = proj(3 * RW + LORA_WA, LORA_G)
    xwa = shift(p_wa, 3 * RW)
    xg = shift(p_g, 3 * RW + LORA_WA)
    wl = w0_ref[...] + _dot(jnp.tanh(xwa).astype(BF16), w2_ref[...])
    alr = _sigmoid(a0_ref[...] + _dot(xwa.astype(BF16), a2_ref[...]))
    g_ref[...] = _dot(_sigmoid(xg).astype(BF16), g2_ref[...]).astype(BF16)
    p_k = proj(RW, RW)
    lw = -EXP_NEG_HALF * _sigmoid(wl)
    rr = _iota((tm, tm), 0)
    cc = _iota((tm, tm), 1)
    same_chunk_tri = jnp.where(rr // chunk == cc // chunk, jnp.where(rr >= cc, 1.0, 0.0), 0.0).astype(BF16)
    lwc = _dot01(same_chunk_tri, lw)
    k = shift(p_k, RW)
    kk = k * kk_ref[...]
    kk_ss = _head_sum(kk * kk)
    p_r = proj(0, RW)
    p_v = proj(2 * RW, RW)
    kkn = kk / jnp.maximum(jnp.sqrt(kk_ss), 1e-12)
    kh = k * (1.0 + (alr - 1.0) * ka_ref[...])
    e_in = jnp.exp(lwc)
    e_out = jnp.exp(-lwc)
    at_ref[...] = (-kkn * jnp.exp(lwc - lw)).astype(BF16)
    bt_ref[...] = (kkn * alr * e_out).astype(BF16)
    kt_ref[...] = (kh * e_out).astype(BF16)
    for c in range(tm // chunk):
        wtot_ref[c] = e_in[(c + 1) * chunk - 1:(c + 1) * chunk, :]
    r = shift(p_r, 0)
    rt_ref[...] = (r * e_in).astype(BF16)
    vb_ref[...] = shift(p_v, 2 * RW).astype(BF16)


def _rwkv_proj(x, w, mu, fprev, w0, w2p, a0, a2p, g2p, k_k, k_a, *, tm, t_seq, chunk):
    m, d = x.shape
    n_seq = m // t_seq
    row = lambda i: (i, 0)
    big = lambda dt: jax.ShapeDtypeStruct((m, RW), dt)
    return pl.pallas_call(
        functools.partial(_rwkv_proj_kernel, tm=tm, t_seq=t_seq, chunk=chunk),
        grid=(m // tm,),
        in_specs=[pl.BlockSpec((tm, d), row),
                  _const_spec(w.shape), _const_spec(mu.shape), _const_spec(fprev.shape),
                  _const_spec(w0.shape), _const_spec(w2p.shape), _const_spec(a0.shape), _const_spec(a2p.shape),
                  _const_spec(g2p.shape), _const_spec(k_k.shape), _const_spec(k_a.shape)],
        out_specs=[pl.BlockSpec((tm, RW), row)] * 6
                  + [pl.BlockSpec((tm // chunk, 1, RW), lambda i: (i, 0, 0)),
                     pl.BlockSpec((n_seq, 1, RP_PAD), lambda i: (0, 0, 0))],
        out_shape=[big(BF16)] * 6 + [jax.ShapeDtypeStruct((m // chunk, 1, RW), F32),
                                     jax.ShapeDtypeStruct((n_seq, 1, RP_PAD), F32)],
        scratch_shapes=[pltpu.VMEM((1, RP_PAD), F32)],
        compiler_params=_params(("arbitrary",)),
        name="rwkv_proj",
    )(x, w, mu, fprev, w0, w2p, a0, a2p, g2p, k_k, k_a)


FW = 1024
FOX_HEADS = FW // HEAD_DIM


def _fox_proj_kernel(x_ref, w_ref, bf_ref, og_ref,
                     qb_ref, kt_ref, vt_ref, kb_ref, vb_ref, vbt_ref, gate_ref, lf_ref, caug_ref, carry_ref,
                     *, tm, t_seq):
    i = pl.program_id(0)
    xb = x_ref[...].astype(BF16)

    def store_t(dst_ref, zt):
        if t_seq >= tm:
            dst_ref[0] = zt.reshape(FOX_HEADS, HEAD_DIM, tm)
        else:
            for j in range(tm // t_seq):
                dst_ref[j] = zt[:, j * t_seq:(j + 1) * t_seq].reshape(FOX_HEADS, HEAD_DIM, t_seq)

    qb_ref[...] = (_dot(xb, w_ref[:, 0:FW]) * (ATTN_SCALE * LOG2E)).astype(BF16)
    k = _dot(xb, w_ref[:, FW:2 * FW])
    store_t(kt_ref, k.T)
    kb_ref[...] = k.astype(BF16)
    v = _dot(xb, w_ref[:, 2 * FW:3 * FW])
    vt = v.T
    store_t(vt_ref, vt)
    vb_ref[...] = v.astype(BF16)
    vbt_ref[0] = vt.astype(BF16)
    og = _dot(xb, w_ref[:, 3 * FW:4 * FW])
    gate_ref[...] = _sigmoid(og) * og_ref[...]
    logf = _log_sigmoid(_dot(xb, w_ref[:, 4 * FW:4 * FW + LANES]) + bf_ref[...])
    lf_ref[...] = logf[:, :FOX_HEADS]
    r = _iota((tm, tm), 0)
    c = _iota((tm, tm), 1)
    if t_seq >= tm:
        cs = _dot01(jnp.where(r >= c, 1.0, 0.0).astype(BF16), logf)
        cs = cs + jnp.where(lax.rem(i, t_seq // tm) == 0, 0.0, carry_ref[...])
        carry_ref[...] = cs[tm - 1:tm, :]
    else:
        same_seq = (r // t_seq) == (c // t_seq)
        cs = _dot01(jnp.where(same_seq, jnp.where(r >= c, 1.0, 0.0), 0.0).astype(BF16), logf)
    xs = -LOG2E * cs
    hi, mid, lo = _split3(xs)
    lane = _iota((tm, LANES), 1)
    zero = jnp.zeros_like(hi)
    caug_ref[...] = jnp.where(lane < FOX_HEADS, hi,
                              jnp.where(lane < 2 * FOX_HEADS, mid, jnp.where(lane < 3 * FOX_HEADS, lo, zero)))


def _cast_job(a, block, index_map, out_shape=None):
    spec = pl.BlockSpec(block, index_map)
    return a, spec, jax.ShapeDtypeStruct(out_shape or a.shape, BF16)


def _fox_proj(x, w, bf_row, og_g, *, tm, t_seq):
    m, d = x.shape
    row = lambda i: (i, 0)
    big = lambda dt: jax.ShapeDtypeStruct((m, FW), dt)
    cache_t = jax.ShapeDtypeStruct((m // t_seq, FOX_HEADS, HEAD_DIM, t_seq), F32)
    if t_seq >= tm:
        tiles = t_seq // tm
        cache_spec = pl.BlockSpec((1, FOX_HEADS, HEAD_DIM, tm), lambda i: (i // tiles, 0, 0, i % tiles))
    else:
        cache_spec = pl.BlockSpec((tm // t_seq, FOX_HEADS, HEAD_DIM, t_seq), lambda i: (i, 0, 0, 0))
    return pl.pallas_call(
        functools.partial(_fox_proj_kernel, tm=tm, t_seq=t_seq),
        grid=(m // tm,),
        in_specs=[pl.BlockSpec((tm, d), row), _const_spec(w.shape), _const_spec(bf_row.shape),
                  _const_spec(og_g.shape)],
        out_specs=[pl.BlockSpec((tm, FW), row), cache_spec, cache_spec, pl.BlockSpec((tm, FW), row),
                   pl.BlockSpec((tm, FW), row),
                   pl.BlockSpec((1, FW, tm), lambda i: (i, 0, 0)), pl.BlockSpec((tm, FW), row),
                   pl.BlockSpec((tm, FOX_HEADS), row), pl.BlockSpec((tm, LANES), row)],
        out_shape=[big(BF16), cache_t, cache_t, big(BF16), big(BF16),
                   jax.ShapeDtypeStruct((m // tm, FW, tm), BF16), big(F32),
                   jax.ShapeDtypeStruct((m, FOX_HEADS), F32), jax.ShapeDtypeStruct((m, LANES), BF16)],
        scratch_shapes=[pltpu.VMEM((1, LANES), F32)],
        compiler_params=_params(("arbitrary",)),
        name="fox_proj",
    )(x, w, bf_row, og_g)


def _stack_heads(x, mask0):
    return jnp.concatenate([jnp.where(mask0, x, 0.0), jnp.where(mask0, 0.0, x)], axis=0)


def _rwkv_scan_kernel(at_ref, rt_ref, bt_ref, kt_ref, vb_ref, wtot_ref, g_ref, s0_ref, lng_ref, lnb_ref, rk_ref,
                      y_ref, sout_ref, state_ref, *, chunk, n_steps):
    step = pl.program_id(1)
    n_pairs = at_ref.shape[1] // LANES
    C = chunk
    n_ch = at_ref.shape[0] // C
    pairs = range(n_pairs)

    @pl.when(step == 0)
    def _():
        zero = jnp.zeros((HEAD_DIM, HEAD_DIM), F32)
        for p in pairs:
            top = jnp.concatenate([s0_ref[0, 2 * p], zero], axis=1)
            bot = jnp.concatenate([zero, s0_ref[0, 2 * p + 1]], axis=1)
            state_ref[p] = jnp.concatenate([top, bot], axis=0)

    fmask0 = _iota((1, LANES), 1) < HEAD_DIM
    tcol = _iota((C, 2 * C), 1)
    trow = _iota((C, 2 * C), 0)
    tmask0 = tcol < C
    tj = jnp.where(tmask0, tcol, tcol - C)
    strict = tj < trow
    incl = tj <= trow
    eye_pair = jnp.where(tj == trow, 1.0, 0.0)
    blk = (_iota((LANES, LANES), 0) // HEAD_DIM) == (_iota((LANES, LANES), 1) // HEAD_DIM)

    def stack_t(x):
        return _stack_heads(x, tmask0)

    def tile(ref, c, p):
        return ref[c * C:(c + 1) * C, p * LANES:(p + 1) * LANES]

    probs = [(c, p) for c in range(n_ch) for p in pairs]
    at = {cp: tile(at_ref, *cp) for cp in probs}
    rt = {cp: tile(rt_ref, *cp) for cp in probs}
    bt = {cp: tile(bt_ref, *cp) for cp in probs}
    kt = {cp: tile(kt_ref, *cp) for cp in probs}
    vb = {cp: tile(vb_ref, *cp) for cp in probs}
    gm = {cp: _dg(jnp.concatenate([at[cp], rt[cp]], axis=0),
                  jnp.concatenate([_stack_heads(bt[cp], fmask0), _stack_heads(kt[cp], fmask0)], axis=0), NT)
          for cp in probs}
    lab = {cp: jnp.where(strict, gm[cp][:C, :2 * C], 0.0) for cp in probs}
    lak_b = {cp: jnp.where(strict, gm[cp][:C, 2 * C:], 0.0).astype(BF16) for cp in probs}
    mr_b = {cp: jnp.concatenate([jnp.where(incl, gm[cp][C:, :2 * C], 0.0),
                                 jnp.where(incl, gm[cp][C:, 2 * C:], 0.0)], axis=1).astype(BF16) for cp in probs}
    kmax = int(math.log2(C)) - 1
    tinv = {cp: eye_pair + lab[cp] for cp in probs}
    pw = {cp: lab[cp].astype(BF16) for cp in probs}
    pw = {cp: _dot(pw[cp], stack_t(pw[cp])).astype(BF16) for cp in probs}
    for _ in range(1, kmax):
        res = {cp: _dot(jnp.concatenate([pw[cp], tinv[cp].astype(BF16)], axis=0), stack_t(pw[cp])) for cp in probs}
        pw = {cp: res[cp][:C].astype(BF16) for cp in probs}
        tinv = {cp: tinv[cp] + res[cp][C:] for cp in probs}
    tinv_b = {cp: (tinv[cp] + _dot(tinv[cp].astype(BF16), stack_t(pw[cp]))).astype(BF16) for cp in probs}

    sd = [state_ref[p] for p in pairs]
    ys = []
    for c in range(n_ch):
        sd_b = [z.astype(BF16) for z in sd]
        vd_b = [_stack_heads(vb[c, p], fmask0) for p in pairs]
        x = [_dg(at[c, p], sd_b[p], NT) + _dot(lak_b[c, p], vd_b[p]) for p in pairs]
        u_b = [_dot(tinv_b[c, p], _stack_heads(x[p].astype(BF16), fmask0)).astype(BF16) for p in pairs]
        ys.append(jnp.concatenate(
            [_dg(rt[c, p], sd_b[p], NT)
             + _dot(mr_b[c, p], jnp.concatenate([_stack_heads(u_b[p], fmask0), vd_b[p]], axis=0)) for p in pairs],
            axis=1))
        w_tot = wtot_ref[c]
        for p in pairs:
            ds = _dg(jnp.concatenate([u_b[p], vb[c, p]], axis=0), jnp.concatenate([bt[c, p], kt[c, p]], axis=0), TN)
            sd[p] = (sd[p] + jnp.where(blk, ds, 0.0)) * w_tot[:, p * LANES:(p + 1) * LANES]
    for p in pairs:
        state_ref[p] = sd[p]

    y = ys[0] if n_ch == 1 else jnp.concatenate(ys, axis=0)
    inv_n = 1.0 / HEAD_DIM
    mu = _head_sum(y) * inv_n
    yc = y - mu
    var = _head_sum(yc * yc) * inv_n
    yn = yc * lax.rsqrt(var + GN_EPS) * lng_ref[...] + lnb_ref[...]
    rk = rt_ref[...].astype(F32) * kt_ref[...].astype(F32) * rk_ref[...]
    bonus = _head_sum(rk) * vb_ref[...].astype(F32)
    y_ref[...] = ((yn + bonus) * g_ref[...].astype(F32)).astype(y_ref.dtype)

    @pl.when(step == n_steps - 1)
    def _():
        for p in pairs:
            sout_ref[0, 2 * p] = sd[p][:HEAD_DIM, :HEAD_DIM]
            sout_ref[0, 2 * p + 1] = sd[p][HEAD_DIM:, HEAD_DIM:]


def _rwkv_scan(at, rt, bt, kt, vb, wtot, g, s0, lng, lnb, rk, *, chunk, rows, t_seq):
    m, w = at.shape
    n_b = m // t_seq
    n_steps = t_seq // rows
    n_heads = w // HEAD_DIM
    blk = pl.BlockSpec((rows, w), lambda bi, si: (bi * n_steps + si, 0))
    wt = pl.BlockSpec((rows // chunk, 1, w), lambda bi, si: (bi * n_steps + si, 0, 0))
    st = pl.BlockSpec((1, n_heads, HEAD_DIM, HEAD_DIM), lambda bi, si: (bi, 0, 0, 0))
    return pl.pallas_call(
        functools.partial(_rwkv_scan_kernel, chunk=chunk, n_steps=n_steps),
        grid=(n_b, n_steps),
        in_specs=[blk] * 5 + [wt, blk, st, _const_spec(lng.shape), _const_spec(lnb.shape), _const_spec(rk.shape)],
        out_specs=[blk, st],
        out_shape=[jax.ShapeDtypeStruct((m, w), BF16),
                   jax.ShapeDtypeStruct((n_b, n_heads, HEAD_DIM, HEAD_DIM), F32)],
        scratch_shapes=[pltpu.VMEM((w // LANES, LANES, LANES), F32)],
        compiler_params=_params(("arbitrary", "arbitrary")),
        name="rwkv_scan",
    )(at, rt, bt, kt, vb, wtot, g, s0, lng, lnb, rk)


def _transpose01(x):
    h = x.shape[1]
    eye = jnp.where(_iota((h, h), 0) == _iota((h, h), 1), 1.0, 0.0).astype(BF16)
    hi, mid, lo = _split3(x)
    return _dg(eye, hi, NT) + _dg(eye, mid, NT) + _dg(eye, lo, NT)


def _lane_cumsum(x, block):
    tri = jnp.where(_iota((block, block), 0) <= _iota((block, block), 1), 1.0, 0.0).astype(BF16)
    hi, mid, lo = _split3(x)
    outs = []
    offset = jnp.zeros((x.shape[0], 1), F32)
    for j in range(x.shape[1] // block):
        sl = slice(j * block, (j + 1) * block)
        c = _dot(hi[:, sl], tri) + _dot(mid[:, sl], tri) + _dot(lo[:, sl], tri)
        outs.append(c + offset)
        offset = offset + c[:, block - 1:block]
    return outs[0] if len(outs) == 1 else jnp.concatenate(outs, axis=1)


def _fox_prompt_kernel(q_ref, k_ref, caug_ref, vt_ref, gate_ref, *rest, tq, tk, n_cast):
    cast_in, y_ref, cast_out = rest[:n_cast], rest[n_cast], rest[n_cast + 1:]
    hp = pl.program_id(1)
    qi = pl.program_id(2)
    nq = k_ref.shape[0] // tq
    vblk = vt_ref.shape[2]
    lane = _iota((tq, LANES), 1)
    first = lane < HEAD_DIM
    q = q_ref[...]
    zero = jnp.zeros_like(q)
    hslot = jnp.where(lane < 3 * FOX_HEADS, lax.rem(lane, FOX_HEADS), -1)
    qsa = []
    for h in range(2):
        ones = jnp.where(hslot == 2 * hp + h, 1.0, 0.0).astype(BF16)
        qh = jnp.where(first, q, zero) if h == 0 else jnp.where(first, zero, q)
        qsa.append(jnp.concatenate([qh, ones], axis=1))
    qsa_t = [z.astype(F32).T.astype(BF16) for z in qsa]

    def scores(item):
        k0, nk, q_lo, _ = item
        kk = jnp.concatenate([k_ref[k0:k0 + nk, :], caug_ref[k0:k0 + nk, :]], axis=1)
        rhs = jnp.concatenate([qsa_t[0][:, q_lo:], qsa_t[1][:, q_lo:]], axis=1)
        return _dot(kk, rhs)

    def update(st, item, state):
        k0, nk, q_lo, masked = item
        nqv = tq - q_lo
        vt = jnp.concatenate([vt_ref[k0 // vblk + i] for i in range(nk // vblk)], axis=1)
        out = []
        for h in range(2):
            m_all, l_all, acc_all = state[h]
            sh = st[:, h * nqv:(h + 1) * nqv]
            if masked:
                sh = jnp.where(_iota((nk, nqv), 0) <= _iota((nk, nqv), 1), sh, NEG_BIG)
            m_old = m_all[:, q_lo:]
            m_new = jnp.maximum(m_old, jnp.max(sh, axis=0, keepdims=True))
            alpha = jnp.exp2(m_old - m_new)
            p = jnp.exp2(sh - m_new)
            l_new = alpha * l_all[:, q_lo:] + jnp.sum(p, axis=0, keepdims=True)
            acc_new = alpha * acc_all[:, q_lo:] + _dot(vt[h * HEAD_DIM:(h + 1) * HEAD_DIM], p.astype(BF16))
            if q_lo:
                m_new = jnp.concatenate([m_all[:, :q_lo], m_new], axis=1)
                l_new = jnp.concatenate([l_all[:, :q_lo], l_new], axis=1)
                acc_new = jnp.concatenate([acc_all[:, :q_lo], acc_new], axis=1)
            out.append((m_new, l_new, acc_new))
        return out

    def program(c):
        items = [(kb * tk, tk, 0, False) for kb in range(c * tq // tk)]
        items += [(c * tq + d * tk, tk, d * tk, True) for d in range(tq // tk)]
        state = [(jnp.full((1, tq), NEG_BIG, F32), jnp.zeros((1, tq), F32), jnp.zeros((HEAD_DIM, tq), F32))
                 for _ in range(2)]
        st_next = scores(items[0])
        for n, item in enumerate(items):
            st = st_next
            if n + 1 < len(items):
                st_next = scores(items[n + 1])
            state = update(st, item, state)
        yts = []
        for _, l_fin, acc in state:
            o = acc * (1.0 / l_fin)
            ms = jnp.mean(o * o, axis=0, keepdims=True)
            yts.append(o * lax.rsqrt(ms + RMS_EPS))
        yt = jnp.concatenate(yts, axis=0)
        y_ref[...] = (yt.T * gate_ref[...]).astype(y_ref.dtype)
        for src, dst in zip(cast_in, cast_out):
            dst[...] = src[...].astype(BF16)

    for c in range(nq):
        pl.when(qi == c)(functools.partial(program, c))


def _fox_prompt_attn(qb, kb, caug, vbt, gate, *, n_b, t_seq, tq, tk, cast_cols=(), cast_rows=()):
    m, w = qb.shape
    n_pairs = w // LANES
    nq = t_seq // tq
    n_steps = n_b * n_pairs * nq
    vblk = vbt.shape[2]
    assert vbt.shape == (m // vblk, w, vblk) and tk % vblk == 0 and tq % tk == 0
    qspec = pl.BlockSpec((tq, LANES), lambda b, hp, qi: (b * nq + qi, hp))
    step = lambda b, hp, qi: (b * n_pairs + hp) * nq + qi
    casts = ([_cast_job(a, (a.shape[0], a.shape[1] // n_steps), lambda b, hp, qi: (0, step(b, hp, qi)))
              for a in cast_cols]
             + [_cast_job(a, (a.shape[0] // n_steps, a.shape[1]), lambda b, hp, qi: (step(b, hp, qi), 0))
                for a in cast_rows])
    outs = pl.pallas_call(
        functools.partial(_fox_prompt_kernel, tq=tq, tk=tk, n_cast=len(casts)),
        grid=(n_b, n_pairs, nq),
        in_specs=[qspec,
                  pl.BlockSpec((t_seq, LANES), lambda b, hp, qi: (b, hp)),
                  pl.BlockSpec((t_seq, LANES), lambda b, hp, qi: (b, 0)),
                  pl.BlockSpec((t_seq // vblk, LANES, vblk), lambda b, hp, qi: (b, hp, 0)),
                  qspec] + [spec for _, spec, _ in casts],
        out_specs=[qspec] + [spec for _, spec, _ in casts],
        out_shape=[jax.ShapeDtypeStruct((m, w), BF16)] + [sds for _, _, sds in casts],
        compiler_params=_params(("arbitrary", "arbitrary", "arbitrary")),
        name="fox_prompt_attn",
    )(qb, kb, caug, vbt, gate, *[a for a, _, _ in casts])
    return outs[0], outs[1:]


def _fox_sample_kernel(q_ref, kn_ref, vn_ref, ckt_ref, cvt_ref, clf_ref, lf_ref, gate_ref, y_ref):
    t = q_ref.shape[0]
    tril = jnp.where(_iota((t, t), 0) >= _iota((t, t), 1), 1.0, 0.0).astype(BF16)
    clf_row = clf_ref[0, 0]
    c_tot = jnp.sum(clf_row, axis=-1, keepdims=True)
    cn_col = _dot01(tril, lf_ref[...])
    cn_row = _transpose01(cn_col)
    cc_row = _lane_cumsum(clf_row, 2 * LANES)
    lane = _iota((t, LANES), 1)
    first = lane < HEAD_DIM
    causal = _iota((t, t), 1) <= _iota((t, t), 0)
    ys = []
    for p in range(q_ref.shape[1] // LANES):
        sl = slice(p * LANES, (p + 1) * LANES)
        q = q_ref[:, sl]
        zero = jnp.zeros_like(q)
        qs = jnp.concatenate([jnp.where(first, q, zero), jnp.where(first, zero, q)], axis=0)
        kct = jnp.concatenate([ckt_ref[0, 0, 2 * p], ckt_ref[0, 0, 2 * p + 1]], axis=0).astype(BF16)
        vct = jnp.concatenate([cvt_ref[0, 0, 2 * p], cvt_ref[0, 0, 2 * p + 1]], axis=0).astype(BF16)
        kn = kn_ref[:, sl]
        vn = vn_ref[:, sl]
        s_c = _dot(qs, kct)
        s_n = _dg(qs, kn, NT)
        o = []
        for h in range(2):
            hd = 2 * p + h
            cq = cn_col[:, hd:hd + 1] + c_tot[hd:hd + 1]
            sc = s_c[h * t:(h + 1) * t] + (cq - cc_row[hd:hd + 1]) * LOG2E
            sn = s_n[h * t:(h + 1) * t] + (cn_col[:, hd:hd + 1] - cn_row[hd:hd + 1]) * LOG2E
            sn = jnp.where(causal, sn, NEG_BIG)
            mx = jnp.maximum(jnp.max(sc, axis=-1, keepdims=True), jnp.max(sn, axis=-1, keepdims=True))
            pc = jnp.exp2(sc - mx)
            pn = jnp.exp2(sn - mx)
            den = jnp.sum(pc, axis=-1, keepdims=True) + jnp.sum(pn, axis=-1, keepdims=True)
            o.append((_dg(pc.astype(BF16), vct, NT) + _dot(pn.astype(BF16), vn)) / den)
        om = jnp.where(first, o[0], o[1])
        sq = om * om
        ms0 = jnp.sum(jnp.where(first, sq, 0.0), axis=-1, keepdims=True)
        ms1 = jnp.sum(jnp.where(first, 0.0, sq), axis=-1, keepdims=True)
        ms = jnp.where(first, ms0, ms1) * (1.0 / HEAD_DIM)
        ys.append(om * lax.rsqrt(ms + RMS_EPS) * gate_ref[:, sl])
    y_ref[...] = jnp.concatenate(ys, axis=1).astype(y_ref.dtype)


def _fox_sample_attn(qb, kb, vb, cache_kt, cache_vt, cache_lft, lf, gate, *, t_seq, layer):
    m, w = qb.shape
    n_b = m // t_seq
    row = pl.BlockSpec((t_seq, w), lambda b: (b, 0))
    cache = pl.BlockSpec((1, 1) + cache_kt.shape[2:], lambda b: (layer, b, 0, 0, 0))
    return pl.pallas_call(
        _fox_sample_kernel,
        grid=(n_b,),
        in_specs=[row, row, row, cache, cache,
                  pl.BlockSpec((1, 1) + cache_lft.shape[2:], lambda b: (layer, b, 0, 0)),
                  pl.BlockSpec((t_seq, lf.shape[1]), lambda b: (b, 0)), row],
        out_specs=row,
        out_shape=jax.ShapeDtypeStruct((m, w), BF16),
        compiler_params=_params(("arbitrary",)),
        name="fox_sample_attn",
    )(qb, kb, vb, cache_kt, cache_vt, cache_lft, lf, gate)


def _out_ln_kernel(yr_ref, yf_ref, x_ref, wo_ref, g_ref, b_ref, h_ref, wob_ref, *, alpha):
    @pl.when(pl.program_id(0) == 0)
    def _():
        wob_ref[...] = wo_ref[...].astype(BF16)

    half = yr_ref.shape[1]
    mix = _dot(yr_ref[...], wob_ref[0:half, :]) + _dot(yf_ref[...], wob_ref[half:, :])
    h_ref[...] = _layer_norm(alpha * x_ref[...] + mix, g_ref[...], b_ref[...])


def _out_ln(yr, yf, x, wo, g, b, *, tm, alpha):
    m, d = x.shape
    row = lambda i: (i, 0)
    return pl.pallas_call(
        functools.partial(_out_ln_kernel, alpha=alpha),
        grid=(m // tm,),
        in_specs=[pl.BlockSpec((tm, yr.shape[1]), row), pl.BlockSpec((tm, yf.shape[1]), row),
                  pl.BlockSpec((tm, d), row), _const_spec(wo.shape), _const_spec(g.shape), _const_spec(b.shape)],
        out_specs=pl.BlockSpec((tm, d), row),
        out_shape=jax.ShapeDtypeStruct((m, d), F32),
        scratch_shapes=[pltpu.VMEM(wo.shape, BF16)],
        compiler_params=_params(("arbitrary",)),
        name="out_ln",
    )(yr, yf, x, wo, g, b)


def _ffn_ln_kernel(h_ref, wu_ref, wd_ref, g_ref, b_ref, o_ref, hb_ref, *, alpha, n_f):
    j = pl.program_id(1)

    @pl.when(j == 0)
    def _():
        h = h_ref[...]
        hb_ref[...] = h.astype(BF16)
        o_ref[...] = alpha * h

    u = jnp.maximum(_dot(hb_ref[...], wu_ref[...]), 0.0)
    o_ref[...] += _dot((u * u).astype(BF16), wd_ref[...])

    @pl.when(j == n_f - 1)
    def _():
        o_ref[...] = _layer_norm(o_ref[...], g_ref[...], b_ref[...])


def _ffn_ln(h, wu, wd, g, b, *, tm, tf, alpha):
    m, d = h.shape
    n_f = wu.shape[1] // tf
    return pl.pallas_call(
        functools.partial(_ffn_ln_kernel, alpha=alpha, n_f=n_f),
        grid=(m // tm, n_f),
        in_specs=[pl.BlockSpec((tm, d), lambda i, j: (i, 0), pipeline_mode=pl.Buffered(1)),
                  pl.BlockSpec((d, tf), lambda i, j: (0, j)),
                  pl.BlockSpec((tf, d), lambda i, j: (j, 0)),
                  _const_spec(g.shape), _const_spec(b.shape)],
        out_specs=pl.BlockSpec((tm, d), lambda i, j: (i, 0)),
        out_shape=jax.ShapeDtypeStruct((m, d), F32),
        scratch_shapes=[pltpu.VMEM((tm, d), BF16)],
        compiler_params=_params(("arbitrary", "arbitrary")),
        name="ffn_ln",
    )(h, wu, wd, g, b)


def _pad_cols(x, n):
    return jnp.pad(x, [(0, 0)] * (x.ndim - 1) + [(0, n - x.shape[-1])])


def _stream(x, shift_prev, s0, wts, *, t_seq, tm, tm_ffn, chunk, cache=None):
    n_b, _, d = x.shape
    m = n_b * t_seq
    x2 = x.reshape(m, d)
    qb, k_t, v_t, kb, vb, vbt, gate, lf, caug = _fox_proj(x2, wts["w_f"], wts["bf_row"], wts["og_g"],
                                                          tm=tm, t_seq=t_seq)
    fprev = _pad_cols(shift_prev, RP_PAD)
    at, rt, bt, kt, vr, g, wtot, sh = _rwkv_proj(
        x2, wts["w_r"], wts["mu"], fprev, wts["w0"], wts["w2p"], wts["a0"], wts["a2p"], wts["g2p"], wts["k_k"],
        wts["k_a"], tm=tm, t_seq=t_seq, chunk=chunk)
    y_r, s_new = _rwkv_scan(at, rt, bt, kt, vr, wtot, g, s0, wts["lnx_g"], wts["lnx_b"], wts["r_k"],
                            chunk=chunk, rows=min(SCAN_CHUNKS * chunk, t_seq), t_seq=t_seq)
    if cache is None:
        y_f, (wts["w_up_b"], wts["w_down_b"]) = _fox_prompt_attn(
            qb, kb, caug, vbt, gate, n_b=n_b, t_seq=t_seq, tq=TQ_ATTN, tk=TK_ATTN,
            cast_cols=(wts["w_up"],), cast_rows=(wts["w_down"],))
    else:
        ck, cv, clf, layer = cache
        y_f = _fox_sample_attn(qb, kb, vb, jnp.transpose(ck, (0, 1, 3, 4, 2)), jnp.transpose(cv, (0, 1, 3, 4, 2)),
                               jnp.transpose(clf, (0, 1, 3, 2)), lf, gate, t_seq=t_seq, layer=layer)
    h = _out_ln(y_r, y_f, x2, wts["w_o"], wts["ln1_g"], wts["ln1_b"], tm=min(TM_OUT, m), alpha=wts["alpha"])
    y = _ffn_ln(h, wts["w_up_b"], wts["w_down_b"], wts["ln2_g"], wts["ln2_b"], tm=tm_ffn, tf=TF_FFN, alpha=wts["alpha"])
    heads = FW // HEAD_DIM
    to_cache = lambda z: jnp.transpose(z, (0, 3, 1, 2))
    return (y.reshape(n_b, t_seq, d), to_cache(k_t), to_cache(v_t), lf.reshape(n_b, t_seq, heads), s_new,
            sh[..., :shift_prev.shape[-1]])


def kernel(x_prompt, x_sample, cache_fox_k, cache_fox_v, cache_fox_logf, state_rwkv_wkv, state_rwkv_shift,
           w_in, rwkv_mu, rwkv_w0, rwkv_w2, rwkv_a0, rwkv_a2, rwkv_g2, rwkv_k_k, rwkv_k_a, rwkv_r_k,
           rwkv_lnx_g, rwkv_lnx_b, fox_b_f, fox_out_g, w_o, ln1_g, ln1_b, w_up, w_down, ln2_g, ln2_b):
    depth = w_in.shape[0]
    assert depth == 1, "single-layer problem"
    rwkv_proj = rwkv_mu.shape[-1]
    lora = (rwkv_w2.shape[1], rwkv_a2.shape[1], rwkv_g2.shape[1])
    assert rwkv_w0.shape[-1] == RW and fox_out_g.shape[-1] == FW and rwkv_proj == 3 * RW + sum(lora)
    assert lora[0] + lora[1] == LORA_WA and lora[2] <= LORA_G and RP_PAD <= w_in.shape[-1]
    alpha = (2 * depth) ** 0.25
    l = 0
    w = w_in[l]
    fo = rwkv_proj
    row = lambda z: z.reshape(1, -1)
    pad_rows = lambda z, n: jnp.pad(z, ((0, n - z.shape[0]), (0, 0)))
    wts = dict(
        alpha=alpha,
        w_r=jnp.transpose(w)[:RP_PAD].astype(BF16),
        w_f=jnp.concatenate([w[:, fo:fo + 3 * FW], w[:, fo + 3 * FW + FOX_HEADS:],
                             _pad_cols(jnp.tile(w[:, fo + 3 * FW:fo + 3 * FW + FOX_HEADS], (1, 3)), LANES)],
                            axis=-1).astype(BF16),
        mu=_pad_cols(row(rwkv_mu[l]), RP_PAD),
        w0=row(rwkv_w0[l]), w2p=jnp.pad(rwkv_w2[l], ((0, lora[1]), (0, 0))).astype(BF16),
        a0=row(rwkv_a0[l]), a2p=jnp.pad(rwkv_a2[l], ((lora[0], 0), (0, 0))).astype(BF16),
        g2p=pad_rows(rwkv_g2[l], LORA_G).astype(BF16),
        k_k=row(rwkv_k_k[l]), k_a=row(rwkv_k_a[l]), r_k=row(rwkv_r_k[l]),
        lnx_g=row(rwkv_lnx_g[l]), lnx_b=row(rwkv_lnx_b[l]),
        bf_row=_pad_cols(jnp.tile(row(fox_b_f[l]), (1, 3)), LANES), og_g=row(fox_out_g[l]),
        w_o=w_o[l], ln1_g=row(ln1_g[l]), ln1_b=row(ln1_b[l]),
        w_up=w_up[l], w_down=w_down[l], ln2_g=row(ln2_g[l]), ln2_b=row(ln2_b[l]),
    )
    n_p, t_p, _ = x_prompt.shape
    n_s, t_s, _ = x_sample.shape
    heads = RW // HEAD_DIM
    shift0 = jnp.zeros((n_p, 1, rwkv_proj), F32)
    s_zero = jnp.zeros((n_p, heads, HEAD_DIM, HEAD_DIM), F32)
    yp, kp, vp, fp, sp, shp = _stream(x_prompt, shift0, s_zero, wts, t_seq=t_p, tm=TM_PROJ, tm_ffn=TM_FFN, chunk=CHUNK)
    ys, ks, vs, fs, ss, shs = _stream(x_sample, state_rwkv_shift[l], state_rwkv_wkv[l], wts, t_seq=t_s,
                                      tm=n_s * t_s, tm_ffn=n_s * t_s, chunk=t_s,
                                      cache=(cache_fox_k, cache_fox_v, cache_fox_logf, l))
    return (yp, ys, kp[None], vp[None], fp[None], sp[None], shp[None],
            ks[None], vs[None], fs[None], ss[None], shs[None])
```
